```python
import math
import jax, jax.numpy as jnp
from jax import lax
import numpy as np

D_MODEL = 1024
BATCH = 32
SEQ = 2048
DEPTH = 4

HEAD_DIM = 64
D_SSM = D_MODEL
SSM_HEADS = D_SSM // HEAD_DIM
SSM_GROUPS = 4
SSM_STATE = 128
CONV_WIDTH = 4
CHUNK = 128
D_ATT = D_MODEL
ATT_HEADS = D_ATT // HEAD_DIM
Q_BLOCK = 128
D_MIX = D_SSM + D_ATT
D_CONV = D_SSM + 2 * SSM_GROUPS * SSM_STATE
D_IN_PROJ = D_SSM + D_CONV + SSM_HEADS + 3 * D_ATT
D_FF = 4 * D_MODEL
EPS = 1e-5
DT_MIN = 1e-3
DT_MAX = 1e-1

kernel_name = "hybrid_ssd_stickbreaking_trunk"


def rmsnorm(x, g):
    xf = x.astype(jnp.float32)
    r = lax.rsqrt(jnp.mean(xf * xf, axis=-1, keepdims=True) + EPS)
    return (xf * r * g.astype(jnp.float32)).astype(x.dtype)


def causal_depthwise_conv(u, w, b):
    c = u.shape[-1]
    out = lax.conv_general_dilated(
        u, w[:, None, :].astype(u.dtype), window_strides=(1,),
        padding=[(CONV_WIDTH - 1, 0)],
        dimension_numbers=("NWC", "WIO", "NWC"), feature_group_count=c)
    return out + b


def ssd_chunked(x, dt, a_neg, bm, cm):
    b, L = x.shape[:2]
    nc = L // CHUNK
    r = SSM_HEADS // SSM_GROUPS
    xdt = (x * dt[..., None]).reshape(b, nc, CHUNK, SSM_GROUPS, r, HEAD_DIM)
    a = (dt * a_neg).reshape(b, nc, CHUNK, SSM_GROUPS, r)
    a = jnp.moveaxis(a, 2, -1)
    a_cum = jnp.cumsum(a, axis=-1)
    bc = bm.reshape(b, nc, CHUNK, SSM_GROUPS, SSM_STATE)
    cc = cm.reshape(b, nc, CHUNK, SSM_GROUPS, SSM_STATE)
    causal = jnp.tril(jnp.ones((CHUNK, CHUNK), dtype=bool))
    seg = a_cum[..., :, None] - a_cum[..., None, :]
    lmat = jnp.exp(jnp.where(causal, seg, -jnp.inf))
    cb = jnp.einsum("bclgn,bcsgn->bcgls", cc, bc)
    w = cb[:, :, :, None] * lmat
    y_diag = jnp.einsum("bcgrls,bcsgrp->bclgrp", w, xdt)
    decay_states = jnp.exp(a_cum[..., -1:] - a_cum)
    states = jnp.einsum("bclgn,bcgrl,bclgrp->bcgrpn", bc, decay_states, xdt)
    chunk_decay = jnp.exp(a_cum[..., -1])

    def step(h, inp):
        s_c, d_c = inp
        h_new = h * d_c[..., None, None] + s_c
        return h_new, h

    h0 = jnp.zeros_like(states[:, 0])
    _, prev_states = lax.scan(step, h0, (jnp.moveaxis(states, 1, 0), jnp.moveaxis(chunk_decay, 1, 0)))
    prev_states = jnp.moveaxis(prev_states, 0, 1)
    state_decay = jnp.moveaxis(jnp.exp(a_cum), -1, 2)[..., None]
    y_off = jnp.einsum("bclgn,bcgrpn->bclgrp", cc, prev_states) * state_decay
    return (y_diag + y_off).reshape(b, L, SSM_HEADS, HEAD_DIM)


def stick_breaking_attention(q, k, v):
    L = q.shape[1]
    scale = HEAD_DIM ** -0.5
    outs = []
    for i in range(L // Q_BLOCK):
        q0 = i * Q_BLOCK
        kv_len = q0 + Q_BLOCK
        qb = q[:, q0:kv_len]
        kb = k[:, :kv_len]
        vb = v[:, :kv_len]
        logits = jnp.einsum("bthd,bshd->bhts", qb, kb).astype(jnp.float32) * scale
        t_pos = q0 + jnp.arange(Q_BLOCK)
        s_pos = jnp.arange(kv_len)
        mask = s_pos[None, :] < t_pos[:, None]
        log_beta = jax.nn.log_sigmoid(logits)
        log_1m_beta = jnp.where(mask, jax.nn.log_sigmoid(-logits), 0.0)
        suffix = lax.cumsum(log_1m_beta, axis=log_1m_beta.ndim - 1, reverse=True) - log_1m_beta
        att = jnp.where(mask, jnp.exp(log_beta + suffix), 0.0)
        outs.append(jnp.einsum("bhts,bshd->bthd", att.astype(vb.dtype), vb))
    return jnp.concatenate(outs, axis=1)


def hybrid_layer(x, norm_mix_g, w_in, conv_w, conv_b, dt_bias, a_log, d_skip,
                 ssd_norm_g, att_norm_g, w_out, norm_mlp_g, w_up, w_down):
    b, L, _ = x.shape
    h = rmsnorm(x, norm_mix_g)
    proj = h @ w_in
    splits = [D_SSM, D_SSM + D_CONV, D_SSM + D_CONV + SSM_HEADS,
              D_SSM + D_CONV + SSM_HEADS + D_ATT,
              D_SSM + D_CONV + SSM_HEADS + 2 * D_ATT]
    z, xbc, dt_raw, q, k, v = jnp.split(proj, splits, axis=-1)
    xbc = jax.nn.silu(causal_depthwise_conv(xbc, conv_w, conv_b))
    xs, bm, cm = jnp.split(xbc, [D_SSM, D_SSM + SSM_GROUPS * SSM_STATE], axis=-1)
    dt = jax.nn.softplus(dt_raw.astype(jnp.float32) + dt_bias.astype(jnp.float32))
    a_neg = -jnp.exp(a_log.astype(jnp.float32))
    xs = xs.reshape(b, L, SSM_HEADS, HEAD_DIM)
    y = ssd_chunked(xs, dt, a_neg,
                    bm.reshape(b, L, SSM_GROUPS, SSM_STATE),
                    cm.reshape(b, L, SSM_GROUPS, SSM_STATE))
    y = (y + xs * d_skip[:, None]).reshape(b, L, D_SSM)
    y_ssd = rmsnorm(y * jax.nn.silu(z), ssd_norm_g)
    y_att = stick_breaking_attention(q.reshape(b, L, ATT_HEADS, HEAD_DIM),
                                     k.reshape(b, L, ATT_HEADS, HEAD_DIM),
                                     v.reshape(b, L, ATT_HEADS, HEAD_DIM))
    y_att = rmsnorm(y_att.reshape(b, L, D_ATT), att_norm_g)
    x = x + jnp.concatenate([y_ssd, y_att], axis=-1) @ w_out
    h = rmsnorm(x, norm_mlp_g)
    x = x + jnp.square(jax.nn.relu(h @ w_up)) @ w_down
    return x


def _fwd_setup_inputs(seed: int = 0) -> dict:
    key = jax.random.key(seed)
    ks = jax.random.split(key, 16)
    f32 = jnp.float32
    nrm = lambda k, shape, s: jax.random.normal(k, shape, f32) * s
    x = jax.random.normal(ks[0], (BATCH, SEQ, D_MODEL), f32)
    norm_mix_g = 1.0 + nrm(ks[1], (DEPTH, D_MODEL), 0.02)
    w_in = nrm(ks[2], (DEPTH, D_MODEL, D_IN_PROJ), D_MODEL ** -0.5)
    conv_w = nrm(ks[3], (DEPTH, CONV_WIDTH, D_CONV), CONV_WIDTH ** -0.5)
    conv_b = nrm(ks[4], (DEPTH, D_CONV), 0.02)
    dt0 = jnp.exp(jax.random.uniform(ks[5], (DEPTH, SSM_HEADS), f32,
                                     math.log(DT_MIN), math.log(DT_MAX)))
    dt_bias = dt0 + jnp.log(-jnp.expm1(-dt0))
    a_log = jnp.log(jax.random.uniform(ks[6], (DEPTH, SSM_HEADS), f32, 1.0, 16.0))
    d_skip = 1.0 + nrm(ks[7], (DEPTH, SSM_HEADS), 0.02)
    ssd_norm_g = 1.0 + nrm(ks[8], (DEPTH, D_SSM), 0.02)
    att_norm_g = 1.0 + nrm(ks[9], (DEPTH, D_ATT), 0.02)
    w_out = nrm(ks[10], (DEPTH, D_MIX, D_MODEL), 0.5 * D_MIX ** -0.5)
    norm_mlp_g = 1.0 + nrm(ks[11], (DEPTH, D_MODEL), 0.02)
    w_up = nrm(ks[12], (DEPTH, D_MODEL, D_FF), D_MODEL ** -0.5)
    w_down = nrm(ks[13], (DEPTH, D_FF, D_MODEL), 0.5 * D_FF ** -0.5)
    final_norm_g = 1.0 + nrm(ks[14], (D_MODEL,), 0.02)
    return {"x": x, "norm_mix_g": norm_mix_g, "w_in": w_in, "conv_w": conv_w,
            "conv_b": conv_b, "dt_bias": dt_bias, "a_log": a_log, "d_skip": d_skip,
            "ssd_norm_g": ssd_norm_g, "att_norm_g": att_norm_g, "w_out": w_out,
            "norm_mlp_g": norm_mlp_g, "w_up": w_up, "w_down": w_down,
            "final_norm_g": final_norm_g}


def _fwd_reference(x, norm_mix_g, w_in, conv_w, conv_b, dt_bias, a_log, d_skip,
              ssd_norm_g, att_norm_g, w_out, norm_mlp_g, w_up, w_down, final_norm_g):
    for l in range(DEPTH):
        x = hybrid_layer(x, norm_mix_g[l], w_in[l], conv_w[l], conv_b[l], dt_bias[l],
                         a_log[l], d_skip[l], ssd_norm_g[l], att_norm_g[l], w_out[l],
                         norm_mlp_g[l], w_up[l], w_down[l])
    return rmsnorm(x, final_norm_g)


import jax as _jax
import jax.numpy as _jnp

TWIN_FORMAT = 'train_step'
FWD_PARAMS = ['x', 'norm_mix_g', 'w_in', 'conv_w', 'conv_b', 'dt_bias', 'a_log', 'd_skip', 'ssd_norm_g', 'att_norm_g', 'w_out', 'norm_mlp_g', 'w_up', 'w_down', 'final_norm_g']
TWIN_WEIGHTS = ['norm_mix_g', 'w_in', 'conv_w', 'conv_b', 'dt_bias', 'a_log', 'd_skip', 'ssd_norm_g', 'att_norm_g', 'w_out', 'norm_mlp_g', 'w_up', 'w_down', 'final_norm_g']
TWIN_DIFF_INPUT = 'x'
TWIN_INPUTS = ['x', 'norm_mix_g', 'w_in', 'conv_w', 'conv_b', 'dt_bias', 'a_log', 'd_skip', 'ssd_norm_g', 'att_norm_g', 'w_out', 'norm_mlp_g', 'w_up', 'w_down', 'final_norm_g', 'loss_target', 'm_norm_mix_g', 'm_w_in', 'm_conv_w', 'm_conv_b', 'm_dt_bias', 'm_a_log', 'm_d_skip', 'm_ssd_norm_g', 'm_att_norm_g', 'm_w_out', 'm_norm_mlp_g', 'm_w_up', 'm_w_down', 'm_final_norm_g', 'v_norm_mix_g', 'v_w_in', 'v_conv_w', 'v_conv_b', 'v_dt_bias', 'v_a_log', 'v_d_skip', 'v_ssd_norm_g', 'v_att_norm_g', 'v_w_out', 'v_norm_mlp_g', 'v_w_up', 'v_w_down', 'v_final_norm_g']
TWIN_OUTPUTS = ['loss', 'grad_x', 'grad_norm_mix_g', 'grad_w_in', 'grad_conv_w', 'grad_conv_b', 'grad_dt_bias', 'grad_a_log', 'grad_d_skip', 'grad_ssd_norm_g', 'grad_att_norm_g', 'grad_w_out', 'grad_norm_mlp_g', 'grad_w_up', 'grad_w_down', 'grad_final_norm_g', 'delta_norm_mix_g', 'delta_w_in', 'delta_conv_w', 'delta_conv_b', 'delta_dt_bias', 'delta_a_log', 'delta_d_skip', 'delta_ssd_norm_g', 'delta_att_norm_g', 'delta_w_out', 'delta_norm_mlp_g', 'delta_w_up', 'delta_w_down', 'delta_final_norm_g', 'new_m_norm_mix_g', 'new_m_w_in', 'new_m_conv_w', 'new_m_conv_b', 'new_m_dt_bias', 'new_m_a_log', 'new_m_d_skip', 'new_m_ssd_norm_g', 'new_m_att_norm_g', 'new_m_w_out', 'new_m_norm_mlp_g', 'new_m_w_up', 'new_m_w_down', 'new_m_final_norm_g', 'new_v_norm_mix_g', 'new_v_w_in', 'new_v_conv_w', 'new_v_conv_b', 'new_v_dt_bias', 'new_v_a_log', 'new_v_d_skip', 'new_v_ssd_norm_g', 'new_v_att_norm_g', 'new_v_w_out', 'new_v_norm_mlp_g', 'new_v_w_up', 'new_v_w_down', 'new_v_final_norm_g']
TWIN_LEAF_KINDS = {'loss': 'loss', 'grad_x': 'grad_x', 'grad_norm_mix_g': 'grad_w', 'grad_w_in': 'grad_w', 'grad_conv_w': 'grad_w', 'grad_conv_b': 'grad_w', 'grad_dt_bias': 'grad_w', 'grad_a_log': 'grad_w', 'grad_d_skip': 'grad_w', 'grad_ssd_norm_g': 'grad_w', 'grad_att_norm_g': 'grad_w', 'grad_w_out': 'grad_w', 'grad_norm_mlp_g': 'grad_w', 'grad_w_up': 'grad_w', 'grad_w_down': 'grad_w', 'grad_final_norm_g': 'grad_w', 'delta_norm_mix_g': 'delta_w', 'delta_w_in': 'delta_w', 'delta_conv_w': 'delta_w', 'delta_conv_b': 'delta_w', 'delta_dt_bias': 'delta_w', 'delta_a_log': 'delta_w', 'delta_d_skip': 'delta_w', 'delta_ssd_norm_g': 'delta_w', 'delta_att_norm_g': 'delta_w', 'delta_w_out': 'delta_w', 'delta_norm_mlp_g': 'delta_w', 'delta_w_up': 'delta_w', 'delta_w_down': 'delta_w', 'delta_final_norm_g': 'delta_w', 'new_m_norm_mix_g': 'new_m', 'new_m_w_in': 'new_m', 'new_m_conv_w': 'new_m', 'new_m_conv_b': 'new_m', 'new_m_dt_bias': 'new_m', 'new_m_a_log': 'new_m', 'new_m_d_skip': 'new_m', 'new_m_ssd_norm_g': 'new_m', 'new_m_att_norm_g': 'new_m', 'new_m_w_out': 'new_m', 'new_m_norm_mlp_g': 'new_m', 'new_m_w_up': 'new_m', 'new_m_w_down': 'new_m', 'new_m_final_norm_g': 'new_m', 'new_v_norm_mix_g': 'new_v', 'new_v_w_in': 'new_v', 'new_v_conv_w': 'new_v', 'new_v_conv_b': 'new_v', 'new_v_dt_bias': 'new_v', 'new_v_a_log': 'new_v', 'new_v_d_skip': 'new_v', 'new_v_ssd_norm_g': 'new_v', 'new_v_att_norm_g': 'new_v', 'new_v_w_out': 'new_v', 'new_v_norm_mlp_g': 'new_v', 'new_v_w_up': 'new_v', 'new_v_w_down': 'new_v', 'new_v_final_norm_g': 'new_v'}


def _forward(args):
    return _fwd_reference(*[args[k] for k in FWD_PARAMS])


def _output_shape():
    out = _jax.eval_shape(lambda: _forward(_fwd_setup_inputs(0)))
    return out.shape, out.dtype

N_MICROBATCH = 1
ADAM_LR = 0.001
ADAM_B1 = 0.9
ADAM_B2 = 0.999
ADAM_EPS = 1e-08
ADAM_WD = 0.01
ADAM_STEP = 10
PER_EXAMPLE_BATCH_AXIS = {'x': 0, 'loss_target': 0}
SHARED_INPUTS = []
_WEIGHT_DTYPES = {'norm_mix_g': _jnp.float32, 'w_in': _jnp.float32, 'conv_w': _jnp.float32, 'conv_b': _jnp.float32, 'dt_bias': _jnp.float32, 'a_log': _jnp.float32, 'd_skip': _jnp.float32, 'ssd_norm_g': _jnp.float32, 'att_norm_g': _jnp.float32, 'w_out': _jnp.float32, 'norm_mlp_g': _jnp.float32, 'w_up': _jnp.float32, 'w_down': _jnp.float32, 'final_norm_g': _jnp.float32}
MOMENT_SCALE = {'norm_mix_g': 1.416663e-01, 'w_in': 5.714446e-02, 'conv_w': 5.523335e-02, 'conv_b': 7.614368e-02, 'dt_bias': 1.688540e-01, 'a_log': 2.341907e-01, 'd_skip': 4.145402e-01, 'ssd_norm_g': 7.223291e-02, 'att_norm_g': 7.384778e-02, 'w_out': 2.139991e-01, 'norm_mlp_g': 1.331246e-01, 'w_up': 6.423449e-02, 'w_down': 2.495188e-01, 'final_norm_g': 6.479323e+01}


def _to_microbatches(a, axis):
    t = _jnp.moveaxis(a, axis, 0)
    t = t.reshape((N_MICROBATCH, t.shape[0] // N_MICROBATCH) + t.shape[1:])
    return _jnp.moveaxis(t, 1, axis + 1)


def setup_inputs(seed: int = 0) -> dict:
    inp = _fwd_setup_inputs(seed)
    key = _jax.random.fold_in(_jax.random.key(seed), 7919)
    shape, _ = _output_shape()
    out = dict(inp)
    out["loss_target"] = _jax.random.normal(_jax.random.fold_in(key, 0), shape, _jnp.float32)
    for i, name in enumerate(TWIN_WEIGHTS):
        w = inp[name].astype(_jnp.float32)
        if MOMENT_SCALE is None:
            s = _jnp.sqrt(_jnp.mean(_jnp.square(w)) + 1e-30)
        else:
            s = MOMENT_SCALE[name]
        km, kv = _jax.random.split(_jax.random.fold_in(key, i + 1))
        out[name] = w
        out["m_" + name] = s * _jax.random.normal(km, w.shape, _jnp.float32)
        out["v_" + name] = (s * s) * _jax.random.uniform(kv, w.shape, _jnp.float32, 0.5, 1.5)
    if N_MICROBATCH > 1:
        for name, axis in PER_EXAMPLE_BATCH_AXIS.items():
            out[name] = _to_microbatches(out[name], axis)
    return {'x': out['x'], 'norm_mix_g': out['norm_mix_g'], 'w_in': out['w_in'], 'conv_w': out['conv_w'], 'conv_b': out['conv_b'], 'dt_bias': out['dt_bias'], 'a_log': out['a_log'], 'd_skip': out['d_skip'], 'ssd_norm_g': out['ssd_norm_g'], 'att_norm_g': out['att_norm_g'], 'w_out': out['w_out'], 'norm_mlp_g': out['norm_mlp_g'], 'w_up': out['w_up'], 'w_down': out['w_down'], 'final_norm_g': out['final_norm_g'], 'loss_target': out['loss_target'], 'm_norm_mix_g': out['m_norm_mix_g'], 'm_w_in': out['m_w_in'], 'm_conv_w': out['m_conv_w'], 'm_conv_b': out['m_conv_b'], 'm_dt_bias': out['m_dt_bias'], 'm_a_log': out['m_a_log'], 'm_d_skip': out['m_d_skip'], 'm_ssd_norm_g': out['m_ssd_norm_g'], 'm_att_norm_g': out['m_att_norm_g'], 'm_w_out': out['m_w_out'], 'm_norm_mlp_g': out['m_norm_mlp_g'], 'm_w_up': out['m_w_up'], 'm_w_down': out['m_w_down'], 'm_final_norm_g': out['m_final_norm_g'], 'v_norm_mix_g': out['v_norm_mix_g'], 'v_w_in': out['v_w_in'], 'v_conv_w': out['v_conv_w'], 'v_conv_b': out['v_conv_b'], 'v_dt_bias': out['v_dt_bias'], 'v_a_log': out['v_a_log'], 'v_d_skip': out['v_d_skip'], 'v_ssd_norm_g': out['v_ssd_norm_g'], 'v_att_norm_g': out['v_att_norm_g'], 'v_w_out': out['v_w_out'], 'v_norm_mlp_g': out['v_norm_mlp_g'], 'v_w_up': out['v_w_up'], 'v_w_down': out['v_w_down'], 'v_final_norm_g': out['v_final_norm_g']}


def _loss(weights, diff, rest, loss_target):
    with _jax.named_scope("forward"):
        args = {**rest, TWIN_DIFF_INPUT: diff, **{k: w.astype(_WEIGHT_DTYPES[k]) for k, w in weights.items()}}
        y = _forward(args)
    with _jax.named_scope("loss_head"):
        err = _jnp.square(y.astype(_jnp.float32) - loss_target)
        return 0.5 * _jnp.sum(_jnp.mean(err, axis=-1)) if err.ndim else 0.5 * err


def _adamw(w, g, m, v):
    m = ADAM_B1 * m + (1.0 - ADAM_B1) * g
    v = ADAM_B2 * v + (1.0 - ADAM_B2) * _jnp.square(g)
    m_hat = m / (1.0 - ADAM_B1 ** ADAM_STEP)
    v_hat = v / (1.0 - ADAM_B2 ** ADAM_STEP)
    delta = -ADAM_LR * (m_hat / (_jnp.sqrt(v_hat) + ADAM_EPS) + ADAM_WD * w)
    return delta, m, v


def reference(x, norm_mix_g, w_in, conv_w, conv_b, dt_bias, a_log, d_skip, ssd_norm_g, att_norm_g, w_out, norm_mlp_g, w_up, w_down, final_norm_g, loss_target, m_norm_mix_g, m_w_in, m_conv_w, m_conv_b, m_dt_bias, m_a_log, m_d_skip, m_ssd_norm_g, m_att_norm_g, m_w_out, m_norm_mlp_g, m_w_up, m_w_down, m_final_norm_g, v_norm_mix_g, v_w_in, v_conv_w, v_conv_b, v_dt_bias, v_a_log, v_d_skip, v_ssd_norm_g, v_att_norm_g, v_w_out, v_norm_mlp_g, v_w_up, v_w_down, v_final_norm_g):
    given = dict(x=x, norm_mix_g=norm_mix_g, w_in=w_in, conv_w=conv_w, conv_b=conv_b, dt_bias=dt_bias, a_log=a_log, d_skip=d_skip, ssd_norm_g=ssd_norm_g, att_norm_g=att_norm_g, w_out=w_out, norm_mlp_g=norm_mlp_g, w_up=w_up, w_down=w_down, final_norm_g=final_norm_g, loss_target=loss_target, m_norm_mix_g=m_norm_mix_g, m_w_in=m_w_in, m_conv_w=m_conv_w, m_conv_b=m_conv_b, m_dt_bias=m_dt_bias, m_a_log=m_a_log, m_d_skip=m_d_skip, m_ssd_norm_g=m_ssd_norm_g, m_att_norm_g=m_att_norm_g, m_w_out=m_w_out, m_norm_mlp_g=m_norm_mlp_g, m_w_up=m_w_up, m_w_down=m_w_down, m_final_norm_g=m_final_norm_g, v_norm_mix_g=v_norm_mix_g, v_w_in=v_w_in, v_conv_w=v_conv_w, v_conv_b=v_conv_b, v_dt_bias=v_dt_bias, v_a_log=v_a_log, v_d_skip=v_d_skip, v_ssd_norm_g=v_ssd_norm_g, v_att_norm_g=v_att_norm_g, v_w_out=v_w_out, v_norm_mlp_g=v_norm_mlp_g, v_w_up=v_w_up, v_w_down=v_w_down, v_final_norm_g=v_final_norm_g)
    weights = {n: given[n] for n in TWIN_WEIGHTS}
    shared = {n: given[n] for n in SHARED_INPUTS}
    per_example = {n: given[n] for n in ['x']}
    grad_fn = _jax.value_and_grad(_loss, argnums=(0, 1))

    def one_microbatch(ex, loss_target):
        ex = dict(ex)
        diff = ex.pop(TWIN_DIFF_INPUT)
        return grad_fn(weights, diff, {**shared, **ex}, loss_target)

    if N_MICROBATCH == 1:
        loss, (grad_w, grad_x) = one_microbatch(per_example, given["loss_target"])
    else:
        def body(carry, xs):
            loss_sum, grad_sum = carry
            l_k, (gw_k, gx_k) = one_microbatch(xs[0], xs[1])
            with _jax.named_scope("update"):
                return (loss_sum + l_k, _jax.tree.map(_jnp.add, grad_sum, gw_k)), gx_k

        init = (_jnp.zeros((), _jnp.float32), _jax.tree.map(_jnp.zeros_like, weights))
        (loss, grad_w), grad_x = _jax.lax.scan(body, init, (per_example, given["loss_target"]))
    with _jax.named_scope("update"):
        delta_w, new_m, new_v = {}, {}, {}
        for n in TWIN_WEIGHTS:
            delta_w[n], new_m[n], new_v[n] = _adamw(weights[n], grad_w[n], given["m_" + n], given["v_" + n])
    return (loss, grad_x, *[grad_w[n] for n in TWIN_WEIGHTS], *[delta_w[n] for n in TWIN_WEIGHTS],
            *[new_m[n] for n in TWIN_WEIGHTS], *[new_v[n] for n in TWIN_WEIGHTS])
```

```python
import functools
import math

import jax
import jax.numpy as jnp
from jax import lax
from jax.experimental import pallas as pl
from jax.experimental.pallas import tpu as pltpu

F32 = jnp.float32
BF16 = jnp.bfloat16
HIGHEST = lax.Precision.HIGHEST

D_MODEL = 1024
DEPTH = 4
HEAD_DIM = 64
N_HEADS = 16
N_GROUPS = 4
N_STATE = 128
N_PAIRS = N_HEADS // 2
CONV_WIDTH = 4
CHUNK = 128
D_CONV = D_MODEL + 2 * N_GROUPS * N_STATE
D_MAIN = D_MODEL + D_CONV + 3 * D_MODEL
D_IN_PROJ = D_MAIN + N_HEADS
D_FF = 4 * D_MODEL
EPS = 1e-5
LANES = 128
VMEM_LIMIT = 48 * 1024 * 1024

ADAM_LR = 0.001
ADAM_B1 = 0.9
ADAM_B2 = 0.999
ADAM_EPS = 1e-08
ADAM_WD = 0.01
ADAM_STEP = 10

N_CHIPS = 4
MESH = pl.DeviceIdType.MESH


def _tile(n, cap):
    if n <= cap:
        return n
    t = cap
    while t >= 8:
        if n % t == 0:
            return t
        t //= 2
    raise ValueError(f"no tile for {n} under {cap}")


def _params(*sem):
    return pltpu.CompilerParams(dimension_semantics=sem, vmem_limit_bytes=VMEM_LIMIT)


_DIMS = {"nn": (((1,), (0,)), ((), ())), "nt": (((1,), (1,)), ((), ())), "tn": (((0,), (0,)), ((), ()))}


def _matmul(a, b, mode, out_dtypes, name, extras=(), epilogue=None, tm=512, tn=1024, tk=512):
    if mode == "nn":
        (m, k), (_, n) = a.shape, b.shape
    elif mode == "nt":
        (m, k), (n, _) = a.shape, b.shape
    else:
        (k, m), (_, n) = a.shape, b.shape
    tm, tn, tk = _tile(m, tm), _tile(n, tn), _tile(k, tk)
    nk = k // tk
    if mode == "tn":
        a_spec = pl.BlockSpec((tk, tm), lambda i, j, kk: (kk, i))
    else:
        a_spec = pl.BlockSpec((tm, tk), lambda i, j, kk: (i, kk))
    if mode == "nt":
        b_spec = pl.BlockSpec((tn, tk), lambda i, j, kk: (j, kk))
    else:
        b_spec = pl.BlockSpec((tk, tn), lambda i, j, kk: (kk, j))
    mn_spec = pl.BlockSpec((tm, tn), lambda i, j, kk: (i, j))
    n_extra, n_out = len(extras), len(out_dtypes)
    dims = _DIMS[mode]

    def body(a_ref, b_ref, *rest):
        extra_refs, out_refs, acc = rest[:n_extra], rest[n_extra:n_extra + n_out], rest[-1]
        kk = pl.program_id(2)

        @pl.when(kk == 0)
        def _():
            acc[...] = jnp.zeros_like(acc)

        acc[...] += lax.dot_general(a_ref[...].astype(BF16), b_ref[...].astype(BF16), dims,
                                    preferred_element_type=F32)

        @pl.when(kk == nk - 1)
        def _():
            res = acc[...]
            outs = epilogue(res, *[e[...] for e in extra_refs]) if epilogue is not None else (res,)
            for o_ref, val in zip(out_refs, outs):
                o_ref[...] = val.astype(o_ref.dtype)

    outs = pl.pallas_call(
        body, name=name, grid=(m // tm, n // tn, nk),
        in_specs=[a_spec, b_spec] + [mn_spec] * n_extra,
        out_specs=[mn_spec] * n_out,
        out_shape=[jax.ShapeDtypeStruct((m, n), dt) for dt in out_dtypes],
        scratch_shapes=[pltpu.VMEM((tm, tn), F32)],
        compiler_params=_params("parallel", "parallel", "arbitrary"),
    )(a, b, *extras)
    return tuple(outs)


def _rowwise(fn, rows, vecs, out_rows, out_accs, name, tm=256):
    t = rows[0].shape[0]
    tm = _tile(t, tm)
    n_rows, n_vecs, n_or, n_oa = len(rows), len(vecs), len(out_rows), len(out_accs)

    def body(*refs):
        ins = [r[...] for r in refs[:n_rows + n_vecs]]
        outs = fn(*ins)
        o_refs = refs[n_rows + n_vecs:]
        for o_ref, val in zip(o_refs[:n_or], outs[:n_or]):
            o_ref[...] = val.astype(o_ref.dtype)

        @pl.when(pl.program_id(0) == 0)
        def _():
            for o_ref in o_refs[n_or:]:
                o_ref[...] = jnp.zeros_like(o_ref)

        for o_ref, val in zip(o_refs[n_or:], outs[n_or:]):
            o_ref[...] += val

    outs = pl.pallas_call(
        body, name=name, grid=(t // tm,),
        in_specs=[pl.BlockSpec((tm, r.shape[1]), lambda i: (i, 0)) for r in rows]
        + [pl.BlockSpec(v.shape, lambda i: (0, 0)) for v in vecs],
        out_specs=[pl.BlockSpec((tm, o.shape[1]), lambda i: (i, 0)) for o in out_rows]
        + [pl.BlockSpec(o.shape, lambda i: (0, 0)) for o in out_accs],
        out_shape=list(out_rows) + list(out_accs),
        compiler_params=_params("arbitrary" if n_oa else "parallel"),
    )(*rows, *vecs)
    return tuple(outs)


def _stack_rows(parts, width):
    rows = lax.broadcasted_iota(jnp.int32, (8, width), 0)
    out = jnp.zeros((8, width), F32)
    for j, part in enumerate(parts):
        out = out + jnp.where(rows == j, part, 0.0)
    return out


def _sds(shape, dtype=F32):
    return jax.ShapeDtypeStruct(shape, dtype)


def _sigmoid(x):
    return 1.0 / (1.0 + jnp.exp(-x))


def _colsum(x):
    return jnp.sum(x, axis=0, keepdims=True)


def _rstd(x):
    return lax.rsqrt(jnp.mean(x * x, axis=-1, keepdims=True) + EPS)


def _rms_bwd_rows(xhat, r, g, dy):
    dxh = dy * g
    dx = r * (dxh - xhat * jnp.mean(dxh * xhat, axis=-1, keepdims=True))
    return dx, _colsum(dy * xhat)


def _rms_fwd(x, g, name):
    t, d = x.shape

    def fn(xb, gv):
        return (xb * _rstd(xb) * gv,)

    return _rowwise(fn, [x], [g], [_sds((t, d), BF16)], [], name)[0]


def _rms_bwd(x, g, dh, dres, name):
    t, d = x.shape

    def fn(xb, dhb, dresb, gv):
        r = _rstd(xb)
        dx, dg = _rms_bwd_rows(xb * r, r, gv, dhb)
        dx = dx + dresb
        return dx, dx, dg

    return _rowwise(fn, [x, dh, dres], [g], [_sds((t, d)), _sds((t, d), BF16)], [_sds((1, d))], name)


def _mixnorm_fwd(y, z, att, gs, ga, name):
    t, d = y.shape

    def fn(yb, zb, ab, gsv, gav):
        yg = yb * (zb * _sigmoid(zb))
        return (jnp.concatenate([yg * _rstd(yg) * gsv, ab * _rstd(ab) * gav], axis=1),)

    tm = _tile(t, 256)

    def body(y_ref, z_ref, a_ref, gs_ref, ga_ref, o_ref):
        o_ref[...] = fn(y_ref[...], z_ref[...], a_ref[...], gs_ref[...], ga_ref[...])[0].astype(BF16)

    row = pl.BlockSpec((tm, d), lambda i: (i, 0))
    vec = pl.BlockSpec((1, d), lambda i: (0, 0))
    out = pl.pallas_call(
        body, name=name, grid=(t // tm,),
        in_specs=[row, row, row, vec, vec],
        out_specs=pl.BlockSpec((tm, 2 * d), lambda i: (i, 0)),
        out_shape=_sds((t, 2 * d), BF16),
        compiler_params=_params("parallel"),
    )(y, z, att, gs, ga)
    return out


def _mixnorm_bwd(y, z, att, gs, ga, dycat, name):
    t, d = y.shape
    tm = _tile(t, 256)

    def body(y_ref, z_ref, a_ref, dyc_ref, gs_ref, ga_ref, dy_ref, dz_ref, da_ref, dgs_ref, dga_ref):
        yb, zb, ab = y_ref[...], z_ref[...], a_ref[...]
        dys, dya = dyc_ref[:, :d], dyc_ref[:, d:]
        sz = _sigmoid(zb)
        silu = zb * sz
        yg = yb * silu
        r = _rstd(yg)
        dyg, dgs = _rms_bwd_rows(yg * r, r, gs_ref[...], dys)
        dy_ref[...] = dyg * silu
        dz_ref[...] = dyg * yb * (sz * (1.0 + zb * (1.0 - sz)))
        r2 = _rstd(ab)
        datt, dga = _rms_bwd_rows(ab * r2, r2, ga_ref[...], dya)
        da_ref[...] = datt

        @pl.when(pl.program_id(0) == 0)
        def _():
            dgs_ref[...] = jnp.zeros_like(dgs_ref)
            dga_ref[...] = jnp.zeros_like(dga_ref)

        dgs_ref[...] += dgs
        dga_ref[...] += dga

    row = pl.BlockSpec((tm, d), lambda i: (i, 0))
    vec = pl.BlockSpec((1, d), lambda i: (0, 0))
    return pl.pallas_call(
        body, name=name, grid=(t // tm,),
        in_specs=[row, row, row, pl.BlockSpec((tm, 2 * d), lambda i: (i, 0)), vec, vec],
        out_specs=[row, row, row, vec, vec],
        out_shape=[_sds((t, d)), _sds((t, d)), _sds((t, d)), _sds((1, d)), _sds((1, d))],
        compiler_params=_params("arbitrary"),
    )(y, z, att, dycat, gs, ga)


def _final_loss(x, g, target, name):
    t, d = x.shape

    def fn(xb, tb, gv):
        r = _rstd(xb)
        xhat = xb * r
        err = xhat * gv - tb
        loss = 0.5 * jnp.sum(jnp.mean(err * err, axis=-1, keepdims=True), axis=0, keepdims=True)
        dx, dg = _rms_bwd_rows(xhat, r, gv, err * (1.0 / d))
        return dx, dx, jnp.broadcast_to(loss, (1, LANES)), dg

    dx, dxb, loss, dg = _rowwise(fn, [x, target], [g], [_sds((t, d)), _sds((t, d), BF16)],
                                 [_sds((1, LANES)), _sds((1, d))], name)
    return loss, dx, dxb, dg


def _adamw(w, g, m, v, name):
    c1 = 1.0 - ADAM_B1 ** ADAM_STEP
    c2 = 1.0 - ADAM_B2 ** ADAM_STEP

    def fn(wb, gb, mb, vb):
        mn = ADAM_B1 * mb + (1.0 - ADAM_B1) * gb
        vn = ADAM_B2 * vb + (1.0 - ADAM_B2) * (gb * gb)
        delta = -ADAM_LR * ((mn / c1) / (jnp.sqrt(vn / c2) + ADAM_EPS) + ADAM_WD * wb)
        return delta, mn, vn

    return _rowwise(fn, [w, g, m, v], [], [_sds(w.shape)] * 3, [], name)


CONV_CB = 512


def _shift_down(u, k):
    if k == 0:
        return u
    rows = lax.broadcasted_iota(jnp.int32, u.shape, 0)
    return jnp.where(rows >= k, pltpu.roll(u, k, 0), 0.0)


def _shift_up(u, k):
    if k == 0:
        return u
    n = u.shape[0]
    rows = lax.broadcasted_iota(jnp.int32, u.shape, 0)
    return jnp.where(rows < n - k, pltpu.roll(u, n - k, 0), 0.0)


def _conv_pre(u, w, b):
    pre = b
    for j in range(CONV_WIDTH):
        pre = pre + w[j:j + 1, :] * _shift_down(u, CONV_WIDTH - 1 - j)
    return pre


def _conv_fwd(proj, w, b, n_seq, name):
    t = proj.shape[0]
    seq = t // n_seq
    off = D_MODEL // CONV_CB

    def body(u_ref, w_ref, b_ref, o_ref):
        pre = _conv_pre(u_ref[...], w_ref[...], b_ref[...])
        o_ref[...] = pre * _sigmoid(pre)

    return pl.pallas_call(
        body, name=name, grid=(n_seq, D_CONV // CONV_CB),
        in_specs=[pl.BlockSpec((seq, CONV_CB), lambda s, c: (s, c + off)),
                  pl.BlockSpec((CONV_WIDTH, CONV_CB), lambda s, c: (0, c)),
                  pl.BlockSpec((1, CONV_CB), lambda s, c: (0, c))],
        out_specs=pl.BlockSpec((seq, CONV_CB), lambda s, c: (s, c)),
        out_shape=_sds((t, D_CONV)),
        compiler_params=_params("parallel", "parallel"),
    )(proj, w, b)


def _conv_bwd(proj, w, b, dxbc, n_seq, name):
    t = proj.shape[0]
    seq = t // n_seq
    off = D_MODEL // CONV_CB

    def body(u_ref, w_ref, b_ref, d_ref, du_ref, wg_ref):
        u, wv = u_ref[...], w_ref[...]
        pre = _conv_pre(u, wv, b_ref[...])
        s = _sigmoid(pre)
        dpre = d_ref[...] * (s * (1.0 + pre * (1.0 - s)))
        du = jnp.zeros_like(u)
        parts = []
        for j in range(CONV_WIDTH):
            k = CONV_WIDTH - 1 - j
            du = du + wv[j:j + 1, :] * _shift_up(dpre, k)
            parts.append(_colsum(dpre * _shift_down(u, k)))
        du_ref[...] = du
        parts.append(_colsum(dpre))

        @pl.when(pl.program_id(1) == 0)
        def _():
            wg_ref[...] = jnp.zeros_like(wg_ref)

        wg_ref[...] += _stack_rows(parts, u.shape[1])

    return pl.pallas_call(
        body, name=name, grid=(D_CONV // CONV_CB, n_seq),
        in_specs=[pl.BlockSpec((seq, CONV_CB), lambda c, s: (s, c + off)),
                  pl.BlockSpec((CONV_WIDTH, CONV_CB), lambda c, s: (0, c)),
                  pl.BlockSpec((1, CONV_CB), lambda c, s: (0, c)),
                  pl.BlockSpec((seq, CONV_CB), lambda c, s: (s, c))],
        out_specs=[pl.BlockSpec((seq, CONV_CB), lambda c, s: (s, c)),
                   pl.BlockSpec((8, CONV_CB), lambda c, s: (0, c))],
        out_shape=[_sds((t, D_CONV)), _sds((8, D_CONV))],
        compiler_params=_params("parallel", "arbitrary"),
    )(proj, w, b, dxbc)


def _iota2(shape, axis):
    return lax.broadcasted_iota(jnp.int32, shape, axis)


def _dot(a, b, dims=_DIMS["nn"], precision=None):
    return lax.dot_general(a, b, dims, precision=precision, preferred_element_type=F32)


def _bdot(a, b, dims=_DIMS["nn"]):
    return lax.dot_general(a.astype(BF16), b.astype(BF16), dims, preferred_element_type=F32)


def _expand_mat():
    return (_iota2((LANES, D_MODEL), 0) == lax.shift_right_logical(_iota2((LANES, D_MODEL), 1), 6)).astype(F32)


def _reduce_mat():
    return (lax.shift_right_logical(_iota2((D_MODEL, LANES), 0), 6) == _iota2((D_MODEL, LANES), 1)).astype(F32)


def _ssd_decay(dtraw, bias, alog):
    row, col = _iota2((CHUNK, CHUNK), 0), _iota2((CHUNK, CHUNK), 1)
    pre = dtraw + bias
    dtb = jnp.maximum(pre, 0.0) + jnp.log(1.0 + jnp.exp(-jnp.abs(pre)))
    a_neg = -jnp.exp(alog)
    a = dtb * a_neg
    tril = (row >= col).astype(F32)
    triu = (row <= col).astype(F32)
    cs = _dot(tril, a, precision=HIGHEST)
    cs_t = _dot(a, triu, _DIMS["tn"], precision=HIGHEST)
    return pre, dtb, a_neg, cs, cs_t


def _pair_rowscale(vec, h0):
    top = _iota2((CHUNK, LANES), 0) < HEAD_DIM
    return jnp.where(top, vec[:, h0:h0 + 1], vec[:, h0 + 1:h0 + 2])


def _decay_mat(cs, cs_t, h):
    row, col = _iota2((CHUNK, CHUNK), 0), _iota2((CHUNK, CHUNK), 1)
    seg = cs[:, h:h + 1] - cs_t[h:h + 1, :]
    return jnp.exp(jnp.where(row >= col, seg, -jnp.inf))


def _ssd_fwd(xbc, dtraw, bias, alog, dskip_x, n_seq, name):
    t = xbc.shape[0]
    nc = t // n_seq // CHUNK

    def body(x_ref, b_ref, c_ref, dt_ref, bias_ref, alog_ref, dsk_ref, y_ref, st_ref, h_ref):
        @pl.when(pl.program_id(1) == 0)
        def _():
            h_ref[...] = jnp.zeros_like(h_ref)

        _, dtb, _, cs, cs_t = _ssd_decay(dt_ref[...], bias_ref[...], alog_ref[...])
        expand = _expand_mat()
        dt_x = _dot(dtb, expand, precision=HIGHEST)
        cs_x = _dot(cs, expand, precision=HIGHEST)
        tot = cs[CHUNK - 1:CHUNK, :]
        etot = jnp.exp(tot)
        x = x_ref[...]
        xdt = x * dt_x
        e_x = jnp.exp(cs_x)
        xdec = xdt * jnp.exp(cs_x[CHUNK - 1:CHUNK, :] - cs_x)
        keeps = [_iota2((CHUNK, LANES), 1) < HEAD_DIM, _iota2((CHUNK, LANES), 1) >= HEAD_DIM]
        for q in range(N_PAIRS):
            g = q // 2
            lanes = slice(q * LANES, (q + 1) * LANES)
            bg = b_ref[:, g * N_STATE:(g + 1) * N_STATE]
            cg = c_ref[:, g * N_STATE:(g + 1) * N_STATE]
            gmat = _bdot(cg, bg, _DIMS["nt"])
            xdt_q = xdt[:, lanes]
            ydiag = jnp.zeros((CHUNK, LANES), F32)
            for r in range(2):
                w = gmat * _decay_mat(cs, cs_t, 2 * q + r)
                ydiag = ydiag + _bdot(w, jnp.where(keeps[r], xdt_q, 0.0))
            prev = h_ref[q]
            yoff = _bdot(cg, prev, _DIMS["nt"]) * e_x[:, lanes]
            y_ref[:, lanes] = ydiag + yoff + x[:, lanes] * dsk_ref[:, lanes]
            st_ref[0, q] = prev
            h_ref[q] = prev * _pair_rowscale(etot, 2 * q) + _bdot(xdec[:, lanes], bg, _DIMS["tn"])

    vec = pl.BlockSpec((1, LANES), lambda s, c: (0, 0))
    return pl.pallas_call(
        body, name=name, grid=(n_seq, nc),
        in_specs=[pl.BlockSpec((CHUNK, D_MODEL), lambda s, c: (s * nc + c, 0)),
                  pl.BlockSpec((CHUNK, N_GROUPS * N_STATE), lambda s, c: (s * nc + c, 2)),
                  pl.BlockSpec((CHUNK, N_GROUPS * N_STATE), lambda s, c: (s * nc + c, 3)),
                  pl.BlockSpec((CHUNK, LANES), lambda s, c: (s * nc + c, 0)),
                  vec, vec, pl.BlockSpec((1, D_MODEL), lambda s, c: (0, 0))],
        out_specs=[pl.BlockSpec((CHUNK, D_MODEL), lambda s, c: (s * nc + c, 0)),
                   pl.BlockSpec((1, N_PAIRS, LANES, N_STATE), lambda s, c: (s * nc + c, 0, 0, 0))],
        out_shape=[_sds((t, D_MODEL)), _sds((t // CHUNK, N_PAIRS, LANES, N_STATE))],
        scratch_shapes=[pltpu.VMEM((N_PAIRS, LANES, N_STATE), F32)],
        compiler_params=_params("parallel", "arbitrary"),
    )(xbc, xbc, xbc, dtraw, bias, alog, dskip_x)


def _ssd_bwd(xbc, dtraw, bias, alog, dskip_x, states, dy, n_seq, name):
    t = xbc.shape[0]
    nc = t // n_seq // CHUNK

    def body(x_ref, b_ref, c_ref, dt_ref, bias_ref, alog_ref, dsk_ref, st_ref, dy_ref,
             dxbc_ref, ddt_ref, pg_ref, dh_ref):
        first = jnp.logical_and(pl.program_id(0) == 0, pl.program_id(1) == 0)

        @pl.when(first)
        def _():
            pg_ref[...] = jnp.zeros_like(pg_ref)

        @pl.when(pl.program_id(1) == 0)
        def _():
            dh_ref[...] = jnp.zeros_like(dh_ref)

        row, col = _iota2((CHUNK, CHUNK), 0), _iota2((CHUNK, CHUNK), 1)
        pre, dtb, a_neg, cs, cs_t = _ssd_decay(dt_ref[...], bias_ref[...], alog_ref[...])
        expand, reduce = _expand_mat(), _reduce_mat()
        dt_x = _dot(dtb, expand, precision=HIGHEST)
        cs_x = _dot(cs, expand, precision=HIGHEST)
        etot = jnp.exp(cs[CHUNK - 1:CHUNK, :])
        x, dy = x_ref[...], dy_ref[...]
        xdt = x * dt_x
        e_x = jnp.exp(cs_x)
        dec_x = jnp.exp(cs_x[CHUNK - 1:CHUNK, :] - cs_x)
        xdec = xdt * dec_x
        dye = dy * e_x
        keeps = [_iota2((CHUNK, LANES), 1) < HEAD_DIM, _iota2((CHUNK, LANES), 1) >= HEAD_DIM]
        row_lo = _iota2((CHUNK, LANES), 0) < HEAD_DIM
        dcs_col = jnp.zeros((CHUNK, LANES), F32)
        dcs_row = jnp.zeros((LANES, CHUNK), F32)
        dtot = jnp.zeros((1, LANES), F32)
        dxdt_parts, zdec_parts, yoff_parts = [], [], []
        for g in range(N_GROUPS):
            bg = b_ref[:, g * N_STATE:(g + 1) * N_STATE]
            cg = c_ref[:, g * N_STATE:(g + 1) * N_STATE]
            gmat = _bdot(cg, bg, _DIMS["nt"])
            dgmat = jnp.zeros((CHUNK, CHUNK), F32)
            dbg = jnp.zeros((CHUNK, N_STATE), F32)
            dcg = jnp.zeros((CHUNK, N_STATE), F32)
            for q in (2 * g, 2 * g + 1):
                lanes = slice(q * LANES, (q + 1) * LANES)
                xdt_q, dy_q = xdt[:, lanes], dy[:, lanes]
                dxdt_q = jnp.zeros((CHUNK, LANES), F32)
                for r in range(2):
                    h = 2 * q + r
                    keep = keeps[r]
                    lm = _decay_mat(cs, cs_t, h)
                    w = gmat * lm
                    dy_h = jnp.where(keep, dy_q, 0.0)
                    dm = _bdot(dy_h, xdt_q, _DIMS["nt"])
                    dxdt_q = dxdt_q + _bdot(w, dy_h, _DIMS["tn"])
                    dgmat = dgmat + dm * lm
                    tmat = dm * w
                    dcs_col = dcs_col + jnp.where(col == h, jnp.sum(tmat, axis=1, keepdims=True), 0.0)
                    dcs_row = dcs_row - jnp.where(row == h, jnp.sum(tmat, axis=0, keepdims=True), 0.0)
                prev = st_ref[0, q]
                dht = dh_ref[q]
                dxdtdec = _bdot(bg, dht, _DIMS["nt"])
                dxdt_q = dxdt_q + dxdtdec * dec_x[:, lanes]
                zdec_parts.append(dxdtdec * xdec[:, lanes])
                dbg = dbg + _bdot(xdec[:, lanes], dht)
                yoff_parts.append(dy_q * (_bdot(cg, prev, _DIMS["nt"]) * e_x[:, lanes]))
                dcg = dcg + _bdot(dye[:, lanes], prev)
                dprev = _bdot(dye[:, lanes], cg, _DIMS["tn"])
                hp = jnp.sum(dht * prev, axis=1, keepdims=True)
                d0 = jnp.sum(jnp.where(row_lo[:, :1], hp, 0.0), axis=0, keepdims=True)
                d1 = jnp.sum(jnp.where(row_lo[:, :1], 0.0, hp), axis=0, keepdims=True)
                lane1 = _iota2((1, LANES), 1)
                dtot = dtot + jnp.where(lane1 == 2 * q, d0, 0.0) + jnp.where(lane1 == 2 * q + 1, d1, 0.0)
                dh_ref[q] = dprev + dht * _pair_rowscale(etot, 2 * q)
                dxdt_parts.append(dxdt_q)
            dcg = dcg + _bdot(dgmat, bg)
            dbg = dbg + _bdot(dgmat, cg, _DIMS["tn"])
            dxbc_ref[:, D_MODEL + g * N_STATE:D_MODEL + (g + 1) * N_STATE] = dbg
            dxbc_ref[:, D_MODEL + (N_GROUPS + g) * N_STATE:D_MODEL + (N_GROUPS + g + 1) * N_STATE] = dcg
        dxdt = jnp.concatenate(dxdt_parts, axis=1)
        zdec = _dot(jnp.concatenate(zdec_parts, axis=1), reduce, precision=HIGHEST)
        yoff_d = _dot(jnp.concatenate(yoff_parts, axis=1), reduce, precision=HIGHEST)
        dtot = dtot * etot + _colsum(zdec)
        last = row[:, :LANES] == CHUNK - 1
        dcs_col = dcs_col + yoff_d - zdec + jnp.where(last, dtot, 0.0)
        triu = (row <= col).astype(F32)
        da = _dot(triu, dcs_col, precision=HIGHEST) + _dot(triu, dcs_row, _DIMS["nt"], precision=HIGHEST)
        ddt = _dot(dxdt * x, reduce, precision=HIGHEST) + da * a_neg
        ddtraw = ddt * _sigmoid(pre)
        ddt_ref[...] = ddtraw
        dxbc_ref[:, :D_MODEL] = dxdt * dt_x + dy * dsk_ref[...]
        dskip = _dot(jnp.broadcast_to(_colsum(dy * x), (8, D_MODEL)), reduce, precision=HIGHEST)[0:1, :]
        pg_ref[...] += _stack_rows([_colsum(ddtraw), _colsum(da * dtb) * a_neg, dskip], LANES)

    vec = pl.BlockSpec((1, LANES), lambda s, c: (0, 0))

    def blk(s, c):
        return s * nc + (nc - 1 - c)

    return pl.pallas_call(
        body, name=name, grid=(n_seq, nc),
        in_specs=[pl.BlockSpec((CHUNK, D_MODEL), lambda s, c: (blk(s, c), 0)),
                  pl.BlockSpec((CHUNK, N_GROUPS * N_STATE), lambda s, c: (blk(s, c), 2)),
                  pl.BlockSpec((CHUNK, N_GROUPS * N_STATE), lambda s, c: (blk(s, c), 3)),
                  pl.BlockSpec((CHUNK, LANES), lambda s, c: (blk(s, c), 0)),
                  vec, vec, pl.BlockSpec((1, D_MODEL), lambda s, c: (0, 0)),
                  pl.BlockSpec((1, N_PAIRS, LANES, N_STATE), lambda s, c: (blk(s, c), 0, 0, 0)),
                  pl.BlockSpec((CHUNK, D_MODEL), lambda s, c: (blk(s, c), 0))],
        out_specs=[pl.BlockSpec((CHUNK, D_CONV), lambda s, c: (blk(s, c), 0)),
                   pl.BlockSpec((CHUNK, LANES), lambda s, c: (blk(s, c), 0)),
                   pl.BlockSpec((8, LANES), lambda s, c: (0, 0))],
        out_shape=[_sds((t, D_CONV)), _sds((t, LANES)), _sds((8, LANES))],
        scratch_shapes=[pltpu.VMEM((N_PAIRS, LANES, N_STATE), F32)],
        compiler_params=_params("arbitrary", "arbitrary"),
    )(xbc, xbc, xbc, dtraw, bias, alog, dskip_x, states, dy)


Q_COL = (D_MODEL + D_CONV) // LANES
K_COL = Q_COL + D_MODEL // LANES
V_COL = K_COL + D_MODEL // LANES
ATT_SCALE = HEAD_DIM ** -0.5


def _split3(x):
    hi = x.astype(BF16)
    r1 = x - hi.astype(F32)
    mid = r1.astype(BF16)
    lo = (r1 - mid.astype(F32)).astype(BF16)
    return hi, mid, lo


def _tri_dot(x, tri):
    hi, mid, lo = _split3(x)
    return (_dot(hi, tri) + _dot(mid, tri)) + _dot(lo, tri)


def _att_block(q_h, k_blk, q0, k0):
    row, col = _iota2((CHUNK, CHUNK), 0), _iota2((CHUNK, CHUNK), 1)
    s = _bdot(q_h, k_blk, _DIMS["nt"]) * ATT_SCALE
    mask = (k0 + col) < (q0 + row)
    lb = jnp.minimum(s, 0.0) - jnp.log(1.0 + jnp.exp(-jnp.abs(s)))
    l1m = jnp.where(mask, lb - s, 0.0)
    return s, lb, l1m, mask


def _att_fwd(proj, n_seq, name):
    t = proj.shape[0]
    seq = t // n_seq
    nq = seq // CHUNK

    def body(q_ref, k_ref, v_ref, o_ref, rt_ref):
        i = pl.program_id(2)
        row, col = _iota2((CHUNK, CHUNK), 0), _iota2((CHUNK, CHUNK), 1)
        upper = (row > col).astype(BF16)
        keeps = [_iota2((CHUNK, LANES), 1) < HEAD_DIM, _iota2((CHUNK, LANES), 1) >= HEAD_DIM]
        qp = q_ref[...]
        q_heads = [jnp.where(keeps[r], qp, 0.0).astype(BF16) for r in range(2)]

        def step(it, carry):
            acc, run0, run1 = carry
            jj = i - it
            k0 = pl.multiple_of(jj * CHUNK, CHUNK)
            k_blk = k_ref[pl.ds(k0, CHUNK), :].astype(BF16)
            v_blk = v_ref[pl.ds(k0, CHUNK), :]
            runs = [run0, run1]
            for r in range(2):
                _, lb, l1m, mask = _att_block(q_heads[r], k_blk, i * CHUNK, k0)
                suffix = _tri_dot(l1m, upper) + runs[r]
                p = jnp.where(mask, jnp.exp(lb + suffix), 0.0)
                acc = acc + _bdot(p, jnp.where(keeps[r], v_blk, 0.0))
                runs[r] = runs[r] + jnp.sum(l1m, axis=1, keepdims=True)
            return acc, runs[0], runs[1]

        zero = jnp.zeros((CHUNK, 1), F32)
        acc, run0, run1 = lax.fori_loop(0, i + 1, step, (jnp.zeros((CHUNK, LANES), F32), zero, zero))
        o_ref[...] = acc
        rt_ref[...] = jnp.where(keeps[0], run0, run1)

    qblk = pl.BlockSpec((CHUNK, LANES), lambda s, p, i: (s * nq + i, p))
    return pl.pallas_call(
        body, name=name, grid=(n_seq, N_PAIRS, nq),
        in_specs=[pl.BlockSpec((CHUNK, LANES), lambda s, p, i: (s * nq + i, Q_COL + p)),
                  pl.BlockSpec((seq, LANES), lambda s, p, i: (s, K_COL + p)),
                  pl.BlockSpec((seq, LANES), lambda s, p, i: (s, V_COL + p))],
        out_specs=[qblk, qblk],
        out_shape=[_sds((t, D_MODEL)), _sds((t, D_MODEL))],
        compiler_params=_params("parallel", "parallel", "arbitrary"),
    )(proj, proj, proj)


def _att_bwd(proj, rtot, datt, n_seq, name):
    t = proj.shape[0]
    seq = t // n_seq
    nq = seq // CHUNK

    def body(q_ref, k_ref, v_ref, rt_ref, do_ref, dq_ref, dk_ref, dv_ref):
        i = pl.program_id(2)

        @pl.when(i == 0)
        def _():
            dk_ref[...] = jnp.zeros_like(dk_ref)
            dv_ref[...] = jnp.zeros_like(dv_ref)

        row, col = _iota2((CHUNK, CHUNK), 0), _iota2((CHUNK, CHUNK), 1)
        upper = (row > col).astype(BF16)
        before = (row < col).astype(BF16)
        keeps = [_iota2((CHUNK, LANES), 1) < HEAD_DIM, _iota2((CHUNK, LANES), 1) >= HEAD_DIM]
        qp, do, rt = q_ref[...], do_ref[...], rt_ref[...]
        q_heads = [jnp.where(keeps[r], qp, 0.0).astype(BF16) for r in range(2)]
        do_heads = [jnp.where(keeps[r], do, 0.0).astype(BF16) for r in range(2)]
        totals = [rt[:, 0:1], rt[:, HEAD_DIM:HEAD_DIM + 1]]

        def step(jj, carry):
            dq, seen0, seen1, dseen0, dseen1 = carry
            k0 = pl.multiple_of(jj * CHUNK, CHUNK)
            k_f = k_ref[pl.ds(k0, CHUNK), :]
            k_blk = k_f.astype(BF16)
            v_blk = v_ref[pl.ds(k0, CHUNK), :].astype(BF16)
            seen, dseen = [seen0, seen1], [dseen0, dseen1]
            dk_blk = jnp.zeros((CHUNK, LANES), F32)
            dv_blk = jnp.zeros((CHUNK, LANES), F32)
            for r in range(2):
                s, lb, l1m, mask = _att_block(q_heads[r], k_blk, i * CHUNK, k0)
                here = jnp.sum(l1m, axis=1, keepdims=True)
                suffix = _tri_dot(l1m, upper) + (totals[r] - seen[r] - here)
                p = jnp.where(mask, jnp.exp(lb + suffix), 0.0)
                dp = _dot(do_heads[r], v_blk, _DIMS["nt"])
                dz = dp * p
                dv_blk = dv_blk + _dot(p.astype(BF16), do_heads[r], _DIMS["tn"])
                dl1m = dseen[r] + _tri_dot(dz, before)
                sig = _sigmoid(s)
                ds = jnp.where(mask, dz * (1.0 - sig) - dl1m * sig, 0.0) * ATT_SCALE
                dsb = ds.astype(BF16)
                dq = dq + _dot(dsb, jnp.where(keeps[r], k_f, 0.0).astype(BF16))
                dk_blk = dk_blk + _dot(dsb, q_heads[r], _DIMS["tn"])
                seen[r] = seen[r] + here
                dseen[r] = dseen[r] + jnp.sum(dz, axis=1, keepdims=True)
            dk_ref[pl.ds(k0, CHUNK), :] += dk_blk
            dv_ref[pl.ds(k0, CHUNK), :] += dv_blk
            return dq, seen[0], seen[1], dseen[0], dseen[1]

        zero = jnp.zeros((CHUNK, 1), F32)
        dq, _, _, _, _ = lax.fori_loop(0, i + 1, step, (jnp.zeros((CHUNK, LANES), F32), zero, zero, zero, zero))
        dq_ref[...] = dq

    qblk = pl.BlockSpec((CHUNK, LANES), lambda s, p, i: (s * nq + i, p))
    kv_out = pl.BlockSpec((seq, LANES), lambda s, p, i: (s, p))
    return pl.pallas_call(
        body, name=name, grid=(n_seq, N_PAIRS, nq),
        in_specs=[pl.BlockSpec((CHUNK, LANES), lambda s, p, i: (s * nq + i, Q_COL + p)),
                  pl.BlockSpec((seq, LANES), lambda s, p, i: (s, K_COL + p)),
                  pl.BlockSpec((seq, LANES), lambda s, p, i: (s, V_COL + p)),
                  qblk, qblk],
        out_specs=[qblk, kv_out, kv_out],
        out_shape=[_sds((t, D_MODEL))] * 3,
        compiler_params=_params("parallel", "parallel", "arbitrary"),
    )(proj, proj, proj, rtot, datt)


def _pad_lanes(v):
    return jnp.pad(v.reshape(1, -1), ((0, 0), (0, LANES - v.shape[0])))


def _layer_fwd(x, p, n_seq, tag):
    h = _rms_fwd(x, p["norm_mix_g"], f"rms_mix_fwd{tag}")
    proj, = _matmul(h, p["w_main"], "nn", [F32], f"in_proj{tag}", tn=1536)
    dtraw, = _matmul(h, p["w_dt"], "nn", [F32], f"dt_proj{tag}")
    xbc = _conv_fwd(proj, p["conv_w"], p["conv_b"], n_seq, f"conv_fwd{tag}")
    y, states = _ssd_fwd(xbc, dtraw, p["dt_bias"], p["a_log"], p["d_skip_x"], n_seq, f"ssd_fwd{tag}")
    att, rtot = _att_fwd(proj, n_seq, f"att_fwd{tag}")
    ycat = _mixnorm_fwd(y, proj, att, p["ssd_norm_g"], p["att_norm_g"], f"mixnorm_fwd{tag}")
    x1, = _matmul(ycat, p["w_out"], "nn", [F32], f"out_proj{tag}", extras=[x], epilogue=lambda acc, xb: (xb + acc,))
    h2 = _rms_fwd(x1, p["norm_mlp_g"], f"rms_mlp_fwd{tag}")
    u, act = _matmul(h2, p["w_up"], "nn", [F32, BF16], f"up_proj{tag}",
                     epilogue=lambda acc: (acc, jnp.square(jnp.maximum(acc, 0.0))))
    x2, = _matmul(act, p["w_down"], "nn", [F32], f"down_proj{tag}", extras=[x1], epilogue=lambda acc, xb: (xb + acc,))
    saved = dict(x=x, h=h, proj=proj, dtraw=dtraw, xbc=xbc, y=y, states=states, att=att, rtot=rtot, ycat=ycat,
                 x1=x1, h2=h2, u=u, act=act)
    return x2, saved


def _layer_bwd(dx2, dx2b, p, s, n_seq, tag):
    g = {}
    g["w_down"], = _matmul(s["act"], dx2b, "tn", [F32], f"dw_down{tag}")
    du, = _matmul(dx2b, p["w_down"], "nt", [BF16], f"d_act{tag}", extras=[s["u"]],
                  epilogue=lambda acc, ub: (acc * (2.0 * jnp.maximum(ub, 0.0)),))
    g["w_up"], = _matmul(s["h2"], du, "tn", [F32], f"dw_up{tag}")
    dh2, = _matmul(du, p["w_up"], "nt", [F32], f"d_h2{tag}")
    dx1, dx1b, g["norm_mlp_g"] = _rms_bwd(s["x1"], p["norm_mlp_g"], dh2, dx2, f"rms_mlp_bwd{tag}")
    g["w_out"], = _matmul(s["ycat"], dx1b, "tn", [F32], f"dw_out{tag}")
    dycat, = _matmul(dx1b, p["w_out"], "nt", [F32], f"d_ycat{tag}")
    dy, dz, datt, g["ssd_norm_g"], g["att_norm_g"] = _mixnorm_bwd(
        s["y"], s["proj"], s["att"], p["ssd_norm_g"], p["att_norm_g"], dycat, f"mixnorm_bwd{tag}")
    dq, dk, dv = _att_bwd(s["proj"], s["rtot"], datt, n_seq, f"att_bwd{tag}")
    dxbc, ddtraw, pg = _ssd_bwd(s["xbc"], s["dtraw"], p["dt_bias"], p["a_log"], p["d_skip_x"], s["states"], dy,
                                n_seq, f"ssd_bwd{tag}")
    g["dt_bias"], g["a_log"], g["d_skip"] = pg[0, :N_HEADS], pg[1, :N_HEADS], pg[2, :N_HEADS]
    du_conv, wg = _conv_bwd(s["proj"], p["conv_w"], p["conv_b"], dxbc, n_seq, f"conv_bwd{tag}")
    g["conv_w"], g["conv_b"] = wg[:CONV_WIDTH], wg[CONV_WIDTH]
    dproj = jnp.concatenate([dz.astype(BF16), du_conv.astype(BF16), dq.astype(BF16), dk.astype(BF16),
                             dv.astype(BF16)], axis=1)
    ddtb = ddtraw.astype(BF16)
    g["w_main"], = _matmul(s["h"], dproj, "tn", [F32], f"dw_in{tag}", tn=1536)
    g["w_dt"], = _matmul(s["h"], ddtb, "tn", [F32], f"dw_dt{tag}")
    dh_dt, = _matmul(ddtb, p["w_dt"], "nt", [F32], f"d_h_dt{tag}")
    dh, = _matmul(dproj, p["w_main"], "nt", [F32], f"d_h{tag}", extras=[dh_dt], epilogue=lambda acc, e: (acc + e,))
    dx, dxb, g["norm_mix_g"] = _rms_bwd(s["x"], p["norm_mix_g"], dh, dx1, f"rms_mix_bwd{tag}")
    return dx, dxb, g


def _split_w_in(w_full):
    c0 = D_MODEL + D_CONV
    main = jnp.concatenate([w_full[:, :c0], w_full[:, c0 + N_HEADS:]], axis=1)
    dt = jnp.pad(w_full[:, c0:c0 + N_HEADS], ((0, 0), (0, LANES - N_HEADS)))
    return main, dt


def _merge_w_in(main, dt):
    c0 = D_MODEL + D_CONV
    return jnp.concatenate([main[:, :c0], dt[:, :N_HEADS], main[:, c0:]], axis=1)


def _prep_layer(l, w_in_full, w_out, w_up, w_down, conv_w, small):
    w_main, w_dt = _split_w_in(w_in_full)
    return dict(
        w_main=w_main, w_dt=w_dt, w_out=w_out, w_up=w_up, w_down=w_down, conv_w=conv_w,
        conv_b=small["conv_b"][l].reshape(1, -1),
        dt_bias=_pad_lanes(small["dt_bias"][l]), a_log=_pad_lanes(small["a_log"][l]),
        d_skip_x=jnp.repeat(small["d_skip"][l], HEAD_DIM).reshape(1, -1),
        norm_mix_g=small["norm_mix_g"][l].reshape(1, -1), ssd_norm_g=small["ssd_norm_g"][l].reshape(1, -1),
        att_norm_g=small["att_norm_g"][l].reshape(1, -1), norm_mlp_g=small["norm_mlp_g"][l].reshape(1, -1),
    )


ANY = pl.BlockSpec(memory_space=pl.ANY)


def _place():
    x, y, c = lax.axis_index("x"), lax.axis_index("y"), lax.axis_index("c")
    return x, y, c, (x, y, 1 - c), [(1 - x, y), (x, 1 - y), (1 - x, 1 - y)]


def _all_gather_chips(shard, name):
    r, ccols = shard.shape
    half = r // 2

    def body(s_ref, o_ref, send_sems, recv_sems, local_sem):
        x, y, c, sibling, chips = _place()

        def slab(px, py, hc):
            return o_ref.at[2 * px + py, pl.ds(hc * half, half), :]

        def copy(k, src, dst, to):
            return pltpu.make_async_remote_copy(src_ref=src, dst_ref=dst, send_sem=send_sems.at[k],
                                                recv_sem=recv_sems.at[k], device_id=to, device_id_type=MESH)

        mine = pltpu.make_async_copy(s_ref, o_ref.at[2 * x + y], local_sem)
        mine.start()
        sends = [copy(k, s_ref.at[pl.ds(c * half, half), :], slab(x, y, c), (px, py, c))
                 for k, (px, py) in enumerate(chips)]
        for cp in sends:
            cp.start()
        passed = []
        for k, (px, py) in enumerate(chips):
            copy(k, slab(px, py, c), slab(px, py, c), (px, py, c)).wait_recv()
            cp = copy(3 + k, slab(px, py, c), slab(px, py, c), sibling)
            cp.start()
            passed.append(cp)
        for k, (px, py) in enumerate(chips):
            copy(3 + k, slab(px, py, 1 - c), slab(px, py, 1 - c), sibling).wait_recv()
        for cp in sends + passed:
            cp.wait_send()
        mine.wait()

    return pl.pallas_call(
        body, name=name, in_specs=[ANY], out_specs=ANY,
        out_shape=_sds((N_CHIPS, r, ccols), shard.dtype),
        scratch_shapes=[pltpu.SemaphoreType.DMA((6,)), pltpu.SemaphoreType.DMA((6,)), pltpu.SemaphoreType.DMA],
    )(shard)


def _sibling_swap(g, name):
    n, r, ccols = g.shape
    half = r // 2

    def body(g_ref, o_ref, send_sem, recv_sem):
        _, _, c, sibling, _ = _place()
        cp = pltpu.make_async_remote_copy(src_ref=g_ref.at[:, pl.ds((1 - c) * half, half), :], dst_ref=o_ref,
                                          send_sem=send_sem, recv_sem=recv_sem, device_id=sibling, device_id_type=MESH)
        cp.start()
        cp.wait()

    return pl.pallas_call(
        body, name=name, in_specs=[ANY], out_specs=ANY, out_shape=_sds((n, half, ccols), g.dtype),
        scratch_shapes=[pltpu.SemaphoreType.DMA, pltpu.SemaphoreType.DMA],
    )(g)


def _chip_exchange(p, name):
    _, h, ccols = p.shape

    def body(p_ref, o_ref, send_sems, recv_sems):
        _, _, c, _, chips = _place()
        copies = [pltpu.make_async_remote_copy(src_ref=p_ref.at[2 * px + py], dst_ref=o_ref.at[k],
                                               send_sem=send_sems.at[k], recv_sem=recv_sems.at[k],
                                               device_id=(px, py, c), device_id_type=MESH)
                  for k, (px, py) in enumerate(chips)]
        for cp in copies:
            cp.start()
        for cp in copies:
            cp.wait()

    return pl.pallas_call(
        body, name=name, in_specs=[ANY], out_specs=ANY, out_shape=_sds((3, h, ccols), p.dtype),
        scratch_shapes=[pltpu.SemaphoreType.DMA((3,)), pltpu.SemaphoreType.DMA((3,))],
    )(p)


def _sibling_share(rh, name):
    h, ccols = rh.shape

    def body(r_ref, o_ref, send_sem, recv_sem, local_sem):
        _, _, c, sibling, _ = _place()
        mine = o_ref.at[pl.ds(c * h, h), :]
        local = pltpu.make_async_copy(r_ref, mine, local_sem)
        local.start()
        cp = pltpu.make_async_remote_copy(src_ref=r_ref, dst_ref=mine, send_sem=send_sem, recv_sem=recv_sem,
                                          device_id=sibling, device_id_type=MESH)
        cp.start()
        theirs = o_ref.at[pl.ds((1 - c) * h, h), :]
        pltpu.make_async_remote_copy(src_ref=r_ref, dst_ref=theirs, send_sem=send_sem, recv_sem=recv_sem,
                                     device_id=sibling, device_id_type=MESH).wait_recv()
        cp.wait_send()
        local.wait()

    return pl.pallas_call(
        body, name=name, in_specs=[ANY], out_specs=ANY, out_shape=_sds((2 * h, ccols), rh.dtype),
        scratch_shapes=[pltpu.SemaphoreType.DMA, pltpu.SemaphoreType.DMA, pltpu.SemaphoreType.DMA],
    )(rh)


def _add_halves(g, a, c, name):
    n, r, ccols = g.shape
    half = r // 2
    tr = _tile(half, 256)
    nb = half // tr

    def body(c_ref, g_ref, a_ref, o_ref):
        o_ref[...] = g_ref[...] + a_ref[...]

    blk = (1, tr, ccols)
    return pl.pallas_call(
        body, name=name,
        grid_spec=pltpu.PrefetchScalarGridSpec(
            num_scalar_prefetch=1, grid=(n, nb),
            in_specs=[pl.BlockSpec(blk, lambda j, i, c_ref: (j, c_ref[0] * nb + i, 0)),
                      pl.BlockSpec(blk, lambda j, i, c_ref: (j, i, 0))],
            out_specs=pl.BlockSpec(blk, lambda j, i, c_ref: (j, i, 0))),
        out_shape=_sds((n, half, ccols)),
        compiler_params=_params("parallel", "parallel"),
    )(c.reshape(1).astype(jnp.int32), g, a)


def _sum_chips(p, got, chip, name):
    _, h, ccols = p.shape
    tr = _tile(h, 256)

    def body(j_ref, p_ref, a_ref, b_ref, c_ref, o_ref):
        o_ref[...] = ((p_ref[...] + a_ref[...]) + b_ref[...]) + c_ref[...]

    blk = (1, tr, ccols)

    def slot(k):
        return pl.BlockSpec(blk, lambda i, j_ref: (k, i, 0))

    out = pl.pallas_call(
        body, name=name,
        grid_spec=pltpu.PrefetchScalarGridSpec(
            num_scalar_prefetch=1, grid=(h // tr,),
            in_specs=[pl.BlockSpec(blk, lambda i, j_ref: (j_ref[0], i, 0)), slot(0), slot(1), slot(2)],
            out_specs=pl.BlockSpec(blk, lambda i, j_ref: (0, i, 0))),
        out_shape=_sds((1, h, ccols)),
        compiler_params=_params("parallel"),
    )(chip.reshape(1).astype(jnp.int32), p, got, got, got)
    return out[0]


def _reduce_scatter(g, chip, c, name):
    a = _sibling_swap(g, f"{name}_swap")
    p = _add_halves(g, a, c, f"{name}_add")
    got = _chip_exchange(p, f"{name}_xchg")
    rh = _sum_chips(p, got, chip, f"{name}_sum")
    return _sibling_share(rh, f"{name}_share")


def _all_reduce_small(v, name):
    r = v.shape[0]

    def body(v_ref, o_ref, buf, send_sems, recv_sems):
        x, y, c, _, _ = _place()
        buf[0] = v_ref[...]
        copies = []
        for rel in range(1, 8):
            fx, fy, fc = (rel >> 2) & 1, (rel >> 1) & 1, rel & 1
            peer = (1 - x if fx else x, 1 - y if fy else y, 1 - c if fc else c)
            cp = pltpu.make_async_remote_copy(src_ref=v_ref, dst_ref=buf.at[rel], send_sem=send_sems.at[rel - 1],
                                              recv_sem=recv_sems.at[rel - 1], device_id=peer, device_id_type=MESH)
            cp.start()
            copies.append(cp)
        for cp in copies:
            cp.wait()
        me = 4 * x + 2 * y + c
        acc = buf[jnp.bitwise_xor(me, 0)]
        for src in range(1, 8):
            acc = acc + buf[jnp.bitwise_xor(me, src)]
        o_ref[...] = acc

    vm = pl.BlockSpec(memory_space=pltpu.VMEM)
    return pl.pallas_call(
        body, name=name, in_specs=[vm], out_specs=vm, out_shape=_sds((r, LANES)),
        scratch_shapes=[pltpu.VMEM((8, r, LANES), F32), pltpu.SemaphoreType.DMA((7,)), pltpu.SemaphoreType.DMA((7,))],
    )(v)


WEIGHTS = ["norm_mix_g", "w_in", "conv_w", "conv_b", "dt_bias", "a_log", "d_skip", "ssd_norm_g", "att_norm_g",
           "w_out", "norm_mlp_g", "w_up", "w_down", "final_norm_g"]
BIG = ["w_in", "w_out", "w_up", "w_down"]
SMALL = [n for n in WEIGHTS if n not in BIG]


def _pack(arrays):
    flat = []
    for a in arrays:
        a = a.reshape(-1)
        flat.append(jnp.pad(a, (0, (-a.shape[0]) % LANES)))
    flat = jnp.concatenate(flat)
    flat = jnp.pad(flat, (0, (-flat.shape[0]) % (8 * LANES)))
    return flat.reshape(-1, LANES)


def _unpack(packed, shapes):
    flat, out, pos = packed.reshape(-1), [], 0
    for shp in shapes:
        n = math.prod(shp)
        out.append(flat[pos:pos + n].reshape(shp))
        pos += n + (-n) % LANES
    return out


def _to_shards(name, g):
    if name in ("w_in", "w_up"):
        l, r, ccols = g.shape
        return g.reshape(l, r, N_CHIPS, ccols // N_CHIPS).transpose(2, 0, 1, 3).reshape(N_CHIPS, l * r, ccols // N_CHIPS)
    l, r, ccols = g.shape
    return g.reshape(l, N_CHIPS, r // N_CHIPS, ccols).transpose(1, 0, 2, 3).reshape(N_CHIPS, l * r // N_CHIPS, ccols)


def _from_gathered(name, g, l):
    rows = g.shape[1] // DEPTH
    part = g[:, l * rows:(l + 1) * rows, :]
    if name in ("w_in", "w_up", "conv_w"):
        return part.transpose(1, 0, 2).reshape(rows, N_CHIPS * g.shape[2])
    return part.reshape(N_CHIPS * rows, g.shape[2])


def kernel(x, norm_mix_g, w_in, conv_w, conv_b, dt_bias, a_log, d_skip, ssd_norm_g, att_norm_g, w_out, norm_mlp_g, w_up, w_down, final_norm_g, loss_target, m_norm_mix_g, m_w_in, m_conv_w, m_conv_b, m_dt_bias, m_a_log, m_d_skip, m_ssd_norm_g, m_att_norm_g, m_w_out, m_norm_mlp_g, m_w_up, m_w_down, m_final_norm_g, v_norm_mix_g, v_w_in, v_conv_w, v_conv_b, v_dt_bias, v_a_log, v_d_skip, v_ssd_norm_g, v_att_norm_g, v_w_out, v_norm_mlp_g, v_w_up, v_w_down, v_final_norm_g):
    w = dict(norm_mix_g=norm_mix_g, w_in=w_in, conv_w=conv_w, conv_b=conv_b, dt_bias=dt_bias, a_log=a_log,
             d_skip=d_skip, ssd_norm_g=ssd_norm_g, att_norm_g=att_norm_g, w_out=w_out, norm_mlp_g=norm_mlp_g,
             w_up=w_up, w_down=w_down, final_norm_g=final_norm_g)
    m = dict(norm_mix_g=m_norm_mix_g, w_in=m_w_in, conv_w=m_conv_w, conv_b=m_conv_b, dt_bias=m_dt_bias,
             a_log=m_a_log, d_skip=m_d_skip, ssd_norm_g=m_ssd_norm_g, att_norm_g=m_att_norm_g, w_out=m_w_out,
             norm_mlp_g=m_norm_mlp_g, w_up=m_w_up, w_down=m_w_down, final_norm_g=m_final_norm_g)
    v = dict(norm_mix_g=v_norm_mix_g, w_in=v_w_in, conv_w=v_conv_w, conv_b=v_conv_b, dt_bias=v_dt_bias,
             a_log=v_a_log, d_skip=v_d_skip, ssd_norm_g=v_ssd_norm_g, att_norm_g=v_att_norm_g, w_out=v_w_out,
             norm_mlp_g=v_norm_mlp_g, w_up=v_w_up, w_down=v_w_down, final_norm_g=v_final_norm_g)
    n_seq, seq, d = x.shape
    t = n_seq * seq
    chip = 2 * lax.axis_index("x") + lax.axis_index("y")
    core = lax.axis_index("c")

    gathered = {n: _all_gather_chips(w[n].astype(BF16).reshape(-1, w[n].shape[-1]), f"gather_{n}") for n in BIG}
    gathered["conv_w"] = _all_gather_chips(conv_w.reshape(-1, conv_w.shape[-1]), "gather_conv_w")
    layers = [_prep_layer(l, *[_from_gathered(n, gathered[n], l) for n in BIG + ["conv_w"]], w) for l in range(DEPTH)]

    xs = x.reshape(t, d)
    saved = []
    for l in range(DEPTH):
        xs, s = _layer_fwd(xs, layers[l], n_seq, f"_l{l}")
        saved.append(s)
    loss_vec, dx, dxb, g_final = _final_loss(xs, final_norm_g.reshape(1, d), loss_target.reshape(t, d), "final_loss")
    loss = lax.psum(loss_vec[0, 0], ("x", "y", "c"))

    grads = [None] * DEPTH
    for l in reversed(range(DEPTH)):
        dx, dxb, grads[l] = _layer_bwd(dx, dxb, layers[l], saved[l], n_seq, f"_l{l}")
        grads[l]["w_in"] = _merge_w_in(grads[l].pop("w_main"), grads[l].pop("w_dt"))
    grad_x = dx.reshape(n_seq, seq, d)

    full = {n: jnp.stack([grads[l][n] for l in range(DEPTH)]) for n in WEIGHTS if n != "final_norm_g"}
    full["final_norm_g"] = g_final.reshape(d)
    g_out = {}
    for n in BIG:
        red = _reduce_scatter(_to_shards(n, full[n]), chip, core, f"rs_{n}")
        g_out[n] = red.reshape(w[n].shape)
    small_sum = _all_reduce_small(_pack([full[n] for n in SMALL]), "allreduce_small")
    small_shapes = [(DEPTH, CONV_WIDTH, D_CONV) if n == "conv_w" else w[n].shape for n in SMALL]
    for n, val in zip(SMALL, _unpack(small_sum, small_shapes)):
        g_out[n] = val
    g_out["conv_w"] = lax.dynamic_slice_in_dim(g_out["conv_w"], chip * conv_w.shape[-1], conv_w.shape[-1], axis=2)

    delta, new_m, new_v = {}, {}, {}
    for n in BIG:
        two_d = (-1, w[n].shape[-1])
        dl, mn, vn = _adamw(w[n].reshape(two_d), g_out[n].reshape(two_d), m[n].reshape(two_d), v[n].reshape(two_d),
                            f"adamw_{n}")
        delta[n], new_m[n], new_v[n] = dl.reshape(w[n].shape), mn.reshape(w[n].shape), vn.reshape(w[n].shape)
    packs = [_pack([src[n] for n in SMALL]) for src in (w, g_out, m, v)]
    shapes = [w[n].shape for n in SMALL]
    for dst, packed in zip((delta, new_m, new_v), _adamw(*packs, "adamw_small")):
        for n, val in zip(SMALL, _unpack(packed, shapes)):
            dst[n] = val

    return (loss, grad_x, *[g_out[n] for n in WEIGHTS], *[delta[n] for n in WEIGHTS],
            *[new_m[n] for n in WEIGHTS], *[new_v[n] for n in WEIGHTS])
```

```python
import functools
import math

import jax
import jax.numpy as jnp
from jax import lax
from jax.experimental import pallas as pl
from jax.experimental.pallas import tpu as pltpu

F32 = jnp.float32
BF16 = jnp.bfloat16
HIGHEST = lax.Precision.HIGHEST

D_MODEL = 1024
DEPTH = 4
HEAD_DIM = 64
N_HEADS = 16
N_GROUPS = 4
N_STATE = 128
N_PAIRS = N_HEADS // 2
CONV_WIDTH = 4
CHUNK = 128
D_CONV = D_MODEL + 2 * N_GROUPS * N_STATE
D_MAIN = D_MODEL + D_CONV + 3 * D_MODEL
D_IN_PROJ = D_MAIN + N_HEADS
D_FF = 4 * D_MODEL
EPS = 1e-5
LANES = 128
VMEM_LIMIT = 48 * 1024 * 1024

ADAM_LR = 0.001
ADAM_B1 = 0.9
ADAM_B2 = 0.999
ADAM_EPS = 1e-08
ADAM_WD = 0.01
ADAM_STEP = 10

N_CHIPS = 4
MESH = pl.DeviceIdType.MESH


def _tile(n, cap):
    if n <= cap:
        return n
    t = cap
    while t >= 8:
        if n % t == 0:
            return t
        t //= 2
    raise ValueError(f"no tile for {n} under {cap}")


def _params(*sem):
    return pltpu.CompilerParams(dimension_semantics=sem, vmem_limit_bytes=VMEM_LIMIT)


_DIMS = {"nn": (((1,), (0,)), ((), ())), "nt": (((1,), (1,)), ((), ())), "tn": (((0,), (0,)), ((), ()))}


def _matmul(a, b, mode, out_dtypes, name, extras=(), epilogue=None, tm=512, tn=1024, tk=512):
    if mode == "nn":
        (m, k), (_, n) = a.shape, b.shape
    elif mode == "nt":
        (m, k), (n, _) = a.shape, b.shape
    else:
        (k, m), (_, n) = a.shape, b.shape
    tm, tn, tk = _tile(m, tm), _tile(n, tn), _tile(k, tk)
    nk = k // tk
    if mode == "tn":
        a_spec = pl.BlockSpec((tk, tm), lambda i, j, kk: (kk, i))
    else:
        a_spec = pl.BlockSpec((tm, tk), lambda i, j, kk: (i, kk))
    if mode == "nt":
        b_spec = pl.BlockSpec((tn, tk), lambda i, j, kk: (j, kk))
    else:
        b_spec = pl.BlockSpec((tk, tn), lambda i, j, kk: (kk, j))
    mn_spec = pl.BlockSpec((tm, tn), lambda i, j, kk: (i, j))
    n_extra, n_out = len(extras), len(out_dtypes)
    dims = _DIMS[mode]

    def body(a_ref, b_ref, *rest):
        extra_refs, out_refs, acc = rest[:n_extra], rest[n_extra:n_extra + n_out], rest[-1]
        kk = pl.program_id(2)

        @pl.when(kk == 0)
        def _():
            acc[...] = jnp.zeros_like(acc)

        acc[...] += lax.dot_general(a_ref[...].astype(BF16), b_ref[...].astype(BF16), dims,
                                    preferred_element_type=F32)

        @pl.when(kk == nk - 1)
        def _():
            res = acc[...]
            outs = epilogue(res, *[e[...] for e in extra_refs]) if epilogue is not None else (res,)
            for o_ref, val in zip(out_refs, outs):
                o_ref[...] = val.astype(o_ref.dtype)

    outs = pl.pallas_call(
        body, name=name, grid=(m // tm, n // tn, nk),
        in_specs=[a_spec, b_spec] + [mn_spec] * n_extra,
        out_specs=[mn_spec] * n_out,
        out_shape=[jax.ShapeDtypeStruct((m, n), dt) for dt in out_dtypes],
        scratch_shapes=[pltpu.VMEM((tm, tn), F32)],
        compiler_params=_params("parallel", "parallel", "arbitrary"),
    )(a, b, *extras)
    return tuple(outs)


def _rowwise(fn, rows, vecs, out_rows, out_accs, name, tm=256):
    t = rows[0].shape[0]
    tm = _tile(t, tm)
    n_rows, n_vecs, n_or, n_oa = len(rows), len(vecs), len(out_rows), len(out_accs)

    def body(*refs):
        ins = [r[...] for r in refs[:n_rows + n_vecs]]
        outs = fn(*ins)
        o_refs = refs[n_rows + n_vecs:]
        for o_ref, val in zip(o_refs[:n_or], outs[:n_or]):
            o_ref[...] = val.astype(o_ref.dtype)

        @pl.when(pl.program_id(0) == 0)
        def _():
            for o_ref in o_refs[n_or:]:
                o_ref[...] = jnp.zeros_like(o_ref)

        for o_ref, val in zip(o_refs[n_or:], outs[n_or:]):
            o_ref[...] += val

    outs = pl.pallas_call(
        body, name=name, grid=(t // tm,),
        in_specs=[pl.BlockSpec((tm, r.shape[1]), lambda i: (i, 0)) for r in rows]
        + [pl.BlockSpec(v.shape, lambda i: (0, 0)) for v in vecs],
        out_specs=[pl.BlockSpec((tm, o.shape[1]), lambda i: (i, 0)) for o in out_rows]
        + [pl.BlockSpec(o.shape, lambda i: (0, 0)) for o in out_accs],
        out_shape=list(out_rows) + list(out_accs),
        compiler_params=_params("arbitrary" if n_oa else "parallel"),
    )(*rows, *vecs)
    return tuple(outs)


def _stack_rows(parts, width):
    rows = lax.broadcasted_iota(jnp.int32, (8, width), 0)
    out = jnp.zeros((8, width), F32)
    for j, part in enumerate(parts):
        out = out + jnp.where(rows == j, part, 0.0)
    return out


def _sds(shape, dtype=F32):
    return jax.ShapeDtypeStruct(shape, dtype)


def _sigmoid(x):
    return 1.0 / (1.0 + jnp.exp(-x))


def _colsum(x):
    return jnp.sum(x, axis=0, keepdims=True)


def _rstd(x):
    return lax.rsqrt(jnp.mean(x * x, axis=-1, keepdims=True) + EPS)


def _rms_bwd_rows(xhat, r, g, dy):
    dxh = dy * g
    dx = r * (dxh - xhat * jnp.mean(dxh * xhat, axis=-1, keepdims=True))
    return dx, _colsum(dy * xhat)


def _rms_fwd(x, g, name):
    t, d = x.shape

    def fn(xb, gv):
        return (xb * _rstd(xb) * gv,)

    return _rowwise(fn, [x], [g], [_sds((t, d), BF16)], [], name)[0]


def _rms_bwd(x, g, dh, dres, name):
    t, d = x.shape

    def fn(xb, dhb, dresb, gv):
        r = _rstd(xb)
        dx, dg = _rms_bwd_rows(xb * r, r, gv, dhb)
        dx = dx + dresb
        return dx, dx, dg

    return _rowwise(fn, [x, dh, dres], [g], [_sds((t, d)), _sds((t, d), BF16)], [_sds((1, d))], name)


def _mixnorm_fwd(y, z, att, gs, ga, name):
    t, d = y.shape

    def fn(yb, zb, ab, gsv, gav):
        yg = yb * (zb * _sigmoid(zb))
        return (jnp.concatenate([yg * _rstd(yg) * gsv, ab * _rstd(ab) * gav], axis=1),)

    tm = _tile(t, 256)

    def body(y_ref, z_ref, a_ref, gs_ref, ga_ref, o_ref):
        o_ref[...] = fn(y_ref[...], z_ref[...], a_ref[...], gs_ref[...], ga_ref[...])[0].astype(BF16)

    row = pl.BlockSpec((tm, d), lambda i: (i, 0))
    vec = pl.BlockSpec((1, d), lambda i: (0, 0))
    out = pl.pallas_call(
        body, name=name, grid=(t // tm,),
        in_specs=[row, row, row, vec, vec],
        out_specs=pl.BlockSpec((tm, 2 * d), lambda i: (i, 0)),
        out_shape=_sds((t, 2 * d), BF16),
        compiler_params=_params("parallel"),
    )(y, z, att, gs, ga)
    return out


def _mixnorm_bwd(y, z, att, gs, ga, dycat, name):
    t, d = y.shape
    tm = _tile(t, 256)

    def body(y_ref, z_ref, a_ref, dyc_ref, gs_ref, ga_ref, dy_ref, dz_ref, da_ref, dgs_ref, dga_ref):
        yb, zb, ab = y_ref[...], z_ref[...], a_ref[...]
        dys, dya = dyc_ref[:, :d], dyc_ref[:, d:]
        sz = _sigmoid(zb)
        silu = zb * sz
        yg = yb * silu
        r = _rstd(yg)
        dyg, dgs = _rms_bwd_rows(yg * r, r, gs_ref[...], dys)
        dy_ref[...] = dyg * silu
        dz_ref[...] = dyg * yb * (sz * (1.0 + zb * (1.0 - sz)))
        r2 = _rstd(ab)
        datt, dga = _rms_bwd_rows(ab * r2, r2, ga_ref[...], dya)
        da_ref[...] = datt

        @pl.when(pl.program_id(0) == 0)
        def _():
            dgs_ref[...] = jnp.zeros_like(dgs_ref)
            dga_ref[...] = jnp.zeros_like(dga_ref)

        dgs_ref[...] += dgs
        dga_ref[...] += dga

    row = pl.BlockSpec((tm, d), lambda i: (i, 0))
    vec = pl.BlockSpec((1, d), lambda i: (0, 0))
    return pl.pallas_call(
        body, name=name, grid=(t // tm,),
        in_specs=[row, row, row, pl.BlockSpec((tm, 2 * d), lambda i: (i, 0)), vec, vec],
        out_specs=[row, row, row, vec, vec],
        out_shape=[_sds((t, d)), _sds((t, d)), _sds((t, d)), _sds((1, d)), _sds((1, d))],
        compiler_params=_params("arbitrary"),
    )(y, z, att, dycat, gs, ga)


def _final_loss(x, g, target, name):
    t, d = x.shape

    def fn(xb, tb, gv):
        r = _rstd(xb)
        xhat = xb * r
        err = xhat * gv - tb
        loss = 0.5 * jnp.sum(jnp.mean(err * err, axis=-1, keepdims=True), axis=0, keepdims=True)
        dx, dg = _rms_bwd_rows(xhat, r, gv, err * (1.0 / d))
        return dx, dx, jnp.broadcast_to(loss, (1, LANES)), dg

    dx, dxb, loss, dg = _rowwise(fn, [x, target], [g], [_sds((t, d)), _sds((t, d), BF16)],
                                 [_sds((1, LANES)), _sds((1, d))], name)
    return loss, dx, dxb, dg


def _adamw(w, g, m, v, name):
    c1 = 1.0 - ADAM_B1 ** ADAM_STEP
    c2 = 1.0 - ADAM_B2 ** ADAM_STEP

    def fn(wb, gb, mb, vb):
        mn = ADAM_B1 * mb + (1.0 - ADAM_B1) * gb
        vn = ADAM_B2 * vb + (1.0 - ADAM_B2) * (gb * gb)
        delta = -ADAM_LR * ((mn / c1) / (jnp.sqrt(vn / c2) + ADAM_EPS) + ADAM_WD * wb)
        return delta, mn, vn

    return _rowwise(fn, [w, g, m, v], [], [_sds(w.shape)] * 3, [], name)


CONV_CB = 512


def _shift_down(u, k):
    if k == 0:
        return u
    rows = lax.broadcasted_iota(jnp.int32, u.shape, 0)
    return jnp.where(rows >= k, pltpu.roll(u, k, 0), 0.0)


def _shift_up(u, k):
    if k == 0:
        return u
    n = u.shape[0]
    rows = lax.broadcasted_iota(jnp.int32, u.shape, 0)
    return jnp.where(rows < n - k, pltpu.roll(u, n - k, 0), 0.0)


def _conv_pre(u, w, b):
    pre = b
    for j in range(CONV_WIDTH):
        pre = pre + w[j:j + 1, :] * _shift_down(u, CONV_WIDTH - 1 - j)
    return pre


def _conv_fwd(proj, w, b, n_seq, name):
    t = proj.shape[0]
    seq = t // n_seq
    off = D_MODEL // CONV_CB

    def body(u_ref, w_ref, b_ref, o_ref):
        pre = _conv_pre(u_ref[...], w_ref[...], b_ref[...])
        o_ref[...] = pre * _sigmoid(pre)

    return pl.pallas_call(
        body, name=name, grid=(n_seq, D_CONV // CONV_CB),
        in_specs=[pl.BlockSpec((seq, CONV_CB), lambda s, c: (s, c + off)),
                  pl.BlockSpec((CONV_WIDTH, CONV_CB), lambda s, c: (0, c)),
                  pl.BlockSpec((1, CONV_CB), lambda s, c: (0, c))],
        out_specs=pl.BlockSpec((seq, CONV_CB), lambda s, c: (s, c)),
        out_shape=_sds((t, D_CONV)),
        compiler_params=_params("parallel", "parallel"),
    )(proj, w, b)


def _conv_bwd(proj, w, b, dxbc, n_seq, name):
    t = proj.shape[0]
    seq = t // n_seq
    off = D_MODEL // CONV_CB

    def body(u_ref, w_ref, b_ref, d_ref, du_ref, wg_ref):
        u, wv = u_ref[...], w_ref[...]
        pre = _conv_pre(u, wv, b_ref[...])
        s = _sigmoid(pre)
        dpre = d_ref[...] * (s * (1.0 + pre * (1.0 - s)))
        du = jnp.zeros_like(u)
        parts = []
        for j in range(CONV_WIDTH):
            k = CONV_WIDTH - 1 - j
            du = du + wv[j:j + 1, :] * _shift_up(dpre, k)
            parts.append(_colsum(dpre * _shift_down(u, k)))
        du_ref[...] = du
        parts.append(_colsum(dpre))

        @pl.when(pl.program_id(1) == 0)
        def _():
            wg_ref[...] = jnp.zeros_like(wg_ref)

        wg_ref[...] += _stack_rows(parts, u.shape[1])

    return pl.pallas_call(
        body, name=name, grid=(D_CONV // CONV_CB, n_seq),
        in_specs=[pl.BlockSpec((seq, CONV_CB), lambda c, s: (s, c + off)),
                  pl.BlockSpec((CONV_WIDTH, CONV_CB), lambda c, s: (0, c)),
                  pl.BlockSpec((1, CONV_CB), lambda c, s: (0, c)),
                  pl.BlockSpec((seq, CONV_CB), lambda c, s: (s, c))],
        out_specs=[pl.BlockSpec((seq, CONV_CB), lambda c, s: (s, c)),
                   pl.BlockSpec((8, CONV_CB), lambda c, s: (0, c))],
        out_shape=[_sds((t, D_CONV)), _sds((8, D_CONV))],
        compiler_params=_params("parallel", "arbitrary"),
    )(proj, w, b, dxbc)


def _iota2(shape, axis):
    return lax.broadcasted_iota(jnp.int32, shape, axis)


def _dot(a, b, dims=_DIMS["nn"], precision=None):
    return lax.dot_general(a, b, dims, precision=precision, preferred_element_type=F32)


def _bdot(a, b, dims=_DIMS["nn"]):
    return lax.dot_general(a.astype(BF16), b.astype(BF16), dims, preferred_element_type=F32)


def _expand_mat():
    return (_iota2((LANES, D_MODEL), 0) == lax.shift_right_logical(_iota2((LANES, D_MODEL), 1), 6)).astype(F32)


def _reduce_mat():
    return (lax.shift_right_logical(_iota2((D_MODEL, LANES), 0), 6) == _iota2((D_MODEL, LANES), 1)).astype(F32)


def _ssd_decay(dtraw, bias, alog):
    row, col = _iota2((CHUNK, CHUNK), 0), _iota2((CHUNK, CHUNK), 1)
    pre = dtraw + bias
    dtb = jnp.maximum(pre, 0.0) + jnp.log(1.0 + jnp.exp(-jnp.abs(pre)))
    a_neg = -jnp.exp(alog)
    a = dtb * a_neg
    tril = (row >= col).astype(F32)
    triu = (row <= col).astype(F32)
    cs = _dot(tril, a, precision=HIGHEST)
    cs_t = _dot(a, triu, _DIMS["tn"], precision=HIGHEST)
    return pre, dtb, a_neg, cs, cs_t


def _pair_rowscale(vec, h0):
    top = _iota2((CHUNK, LANES), 0) < HEAD_DIM
    return jnp.where(top, vec[:, h0:h0 + 1], vec[:, h0 + 1:h0 + 2])


def _decay_mat(cs, cs_t, h):
    row, col = _iota2((CHUNK, CHUNK), 0), _iota2((CHUNK, CHUNK), 1)
    seg = cs[:, h:h + 1] - cs_t[h:h + 1, :]
    return jnp.exp(jnp.where(row >= col, seg, -jnp.inf))


def _ssd_fwd(xbc, dtraw, bias, alog, dskip_x, n_seq, name):
    t = xbc.shape[0]
    nc = t // n_seq // CHUNK

    def body(x_ref, b_ref, c_ref, dt_ref, bias_ref, alog_ref, dsk_ref, y_ref, st_ref, h_ref):
        @pl.when(pl.program_id(1) == 0)
        def _():
            h_ref[...] = jnp.zeros_like(h_ref)

        _, dtb, _, cs, cs_t = _ssd_decay(dt_ref[...], bias_ref[...], alog_ref[...])
        expand = _expand_mat()
        dt_x = _dot(dtb, expand, precision=HIGHEST)
        cs_x = _dot(cs, expand, precision=HIGHEST)
        tot = cs[CHUNK - 1:CHUNK, :]
        etot = jnp.exp(tot)
        x = x_ref[...]
        xdt = x * dt_x
        e_x = jnp.exp(cs_x)
        xdec = xdt * jnp.exp(cs_x[CHUNK - 1:CHUNK, :] - cs_x)
        keeps = [_iota2((CHUNK, LANES), 1) < HEAD_DIM, _iota2((CHUNK, LANES), 1) >= HEAD_DIM]
        for q in range(N_PAIRS):
            g = q // 2
            lanes = slice(q * LANES, (q + 1) * LANES)
            bg = b_ref[:, g * N_STATE:(g + 1) * N_STATE]
            cg = c_ref[:, g * N_STATE:(g + 1) * N_STATE]
            gmat = _bdot(cg, bg, _DIMS["nt"])
            xdt_q = xdt[:, lanes]
            ydiag = jnp.zeros((CHUNK, LANES), F32)
            for r in range(2):
                w = gmat * _decay_mat(cs, cs_t, 2 * q + r)
                ydiag = ydiag + _bdot(w, jnp.where(keeps[r], xdt_q, 0.0))
            prev = h_ref[q]
            yoff = _bdot(cg, prev, _DIMS["nt"]) * e_x[:, lanes]
            y_ref[:, lanes] = ydiag + yoff + x[:, lanes] * dsk_ref[:, lanes]
            st_ref[0, q] = prev
            h_ref[q] = prev * _pair_rowscale(etot, 2 * q) + _bdot(xdec[:, lanes], bg, _DIMS["tn"])

    vec = pl.BlockSpec((1, LANES), lambda s, c: (0, 0))
    return pl.pallas_call(
        body, name=name, grid=(n_seq, nc),
        in_specs=[pl.BlockSpec((CHUNK, D_MODEL), lambda s, c: (s * nc + c, 0)),
                  pl.BlockSpec((CHUNK, N_GROUPS * N_STATE), lambda s, c: (s * nc + c, 2)),
                  pl.BlockSpec((CHUNK, N_GROUPS * N_STATE), lambda s, c: (s * nc + c, 3)),
                  pl.BlockSpec((CHUNK, LANES), lambda s, c: (s * nc + c, 0)),
                  vec, vec, pl.BlockSpec((1, D_MODEL), lambda s, c: (0, 0))],
        out_specs=[pl.BlockSpec((CHUNK, D_MODEL), lambda s, c: (s * nc + c, 0)),
                   pl.BlockSpec((1, N_PAIRS, LANES, N_STATE), lambda s, c: (s * nc + c, 0, 0, 0))],
        out_shape=[_sds((t, D_MODEL)), _sds((t // CHUNK, N_PAIRS, LANES, N_STATE))],
        scratch_shapes=[pltpu.VMEM((N_PAIRS, LANES, N_STATE), F32)],
        compiler_params=_params("parallel", "arbitrary"),
    )(xbc, xbc, xbc, dtraw, bias, alog, dskip_x)


def _ssd_bwd(xbc, dtraw, bias, alog, dskip_x, states, dy, n_seq, name):
    t = xbc.shape[0]
    nc = t // n_seq // CHUNK

    def body(x_ref, b_ref, c_ref, dt_ref, bias_ref, alog_ref, dsk_ref, st_ref, dy_ref,
             dxbc_ref, ddt_ref, pg_ref, dh_ref):
        first = jnp.logical_and(pl.program_id(0) == 0, pl.program_id(1) == 0)

        @pl.when(first)
        def _():
            pg_ref[...] = jnp.zeros_like(pg_ref)

        @pl.when(pl.program_id(1) == 0)
        def _():
            dh_ref[...] = jnp.zeros_like(dh_ref)

        row, col = _iota2((CHUNK, CHUNK), 0), _iota2((CHUNK, CHUNK), 1)
        pre, dtb, a_neg, cs, cs_t = _ssd_decay(dt_ref[...], bias_ref[...], alog_ref[...])
        expand, reduce = _expand_mat(), _reduce_mat()
        dt_x = _dot(dtb, expand, precision=HIGHEST)
        cs_x = _dot(cs, expand, precision=HIGHEST)
        etot = jnp.exp(cs[CHUNK - 1:CHUNK, :])
        x, dy = x_ref[...], dy_ref[...]
        xdt = x * dt_x
        e_x = jnp.exp(cs_x)
        dec_x = jnp.exp(cs_x[CHUNK - 1:CHUNK, :] - cs_x)
        xdec = xdt * dec_x
        dye = dy * e_x
        keeps = [_iota2((CHUNK, LANES), 1) < HEAD_DIM, _iota2((CHUNK, LANES), 1) >= HEAD_DIM]
        row_lo = _iota2((CHUNK, LANES), 0) < HEAD_DIM
        dcs_col = jnp.zeros((CHUNK, LANES), F32)
        dcs_row = jnp.zeros((LANES, CHUNK), F32)
        dtot = jnp.zeros((1, LANES), F32)
        dxdt_parts, zdec_parts, yoff_parts = [], [], []
        for g in range(N_GROUPS):
            bg = b_ref[:, g * N_STATE:(g + 1) * N_STATE]
            cg = c_ref[:, g * N_STATE:(g + 1) * N_STATE]
            gmat = _bdot(cg, bg, _DIMS["nt"])
            dgmat = jnp.zeros((CHUNK, CHUNK), F32)
            dbg = jnp.zeros((CHUNK, N_STATE), F32)
            dcg = jnp.zeros((CHUNK, N_STATE), F32)
            for q in (2 * g, 2 * g + 1):
                lanes = slice(q * LANES, (q + 1) * LANES)
                xdt_q, dy_q = xdt[:, lanes], dy[:, lanes]
                dxdt_q = jnp.zeros((CHUNK, LANES), F32)
                for r in range(2):
                    h = 2 * q + r
                    keep = keeps[r]
                    lm = _decay_mat(cs, cs_t, h)
                    w = gmat * lm
                    dy_h = jnp.where(keep, dy_q, 0.0)
                    dm = _bdot(dy_h, xdt_q, _DIMS["nt"])
                    dxdt_q = dxdt_q + _bdot(w, dy_h, _DIMS["tn"])
                    dgmat = dgmat + dm * lm
                    tmat = dm * w
                    dcs_col = dcs_col + jnp.where(col == h, jnp.sum(tmat, axis=1, keepdims=True), 0.0)
                    dcs_row = dcs_row - jnp.where(row == h, jnp.sum(tmat, axis=0, keepdims=True), 0.0)
                prev = st_ref[0, q]
                dht = dh_ref[q]
                dxdtdec = _bdot(bg, dht, _DIMS["nt"])
                dxdt_q = dxdt_q + dxdtdec * dec_x[:, lanes]
                zdec_parts.append(dxdtdec * xdec[:, lanes])
                dbg = dbg + _bdot(xdec[:, lanes], dht)
                yoff_parts.append(dy_q * (_bdot(cg, prev, _DIMS["nt"]) * e_x[:, lanes]))
                dcg = dcg + _bdot(dye[:, lanes], prev)
                dprev = _bdot(dye[:, lanes], cg, _DIMS["tn"])
                hp = jnp.sum(dht * prev, axis=1, keepdims=True)
                d0 = jnp.sum(jnp.where(row_lo[:, :1], hp, 0.0), axis=0, keepdims=True)
                d1 = jnp.sum(jnp.where(row_lo[:, :1], 0.0, hp), axis=0, keepdims=True)
                lane1 = _iota2((1, LANES), 1)
                dtot = dtot + jnp.where(lane1 == 2 * q, d0, 0.0) + jnp.where(lane1 == 2 * q + 1, d1, 0.0)
                dh_ref[q] = dprev + dht * _pair_rowscale(etot, 2 * q)
                dxdt_parts.append(dxdt_q)
            dcg = dcg + _bdot(dgmat, bg)
            dbg = dbg + _bdot(dgmat, cg, _DIMS["tn"])
            dxbc_ref[:, D_MODEL + g * N_STATE:D_MODEL + (g + 1) * N_STATE] = dbg
            dxbc_ref[:, D_MODEL + (N_GROUPS + g) * N_STATE:D_MODEL + (N_GROUPS + g + 1) * N_STATE] = dcg
        dxdt = jnp.concatenate(dxdt_parts, axis=1)
        zdec = _dot(jnp.concatenate(zdec_parts, axis=1), reduce, precision=HIGHEST)
        yoff_d = _dot(jnp.concatenate(yoff_parts, axis=1), reduce, precision=HIGHEST)
        dtot = dtot * etot + _colsum(zdec)
        last = row[:, :LANES] == CHUNK - 1
        dcs_col = dcs_col + yoff_d - zdec + jnp.where(last, dtot, 0.0)
        triu = (row <= col).astype(F32)
        da = _dot(triu, dcs_col, precision=HIGHEST) + _dot(triu, dcs_row, _DIMS["nt"], precision=HIGHEST)
        ddt = _dot(dxdt * x, reduce, precision=HIGHEST) + da * a_neg
        ddtraw = ddt * _sigmoid(pre)
        ddt_ref[...] = ddtraw
        dxbc_ref[:, :D_MODEL] = dxdt * dt_x + dy * dsk_ref[...]
        dskip = _dot(jnp.broadcast_to(_colsum(dy * x), (8, D_MODEL)), reduce, precision=HIGHEST)[0:1, :]
        pg_ref[...] += _stack_rows([_colsum(ddtraw), _colsum(da * dtb) * a_neg, dskip], LANES)

    vec = pl.BlockSpec((1, LANES), lambda s, c: (0, 0))

    def blk(s, c):
        return s * nc + (nc - 1 - c)

    return pl.pallas_call(
        body, name=name, grid=(n_seq, nc),
        in_specs=[pl.BlockSpec((CHUNK, D_MODEL), lambda s, c: (blk(s, c), 0)),
                  pl.BlockSpec((CHUNK, N_GROUPS * N_STATE), lambda s, c: (blk(s, c), 2)),
                  pl.BlockSpec((CHUNK, N_GROUPS * N_STATE), lambda s, c: (blk(s, c), 3)),
                  pl.BlockSpec((CHUNK, LANES), lambda s, c: (blk(s, c), 0)),
                  vec, vec, pl.BlockSpec((1, D_MODEL), lambda s, c: (0, 0)),
                  pl.BlockSpec((1, N_PAIRS, LANES, N_STATE), lambda s, c: (blk(s, c), 0, 0, 0)),
                  pl.BlockSpec((CHUNK, D_MODEL), lambda s, c: (blk(s, c), 0))],
        out_specs=[pl.BlockSpec((CHUNK, D_CONV), lambda s, c: (blk(s, c), 0)),
                   pl.BlockSpec((CHUNK, LANES), lambda s, c: (blk(s, c), 0)),
                   pl.BlockSpec((8, LANES), lambda s, c: (0, 0))],
        out_shape=[_sds((t, D_CONV)), _sds((t, LANES)), _sds((8, LANES))],
        scratch_shapes=[pltpu.VMEM((N_PAIRS, LANES, N_STATE), F32)],
        compiler_params=_params("arbitrary", "arbitrary"),
    )(xbc, xbc, xbc, dtraw, bias, alog, dskip_x, states, dy)


Q_COL = (D_MODEL + D_CONV) // LANES
K_COL = Q_COL + D_MODEL // LANES
V_COL = K_COL + D_MODEL // LANES
ATT_SCALE = HEAD_DIM ** -0.5


ATT_SUB = 2
ATT_TILE = ATT_SUB * CHUNK


ATT_ROWS = 2 * ATT_TILE


def _tri_dot(x, tri):
    hi = x.astype(BF16)
    lo = (x - hi.astype(F32)).astype(BF16)
    both = _dot(jnp.concatenate([hi, lo], axis=0), tri)
    return both[:x.shape[0]] + both[x.shape[0]:]


def _att_stack(ref, keeps):
    parts = []
    for a in range(ATT_SUB):
        blk = ref[a * CHUNK:(a + 1) * CHUNK, :]
        parts += [jnp.where(keeps[r], blk, 0.0).astype(BF16) for r in range(2)]
    return jnp.concatenate(parts, axis=0)


def _att_unstack(x, a, keeps):
    return jnp.where(keeps[0], x[2 * a * CHUNK:(2 * a + 1) * CHUNK], x[(2 * a + 1) * CHUNK:(2 * a + 2) * CHUNK])


def _att_logits(q_stack, k_blk, b, diag):
    s = _dot(q_stack, k_blk, _DIMS["nt"]) * ATT_SCALE
    lb = jnp.minimum(s, 0.0) - jnp.log(1.0 + jnp.exp(-jnp.abs(s)))
    l1m = lb - s
    mask = None
    if diag:
        srow, scol = _iota2((ATT_ROWS, CHUNK), 0), _iota2((ATT_ROWS, CHUNK), 1)
        qpos = lax.shift_right_logical(srow, 8) * CHUNK + jnp.bitwise_and(srow, CHUNK - 1)
        mask = qpos > (b * CHUNK + scol)
        l1m = jnp.where(mask, l1m, 0.0)
    return lb, l1m, mask


def _att_fwd(proj, n_seq, name):
    t = proj.shape[0]
    seq = t // n_seq
    nq = seq // ATT_TILE

    def body(q_ref, k_ref, v_ref, o_ref, rt_ref):
        i = pl.program_id(2)
        upper = (_iota2((CHUNK, CHUNK), 0) > _iota2((CHUNK, CHUNK), 1)).astype(BF16)
        keeps = [_iota2((CHUNK, LANES), 1) < HEAD_DIM, _iota2((CHUNK, LANES), 1) >= HEAD_DIM]
        q_stack = _att_stack(q_ref, keeps)

        def tile(jt, carry, diag):
            acc, run = list(carry[:ATT_SUB]), carry[ATT_SUB]
            for b in reversed(range(ATT_SUB)):
                k0 = pl.multiple_of(jt * ATT_TILE + b * CHUNK, CHUNK)
                k_blk = k_ref[pl.ds(k0, CHUNK), :].astype(BF16)
                v_blk = v_ref[pl.ds(k0, CHUNK), :].astype(BF16)
                lb, l1m, mask = _att_logits(q_stack, k_blk, b, diag)
                p = jnp.exp(lb + (_tri_dot(l1m, upper) + run))
                if diag:
                    p = jnp.where(mask, p, 0.0)
                pv = _dot(p.astype(BF16), v_blk)
                for a in range(ATT_SUB):
                    acc[a] = acc[a] + _att_unstack(pv, a, keeps)
                run = run + jnp.sum(l1m, axis=1, keepdims=True)
            return tuple(acc) + (run,)

        init = tuple(jnp.zeros((CHUNK, LANES), F32) for _ in range(ATT_SUB)) + (jnp.zeros((ATT_ROWS, 1), F32),)
        carry = tile(i, init, True)
        carry = lax.fori_loop(0, i, lambda it, c: tile(i - 1 - it, c, False), carry)
        for a in range(ATT_SUB):
            o_ref[a * CHUNK:(a + 1) * CHUNK, :] = carry[a]
            rt_ref[a * CHUNK:(a + 1) * CHUNK, :] = _att_unstack(carry[ATT_SUB], a, keeps)

    qblk = pl.BlockSpec((ATT_TILE, LANES), lambda s, p, i: (s * nq + i, p))
    return pl.pallas_call(
        body, name=name, grid=(n_seq, N_PAIRS, nq),
        in_specs=[pl.BlockSpec((ATT_TILE, LANES), lambda s, p, i: (s * nq + i, Q_COL + p)),
                  pl.BlockSpec((seq, LANES), lambda s, p, i: (s, K_COL + p)),
                  pl.BlockSpec((seq, LANES), lambda s, p, i: (s, V_COL + p))],
        out_specs=[qblk, qblk],
        out_shape=[_sds((t, D_MODEL)), _sds((t, D_MODEL))],
        compiler_params=_params("parallel", "parallel", "arbitrary"),
    )(proj, proj, proj)


def _att_bwd(proj, rtot, datt, n_seq, name):
    t = proj.shape[0]
    seq = t // n_seq
    nq = seq // ATT_TILE

    def body(q_ref, k_ref, v_ref, rt_ref, do_ref, dq_ref, dk_ref, dv_ref):
        i = pl.program_id(2)

        @pl.when(i == 0)
        def _():
            dk_ref[...] = jnp.zeros_like(dk_ref)
            dv_ref[...] = jnp.zeros_like(dv_ref)

        row, col = _iota2((CHUNK, CHUNK), 0), _iota2((CHUNK, CHUNK), 1)
        upper = (row > col).astype(BF16)
        before = (row < col).astype(BF16)
        keeps = [_iota2((CHUNK, LANES), 1) < HEAD_DIM, _iota2((CHUNK, LANES), 1) >= HEAD_DIM]
        q_stack, do_stack = _att_stack(q_ref, keeps), _att_stack(do_ref, keeps)
        totals = jnp.concatenate([rt_ref[a * CHUNK:(a + 1) * CHUNK, r * HEAD_DIM:r * HEAD_DIM + 1]
                                  for a in range(ATT_SUB) for r in range(2)], axis=0)

        def tile(jt, carry, diag):
            dq, seen, dseen = list(carry[:ATT_SUB]), carry[ATT_SUB], carry[ATT_SUB + 1]
            for b in range(ATT_SUB):
                k0 = pl.multiple_of(jt * ATT_TILE + b * CHUNK, CHUNK)
                k_blk = k_ref[pl.ds(k0, CHUNK), :].astype(BF16)
                v_blk = v_ref[pl.ds(k0, CHUNK), :].astype(BF16)
                lb, l1m, mask = _att_logits(q_stack, k_blk, b, diag)
                here = jnp.sum(l1m, axis=1, keepdims=True)
                p = jnp.exp(lb + (_tri_dot(l1m, upper) + (totals - seen - here)))
                if diag:
                    p = jnp.where(mask, p, 0.0)
                pb = p.astype(BF16)
                dz = _dot(do_stack, v_blk, _DIMS["nt"]) * p
                dl1m = dseen + _tri_dot(dz, before)
                sig = jnp.exp(lb)
                ds = (dz * (1.0 - sig) - dl1m * sig) * ATT_SCALE
                if diag:
                    ds = jnp.where(mask, ds, 0.0)
                dsb = ds.astype(BF16)
                dq_all = _dot(dsb, k_blk)
                for a in range(ATT_SUB):
                    dq[a] = dq[a] + _att_unstack(dq_all, a, keeps)
                dk_ref[pl.ds(k0, CHUNK), :] += _dot(dsb, q_stack, _DIMS["tn"])
                dv_ref[pl.ds(k0, CHUNK), :] += _dot(pb, do_stack, _DIMS["tn"])
                seen = seen + here
                dseen = dseen + jnp.sum(dz, axis=1, keepdims=True)
            return tuple(dq) + (seen, dseen)

        init = tuple(jnp.zeros((CHUNK, LANES), F32) for _ in range(ATT_SUB)) + \
            (jnp.zeros((ATT_ROWS, 1), F32), jnp.zeros((ATT_ROWS, 1), F32))
        carry = lax.fori_loop(0, i, lambda jt, c: tile(jt, c, False), init)
        carry = tile(i, carry, True)
        for a in range(ATT_SUB):
            dq_ref[a * CHUNK:(a + 1) * CHUNK, :] = carry[a]

    qblk = pl.BlockSpec((ATT_TILE, LANES), lambda s, p, i: (s * nq + i, p))
    kv_out = pl.BlockSpec((seq, LANES), lambda s, p, i: (s, p))
    return pl.pallas_call(
        body, name=name, grid=(n_seq, N_PAIRS, nq),
        in_specs=[pl.BlockSpec((ATT_TILE, LANES), lambda s, p, i: (s * nq + i, Q_COL + p)),
                  pl.BlockSpec((seq, LANES), lambda s, p, i: (s, K_COL + p)),
                  pl.BlockSpec((seq, LANES), lambda s, p, i: (s, V_COL + p)),
                  qblk, qblk],
        out_specs=[qblk, kv_out, kv_out],
        out_shape=[_sds((t, D_MODEL))] * 3,
        compiler_params=_params("parallel", "parallel", "arbitrary"),
    )(proj, proj, proj, rtot, datt)


def _pad_lanes(v):
    return jnp.pad(v.reshape(1, -1), ((0, 0), (0, LANES - v.shape[0])))


def _layer_fwd(x, p, n_seq, tag):
    h = _rms_fwd(x, p["norm_mix_g"], f"rms_mix_fwd{tag}")
    proj, = _matmul(h, p["w_main"], "nn", [F32], f"in_proj{tag}", tn=1536)
    dtraw, = _matmul(h, p["w_dt"], "nn", [F32], f"dt_proj{tag}")
    xbc = _conv_fwd(proj, p["conv_w"], p["conv_b"], n_seq, f"conv_fwd{tag}")
    y, states = _ssd_fwd(xbc, dtraw, p["dt_bias"], p["a_log"], p["d_skip_x"], n_seq, f"ssd_fwd{tag}")
    att, rtot = _att_fwd(proj, n_seq, f"att_fwd{tag}")
    ycat = _mixnorm_fwd(y, proj, att, p["ssd_norm_g"], p["att_norm_g"], f"mixnorm_fwd{tag}")
    x1, = _matmul(ycat, p["w_out"], "nn", [F32], f"out_proj{tag}", extras=[x], epilogue=lambda acc, xb: (xb + acc,))
    h2 = _rms_fwd(x1, p["norm_mlp_g"], f"rms_mlp_fwd{tag}")
    u, act = _matmul(h2, p["w_up"], "nn", [F32, BF16], f"up_proj{tag}",
                     epilogue=lambda acc: (acc, jnp.square(jnp.maximum(acc, 0.0))))
    x2, = _matmul(act, p["w_down"], "nn", [F32], f"down_proj{tag}", extras=[x1], epilogue=lambda acc, xb: (xb + acc,))
    saved = dict(x=x, h=h, proj=proj, dtraw=dtraw, xbc=xbc, y=y, states=states, att=att, rtot=rtot, ycat=ycat,
                 x1=x1, h2=h2, u=u, act=act)
    return x2, saved


def _layer_bwd(dx2, dx2b, p, s, n_seq, tag):
    g = {}
    g["w_down"], = _matmul(s["act"], dx2b, "tn", [F32], f"dw_down{tag}")
    du, = _matmul(dx2b, p["w_down"], "nt", [BF16], f"d_act{tag}", extras=[s["u"]],
                  epilogue=lambda acc, ub: (acc * (2.0 * jnp.maximum(ub, 0.0)),))
    g["w_up"], = _matmul(s["h2"], du, "tn", [F32], f"dw_up{tag}")
    dh2, = _matmul(du, p["w_up"], "nt", [F32], f"d_h2{tag}")
    dx1, dx1b, g["norm_mlp_g"] = _rms_bwd(s["x1"], p["norm_mlp_g"], dh2, dx2, f"rms_mlp_bwd{tag}")
    g["w_out"], = _matmul(s["ycat"], dx1b, "tn", [F32], f"dw_out{tag}")
    dycat, = _matmul(dx1b, p["w_out"], "nt", [F32], f"d_ycat{tag}")
    dy, dz, datt, g["ssd_norm_g"], g["att_norm_g"] = _mixnorm_bwd(
        s["y"], s["proj"], s["att"], p["ssd_norm_g"], p["att_norm_g"], dycat, f"mixnorm_bwd{tag}")
    dq, dk, dv = _att_bwd(s["proj"], s["rtot"], datt, n_seq, f"att_bwd{tag}")
    dxbc, ddtraw, pg = _ssd_bwd(s["xbc"], s["dtraw"], p["dt_bias"], p["a_log"], p["d_skip_x"], s["states"], dy,
                                n_seq, f"ssd_bwd{tag}")
    g["dt_bias"], g["a_log"], g["d_skip"] = pg[0, :N_HEADS], pg[1, :N_HEADS], pg[2, :N_HEADS]
    du_conv, wg = _conv_bwd(s["proj"], p["conv_w"], p["conv_b"], dxbc, n_seq, f"conv_bwd{tag}")
    g["conv_w"], g["conv_b"] = wg[:CONV_WIDTH], wg[CONV_WIDTH]
    dproj = jnp.concatenate([dz.astype(BF16), du_conv.astype(BF16), dq.astype(BF16), dk.astype(BF16),
                             dv.astype(BF16)], axis=1)
    ddtb = ddtraw.astype(BF16)
    g["w_main"], = _matmul(s["h"], dproj, "tn", [F32], f"dw_in{tag}", tn=1536)
    g["w_dt"], = _matmul(s["h"], ddtb, "tn", [F32], f"dw_dt{tag}")
    dh_dt, = _matmul(ddtb, p["w_dt"], "nt", [F32], f"d_h_dt{tag}")
    dh, = _matmul(dproj, p["w_main"], "nt", [F32], f"d_h{tag}", extras=[dh_dt], epilogue=lambda acc, e: (acc + e,))
    dx, dxb, g["norm_mix_g"] = _rms_bwd(s["x"], p["norm_mix_g"], dh, dx1, f"rms_mix_bwd{tag}")
    return dx, dxb, g


def _split_w_in(w_full):
    c0 = D_MODEL + D_CONV
    main = jnp.concatenate([w_full[:, :c0], w_full[:, c0 + N_HEADS:]], axis=1)
    dt = jnp.pad(w_full[:, c0:c0 + N_HEADS], ((0, 0), (0, LANES - N_HEADS)))
    return main, dt


def _merge_w_in(main, dt):
    c0 = D_MODEL + D_CONV
    return jnp.concatenate([main[:, :c0], dt[:, :N_HEADS], main[:, c0:]], axis=1)


def _prep_layer(l, w_in_full, w_out, w_up, w_down, conv_w, small):
    w_main, w_dt = _split_w_in(w_in_full)
    return dict(
        w_main=w_main, w_dt=w_dt, w_out=w_out, w_up=w_up, w_down=w_down, conv_w=conv_w,
        conv_b=small["conv_b"][l].reshape(1, -1),
        dt_bias=_pad_lanes(small["dt_bias"][l]), a_log=_pad_lanes(small["a_log"][l]),
        d_skip_x=jnp.repeat(small["d_skip"][l], HEAD_DIM).reshape(1, -1),
        norm_mix_g=small["norm_mix_g"][l].reshape(1, -1), ssd_norm_g=small["ssd_norm_g"][l].reshape(1, -1),
        att_norm_g=small["att_norm_g"][l].reshape(1, -1), norm_mlp_g=small["norm_mlp_g"][l].reshape(1, -1),
    )


ANY = pl.BlockSpec(memory_space=pl.ANY)


def _place():
    x, y, c = lax.axis_index("x"), lax.axis_index("y"), lax.axis_index("c")
    return x, y, c, (x, y, 1 - c), [(1 - x, y), (x, 1 - y), (1 - x, 1 - y)]


def _all_gather_chips(shard, name):
    r, ccols = shard.shape
    half = r // 2

    def body(s_ref, o_ref, send_sems, recv_sems, local_sem):
        x, y, c, sibling, chips = _place()

        def slab(px, py, hc):
            return o_ref.at[2 * px + py, pl.ds(hc * half, half), :]

        def copy(k, src, dst, to):
            return pltpu.make_async_remote_copy(src_ref=src, dst_ref=dst, send_sem=send_sems.at[k],
                                                recv_sem=recv_sems.at[k], device_id=to, device_id_type=MESH)

        mine = pltpu.make_async_copy(s_ref, o_ref.at[2 * x + y], local_sem)
        mine.start()
        sends = [copy(k, s_ref.at[pl.ds(c * half, half), :], slab(x, y, c), (px, py, c))
                 for k, (px, py) in enumerate(chips)]
        for cp in sends:
            cp.start()
        passed = []
        for k, (px, py) in enumerate(chips):
            copy(k, slab(px, py, c), slab(px, py, c), (px, py, c)).wait_recv()
            cp = copy(3 + k, slab(px, py, c), slab(px, py, c), sibling)
            cp.start()
            passed.append(cp)
        for k, (px, py) in enumerate(chips):
            copy(3 + k, slab(px, py, 1 - c), slab(px, py, 1 - c), sibling).wait_recv()
        for cp in sends + passed:
            cp.wait_send()
        mine.wait()

    return pl.pallas_call(
        body, name=name, in_specs=[ANY], out_specs=ANY,
        out_shape=_sds((N_CHIPS, r, ccols), shard.dtype),
        scratch_shapes=[pltpu.SemaphoreType.DMA((6,)), pltpu.SemaphoreType.DMA((6,)), pltpu.SemaphoreType.DMA],
    )(shard)


def _sibling_swap(g, name):
    n, r, ccols = g.shape
    half = r // 2

    def body(g_ref, o_ref, send_sem, recv_sem):
        _, _, c, sibling, _ = _place()
        cp = pltpu.make_async_remote_copy(src_ref=g_ref.at[:, pl.ds((1 - c) * half, half), :], dst_ref=o_ref,
                                          send_sem=send_sem, recv_sem=recv_sem, device_id=sibling, device_id_type=MESH)
        cp.start()
        cp.wait()

    return pl.pallas_call(
        body, name=name, in_specs=[ANY], out_specs=ANY, out_shape=_sds((n, half, ccols), g.dtype),
        scratch_shapes=[pltpu.SemaphoreType.DMA, pltpu.SemaphoreType.DMA],
    )(g)


def _chip_exchange(p, name):
    _, h, ccols = p.shape

    def body(p_ref, o_ref, send_sems, recv_sems):
        _, _, c, _, chips = _place()
        copies = [pltpu.make_async_remote_copy(src_ref=p_ref.at[2 * px + py], dst_ref=o_ref.at[k],
                                               send_sem=send_sems.at[k], recv_sem=recv_sems.at[k],
                                               device_id=(px, py, c), device_id_type=MESH)
                  for k, (px, py) in enumerate(chips)]
        for cp in copies:
            cp.start()
        for cp in copies:
            cp.wait()

    return pl.pallas_call(
        body, name=name, in_specs=[ANY], out_specs=ANY, out_shape=_sds((3, h, ccols), p.dtype),
        scratch_shapes=[pltpu.SemaphoreType.DMA((3,)), pltpu.SemaphoreType.DMA((3,))],
    )(p)


def _sibling_share(rh, name):
    h, ccols = rh.shape

    def body(r_ref, o_ref, send_sem, recv_sem, local_sem):
        _, _, c, sibling, _ = _place()
        mine = o_ref.at[pl.ds(c * h, h), :]
        local = pltpu.make_async_copy(r_ref, mine, local_sem)
        local.start()
        cp = pltpu.make_async_remote_copy(src_ref=r_ref, dst_ref=mine, send_sem=send_sem, recv_sem=recv_sem,
                                          device_id=sibling, device_id_type=MESH)
        cp.start()
        theirs = o_ref.at[pl.ds((1 - c) * h, h), :]
        pltpu.make_async_remote_copy(src_ref=r_ref, dst_ref=theirs, send_sem=send_sem, recv_sem=recv_sem,
                                     device_id=sibling, device_id_type=MESH).wait_recv()
        cp.wait_send()
        local.wait()

    return pl.pallas_call(
        body, name=name, in_specs=[ANY], out_specs=ANY, out_shape=_sds((2 * h, ccols), rh.dtype),
        scratch_shapes=[pltpu.SemaphoreType.DMA, pltpu.SemaphoreType.DMA, pltpu.SemaphoreType.DMA],
    )(rh)


def _add_halves(g, a, c, name):
    n, r, ccols = g.shape
    half = r // 2
    tr = _tile(half, 256)
    nb = half // tr

    def body(c_ref, g_ref, a_ref, o_ref):
        o_ref[...] = g_ref[...] + a_ref[...]

    blk = (1, tr, ccols)
    return pl.pallas_call(
        body, name=name,
        grid_spec=pltpu.PrefetchScalarGridSpec(
            num_scalar_prefetch=1, grid=(n, nb),
            in_specs=[pl.BlockSpec(blk, lambda j, i, c_ref: (j, c_ref[0] * nb + i, 0)),
                      pl.BlockSpec(blk, lambda j, i, c_ref: (j, i, 0))],
            out_specs=pl.BlockSpec(blk, lambda j, i, c_ref: (j, i, 0))),
        out_shape=_sds((n, half, ccols)),
        compiler_params=_params("parallel", "parallel"),
    )(c.reshape(1).astype(jnp.int32), g, a)


def _sum_chips(p, got, chip, name):
    _, h, ccols = p.shape
    tr = _tile(h, 256)

    def body(j_ref, p_ref, a_ref, b_ref, c_ref, o_ref):
        o_ref[...] = ((p_ref[...] + a_ref[...]) + b_ref[...]) + c_ref[...]

    blk = (1, tr, ccols)

    def slot(k):
        return pl.BlockSpec(blk, lambda i, j_ref: (k, i, 0))

    out = pl.pallas_call(
        body, name=name,
        grid_spec=pltpu.PrefetchScalarGridSpec(
            num_scalar_prefetch=1, grid=(h // tr,),
            in_specs=[pl.BlockSpec(blk, lambda i, j_ref: (j_ref[0], i, 0)), slot(0), slot(1), slot(2)],
            out_specs=pl.BlockSpec(blk, lambda i, j_ref: (0, i, 0))),
        out_shape=_sds((1, h, ccols)),
        compiler_params=_params("parallel"),
    )(chip.reshape(1).astype(jnp.int32), p, got, got, got)
    return out[0]


def _reduce_scatter(g, chip, c, name):
    a = _sibling_swap(g, f"{name}_swap")
    p = _add_halves(g, a, c, f"{name}_add")
    got = _chip_exchange(p, f"{name}_xchg")
    rh = _sum_chips(p, got, chip, f"{name}_sum")
    return _sibling_share(rh, f"{name}_share")


def _all_reduce_small(v, name):
    r = v.shape[0]

    def body(v_ref, o_ref, buf, send_sems, recv_sems):
        x, y, c, _, _ = _place()
        buf[0] = v_ref[...]
        copies = []
        for rel in range(1, 8):
            fx, fy, fc = (rel >> 2) & 1, (rel >> 1) & 1, rel & 1
            peer = (1 - x if fx else x, 1 - y if fy else y, 1 - c if fc else c)
            cp = pltpu.make_async_remote_copy(src_ref=v_ref, dst_ref=buf.at[rel], send_sem=send_sems.at[rel - 1],
                                              recv_sem=recv_sems.at[rel - 1], device_id=peer, device_id_type=MESH)
            cp.start()
            copies.append(cp)
        for cp in copies:
            cp.wait()
        me = 4 * x + 2 * y + c
        acc = buf[jnp.bitwise_xor(me, 0)]
        for src in range(1, 8):
            acc = acc + buf[jnp.bitwise_xor(me, src)]
        o_ref[...] = acc

    vm = pl.BlockSpec(memory_space=pltpu.VMEM)
    return pl.pallas_call(
        body, name=name, in_specs=[vm], out_specs=vm, out_shape=_sds((r, LANES)),
        scratch_shapes=[pltpu.VMEM((8, r, LANES), F32), pltpu.SemaphoreType.DMA((7,)), pltpu.SemaphoreType.DMA((7,))],
    )(v)


WEIGHTS = ["norm_mix_g", "w_in", "conv_w", "conv_b", "dt_bias", "a_log", "d_skip", "ssd_norm_g", "att_norm_g",
           "w_out", "norm_mlp_g", "w_up", "w_down", "final_norm_g"]
BIG = ["w_in", "w_out", "w_up", "w_down"]
SMALL = [n for n in WEIGHTS if n not in BIG]


def _pack(arrays):
    flat = []
    for a in arrays:
        a = a.reshape(-1)
        flat.append(jnp.pad(a, (0, (-a.shape[0]) % LANES)))
    flat = jnp.concatenate(flat)
    flat = jnp.pad(flat, (0, (-flat.shape[0]) % (8 * LANES)))
    return flat.reshape(-1, LANES)


def _unpack(packed, shapes):
    flat, out, pos = packed.reshape(-1), [], 0
    for shp in shapes:
        n = math.prod(shp)
        out.append(flat[pos:pos + n].reshape(shp))
        pos += n + (-n) % LANES
    return out


def _to_shards(name, g):
    if name in ("w_in", "w_up"):
        l, r, ccols = g.shape
        return g.reshape(l, r, N_CHIPS, ccols // N_CHIPS).transpose(2, 0, 1, 3).reshape(N_CHIPS, l * r, ccols // N_CHIPS)
    l, r, ccols = g.shape
    return g.reshape(l, N_CHIPS, r // N_CHIPS, ccols).transpose(1, 0, 2, 3).reshape(N_CHIPS, l * r // N_CHIPS, ccols)


def _from_gathered(name, g, l):
    rows = g.shape[1] // DEPTH
    part = g[:, l * rows:(l + 1) * rows, :]
    if name in ("w_in", "w_up", "conv_w"):
        return part.transpose(1, 0, 2).reshape(rows, N_CHIPS * g.shape[2])
    return part.reshape(N_CHIPS * rows, g.shape[2])


def kernel(x, norm_mix_g, w_in, conv_w, conv_b, dt_bias, a_log, d_skip, ssd_norm_g, att_norm_g, w_out, norm_mlp_g, w_up, w_down, final_norm_g, loss_target, m_norm_mix_g, m_w_in, m_conv_w, m_conv_b, m_dt_bias, m_a_log, m_d_skip, m_ssd_norm_g, m_att_norm_g, m_w_out, m_norm_mlp_g, m_w_up, m_w_down, m_final_norm_g, v_norm_mix_g, v_w_in, v_conv_w, v_conv_b, v_dt_bias, v_a_log, v_d_skip, v_ssd_norm_g, v_att_norm_g, v_w_out, v_norm_mlp_g, v_w_up, v_w_down, v_final_norm_g):
    w = dict(norm_mix_g=norm_mix_g, w_in=w_in, conv_w=conv_w, conv_b=conv_b, dt_bias=dt_bias, a_log=a_log,
             d_skip=d_skip, ssd_norm_g=ssd_norm_g, att_norm_g=att_norm_g, w_out=w_out, norm_mlp_g=norm_mlp_g,
             w_up=w_up, w_down=w_down, final_norm_g=final_norm_g)
    m = dict(norm_mix_g=m_norm_mix_g, w_in=m_w_in, conv_w=m_conv_w, conv_b=m_conv_b, dt_bias=m_dt_bias,
             a_log=m_a_log, d_skip=m_d_skip, ssd_norm_g=m_ssd_norm_g, att_norm_g=m_att_norm_g, w_out=m_w_out,
             norm_mlp_g=m_norm_mlp_g, w_up=m_w_up, w_down=m_w_down, final_norm_g=m_final_norm_g)
    v = dict(norm_mix_g=v_norm_mix_g, w_in=v_w_in, conv_w=v_conv_w, conv_b=v_conv_b, dt_bias=v_dt_bias,
             a_log=v_a_log, d_skip=v_d_skip, ssd_norm_g=v_ssd_norm_g, att_norm_g=v_att_norm_g, w_out=v_w_out,
             norm_mlp_g=v_norm_mlp_g, w_up=v_w_up, w_down=v_w_down, final_norm_g=v_final_norm_g)
    n_seq, seq, d = x.shape
    t = n_seq * seq
    chip = 2 * lax.axis_index("x") + lax.axis_index("y")
    core = lax.axis_index("c")

    gathered = {n: _all_gather_chips(w[n].astype(BF16).reshape(-1, w[n].shape[-1]), f"gather_{n}") for n in BIG}
    gathered["conv_w"] = _all_gather_chips(conv_w.reshape(-1, conv_w.shape[-1]), "gather_conv_w")
    layers = [_prep_layer(l, *[_from_gathered(n, gathered[n], l) for n in BIG + ["conv_w"]], w) for l in range(DEPTH)]

    xs = x.reshape(t, d)
    saved = []
    for l in range(DEPTH):
        xs, s = _layer_fwd(xs, layers[l], n_seq, f"_l{l}")
        saved.append(s)
    loss_vec, dx, dxb, g_final = _final_loss(xs, final_norm_g.reshape(1, d), loss_target.reshape(t, d), "final_loss")
    loss = lax.psum(loss_vec[0, 0], ("x", "y", "c"))

    grads = [None] * DEPTH
    for l in reversed(range(DEPTH)):
        dx, dxb, grads[l] = _layer_bwd(dx, dxb, layers[l], saved[l], n_seq, f"_l{l}")
        grads[l]["w_in"] = _merge_w_in(grads[l].pop("w_main"), grads[l].pop("w_dt"))
    grad_x = dx.reshape(n_seq, seq, d)

    full = {n: jnp.stack([grads[l][n] for l in range(DEPTH)]) for n in WEIGHTS if n != "final_norm_g"}
    full["final_norm_g"] = g_final.reshape(d)
    g_out = {}
    for n in BIG:
        red = _reduce_scatter(_to_shards(n, full[n]), chip, core, f"rs_{n}")
        g_out[n] = red.reshape(w[n].shape)
    small_sum = _all_reduce_small(_pack([full[n] for n in SMALL]), "allreduce_small")
    small_shapes = [(DEPTH, CONV_WIDTH, D_CONV) if n == "conv_w" else w[n].shape for n in SMALL]
    for n, val in zip(SMALL, _unpack(small_sum, small_shapes)):
        g_out[n] = val
    g_out["conv_w"] = lax.dynamic_slice_in_dim(g_out["conv_w"], chip * conv_w.shape[-1], conv_w.shape[-1], axis=2)

    delta, new_m, new_v = {}, {}, {}
    for n in BIG:
        two_d = (-1, w[n].shape[-1])
        dl, mn, vn = _adamw(w[n].reshape(two_d), g_out[n].reshape(two_d), m[n].reshape(two_d), v[n].reshape(two_d),
                            f"adamw_{n}")
        delta[n], new_m[n], new_v[n] = dl.reshape(w[n].shape), mn.reshape(w[n].shape), vn.reshape(w[n].shape)
    packs = [_pack([src[n] for n in SMALL]) for src in (w, g_out, m, v)]
    shapes = [w[n].shape for n in SMALL]
    for dst, packed in zip((delta, new_m, new_v), _adamw(*packs, "adamw_small")):
        for n, val in zip(SMALL, _unpack(packed, shapes)):
            dst[n] = val

    return (loss, grad_x, *[g_out[n] for n in WEIGHTS], *[delta[n] for n in WEIGHTS],
            *[new_m[n] for n in WEIGHTS], *[new_v[n] for n in WEIGHTS])
```

```python
import functools
import math

import jax
import jax.numpy as jnp
from jax import lax
from jax.experimental import pallas as pl
from jax.experimental.pallas import tpu as pltpu

F32 = jnp.float32
BF16 = jnp.bfloat16
HIGHEST = lax.Precision.HIGHEST

D_MODEL = 1024
DEPTH = 4
HEAD_DIM = 64
N_HEADS = 16
N_GROUPS = 4
N_STATE = 128
N_PAIRS = N_HEADS // 2
CONV_WIDTH = 4
CHUNK = 128
D_CONV = D_MODEL + 2 * N_GROUPS * N_STATE
D_MAIN = D_MODEL + D_CONV + 3 * D_MODEL
D_IN_PROJ = D_MAIN + N_HEADS
D_FF = 4 * D_MODEL
EPS = 1e-5
LANES = 128
VMEM_LIMIT = 48 * 1024 * 1024

ADAM_LR = 0.001
ADAM_B1 = 0.9
ADAM_B2 = 0.999
ADAM_EPS = 1e-08
ADAM_WD = 0.01
ADAM_STEP = 10

N_CHIPS = 4
MESH = pl.DeviceIdType.MESH


def _tile(n, cap):
    if n <= cap:
        return n
    t = cap
    while t >= 8:
        if n % t == 0:
            return t
        t //= 2
    raise ValueError(f"no tile for {n} under {cap}")


def _params(*sem):
    return pltpu.CompilerParams(dimension_semantics=sem, vmem_limit_bytes=VMEM_LIMIT)


_DIMS = {"nn": (((1,), (0,)), ((), ())), "nt": (((1,), (1,)), ((), ())), "tn": (((0,), (0,)), ((), ()))}


def _matmul(a, b, mode, out_dtypes, name, extras=(), epilogue=None, tm=1024, tn=1024, tk=1024):
    if mode == "nn":
        (m, k), (_, n) = a.shape, b.shape
    elif mode == "nt":
        (m, k), (n, _) = a.shape, b.shape
    else:
        (k, m), (_, n) = a.shape, b.shape
    tm, tn, tk = _tile(m, tm), _tile(n, tn), _tile(k, tk)
    nk = k // tk
    if mode == "tn":
        a_spec = pl.BlockSpec((tk, tm), lambda i, j, kk: (kk, i))
    else:
        a_spec = pl.BlockSpec((tm, tk), lambda i, j, kk: (i, kk))
    if mode == "nt":
        b_spec = pl.BlockSpec((tn, tk), lambda i, j, kk: (j, kk))
    else:
        b_spec = pl.BlockSpec((tk, tn), lambda i, j, kk: (kk, j))
    mn_spec = pl.BlockSpec((tm, tn), lambda i, j, kk: (i, j))
    n_extra, n_out = len(extras), len(out_dtypes)
    dims = _DIMS[mode]

    def body(a_ref, b_ref, *rest):
        extra_refs, out_refs, acc = rest[:n_extra], rest[n_extra:n_extra + n_out], rest[-1]
        kk = pl.program_id(2)

        @pl.when(kk == 0)
        def _():
            acc[...] = jnp.zeros_like(acc)

        acc[...] += lax.dot_general(a_ref[...].astype(BF16), b_ref[...].astype(BF16), dims,
                                    preferred_element_type=F32)

        @pl.when(kk == nk - 1)
        def _():
            res = acc[...]
            outs = epilogue(res, *[e[...] for e in extra_refs]) if epilogue is not None else (res,)
            for o_ref, val in zip(out_refs, outs):
                o_ref[...] = val.astype(o_ref.dtype)

    outs = pl.pallas_call(
        body, name=name, grid=(m // tm, n // tn, nk),
        in_specs=[a_spec, b_spec] + [mn_spec] * n_extra,
        out_specs=[mn_spec] * n_out,
        out_shape=[jax.ShapeDtypeStruct((m, n), dt) for dt in out_dtypes],
        scratch_shapes=[pltpu.VMEM((tm, tn), F32)],
        compiler_params=_params("parallel", "parallel", "arbitrary"),
    )(a, b, *extras)
    return tuple(outs)


def _rowwise(fn, rows, vecs, out_rows, out_accs, name, tm=256):
    t = rows[0].shape[0]
    tm = _tile(t, tm)
    n_rows, n_vecs, n_or, n_oa = len(rows), len(vecs), len(out_rows), len(out_accs)

    def body(*refs):
        ins = [r[...] for r in refs[:n_rows + n_vecs]]
        outs = fn(*ins)
        o_refs = refs[n_rows + n_vecs:]
        for o_ref, val in zip(o_refs[:n_or], outs[:n_or]):
            o_ref[...] = val.astype(o_ref.dtype)

        @pl.when(pl.program_id(0) == 0)
        def _():
            for o_ref in o_refs[n_or:]:
                o_ref[...] = jnp.zeros_like(o_ref)

        for o_ref, val in zip(o_refs[n_or:], outs[n_or:]):
            o_ref[...] += val

    outs = pl.pallas_call(
        body, name=name, grid=(t // tm,),
        in_specs=[pl.BlockSpec((tm, r.shape[1]), lambda i: (i, 0)) for r in rows]
        + [pl.BlockSpec(v.shape, lambda i: (0, 0)) for v in vecs],
        out_specs=[pl.BlockSpec((tm, o.shape[1]), lambda i: (i, 0)) for o in out_rows]
        + [pl.BlockSpec(o.shape, lambda i: (0, 0)) for o in out_accs],
        out_shape=list(out_rows) + list(out_accs),
        compiler_params=_params("arbitrary" if n_oa else "parallel"),
    )(*rows, *vecs)
    return tuple(outs)


def _stack_rows(parts, width):
    rows = lax.broadcasted_iota(jnp.int32, (8, width), 0)
    out = jnp.zeros((8, width), F32)
    for j, part in enumerate(parts):
        out = out + jnp.where(rows == j, part, 0.0)
    return out


def _sds(shape, dtype=F32):
    return jax.ShapeDtypeStruct(shape, dtype)


def _sigmoid(x):
    return 1.0 / (1.0 + jnp.exp(-x))


def _colsum(x):
    return jnp.sum(x, axis=0, keepdims=True)


def _rstd(x):
    return lax.rsqrt(jnp.mean(x * x, axis=-1, keepdims=True) + EPS)


def _rms_bwd_rows(xhat, r, g, dy):
    dxh = dy * g
    dx = r * (dxh - xhat * jnp.mean(dxh * xhat, axis=-1, keepdims=True))
    return dx, _colsum(dy * xhat)


def _rms_fwd(x, g, name):
    t, d = x.shape

    def fn(xb, gv):
        return (xb * _rstd(xb) * gv,)

    return _rowwise(fn, [x], [g], [_sds((t, d), BF16)], [], name)[0]


def _rms_bwd(x, g, dh, dres, name):
    t, d = x.shape

    def fn(xb, dhb, dresb, gv):
        r = _rstd(xb)
        dx, dg = _rms_bwd_rows(xb * r, r, gv, dhb)
        dx = dx + dresb
        return dx, dx, dg

    return _rowwise(fn, [x, dh, dres], [g], [_sds((t, d)), _sds((t, d), BF16)], [_sds((1, d))], name)


def _mixnorm_fwd(y, z, att, gs, ga, name):
    t, d = y.shape

    def fn(yb, zb, ab, gsv, gav):
        yg = yb * (zb * _sigmoid(zb))
        return (jnp.concatenate([yg * _rstd(yg) * gsv, ab * _rstd(ab) * gav], axis=1),)

    tm = _tile(t, 256)

    def body(y_ref, z_ref, a_ref, gs_ref, ga_ref, o_ref):
        o_ref[...] = fn(y_ref[...], z_ref[...], a_ref[...], gs_ref[...], ga_ref[...])[0].astype(BF16)

    row = pl.BlockSpec((tm, d), lambda i: (i, 0))
    vec = pl.BlockSpec((1, d), lambda i: (0, 0))
    out = pl.pallas_call(
        body, name=name, grid=(t // tm,),
        in_specs=[row, row, row, vec, vec],
        out_specs=pl.BlockSpec((tm, 2 * d), lambda i: (i, 0)),
        out_shape=_sds((t, 2 * d), BF16),
        compiler_params=_params("parallel"),
    )(y, z, att, gs, ga)
    return out


def _mixnorm_bwd(y, z, att, gs, ga, dycat, name):
    t, d = y.shape
    tm = _tile(t, 256)

    def body(y_ref, z_ref, a_ref, dyc_ref, gs_ref, ga_ref, dy_ref, dz_ref, da_ref, dgs_ref, dga_ref):
        yb, zb, ab = y_ref[...], z_ref[...], a_ref[...]
        dys, dya = dyc_ref[:, :d], dyc_ref[:, d:]
        sz = _sigmoid(zb)
        silu = zb * sz
        yg = yb * silu
        r = _rstd(yg)
        dyg, dgs = _rms_bwd_rows(yg * r, r, gs_ref[...], dys)
        dy_ref[...] = dyg * silu
        dz_ref[...] = dyg * yb * (sz * (1.0 + zb * (1.0 - sz)))
        r2 = _rstd(ab)
        datt, dga = _rms_bwd_rows(ab * r2, r2, ga_ref[...], dya)
        da_ref[...] = datt

        @pl.when(pl.program_id(0) == 0)
        def _():
            dgs_ref[...] = jnp.zeros_like(dgs_ref)
            dga_ref[...] = jnp.zeros_like(dga_ref)

        dgs_ref[...] += dgs
        dga_ref[...] += dga

    row = pl.BlockSpec((tm, d), lambda i: (i, 0))
    vec = pl.BlockSpec((1, d), lambda i: (0, 0))
    return pl.pallas_call(
        body, name=name, grid=(t // tm,),
        in_specs=[row, row, row, pl.BlockSpec((tm, 2 * d), lambda i: (i, 0)), vec, vec],
        out_specs=[row, row, row, vec, vec],
        out_shape=[_sds((t, d)), _sds((t, d)), _sds((t, d)), _sds((1, d)), _sds((1, d))],
        compiler_params=_params("arbitrary"),
    )(y, z, att, dycat, gs, ga)


def _final_loss(x, g, target, name):
    t, d = x.shape

    def fn(xb, tb, gv):
        r = _rstd(xb)
        xhat = xb * r
        err = xhat * gv - tb
        loss = 0.5 * jnp.sum(jnp.mean(err * err, axis=-1, keepdims=True), axis=0, keepdims=True)
        dx, dg = _rms_bwd_rows(xhat, r, gv, err * (1.0 / d))
        return dx, dx, jnp.broadcast_to(loss, (1, LANES)), dg

    dx, dxb, loss, dg = _rowwise(fn, [x, target], [g], [_sds((t, d)), _sds((t, d), BF16)],
                                 [_sds((1, LANES)), _sds((1, d))], name)
    return loss, dx, dxb, dg


def _adamw(w, g, m, v, name):
    c1 = 1.0 - ADAM_B1 ** ADAM_STEP
    c2 = 1.0 - ADAM_B2 ** ADAM_STEP

    def fn(wb, gb, mb, vb):
        mn = ADAM_B1 * mb + (1.0 - ADAM_B1) * gb
        vn = ADAM_B2 * vb + (1.0 - ADAM_B2) * (gb * gb)
        delta = -ADAM_LR * ((mn / c1) / (jnp.sqrt(vn / c2) + ADAM_EPS) + ADAM_WD * wb)
        return delta, mn, vn

    return _rowwise(fn, [w, g, m, v], [], [_sds(w.shape)] * 3, [], name)


CONV_CB = 512


def _shift_down(u, k):
    if k == 0:
        return u
    rows = lax.broadcasted_iota(jnp.int32, u.shape, 0)
    return jnp.where(rows >= k, pltpu.roll(u, k, 0), 0.0)


def _shift_up(u, k):
    if k == 0:
        return u
    n = u.shape[0]
    rows = lax.broadcasted_iota(jnp.int32, u.shape, 0)
    return jnp.where(rows < n - k, pltpu.roll(u, n - k, 0), 0.0)


def _conv_pre(u, w, b):
    pre = b
    for j in range(CONV_WIDTH):
        pre = pre + w[j:j + 1, :] * _shift_down(u, CONV_WIDTH - 1 - j)
    return pre


def _conv_fwd(proj, w, b, n_seq, name):
    t = proj.shape[0]
    seq = t // n_seq
    off = D_MODEL // CONV_CB

    def body(u_ref, w_ref, b_ref, o_ref):
        pre = _conv_pre(u_ref[...], w_ref[...], b_ref[...])
        o_ref[...] = pre * _sigmoid(pre)

    return pl.pallas_call(
        body, name=name, grid=(n_seq, D_CONV // CONV_CB),
        in_specs=[pl.BlockSpec((seq, CONV_CB), lambda s, c: (s, c + off)),
                  pl.BlockSpec((CONV_WIDTH, CONV_CB), lambda s, c: (0, c)),
                  pl.BlockSpec((1, CONV_CB), lambda s, c: (0, c))],
        out_specs=pl.BlockSpec((seq, CONV_CB), lambda s, c: (s, c)),
        out_shape=_sds((t, D_CONV)),
        compiler_params=_params("parallel", "parallel"),
    )(proj, w, b)


def _conv_bwd(proj, w, b, dxbc, n_seq, name):
    t = proj.shape[0]
    seq = t // n_seq
    off = D_MODEL // CONV_CB

    def body(u_ref, w_ref, b_ref, d_ref, du_ref, wg_ref):
        u, wv = u_ref[...], w_ref[...]
        pre = _conv_pre(u, wv, b_ref[...])
        s = _sigmoid(pre)
        dpre = d_ref[...] * (s * (1.0 + pre * (1.0 - s)))
        du = jnp.zeros_like(u)
        parts = []
        for j in range(CONV_WIDTH):
            k = CONV_WIDTH - 1 - j
            du = du + wv[j:j + 1, :] * _shift_up(dpre, k)
            parts.append(_colsum(dpre * _shift_down(u, k)))
        du_ref[...] = du
        parts.append(_colsum(dpre))

        @pl.when(pl.program_id(1) == 0)
        def _():
            wg_ref[...] = jnp.zeros_like(wg_ref)

        wg_ref[...] += _stack_rows(parts, u.shape[1])

    return pl.pallas_call(
        body, name=name, grid=(D_CONV // CONV_CB, n_seq),
        in_specs=[pl.BlockSpec((seq, CONV_CB), lambda c, s: (s, c + off)),
                  pl.BlockSpec((CONV_WIDTH, CONV_CB), lambda c, s: (0, c)),
                  pl.BlockSpec((1, CONV_CB), lambda c, s: (0, c)),
                  pl.BlockSpec((seq, CONV_CB), lambda c, s: (s, c))],
        out_specs=[pl.BlockSpec((seq, CONV_CB), lambda c, s: (s, c)),
                   pl.BlockSpec((8, CONV_CB), lambda c, s: (0, c))],
        out_shape=[_sds((t, D_CONV)), _sds((8, D_CONV))],
        compiler_params=_params("parallel", "arbitrary"),
    )(proj, w, b, dxbc)


def _iota2(shape, axis):
    return lax.broadcasted_iota(jnp.int32, shape, axis)


def _dot(a, b, dims=_DIMS["nn"], precision=None):
    return lax.dot_general(a, b, dims, precision=precision, preferred_element_type=F32)


def _bdot(a, b, dims=_DIMS["nn"]):
    return lax.dot_general(a.astype(BF16), b.astype(BF16), dims, preferred_element_type=F32)


def _expand_mat():
    return (_iota2((LANES, D_MODEL), 0) == lax.shift_right_logical(_iota2((LANES, D_MODEL), 1), 6)).astype(F32)


def _reduce_mat():
    return (lax.shift_right_logical(_iota2((D_MODEL, LANES), 0), 6) == _iota2((D_MODEL, LANES), 1)).astype(F32)


def _ssd_decay(dtraw, bias, alog):
    row, col = _iota2((CHUNK, CHUNK), 0), _iota2((CHUNK, CHUNK), 1)
    pre = dtraw + bias
    dtb = jnp.maximum(pre, 0.0) + jnp.log(1.0 + jnp.exp(-jnp.abs(pre)))
    a_neg = -jnp.exp(alog)
    a = dtb * a_neg
    tril = (row >= col).astype(F32)
    triu = (row <= col).astype(F32)
    cs = _dot(tril, a, precision=HIGHEST)
    cs_t = _dot(a, triu, _DIMS["tn"], precision=HIGHEST)
    return pre, dtb, a_neg, cs, cs_t


def _pair_rowscale(vec, h0):
    top = _iota2((CHUNK, LANES), 0) < HEAD_DIM
    return jnp.where(top, vec[:, h0:h0 + 1], vec[:, h0 + 1:h0 + 2])


def _decay_mat(cs, cs_t, h):
    row, col = _iota2((CHUNK, CHUNK), 0), _iota2((CHUNK, CHUNK), 1)
    seg = cs[:, h:h + 1] - cs_t[h:h + 1, :]
    return jnp.exp(jnp.where(row >= col, seg, -jnp.inf))


def _ssd_fwd(xbc, dtraw, bias, alog, dskip_x, n_seq, name):
    t = xbc.shape[0]
    nc = t // n_seq // CHUNK

    def body(x_ref, b_ref, c_ref, dt_ref, bias_ref, alog_ref, dsk_ref, y_ref, st_ref, h_ref):
        @pl.when(pl.program_id(1) == 0)
        def _():
            h_ref[...] = jnp.zeros_like(h_ref)

        _, dtb, _, cs, cs_t = _ssd_decay(dt_ref[...], bias_ref[...], alog_ref[...])
        expand = _expand_mat()
        dt_x = _dot(dtb, expand, precision=HIGHEST)
        cs_x = _dot(cs, expand, precision=HIGHEST)
        tot = cs[CHUNK - 1:CHUNK, :]
        etot = jnp.exp(tot)
        x = x_ref[...]
        xdt = x * dt_x
        e_x = jnp.exp(cs_x)
        xdec = xdt * jnp.exp(cs_x[CHUNK - 1:CHUNK, :] - cs_x)
        keeps = [_iota2((CHUNK, LANES), 1) < HEAD_DIM, _iota2((CHUNK, LANES), 1) >= HEAD_DIM]
        for q in range(N_PAIRS):
            g = q // 2
            lanes = slice(q * LANES, (q + 1) * LANES)
            bg = b_ref[:, g * N_STATE:(g + 1) * N_STATE]
            cg = c_ref[:, g * N_STATE:(g + 1) * N_STATE]
            gmat = _bdot(cg, bg, _DIMS["nt"])
            xdt_q = xdt[:, lanes]
            ydiag = jnp.zeros((CHUNK, LANES), F32)
            for r in range(2):
                w = gmat * _decay_mat(cs, cs_t, 2 * q + r)
                ydiag = ydiag + _bdot(w, jnp.where(keeps[r], xdt_q, 0.0))
            prev = h_ref[q]
            yoff = _bdot(cg, prev, _DIMS["nt"]) * e_x[:, lanes]
            y_ref[:, lanes] = ydiag + yoff + x[:, lanes] * dsk_ref[:, lanes]
            st_ref[0, q] = prev
            h_ref[q] = prev * _pair_rowscale(etot, 2 * q) + _bdot(xdec[:, lanes], bg, _DIMS["tn"])

    vec = pl.BlockSpec((1, LANES), lambda s, c: (0, 0))
    return pl.pallas_call(
        body, name=name, grid=(n_seq, nc),
        in_specs=[pl.BlockSpec((CHUNK, D_MODEL), lambda s, c: (s * nc + c, 0)),
                  pl.BlockSpec((CHUNK, N_GROUPS * N_STATE), lambda s, c: (s * nc + c, 2)),
                  pl.BlockSpec((CHUNK, N_GROUPS * N_STATE), lambda s, c: (s * nc + c, 3)),
                  pl.BlockSpec((CHUNK, LANES), lambda s, c: (s * nc + c, 0)),
                  vec, vec, pl.BlockSpec((1, D_MODEL), lambda s, c: (0, 0))],
        out_specs=[pl.BlockSpec((CHUNK, D_MODEL), lambda s, c: (s * nc + c, 0)),
                   pl.BlockSpec((1, N_PAIRS, LANES, N_STATE), lambda s, c: (s * nc + c, 0, 0, 0))],
        out_shape=[_sds((t, D_MODEL)), _sds((t // CHUNK, N_PAIRS, LANES, N_STATE))],
        scratch_shapes=[pltpu.VMEM((N_PAIRS, LANES, N_STATE), F32)],
        compiler_params=_params("parallel", "arbitrary"),
    )(xbc, xbc, xbc, dtraw, bias, alog, dskip_x)


def _ssd_bwd(xbc, dtraw, bias, alog, dskip_x, states, dy, n_seq, name):
    t = xbc.shape[0]
    nc = t // n_seq // CHUNK

    def body(x_ref, b_ref, c_ref, dt_ref, bias_ref, alog_ref, dsk_ref, st_ref, dy_ref,
             dxbc_ref, ddt_ref, pg_ref, dh_ref):
        first = jnp.logical_and(pl.program_id(0) == 0, pl.program_id(1) == 0)

        @pl.when(first)
        def _():
            pg_ref[...] = jnp.zeros_like(pg_ref)

        @pl.when(pl.program_id(1) == 0)
        def _():
            dh_ref[...] = jnp.zeros_like(dh_ref)

        row, col = _iota2((CHUNK, CHUNK), 0), _iota2((CHUNK, CHUNK), 1)
        pre, dtb, a_neg, cs, cs_t = _ssd_decay(dt_ref[...], bias_ref[...], alog_ref[...])
        expand, reduce = _expand_mat(), _reduce_mat()
        dt_x = _dot(dtb, expand, precision=HIGHEST)
        cs_x = _dot(cs, expand, precision=HIGHEST)
        etot = jnp.exp(cs[CHUNK - 1:CHUNK, :])
        x, dy = x_ref[...], dy_ref[...]
        xdt = x * dt_x
        e_x = jnp.exp(cs_x)
        dec_x = jnp.exp(cs_x[CHUNK - 1:CHUNK, :] - cs_x)
        xdec = xdt * dec_x
        dye = dy * e_x
        keeps = [_iota2((CHUNK, LANES), 1) < HEAD_DIM, _iota2((CHUNK, LANES), 1) >= HEAD_DIM]
        row_lo = _iota2((CHUNK, LANES), 0) < HEAD_DIM
        dcs_col = jnp.zeros((CHUNK, LANES), F32)
        dcs_row = jnp.zeros((LANES, CHUNK), F32)
        dtot = jnp.zeros((1, LANES), F32)
        dxdt_parts, zdec_parts, yoff_parts = [], [], []
        for g in range(N_GROUPS):
            bg = b_ref[:, g * N_STATE:(g + 1) * N_STATE]
            cg = c_ref[:, g * N_STATE:(g + 1) * N_STATE]
            gmat = _bdot(cg, bg, _DIMS["nt"])
            dgmat = jnp.zeros((CHUNK, CHUNK), F32)
            dbg = jnp.zeros((CHUNK, N_STATE), F32)
            dcg = jnp.zeros((CHUNK, N_STATE), F32)
            for q in (2 * g, 2 * g + 1):
                lanes = slice(q * LANES, (q + 1) * LANES)
                xdt_q, dy_q = xdt[:, lanes], dy[:, lanes]
                dxdt_q = jnp.zeros((CHUNK, LANES), F32)
                for r in range(2):
                    h = 2 * q + r
                    keep = keeps[r]
                    lm = _decay_mat(cs, cs_t, h)
                    w = gmat * lm
                    dy_h = jnp.where(keep, dy_q, 0.0)
                    dm = _bdot(dy_h, xdt_q, _DIMS["nt"])
                    dxdt_q = dxdt_q + _bdot(w, dy_h, _DIMS["tn"])
                    dgmat = dgmat + dm * lm
                    tmat = dm * w
                    dcs_col = dcs_col + jnp.where(col == h, jnp.sum(tmat, axis=1, keepdims=True), 0.0)
                    dcs_row = dcs_row - jnp.where(row == h, jnp.sum(tmat, axis=0, keepdims=True), 0.0)
                prev = st_ref[0, q]
                dht = dh_ref[q]
                dxdtdec = _bdot(bg, dht, _DIMS["nt"])
                dxdt_q = dxdt_q + dxdtdec * dec_x[:, lanes]
                zdec_parts.append(dxdtdec * xdec[:, lanes])
                dbg = dbg + _bdot(xdec[:, lanes], dht)
                yoff_parts.append(dy_q * (_bdot(cg, prev, _DIMS["nt"]) * e_x[:, lanes]))
                dcg = dcg + _bdot(dye[:, lanes], prev)
                dprev = _bdot(dye[:, lanes], cg, _DIMS["tn"])
                hp = jnp.sum(dht * prev, axis=1, keepdims=True)
                d0 = jnp.sum(jnp.where(row_lo[:, :1], hp, 0.0), axis=0, keepdims=True)
                d1 = jnp.sum(jnp.where(row_lo[:, :1], 0.0, hp), axis=0, keepdims=True)
                lane1 = _iota2((1, LANES), 1)
                dtot = dtot + jnp.where(lane1 == 2 * q, d0, 0.0) + jnp.where(lane1 == 2 * q + 1, d1, 0.0)
                dh_ref[q] = dprev + dht * _pair_rowscale(etot, 2 * q)
                dxdt_parts.append(dxdt_q)
            dcg = dcg + _bdot(dgmat, bg)
            dbg = dbg + _bdot(dgmat, cg, _DIMS["tn"])
            dxbc_ref[:, D_MODEL + g * N_STATE:D_MODEL + (g + 1) * N_STATE] = dbg
            dxbc_ref[:, D_MODEL + (N_GROUPS + g) * N_STATE:D_MODEL + (N_GROUPS + g + 1) * N_STATE] = dcg
        dxdt = jnp.concatenate(dxdt_parts, axis=1)
        zdec = _dot(jnp.concatenate(zdec_parts, axis=1), reduce, precision=HIGHEST)
        yoff_d = _dot(jnp.concatenate(yoff_parts, axis=1), reduce, precision=HIGHEST)
        dtot = dtot * etot + _colsum(zdec)
        last = row[:, :LANES] == CHUNK - 1
        dcs_col = dcs_col + yoff_d - zdec + jnp.where(last, dtot, 0.0)
        triu = (row <= col).astype(F32)
        da = _dot(triu, dcs_col, precision=HIGHEST) + _dot(triu, dcs_row, _DIMS["nt"], precision=HIGHEST)
        ddt = _dot(dxdt * x, reduce, precision=HIGHEST) + da * a_neg
        ddtraw = ddt * _sigmoid(pre)
        ddt_ref[...] = ddtraw
        dxbc_ref[:, :D_MODEL] = dxdt * dt_x + dy * dsk_ref[...]
        dskip = _dot(jnp.broadcast_to(_colsum(dy * x), (8, D_MODEL)), reduce, precision=HIGHEST)[0:1, :]
        pg_ref[...] += _stack_rows([_colsum(ddtraw), _colsum(da * dtb) * a_neg, dskip], LANES)

    vec = pl.BlockSpec((1, LANES), lambda s, c: (0, 0))

    def blk(s, c):
        return s * nc + (nc - 1 - c)

    return pl.pallas_call(
        body, name=name, grid=(n_seq, nc),
        in_specs=[pl.BlockSpec((CHUNK, D_MODEL), lambda s, c: (blk(s, c), 0)),
                  pl.BlockSpec((CHUNK, N_GROUPS * N_STATE), lambda s, c: (blk(s, c), 2)),
                  pl.BlockSpec((CHUNK, N_GROUPS * N_STATE), lambda s, c: (blk(s, c), 3)),
                  pl.BlockSpec((CHUNK, LANES), lambda s, c: (blk(s, c), 0)),
                  vec, vec, pl.BlockSpec((1, D_MODEL), lambda s, c: (0, 0)),
                  pl.BlockSpec((1, N_PAIRS, LANES, N_STATE), lambda s, c: (blk(s, c), 0, 0, 0)),
                  pl.BlockSpec((CHUNK, D_MODEL), lambda s, c: (blk(s, c), 0))],
        out_specs=[pl.BlockSpec((CHUNK, D_CONV), lambda s, c: (blk(s, c), 0)),
                   pl.BlockSpec((CHUNK, LANES), lambda s, c: (blk(s, c), 0)),
                   pl.BlockSpec((8, LANES), lambda s, c: (0, 0))],
        out_shape=[_sds((t, D_CONV)), _sds((t, LANES)), _sds((8, LANES))],
        scratch_shapes=[pltpu.VMEM((N_PAIRS, LANES, N_STATE), F32)],
        compiler_params=_params("arbitrary", "arbitrary"),
    )(xbc, xbc, xbc, dtraw, bias, alog, dskip_x, states, dy)


Q_COL = (D_MODEL + D_CONV) // LANES
K_COL = Q_COL + D_MODEL // LANES
V_COL = K_COL + D_MODEL // LANES
ATT_SCALE = HEAD_DIM ** -0.5


ATT_SUB = 2
ATT_TILE = ATT_SUB * CHUNK
ATT_NP = 2


ATT_ROWS = 2 * ATT_TILE


def _tri_dot(x, tri):
    hi = x.astype(BF16)
    lo = (x - hi.astype(F32)).astype(BF16)
    both = _dot(jnp.concatenate([hi, lo], axis=0), tri)
    return both[:x.shape[0]] + both[x.shape[0]:]


def _att_stack(ref, lanes, keeps, scale=1.0):
    parts = []
    for a in range(ATT_SUB):
        blk = ref[a * CHUNK:(a + 1) * CHUNK, lanes] * scale
        parts += [jnp.where(keeps[r], blk, 0.0).astype(BF16) for r in range(2)]
    return jnp.concatenate(parts, axis=0)


def _att_unstack(x, a, keeps):
    return jnp.where(keeps[0], x[2 * a * CHUNK:(2 * a + 1) * CHUNK], x[(2 * a + 1) * CHUNK:(2 * a + 2) * CHUNK])


def _att_logits(q_stack, k_blk, b, diag):
    s = _dot(q_stack, k_blk, _DIMS["nt"])
    lb = jnp.minimum(s, 0.0) - jnp.log(1.0 + jnp.exp(-jnp.abs(s)))
    l1m = lb - s
    mask = None
    if diag:
        srow, scol = _iota2((ATT_ROWS, CHUNK), 0), _iota2((ATT_ROWS, CHUNK), 1)
        qpos = lax.shift_right_logical(srow, 8) * CHUNK + jnp.bitwise_and(srow, CHUNK - 1)
        mask = qpos > (b * CHUNK + scol)
        l1m = jnp.where(mask, l1m, 0.0)
    return lb, l1m, mask


def _att_fwd(proj, n_seq, name):
    t = proj.shape[0]
    seq = t // n_seq
    nq = seq // ATT_TILE

    def body(q_ref, k_ref, v_ref, o_ref, rt_ref):
        i = pl.program_id(2)
        upper = (_iota2((CHUNK, CHUNK), 0) > _iota2((CHUNK, CHUNK), 1)).astype(BF16)
        keeps = [_iota2((CHUNK, LANES), 1) < HEAD_DIM, _iota2((CHUNK, LANES), 1) >= HEAD_DIM]
        pair_lanes = [slice(pr * LANES, (pr + 1) * LANES) for pr in range(ATT_NP)]
        q_stacks = [_att_stack(q_ref, lanes, keeps, ATT_SCALE) for lanes in pair_lanes]
        per_pair = ATT_SUB + 1

        def tile(jt, carry, diag):
            state = list(carry)
            for b in reversed(range(ATT_SUB)):
                k0 = pl.multiple_of(jt * ATT_TILE + b * CHUNK, CHUNK)
                for pr, lanes in enumerate(pair_lanes):
                    base = pr * per_pair
                    run = state[base + ATT_SUB]
                    k_blk = k_ref[pl.ds(k0, CHUNK), lanes].astype(BF16)
                    v_blk = v_ref[pl.ds(k0, CHUNK), lanes].astype(BF16)
                    lb, l1m, mask = _att_logits(q_stacks[pr], k_blk, b, diag)
                    p = jnp.exp(lb + (_tri_dot(l1m, upper) + run))
                    if diag:
                        p = jnp.where(mask, p, 0.0)
                    pv = _dot(p.astype(BF16), v_blk)
                    for a in range(ATT_SUB):
                        state[base + a] = state[base + a] + _att_unstack(pv, a, keeps)
                    state[base + ATT_SUB] = run + jnp.sum(l1m, axis=1, keepdims=True)
            return tuple(state)

        init = (tuple(jnp.zeros((CHUNK, LANES), F32) for _ in range(ATT_SUB)) + (jnp.zeros((ATT_ROWS, 1), F32),)) * ATT_NP
        carry = tile(i, init, True)
        carry = lax.fori_loop(0, i, lambda it, c: tile(i - 1 - it, c, False), carry)
        for pr, lanes in enumerate(pair_lanes):
            for a in range(ATT_SUB):
                o_ref[a * CHUNK:(a + 1) * CHUNK, lanes] = carry[pr * per_pair + a]
                rt_ref[a * CHUNK:(a + 1) * CHUNK, lanes] = _att_unstack(carry[pr * per_pair + ATT_SUB], a, keeps)

    width = ATT_NP * LANES
    qblk = pl.BlockSpec((ATT_TILE, width), lambda s, p, i: (s * nq + i, p))
    return pl.pallas_call(
        body, name=name, grid=(n_seq, N_PAIRS // ATT_NP, nq),
        in_specs=[pl.BlockSpec((ATT_TILE, width), lambda s, p, i: (s * nq + i, Q_COL // ATT_NP + p)),
                  pl.BlockSpec((seq, width), lambda s, p, i: (s, K_COL // ATT_NP + p)),
                  pl.BlockSpec((seq, width), lambda s, p, i: (s, V_COL // ATT_NP + p))],
        out_specs=[qblk, qblk],
        out_shape=[_sds((t, D_MODEL)), _sds((t, D_MODEL))],
        compiler_params=_params("parallel", "parallel", "arbitrary"),
    )(proj, proj, proj)


def _att_bwd(proj, rtot, datt, n_seq, name):
    t = proj.shape[0]
    seq = t // n_seq
    nq = seq // ATT_TILE

    def body(q_ref, k_ref, v_ref, rt_ref, do_ref, dq_ref, dk_ref, dv_ref):
        i = pl.program_id(2)

        @pl.when(i == 0)
        def _():
            dk_ref[...] = jnp.zeros_like(dk_ref)
            dv_ref[...] = jnp.zeros_like(dv_ref)

        row, col = _iota2((CHUNK, CHUNK), 0), _iota2((CHUNK, CHUNK), 1)
        upper = (row > col).astype(BF16)
        before = (row < col).astype(BF16)
        keeps = [_iota2((CHUNK, LANES), 1) < HEAD_DIM, _iota2((CHUNK, LANES), 1) >= HEAD_DIM]
        pair_lanes = [slice(pr * LANES, (pr + 1) * LANES) for pr in range(ATT_NP)]
        q_stacks = [_att_stack(q_ref, lanes, keeps, ATT_SCALE) for lanes in pair_lanes]
        do_stacks = [_att_stack(do_ref, lanes, keeps) for lanes in pair_lanes]
        totals = [jnp.concatenate([rt_ref[a * CHUNK:(a + 1) * CHUNK, pr * LANES + r * HEAD_DIM:pr * LANES + r * HEAD_DIM + 1]
                                   for a in range(ATT_SUB) for r in range(2)], axis=0) for pr in range(ATT_NP)]
        per_pair = ATT_SUB + 2

        def tile(jt, carry, diag):
            state = list(carry)
            for b in range(ATT_SUB):
                k0 = pl.multiple_of(jt * ATT_TILE + b * CHUNK, CHUNK)
                for pr, lanes in enumerate(pair_lanes):
                    base = pr * per_pair
                    seen, dseen = state[base + ATT_SUB], state[base + ATT_SUB + 1]
                    q_stack, do_stack = q_stacks[pr], do_stacks[pr]
                    k_blk = k_ref[pl.ds(k0, CHUNK), lanes].astype(BF16)
                    v_blk = v_ref[pl.ds(k0, CHUNK), lanes].astype(BF16)
                    lb, l1m, mask = _att_logits(q_stack, k_blk, b, diag)
                    here = jnp.sum(l1m, axis=1, keepdims=True)
                    p = jnp.exp(lb + (_tri_dot(l1m, upper) + (totals[pr] - seen - here)))
                    if diag:
                        p = jnp.where(mask, p, 0.0)
                    pb = p.astype(BF16)
                    dz = _dot(do_stack, v_blk, _DIMS["nt"]) * p
                    dl1m = dseen + _tri_dot(dz, before)
                    sig = jnp.exp(lb)
                    ds = dz * (1.0 - sig) - dl1m * sig
                    if diag:
                        ds = jnp.where(mask, ds, 0.0)
                    dsb = ds.astype(BF16)
                    dq_all = _dot(dsb, k_blk)
                    for a in range(ATT_SUB):
                        state[base + a] = state[base + a] + _att_unstack(dq_all, a, keeps)
                    dk_ref[pl.ds(k0, CHUNK), lanes] += _dot(dsb, q_stack, _DIMS["tn"])
                    dv_ref[pl.ds(k0, CHUNK), lanes] += _dot(pb, do_stack, _DIMS["tn"])
                    state[base + ATT_SUB] = seen + here
                    state[base + ATT_SUB + 1] = dseen + jnp.sum(dz, axis=1, keepdims=True)
            return tuple(state)

        init = (tuple(jnp.zeros((CHUNK, LANES), F32) for _ in range(ATT_SUB)) +
                (jnp.zeros((ATT_ROWS, 1), F32), jnp.zeros((ATT_ROWS, 1), F32))) * ATT_NP
        carry = lax.fori_loop(0, i, lambda jt, c: tile(jt, c, False), init)
        carry = tile(i, carry, True)
        for pr, lanes in enumerate(pair_lanes):
            for a in range(ATT_SUB):
                dq_ref[a * CHUNK:(a + 1) * CHUNK, lanes] = carry[pr * per_pair + a] * ATT_SCALE

    width = ATT_NP * LANES
    qblk = pl.BlockSpec((ATT_TILE, width), lambda s, p, i: (s * nq + i, p))
    kv_out = pl.BlockSpec((seq, width), lambda s, p, i: (s, p))
    return pl.pallas_call(
        body, name=name, grid=(n_seq, N_PAIRS // ATT_NP, nq),
        in_specs=[pl.BlockSpec((ATT_TILE, width), lambda s, p, i: (s * nq + i, Q_COL // ATT_NP + p)),
                  pl.BlockSpec((seq, width), lambda s, p, i: (s, K_COL // ATT_NP + p)),
                  pl.BlockSpec((seq, width), lambda s, p, i: (s, V_COL // ATT_NP + p)),
                  qblk, qblk],
        out_specs=[qblk, kv_out, kv_out],
        out_shape=[_sds((t, D_MODEL))] * 3,
        compiler_params=_params("parallel", "parallel", "arbitrary"),
    )(proj, proj, proj, rtot, datt)


def _pad_lanes(v):
    return jnp.pad(v.reshape(1, -1), ((0, 0), (0, LANES - v.shape[0])))


def _layer_fwd(x, p, n_seq, tag):
    h = _rms_fwd(x, p["norm_mix_g"], f"rms_mix_fwd{tag}")
    proj, = _matmul(h, p["w_main"], "nn", [F32], f"in_proj{tag}", tn=1536)
    dtraw, = _matmul(h, p["w_dt"], "nn", [F32], f"dt_proj{tag}")
    xbc = _conv_fwd(proj, p["conv_w"], p["conv_b"], n_seq, f"conv_fwd{tag}")
    y, states = _ssd_fwd(xbc, dtraw, p["dt_bias"], p["a_log"], p["d_skip_x"], n_seq, f"ssd_fwd{tag}")
    att, rtot = _att_fwd(proj, n_seq, f"att_fwd{tag}")
    ycat = _mixnorm_fwd(y, proj, att, p["ssd_norm_g"], p["att_norm_g"], f"mixnorm_fwd{tag}")
    x1, = _matmul(ycat, p["w_out"], "nn", [F32], f"out_proj{tag}", extras=[x], epilogue=lambda acc, xb: (xb + acc,))
    h2 = _rms_fwd(x1, p["norm_mlp_g"], f"rms_mlp_fwd{tag}")
    u, act = _matmul(h2, p["w_up"], "nn", [F32, BF16], f"up_proj{tag}",
                     epilogue=lambda acc: (acc, jnp.square(jnp.maximum(acc, 0.0))))
    x2, = _matmul(act, p["w_down"], "nn", [F32], f"down_proj{tag}", extras=[x1], epilogue=lambda acc, xb: (xb + acc,))
    saved = dict(x=x, h=h, proj=proj, dtraw=dtraw, xbc=xbc, y=y, states=states, att=att, rtot=rtot, ycat=ycat,
                 x1=x1, h2=h2, u=u, act=act)
    return x2, saved


def _layer_bwd(dx2, dx2b, p, s, n_seq, tag):
    g = {}
    g["w_down"], = _matmul(s["act"], dx2b, "tn", [F32], f"dw_down{tag}")
    du, = _matmul(dx2b, p["w_down"], "nt", [BF16], f"d_act{tag}", extras=[s["u"]],
                  epilogue=lambda acc, ub: (acc * (2.0 * jnp.maximum(ub, 0.0)),))
    g["w_up"], = _matmul(s["h2"], du, "tn", [F32], f"dw_up{tag}")
    dh2, = _matmul(du, p["w_up"], "nt", [F32], f"d_h2{tag}")
    dx1, dx1b, g["norm_mlp_g"] = _rms_bwd(s["x1"], p["norm_mlp_g"], dh2, dx2, f"rms_mlp_bwd{tag}")
    g["w_out"], = _matmul(s["ycat"], dx1b, "tn", [F32], f"dw_out{tag}")
    dycat, = _matmul(dx1b, p["w_out"], "nt", [F32], f"d_ycat{tag}")
    dy, dz, datt, g["ssd_norm_g"], g["att_norm_g"] = _mixnorm_bwd(
        s["y"], s["proj"], s["att"], p["ssd_norm_g"], p["att_norm_g"], dycat, f"mixnorm_bwd{tag}")
    dq, dk, dv = _att_bwd(s["proj"], s["rtot"], datt, n_seq, f"att_bwd{tag}")
    dxbc, ddtraw, pg = _ssd_bwd(s["xbc"], s["dtraw"], p["dt_bias"], p["a_log"], p["d_skip_x"], s["states"], dy,
                                n_seq, f"ssd_bwd{tag}")
    g["dt_bias"], g["a_log"], g["d_skip"] = pg[0, :N_HEADS], pg[1, :N_HEADS], pg[2, :N_HEADS]
    du_conv, wg = _conv_bwd(s["proj"], p["conv_w"], p["conv_b"], dxbc, n_seq, f"conv_bwd{tag}")
    g["conv_w"], g["conv_b"] = wg[:CONV_WIDTH], wg[CONV_WIDTH]
    dproj = jnp.concatenate([dz.astype(BF16), du_conv.astype(BF16), dq.astype(BF16), dk.astype(BF16),
                             dv.astype(BF16)], axis=1)
    ddtb = ddtraw.astype(BF16)
    g["w_main"], = _matmul(s["h"], dproj, "tn", [F32], f"dw_in{tag}", tn=1536)
    g["w_dt"], = _matmul(s["h"], ddtb, "tn", [F32], f"dw_dt{tag}")
    dh_dt, = _matmul(ddtb, p["w_dt"], "nt", [F32], f"d_h_dt{tag}")
    dh, = _matmul(dproj, p["w_main"], "nt", [F32], f"d_h{tag}", extras=[dh_dt], epilogue=lambda acc, e: (acc + e,))
    dx, dxb, g["norm_mix_g"] = _rms_bwd(s["x"], p["norm_mix_g"], dh, dx1, f"rms_mix_bwd{tag}")
    return dx, dxb, g


def _split_w_in(w_full):
    c0 = D_MODEL + D_CONV
    main = jnp.concatenate([w_full[:, :c0], w_full[:, c0 + N_HEADS:]], axis=1)
    dt = jnp.pad(w_full[:, c0:c0 + N_HEADS], ((0, 0), (0, LANES - N_HEADS)))
    return main, dt


def _merge_w_in(main, dt):
    c0 = D_MODEL + D_CONV
    return jnp.concatenate([main[:, :c0], dt[:, :N_HEADS], main[:, c0:]], axis=1)


def _prep_layer(l, w_in_full, w_out, w_up, w_down, conv_w, small):
    w_main, w_dt = _split_w_in(w_in_full)
    return dict(
        w_main=w_main, w_dt=w_dt, w_out=w_out, w_up=w_up, w_down=w_down, conv_w=conv_w,
        conv_b=small["conv_b"][l].reshape(1, -1),
        dt_bias=_pad_lanes(small["dt_bias"][l]), a_log=_pad_lanes(small["a_log"][l]),
        d_skip_x=jnp.repeat(small["d_skip"][l], HEAD_DIM).reshape(1, -1),
        norm_mix_g=small["norm_mix_g"][l].reshape(1, -1), ssd_norm_g=small["ssd_norm_g"][l].reshape(1, -1),
        att_norm_g=small["att_norm_g"][l].reshape(1, -1), norm_mlp_g=small["norm_mlp_g"][l].reshape(1, -1),
    )


ANY = pl.BlockSpec(memory_space=pl.ANY)


def _place():
    x, y, c = lax.axis_index("x"), lax.axis_index("y"), lax.axis_index("c")
    return x, y, c, (x, y, 1 - c), [(1 - x, y), (x, 1 - y), (1 - x, 1 - y)]


def _all_gather_chips(shard, chip, name):
    r, ccols = shard.shape
    half = r // 2

    def body(s_ref, o_ref, send_sems, recv_sems):
        x, y, c, sibling, chips = _place()

        def slab(px, py, hc):
            return o_ref.at[2 * px + py, pl.ds(hc * half, half), :]

        def copy(k, src, dst, to):
            return pltpu.make_async_remote_copy(src_ref=src, dst_ref=dst, send_sem=send_sems.at[k],
                                                recv_sem=recv_sems.at[k], device_id=to, device_id_type=MESH)

        sends = [copy(k, s_ref.at[pl.ds(c * half, half), :], slab(x, y, c), (px, py, c))
                 for k, (px, py) in enumerate(chips)]
        for cp in sends:
            cp.start()
        passed = []
        for k, (px, py) in enumerate(chips):
            copy(k, slab(px, py, c), slab(px, py, c), (px, py, c)).wait_recv()
            cp = copy(3 + k, slab(px, py, c), slab(px, py, c), sibling)
            cp.start()
            passed.append(cp)
        for k, (px, py) in enumerate(chips):
            copy(3 + k, slab(px, py, 1 - c), slab(px, py, 1 - c), sibling).wait_recv()
        for cp in sends + passed:
            cp.wait_send()

    others = pl.pallas_call(
        body, name=name, in_specs=[ANY], out_specs=ANY,
        out_shape=_sds((N_CHIPS, r, ccols), shard.dtype),
        scratch_shapes=[pltpu.SemaphoreType.DMA((6,)), pltpu.SemaphoreType.DMA((6,))],
    )(shard)
    return lax.dynamic_update_slice(others, shard[None], (chip, 0, 0))


def _sibling_swap(g, name):
    n, r, ccols = g.shape
    half = r // 2

    def body(g_ref, o_ref, send_sem, recv_sem):
        _, _, c, sibling, _ = _place()
        cp = pltpu.make_async_remote_copy(src_ref=g_ref.at[:, pl.ds((1 - c) * half, half), :], dst_ref=o_ref,
                                          send_sem=send_sem, recv_sem=recv_sem, device_id=sibling, device_id_type=MESH)
        cp.start()
        cp.wait()

    return pl.pallas_call(
        body, name=name, in_specs=[ANY], out_specs=ANY, out_shape=_sds((n, half, ccols), g.dtype),
        scratch_shapes=[pltpu.SemaphoreType.DMA, pltpu.SemaphoreType.DMA],
    )(g)


def _chip_exchange(p, name):
    _, h, ccols = p.shape

    def body(p_ref, o_ref, send_sems, recv_sems):
        _, _, c, _, chips = _place()
        copies = [pltpu.make_async_remote_copy(src_ref=p_ref.at[2 * px + py], dst_ref=o_ref.at[k],
                                               send_sem=send_sems.at[k], recv_sem=recv_sems.at[k],
                                               device_id=(px, py, c), device_id_type=MESH)
                  for k, (px, py) in enumerate(chips)]
        for cp in copies:
            cp.start()
        for cp in copies:
            cp.wait()

    return pl.pallas_call(
        body, name=name, in_specs=[ANY], out_specs=ANY, out_shape=_sds((3, h, ccols), p.dtype),
        scratch_shapes=[pltpu.SemaphoreType.DMA((3,)), pltpu.SemaphoreType.DMA((3,))],
    )(p)


def _sibling_share(full, name):
    r, ccols = full.shape
    h = r // 2

    def body(f_ref, o_ref, send_sem, recv_sem):
        _, _, c, sibling, _ = _place()
        mine = pl.ds(c * h, h)
        cp = pltpu.make_async_remote_copy(src_ref=f_ref.at[mine, :], dst_ref=o_ref.at[mine, :], send_sem=send_sem,
                                          recv_sem=recv_sem, device_id=sibling, device_id_type=MESH)
        cp.start()
        theirs = o_ref.at[pl.ds((1 - c) * h, h), :]
        pltpu.make_async_remote_copy(src_ref=theirs, dst_ref=theirs, send_sem=send_sem, recv_sem=recv_sem,
                                     device_id=sibling, device_id_type=MESH).wait_recv()
        cp.wait_send()

    return pl.pallas_call(
        body, name=name, in_specs=[ANY], out_specs=ANY, out_shape=_sds((r, ccols), full.dtype),
        input_output_aliases={0: 0},
        scratch_shapes=[pltpu.SemaphoreType.DMA, pltpu.SemaphoreType.DMA],
    )(full)


def _add_halves(g, a, c, name):
    n, r, ccols = g.shape
    half = r // 2
    tr = _tile(half, 256)
    nb = half // tr

    def body(c_ref, g_ref, a_ref, o_ref):
        o_ref[...] = (g_ref[...] + a_ref[...]).astype(BF16)

    blk = (1, tr, ccols)
    return pl.pallas_call(
        body, name=name,
        grid_spec=pltpu.PrefetchScalarGridSpec(
            num_scalar_prefetch=1, grid=(n, nb),
            in_specs=[pl.BlockSpec(blk, lambda j, i, c_ref: (j, c_ref[0] * nb + i, 0)),
                      pl.BlockSpec(blk, lambda j, i, c_ref: (j, i, 0))],
            out_specs=pl.BlockSpec(blk, lambda j, i, c_ref: (j, i, 0))),
        out_shape=_sds((n, half, ccols), BF16),
        compiler_params=_params("parallel", "parallel"),
    )(c.reshape(1).astype(jnp.int32), g, a)


def _sum_chips(p, got, chip, c, name):
    _, h, ccols = p.shape
    tr = _tile(h, 256)

    def body(j_ref, h_ref, p_ref, a_ref, b_ref, c_ref, o_ref):
        f32 = [ref[...].astype(F32) for ref in (p_ref, a_ref, b_ref, c_ref)]
        o_ref[...] = ((f32[0] + f32[1]) + f32[2]) + f32[3]

    blk = (1, tr, ccols)

    def slot(k):
        return pl.BlockSpec(blk, lambda i, j_ref, h_ref: (k, i, 0))

    return pl.pallas_call(
        body, name=name,
        grid_spec=pltpu.PrefetchScalarGridSpec(
            num_scalar_prefetch=2, grid=(h // tr,),
            in_specs=[pl.BlockSpec(blk, lambda i, j_ref, h_ref: (j_ref[0], i, 0)), slot(0), slot(1), slot(2)],
            out_specs=pl.BlockSpec(blk, lambda i, j_ref, h_ref: (h_ref[0], i, 0))),
        out_shape=_sds((2, h, ccols)),
        compiler_params=_params("parallel"),
    )(chip.reshape(1).astype(jnp.int32), c.reshape(1).astype(jnp.int32), p, got, got, got)


def _reduce_scatter(g, chip, c, name):
    a = _sibling_swap(g, f"{name}_swap")
    p = _add_halves(g, a, c, f"{name}_add")
    got = _chip_exchange(p, f"{name}_xchg")
    halves = _sum_chips(p, got, chip, c, f"{name}_sum")
    return _sibling_share(halves.reshape(g.shape[1], g.shape[2]), f"{name}_share")


def _all_reduce_small(v, name):
    r = v.shape[0]

    def body(v_ref, o_ref, buf, send_sems, recv_sems):
        x, y, c, _, _ = _place()
        buf[0] = v_ref[...]
        copies = []
        for rel in range(1, 8):
            fx, fy, fc = (rel >> 2) & 1, (rel >> 1) & 1, rel & 1
            peer = (1 - x if fx else x, 1 - y if fy else y, 1 - c if fc else c)
            cp = pltpu.make_async_remote_copy(src_ref=v_ref, dst_ref=buf.at[rel], send_sem=send_sems.at[rel - 1],
                                              recv_sem=recv_sems.at[rel - 1], device_id=peer, device_id_type=MESH)
            cp.start()
            copies.append(cp)
        for cp in copies:
            cp.wait()
        me = 4 * x + 2 * y + c
        acc = buf[jnp.bitwise_xor(me, 0)]
        for src in range(1, 8):
            acc = acc + buf[jnp.bitwise_xor(me, src)]
        o_ref[...] = acc

    vm = pl.BlockSpec(memory_space=pltpu.VMEM)
    return pl.pallas_call(
        body, name=name, in_specs=[vm], out_specs=vm, out_shape=_sds((r, LANES)),
        scratch_shapes=[pltpu.VMEM((8, r, LANES), F32), pltpu.SemaphoreType.DMA((7,)), pltpu.SemaphoreType.DMA((7,))],
    )(v)


WEIGHTS = ["norm_mix_g", "w_in", "conv_w", "conv_b", "dt_bias", "a_log", "d_skip", "ssd_norm_g", "att_norm_g",
           "w_out", "norm_mlp_g", "w_up", "w_down", "final_norm_g"]
BIG = ["w_in", "w_out", "w_up", "w_down"]
SMALL = [n for n in WEIGHTS if n not in BIG]


def _pack(arrays):
    flat = []
    for a in arrays:
        a = a.reshape(-1)
        flat.append(jnp.pad(a, (0, (-a.shape[0]) % LANES)))
    flat = jnp.concatenate(flat)
    flat = jnp.pad(flat, (0, (-flat.shape[0]) % (8 * LANES)))
    return flat.reshape(-1, LANES)


def _unpack(packed, shapes):
    flat, out, pos = packed.reshape(-1), [], 0
    for shp in shapes:
        n = math.prod(shp)
        out.append(flat[pos:pos + n].reshape(shp))
        pos += n + (-n) % LANES
    return out


def _to_shards(name, g):
    if name in ("w_in", "w_up"):
        l, r, ccols = g.shape
        return g.reshape(l, r, N_CHIPS, ccols // N_CHIPS).transpose(2, 0, 1, 3).reshape(N_CHIPS, l * r, ccols // N_CHIPS)
    l, r, ccols = g.shape
    return g.reshape(l, N_CHIPS, r // N_CHIPS, ccols).transpose(1, 0, 2, 3).reshape(N_CHIPS, l * r // N_CHIPS, ccols)


def _from_gathered(name, g, l):
    rows = g.shape[1] // DEPTH
    part = g[:, l * rows:(l + 1) * rows, :]
    if name in ("w_in", "w_up", "conv_w"):
        return part.transpose(1, 0, 2).reshape(rows, N_CHIPS * g.shape[2])
    return part.reshape(N_CHIPS * rows, g.shape[2])


def kernel(x, norm_mix_g, w_in, conv_w, conv_b, dt_bias, a_log, d_skip, ssd_norm_g, att_norm_g, w_out, norm_mlp_g, w_up, w_down, final_norm_g, loss_target, m_norm_mix_g, m_w_in, m_conv_w, m_conv_b, m_dt_bias, m_a_log, m_d_skip, m_ssd_norm_g, m_att_norm_g, m_w_out, m_norm_mlp_g, m_w_up, m_w_down, m_final_norm_g, v_norm_mix_g, v_w_in, v_conv_w, v_conv_b, v_dt_bias, v_a_log, v_d_skip, v_ssd_norm_g, v_att_norm_g, v_w_out, v_norm_mlp_g, v_w_up, v_w_down, v_final_norm_g):
    w = dict(norm_mix_g=norm_mix_g, w_in=w_in, conv_w=conv_w, conv_b=conv_b, dt_bias=dt_bias, a_log=a_log,
             d_skip=d_skip, ssd_norm_g=ssd_norm_g, att_norm_g=att_norm_g, w_out=w_out, norm_mlp_g=norm_mlp_g,
             w_up=w_up, w_down=w_down, final_norm_g=final_norm_g)
    m = dict(norm_mix_g=m_norm_mix_g, w_in=m_w_in, conv_w=m_conv_w, conv_b=m_conv_b, dt_bias=m_dt_bias,
             a_log=m_a_log, d_skip=m_d_skip, ssd_norm_g=m_ssd_norm_g, att_norm_g=m_att_norm_g, w_out=m_w_out,
             norm_mlp_g=m_norm_mlp_g, w_up=m_w_up, w_down=m_w_down, final_norm_g=m_final_norm_g)
    v = dict(norm_mix_g=v_norm_mix_g, w_in=v_w_in, conv_w=v_conv_w, conv_b=v_conv_b, dt_bias=v_dt_bias,
             a_log=v_a_log, d_skip=v_d_skip, ssd_norm_g=v_ssd_norm_g, att_norm_g=v_att_norm_g, w_out=v_w_out,
             norm_mlp_g=v_norm_mlp_g, w_up=v_w_up, w_down=v_w_down, final_norm_g=v_final_norm_g)
    n_seq, seq, d = x.shape
    t = n_seq * seq
    chip = 2 * lax.axis_index("x") + lax.axis_index("y")
    core = lax.axis_index("c")

    gathered = {n: _all_gather_chips(w[n].astype(BF16).reshape(-1, w[n].shape[-1]), chip, f"gather_{n}") for n in BIG}
    gathered["conv_w"] = _all_gather_chips(conv_w.reshape(-1, conv_w.shape[-1]), chip, "gather_conv_w")
    layers = [_prep_layer(l, *[_from_gathered(n, gathered[n], l) for n in BIG + ["conv_w"]], w) for l in range(DEPTH)]

    xs = x.reshape(t, d)
    saved = []
    for l in range(DEPTH):
        xs, s = _layer_fwd(xs, layers[l], n_seq, f"_l{l}")
        saved.append(s)
    loss_vec, dx, dxb, g_final = _final_loss(xs, final_norm_g.reshape(1, d), loss_target.reshape(t, d), "final_loss")
    loss = lax.psum(loss_vec[0, 0], ("x", "y", "c"))

    grads = [None] * DEPTH
    for l in reversed(range(DEPTH)):
        dx, dxb, grads[l] = _layer_bwd(dx, dxb, layers[l], saved[l], n_seq, f"_l{l}")
        grads[l]["w_in"] = _merge_w_in(grads[l].pop("w_main"), grads[l].pop("w_dt"))
    grad_x = dx.reshape(n_seq, seq, d)

    full = {n: jnp.stack([grads[l][n] for l in range(DEPTH)]) for n in WEIGHTS if n != "final_norm_g"}
    full["final_norm_g"] = g_final.reshape(d)
    g_out = {}
    for n in BIG:
        red = _reduce_scatter(_to_shards(n, full[n]), chip, core, f"rs_{n}")
        g_out[n] = red.reshape(w[n].shape)
    small_sum = _all_reduce_small(_pack([full[n] for n in SMALL]), "allreduce_small")
    small_shapes = [(DEPTH, CONV_WIDTH, D_CONV) if n == "conv_w" else w[n].shape for n in SMALL]
    for n, val in zip(SMALL, _unpack(small_sum, small_shapes)):
        g_out[n] = val
    g_out["conv_w"] = lax.dynamic_slice_in_dim(g_out["conv_w"], chip * conv_w.shape[-1], conv_w.shape[-1], axis=2)

    delta, new_m, new_v = {}, {}, {}
    for n in BIG:
        two_d = (-1, w[n].shape[-1])
        dl, mn, vn = _adamw(w[n].reshape(two_d), g_out[n].reshape(two_d), m[n].reshape(two_d), v[n].reshape(two_d),
                            f"adamw_{n}")
        delta[n], new_m[n], new_v[n] = dl.reshape(w[n].shape), mn.reshape(w[n].shape), vn.reshape(w[n].shape)
    packs = [_pack([src[n] for n in SMALL]) for src in (w, g_out, m, v)]
    shapes = [w[n].shape for n in SMALL]
    for dst, packed in zip((delta, new_m, new_v), _adamw(*packs, "adamw_small")):
        for n, val in zip(SMALL, _unpack(packed, shapes)):
            dst[n] = val

    return (loss, grad_x, *[g_out[n] for n in WEIGHTS], *[delta[n] for n in WEIGHTS],
            *[new_m[n] for n in WEIGHTS], *[new_v[n] for n in WEIGHTS])
```

```python
import functools
import math

import jax
import jax.numpy as jnp
from jax import lax
from jax.experimental import pallas as pl
from jax.experimental.pallas import tpu as pltpu

F32 = jnp.float32
BF16 = jnp.bfloat16
HIGHEST = lax.Precision.HIGHEST

D_MODEL = 1024
DEPTH = 4
HEAD_DIM = 64
N_HEADS = 16
N_GROUPS = 4
N_STATE = 128
N_PAIRS = N_HEADS // 2
CONV_WIDTH = 4
CHUNK = 128
D_CONV = D_MODEL + 2 * N_GROUPS * N_STATE
D_MAIN = D_MODEL + D_CONV + 3 * D_MODEL
D_IN_PROJ = D_MAIN + N_HEADS
D_FF = 4 * D_MODEL
EPS = 1e-5
LANES = 128
VMEM_LIMIT = 48 * 1024 * 1024

ADAM_LR = 0.001
ADAM_B1 = 0.9
ADAM_B2 = 0.999
ADAM_EPS = 1e-08
ADAM_WD = 0.01
ADAM_STEP = 10

N_CHIPS = 4
MESH = pl.DeviceIdType.MESH


def _tile(n, cap):
    if n <= cap:
        return n
    t = cap
    while t >= 8:
        if n % t == 0:
            return t
        t //= 2
    raise ValueError(f"no tile for {n} under {cap}")


def _params(*sem):
    return pltpu.CompilerParams(dimension_semantics=sem, vmem_limit_bytes=VMEM_LIMIT)


_DIMS = {"nn": (((1,), (0,)), ((), ())), "nt": (((1,), (1,)), ((), ())), "tn": (((0,), (0,)), ((), ()))}


def _matmul(a, b, mode, out_dtypes, name, extras=(), epilogue=None, tm=1024, tn=1024, tk=1024):
    if mode == "nn":
        (m, k), (_, n) = a.shape, b.shape
    elif mode == "nt":
        (m, k), (n, _) = a.shape, b.shape
    else:
        (k, m), (_, n) = a.shape, b.shape
    tm, tn, tk = _tile(m, tm), _tile(n, tn), _tile(k, tk)
    nk = k // tk
    if mode == "tn":
        a_spec = pl.BlockSpec((tk, tm), lambda i, j, kk: (kk, i))
    else:
        a_spec = pl.BlockSpec((tm, tk), lambda i, j, kk: (i, kk))
    if mode == "nt":
        b_spec = pl.BlockSpec((tn, tk), lambda i, j, kk: (j, kk))
    else:
        b_spec = pl.BlockSpec((tk, tn), lambda i, j, kk: (kk, j))
    mn_spec = pl.BlockSpec((tm, tn), lambda i, j, kk: (i, j))
    n_extra, n_out = len(extras), len(out_dtypes)
    dims = _DIMS[mode]

    def body(a_ref, b_ref, *rest):
        extra_refs, out_refs, acc = rest[:n_extra], rest[n_extra:n_extra + n_out], rest[-1]
        kk = pl.program_id(2)

        @pl.when(kk == 0)
        def _():
            acc[...] = jnp.zeros_like(acc)

        acc[...] += lax.dot_general(a_ref[...].astype(BF16), b_ref[...].astype(BF16), dims,
                                    preferred_element_type=F32)

        @pl.when(kk == nk - 1)
        def _():
            res = acc[...]
            outs = epilogue(res, *[e[...] for e in extra_refs]) if epilogue is not None else (res,)
            for o_ref, val in zip(out_refs, outs):
                o_ref[...] = val.astype(o_ref.dtype)

    outs = pl.pallas_call(
        body, name=name, grid=(m // tm, n // tn, nk),
        in_specs=[a_spec, b_spec] + [mn_spec] * n_extra,
        out_specs=[mn_spec] * n_out,
        out_shape=[jax.ShapeDtypeStruct((m, n), dt) for dt in out_dtypes],
        scratch_shapes=[pltpu.VMEM((tm, tn), F32)],
        compiler_params=_params("parallel", "parallel", "arbitrary"),
    )(a, b, *extras)
    return tuple(outs)


def _rowwise(fn, rows, vecs, out_rows, out_accs, name, tm=256):
    t = rows[0].shape[0]
    tm = _tile(t, tm)
    n_rows, n_vecs, n_or, n_oa = len(rows), len(vecs), len(out_rows), len(out_accs)

    def body(*refs):
        ins = [r[...] for r in refs[:n_rows + n_vecs]]
        outs = fn(*ins)
        o_refs = refs[n_rows + n_vecs:]
        for o_ref, val in zip(o_refs[:n_or], outs[:n_or]):
            o_ref[...] = val.astype(o_ref.dtype)

        @pl.when(pl.program_id(0) == 0)
        def _():
            for o_ref in o_refs[n_or:]:
                o_ref[...] = jnp.zeros_like(o_ref)

        for o_ref, val in zip(o_refs[n_or:], outs[n_or:]):
            o_ref[...] += val

    outs = pl.pallas_call(
        body, name=name, grid=(t // tm,),
        in_specs=[pl.BlockSpec((tm, r.shape[1]), lambda i: (i, 0)) for r in rows]
        + [pl.BlockSpec(v.shape, lambda i: (0, 0)) for v in vecs],
        out_specs=[pl.BlockSpec((tm, o.shape[1]), lambda i: (i, 0)) for o in out_rows]
        + [pl.BlockSpec(o.shape, lambda i: (0, 0)) for o in out_accs],
        out_shape=list(out_rows) + list(out_accs),
        compiler_params=_params("arbitrary" if n_oa else "parallel"),
    )(*rows, *vecs)
    return tuple(outs)


def _stack_rows(parts, width):
    rows = lax.broadcasted_iota(jnp.int32, (8, width), 0)
    out = jnp.zeros((8, width), F32)
    for j, part in enumerate(parts):
        out = out + jnp.where(rows == j, part, 0.0)
    return out


def _sds(shape, dtype=F32):
    return jax.ShapeDtypeStruct(shape, dtype)


def _sigmoid(x):
    return 1.0 / (1.0 + jnp.exp(-x))


def _colsum(x):
    return jnp.sum(x, axis=0, keepdims=True)


def _rstd(x):
    return lax.rsqrt(jnp.mean(x * x, axis=-1, keepdims=True) + EPS)


def _rms_bwd_rows(xhat, r, g, dy):
    dxh = dy * g
    dx = r * (dxh - xhat * jnp.mean(dxh * xhat, axis=-1, keepdims=True))
    return dx, _colsum(dy * xhat)


def _rms_fwd(x, g, name):
    t, d = x.shape

    def fn(xb, gv):
        return (xb * _rstd(xb) * gv,)

    return _rowwise(fn, [x], [g], [_sds((t, d), BF16)], [], name)[0]


def _rms_bwd(x, g, dh, dres, name):
    t, d = x.shape

    def fn(xb, dhb, dresb, gv):
        r = _rstd(xb)
        dx, dg = _rms_bwd_rows(xb * r, r, gv, dhb)
        dx = dx + dresb
        return dx, dx, dg

    return _rowwise(fn, [x, dh, dres], [g], [_sds((t, d)), _sds((t, d), BF16)], [_sds((1, d))], name)


def _mixnorm_fwd(y, z, att, gs, ga, name):
    t, d = y.shape

    def fn(yb, zb, ab, gsv, gav):
        yg = yb * (zb * _sigmoid(zb))
        return (jnp.concatenate([yg * _rstd(yg) * gsv, ab * _rstd(ab) * gav], axis=1),)

    tm = _tile(t, 256)

    def body(y_ref, z_ref, a_ref, gs_ref, ga_ref, o_ref):
        o_ref[...] = fn(y_ref[...], z_ref[...], a_ref[...], gs_ref[...], ga_ref[...])[0].astype(BF16)

    row = pl.BlockSpec((tm, d), lambda i: (i, 0))
    vec = pl.BlockSpec((1, d), lambda i: (0, 0))
    out = pl.pallas_call(
        body, name=name, grid=(t // tm,),
        in_specs=[row, row, row, vec, vec],
        out_specs=pl.BlockSpec((tm, 2 * d), lambda i: (i, 0)),
        out_shape=_sds((t, 2 * d), BF16),
        compiler_params=_params("parallel"),
    )(y, z, att, gs, ga)
    return out


def _mixnorm_bwd(y, z, att, gs, ga, dycat, name):
    t, d = y.shape
    tm = _tile(t, 256)

    def body(y_ref, z_ref, a_ref, dyc_ref, gs_ref, ga_ref, dy_ref, dz_ref, da_ref, dgs_ref, dga_ref):
        yb, zb, ab = y_ref[...], z_ref[...], a_ref[...]
        dys, dya = dyc_ref[:, :d], dyc_ref[:, d:]
        sz = _sigmoid(zb)
        silu = zb * sz
        yg = yb * silu
        r = _rstd(yg)
        dyg, dgs = _rms_bwd_rows(yg * r, r, gs_ref[...], dys)
        dy_ref[...] = dyg * silu
        dz_ref[...] = (dyg * yb * (sz * (1.0 + zb * (1.0 - sz)))).astype(BF16)
        r2 = _rstd(ab)
        datt, dga = _rms_bwd_rows(ab * r2, r2, ga_ref[...], dya)
        da_ref[...] = datt

        @pl.when(pl.program_id(0) == 0)
        def _():
            dgs_ref[...] = jnp.zeros_like(dgs_ref)
            dga_ref[...] = jnp.zeros_like(dga_ref)

        dgs_ref[...] += dgs
        dga_ref[...] += dga

    row = pl.BlockSpec((tm, d), lambda i: (i, 0))
    vec = pl.BlockSpec((1, d), lambda i: (0, 0))
    return pl.pallas_call(
        body, name=name, grid=(t // tm,),
        in_specs=[row, row, row, pl.BlockSpec((tm, 2 * d), lambda i: (i, 0)), vec, vec],
        out_specs=[row, row, row, vec, vec],
        out_shape=[_sds((t, d)), _sds((t, d), BF16), _sds((t, d)), _sds((1, d)), _sds((1, d))],
        compiler_params=_params("arbitrary"),
    )(y, z, att, dycat, gs, ga)


def _final_loss(x, g, target, name):
    t, d = x.shape

    def fn(xb, tb, gv):
        r = _rstd(xb)
        xhat = xb * r
        err = xhat * gv - tb
        loss = 0.5 * jnp.sum(jnp.mean(err * err, axis=-1, keepdims=True), axis=0, keepdims=True)
        dx, dg = _rms_bwd_rows(xhat, r, gv, err * (1.0 / d))
        return dx, dx, jnp.broadcast_to(loss, (1, LANES)), dg

    dx, dxb, loss, dg = _rowwise(fn, [x, target], [g], [_sds((t, d)), _sds((t, d), BF16)],
                                 [_sds((1, LANES)), _sds((1, d))], name)
    return loss, dx, dxb, dg


def _adamw(w, g, m, v, name):
    c1 = 1.0 - ADAM_B1 ** ADAM_STEP
    c2 = 1.0 - ADAM_B2 ** ADAM_STEP

    def fn(wb, gb, mb, vb):
        mn = ADAM_B1 * mb + (1.0 - ADAM_B1) * gb
        vn = ADAM_B2 * vb + (1.0 - ADAM_B2) * (gb * gb)
        delta = -ADAM_LR * ((mn / c1) / (jnp.sqrt(vn / c2) + ADAM_EPS) + ADAM_WD * wb)
        return delta, mn, vn

    return _rowwise(fn, [w, g, m, v], [], [_sds(w.shape)] * 3, [], name)


CONV_CB = 512


def _shift_down(u, k):
    if k == 0:
        return u
    rows = lax.broadcasted_iota(jnp.int32, u.shape, 0)
    return jnp.where(rows >= k, pltpu.roll(u, k, 0), 0.0)


def _shift_up(u, k):
    if k == 0:
        return u
    n = u.shape[0]
    rows = lax.broadcasted_iota(jnp.int32, u.shape, 0)
    return jnp.where(rows < n - k, pltpu.roll(u, n - k, 0), 0.0)


def _conv_pre(u, w, b):
    pre = b
    for j in range(CONV_WIDTH):
        pre = pre + w[j:j + 1, :] * _shift_down(u, CONV_WIDTH - 1 - j)
    return pre


def _conv_fwd(proj, w, b, n_seq, name):
    t = proj.shape[0]
    seq = t // n_seq
    off = D_MODEL // CONV_CB

    def body(u_ref, w_ref, b_ref, o_ref):
        pre = _conv_pre(u_ref[...], w_ref[...], b_ref[...])
        o_ref[...] = pre * _sigmoid(pre)

    return pl.pallas_call(
        body, name=name, grid=(n_seq, D_CONV // CONV_CB),
        in_specs=[pl.BlockSpec((seq, CONV_CB), lambda s, c: (s, c + off)),
                  pl.BlockSpec((CONV_WIDTH, CONV_CB), lambda s, c: (0, c)),
                  pl.BlockSpec((1, CONV_CB), lambda s, c: (0, c))],
        out_specs=pl.BlockSpec((seq, CONV_CB), lambda s, c: (s, c)),
        out_shape=_sds((t, D_CONV)),
        compiler_params=_params("parallel", "parallel"),
    )(proj, w, b)


def _conv_bwd(proj, w, b, dxbc, n_seq, name):
    t = proj.shape[0]
    seq = t // n_seq
    off = D_MODEL // CONV_CB

    def body(u_ref, w_ref, b_ref, d_ref, du_ref, wg_ref):
        u, wv = u_ref[...], w_ref[...]
        pre = _conv_pre(u, wv, b_ref[...])
        s = _sigmoid(pre)
        dpre = d_ref[...] * (s * (1.0 + pre * (1.0 - s)))
        du = jnp.zeros_like(u)
        parts = []
        for j in range(CONV_WIDTH):
            k = CONV_WIDTH - 1 - j
            du = du + wv[j:j + 1, :] * _shift_up(dpre, k)
            parts.append(_colsum(dpre * _shift_down(u, k)))
        du_ref[...] = du.astype(BF16)
        parts.append(_colsum(dpre))

        @pl.when(pl.program_id(1) == 0)
        def _():
            wg_ref[...] = jnp.zeros_like(wg_ref)

        wg_ref[...] += _stack_rows(parts, u.shape[1])

    return pl.pallas_call(
        body, name=name, grid=(D_CONV // CONV_CB, n_seq),
        in_specs=[pl.BlockSpec((seq, CONV_CB), lambda c, s: (s, c + off)),
                  pl.BlockSpec((CONV_WIDTH, CONV_CB), lambda c, s: (0, c)),
                  pl.BlockSpec((1, CONV_CB), lambda c, s: (0, c)),
                  pl.BlockSpec((seq, CONV_CB), lambda c, s: (s, c))],
        out_specs=[pl.BlockSpec((seq, CONV_CB), lambda c, s: (s, c)),
                   pl.BlockSpec((8, CONV_CB), lambda c, s: (0, c))],
        out_shape=[_sds((t, D_CONV), BF16), _sds((8, D_CONV))],
        compiler_params=_params("parallel", "arbitrary"),
    )(proj, w, b, dxbc)


def _iota2(shape, axis):
    return lax.broadcasted_iota(jnp.int32, shape, axis)


def _dot(a, b, dims=_DIMS["nn"], precision=None):
    return lax.dot_general(a, b, dims, precision=precision, preferred_element_type=F32)


def _bdot(a, b, dims=_DIMS["nn"]):
    return lax.dot_general(a.astype(BF16), b.astype(BF16), dims, preferred_element_type=F32)


def _expand_mat():
    return (_iota2((LANES, D_MODEL), 0) == lax.shift_right_logical(_iota2((LANES, D_MODEL), 1), 6)).astype(F32)


def _reduce_mat():
    return (lax.shift_right_logical(_iota2((D_MODEL, LANES), 0), 6) == _iota2((D_MODEL, LANES), 1)).astype(F32)


def _ssd_decay(dtraw, bias, alog):
    row, col = _iota2((CHUNK, CHUNK), 0), _iota2((CHUNK, CHUNK), 1)
    pre = dtraw + bias
    dtb = jnp.maximum(pre, 0.0) + jnp.log(1.0 + jnp.exp(-jnp.abs(pre)))
    a_neg = -jnp.exp(alog)
    a = dtb * a_neg
    tril = (row >= col).astype(F32)
    triu = (row <= col).astype(F32)
    cs = _dot(tril, a, precision=HIGHEST)
    cs_t = _dot(a, triu, _DIMS["tn"], precision=HIGHEST)
    return pre, dtb, a_neg, cs, cs_t


def _pair_rowscale(vec, h0):
    top = _iota2((CHUNK, LANES), 0) < HEAD_DIM
    return jnp.where(top, vec[:, h0:h0 + 1], vec[:, h0 + 1:h0 + 2])


def _decay_mat(cs, cs_t, h):
    row, col = _iota2((CHUNK, CHUNK), 0), _iota2((CHUNK, CHUNK), 1)
    seg = cs[:, h:h + 1] - cs_t[h:h + 1, :]
    return jnp.exp(jnp.where(row >= col, seg, -jnp.inf))


def _ssd_fwd(xbc, dtraw, bias, alog, dskip_x, n_seq, name):
    t = xbc.shape[0]
    nc = t // n_seq // CHUNK

    def body(x_ref, b_ref, c_ref, dt_ref, bias_ref, alog_ref, dsk_ref, y_ref, st_ref, h_ref):
        @pl.when(pl.program_id(1) == 0)
        def _():
            h_ref[...] = jnp.zeros_like(h_ref)

        _, dtb, _, cs, cs_t = _ssd_decay(dt_ref[...], bias_ref[...], alog_ref[...])
        expand = _expand_mat()
        dt_x = _dot(dtb, expand, precision=HIGHEST)
        cs_x = _dot(cs, expand, precision=HIGHEST)
        tot = cs[CHUNK - 1:CHUNK, :]
        etot = jnp.exp(tot)
        x = x_ref[...]
        xdt = x * dt_x
        e_x = jnp.exp(cs_x)
        xdec = xdt * jnp.exp(cs_x[CHUNK - 1:CHUNK, :] - cs_x)
        keeps = [_iota2((CHUNK, LANES), 1) < HEAD_DIM, _iota2((CHUNK, LANES), 1) >= HEAD_DIM]
        for q in range(N_PAIRS):
            g = q // 2
            lanes = slice(q * LANES, (q + 1) * LANES)
            bg = b_ref[:, g * N_STATE:(g + 1) * N_STATE]
            cg = c_ref[:, g * N_STATE:(g + 1) * N_STATE]
            gmat = _bdot(cg, bg, _DIMS["nt"])
            xdt_q = xdt[:, lanes]
            ydiag = jnp.zeros((CHUNK, LANES), F32)
            for r in range(2):
                w = gmat * _decay_mat(cs, cs_t, 2 * q + r)
                ydiag = ydiag + _bdot(w, jnp.where(keeps[r], xdt_q, 0.0))
            prev = h_ref[q]
            yoff = _bdot(cg, prev, _DIMS["nt"]) * e_x[:, lanes]
            y_ref[:, lanes] = ydiag + yoff + x[:, lanes] * dsk_ref[:, lanes]
            st_ref[0, q] = prev
            h_ref[q] = prev * _pair_rowscale(etot, 2 * q) + _bdot(xdec[:, lanes], bg, _DIMS["tn"])

    vec = pl.BlockSpec((1, LANES), lambda s, c: (0, 0))
    return pl.pallas_call(
        body, name=name, grid=(n_seq, nc),
        in_specs=[pl.BlockSpec((CHUNK, D_MODEL), lambda s, c: (s * nc + c, 0)),
                  pl.BlockSpec((CHUNK, N_GROUPS * N_STATE), lambda s, c: (s * nc + c, 2)),
                  pl.BlockSpec((CHUNK, N_GROUPS * N_STATE), lambda s, c: (s * nc + c, 3)),
                  pl.BlockSpec((CHUNK, LANES), lambda s, c: (s * nc + c, 0)),
                  vec, vec, pl.BlockSpec((1, D_MODEL), lambda s, c: (0, 0))],
        out_specs=[pl.BlockSpec((CHUNK, D_MODEL), lambda s, c: (s * nc + c, 0)),
                   pl.BlockSpec((1, N_PAIRS, LANES, N_STATE), lambda s, c: (s * nc + c, 0, 0, 0))],
        out_shape=[_sds((t, D_MODEL)), _sds((t // CHUNK, N_PAIRS, LANES, N_STATE))],
        scratch_shapes=[pltpu.VMEM((N_PAIRS, LANES, N_STATE), F32)],
        compiler_params=_params("parallel", "arbitrary"),
    )(xbc, xbc, xbc, dtraw, bias, alog, dskip_x)


def _ssd_bwd(xbc, dtraw, bias, alog, dskip_x, states, dy, n_seq, name):
    t = xbc.shape[0]
    nc = t // n_seq // CHUNK

    def body(x_ref, b_ref, c_ref, dt_ref, bias_ref, alog_ref, dsk_ref, st_ref, dy_ref,
             dxbc_ref, ddt_ref, pg_ref, dh_ref):
        first = jnp.logical_and(pl.program_id(0) == 0, pl.program_id(1) == 0)

        @pl.when(first)
        def _():
            pg_ref[...] = jnp.zeros_like(pg_ref)

        @pl.when(pl.program_id(1) == 0)
        def _():
            dh_ref[...] = jnp.zeros_like(dh_ref)

        row, col = _iota2((CHUNK, CHUNK), 0), _iota2((CHUNK, CHUNK), 1)
        pre, dtb, a_neg, cs, cs_t = _ssd_decay(dt_ref[...], bias_ref[...], alog_ref[...])
        expand, reduce = _expand_mat(), _reduce_mat()
        dt_x = _dot(dtb, expand, precision=HIGHEST)
        cs_x = _dot(cs, expand, precision=HIGHEST)
        etot = jnp.exp(cs[CHUNK - 1:CHUNK, :])
        x, dy = x_ref[...], dy_ref[...]
        xdt = x * dt_x
        e_x = jnp.exp(cs_x)
        dec_x = jnp.exp(cs_x[CHUNK - 1:CHUNK, :] - cs_x)
        xdec = xdt * dec_x
        dye = dy * e_x
        keeps = [_iota2((CHUNK, LANES), 1) < HEAD_DIM, _iota2((CHUNK, LANES), 1) >= HEAD_DIM]
        row_lo = _iota2((CHUNK, LANES), 0) < HEAD_DIM
        dcs_col = jnp.zeros((CHUNK, LANES), F32)
        dcs_row = jnp.zeros((LANES, CHUNK), F32)
        dtot = jnp.zeros((1, LANES), F32)
        dxdt_parts, zdec_parts, yoff_parts = [], [], []
        for g in range(N_GROUPS):
            bg = b_ref[:, g * N_STATE:(g + 1) * N_STATE]
            cg = c_ref[:, g * N_STATE:(g + 1) * N_STATE]
            gmat = _bdot(cg, bg, _DIMS["nt"])
            dgmat = jnp.zeros((CHUNK, CHUNK), F32)
            dbg = jnp.zeros((CHUNK, N_STATE), F32)
            dcg = jnp.zeros((CHUNK, N_STATE), F32)
            for q in (2 * g, 2 * g + 1):
                lanes = slice(q * LANES, (q + 1) * LANES)
                xdt_q, dy_q = xdt[:, lanes], dy[:, lanes]
                dxdt_q = jnp.zeros((CHUNK, LANES), F32)
                for r in range(2):
                    h = 2 * q + r
                    keep = keeps[r]
                    lm = _decay_mat(cs, cs_t, h)
                    w = gmat * lm
                    dy_h = jnp.where(keep, dy_q, 0.0)
                    dm = _bdot(dy_h, xdt_q, _DIMS["nt"])
                    dxdt_q = dxdt_q + _bdot(w, dy_h, _DIMS["tn"])
                    dgmat = dgmat + dm * lm
                    tmat = dm * w
                    dcs_col = dcs_col + jnp.where(col == h, jnp.sum(tmat, axis=1, keepdims=True), 0.0)
                    dcs_row = dcs_row - jnp.where(row == h, jnp.sum(tmat, axis=0, keepdims=True), 0.0)
                prev = st_ref[0, q]
                dht = dh_ref[q]
                dxdtdec = _bdot(bg, dht, _DIMS["nt"])
                dxdt_q = dxdt_q + dxdtdec * dec_x[:, lanes]
                zdec_parts.append(dxdtdec * xdec[:, lanes])
                dbg = dbg + _bdot(xdec[:, lanes], dht)
                yoff_parts.append(dy_q * (_bdot(cg, prev, _DIMS["nt"]) * e_x[:, lanes]))
                dcg = dcg + _bdot(dye[:, lanes], prev)
                dprev = _bdot(dye[:, lanes], cg, _DIMS["tn"])
                hp = jnp.sum(dht * prev, axis=1, keepdims=True)
                d0 = jnp.sum(jnp.where(row_lo[:, :1], hp, 0.0), axis=0, keepdims=True)
                d1 = jnp.sum(jnp.where(row_lo[:, :1], 0.0, hp), axis=0, keepdims=True)
                lane1 = _iota2((1, LANES), 1)
                dtot = dtot + jnp.where(lane1 == 2 * q, d0, 0.0) + jnp.where(lane1 == 2 * q + 1, d1, 0.0)
                dh_ref[q] = dprev + dht * _pair_rowscale(etot, 2 * q)
                dxdt_parts.append(dxdt_q)
            dcg = dcg + _bdot(dgmat, bg)
            dbg = dbg + _bdot(dgmat, cg, _DIMS["tn"])
            dxbc_ref[:, D_MODEL + g * N_STATE:D_MODEL + (g + 1) * N_STATE] = dbg
            dxbc_ref[:, D_MODEL + (N_GROUPS + g) * N_STATE:D_MODEL + (N_GROUPS + g + 1) * N_STATE] = dcg
        dxdt = jnp.concatenate(dxdt_parts, axis=1)
        zdec = _dot(jnp.concatenate(zdec_parts, axis=1), reduce, precision=HIGHEST)
        yoff_d = _dot(jnp.concatenate(yoff_parts, axis=1), reduce, precision=HIGHEST)
        dtot = dtot * etot + _colsum(zdec)
        last = row[:, :LANES] == CHUNK - 1
        dcs_col = dcs_col + yoff_d - zdec + jnp.where(last, dtot, 0.0)
        triu = (row <= col).astype(F32)
        da = _dot(triu, dcs_col, precision=HIGHEST) + _dot(triu, dcs_row, _DIMS["nt"], precision=HIGHEST)
        ddt = _dot(dxdt * x, reduce, precision=HIGHEST) + da * a_neg
        ddtraw = ddt * _sigmoid(pre)
        ddt_ref[...] = ddtraw.astype(BF16)
        dxbc_ref[:, :D_MODEL] = dxdt * dt_x + dy * dsk_ref[...]
        dskip = _dot(jnp.broadcast_to(_colsum(dy * x), (8, D_MODEL)), reduce, precision=HIGHEST)[0:1, :]
        pg_ref[...] += _stack_rows([_colsum(ddtraw), _colsum(da * dtb) * a_neg, dskip], LANES)

    vec = pl.BlockSpec((1, LANES), lambda s, c: (0, 0))

    def blk(s, c):
        return s * nc + (nc - 1 - c)

    return pl.pallas_call(
        body, name=name, grid=(n_seq, nc),
        in_specs=[pl.BlockSpec((CHUNK, D_MODEL), lambda s, c: (blk(s, c), 0)),
                  pl.BlockSpec((CHUNK, N_GROUPS * N_STATE), lambda s, c: (blk(s, c), 2)),
                  pl.BlockSpec((CHUNK, N_GROUPS * N_STATE), lambda s, c: (blk(s, c), 3)),
                  pl.BlockSpec((CHUNK, LANES), lambda s, c: (blk(s, c), 0)),
                  vec, vec, pl.BlockSpec((1, D_MODEL), lambda s, c: (0, 0)),
                  pl.BlockSpec((1, N_PAIRS, LANES, N_STATE), lambda s, c: (blk(s, c), 0, 0, 0)),
                  pl.BlockSpec((CHUNK, D_MODEL), lambda s, c: (blk(s, c), 0))],
        out_specs=[pl.BlockSpec((CHUNK, D_CONV), lambda s, c: (blk(s, c), 0)),
                   pl.BlockSpec((CHUNK, LANES), lambda s, c: (blk(s, c), 0)),
                   pl.BlockSpec((8, LANES), lambda s, c: (0, 0))],
        out_shape=[_sds((t, D_CONV)), _sds((t, LANES), BF16), _sds((8, LANES))],
        scratch_shapes=[pltpu.VMEM((N_PAIRS, LANES, N_STATE), F32)],
        compiler_params=_params("arbitrary", "arbitrary"),
    )(xbc, xbc, xbc, dtraw, bias, alog, dskip_x, states, dy)


Q_COL = (D_MODEL + D_CONV) // LANES
K_COL = Q_COL + D_MODEL // LANES
V_COL = K_COL + D_MODEL // LANES
ATT_SCALE = HEAD_DIM ** -0.5


ATT_SUB = 2
ATT_TILE = ATT_SUB * CHUNK
ATT_NP = 2


ATT_ROWS = 2 * ATT_TILE


def _tri_dot(x, tri):
    hi = x.astype(BF16)
    lo = (x - hi.astype(F32)).astype(BF16)
    both = _dot(jnp.concatenate([hi, lo], axis=0), tri)
    return both[:x.shape[0]] + both[x.shape[0]:]


def _att_stack(ref, lanes, keeps, scale=1.0):
    parts = []
    for a in range(ATT_SUB):
        blk = ref[a * CHUNK:(a + 1) * CHUNK, lanes] * scale
        parts += [jnp.where(keeps[r], blk, 0.0).astype(BF16) for r in range(2)]
    return jnp.concatenate(parts, axis=0)


def _att_unstack(x, a, keeps):
    return jnp.where(keeps[0], x[2 * a * CHUNK:(2 * a + 1) * CHUNK], x[(2 * a + 1) * CHUNK:(2 * a + 2) * CHUNK])


def _att_logits(q_stack, k_tile, diag):
    s = _dot(q_stack, k_tile, _DIMS["nt"])
    lb = jnp.minimum(s, 0.0) - jnp.log(1.0 + jnp.exp(-jnp.abs(s)))
    l1m = lb - s
    mask = None
    if diag:
        srow, scol = _iota2((ATT_ROWS, ATT_TILE), 0), _iota2((ATT_ROWS, ATT_TILE), 1)
        qpos = lax.shift_right_logical(srow, 8) * CHUNK + jnp.bitwise_and(srow, CHUNK - 1)
        mask = qpos > scol
        l1m = jnp.where(mask, l1m, 0.0)
    return lb, l1m, mask


def _att_fwd(proj, n_seq, name):
    t = proj.shape[0]
    seq = t // n_seq
    nq = seq // ATT_TILE

    def body(q_ref, k_ref, v_ref, o_ref, rt_ref):
        i = pl.program_id(2)
        upper = (_iota2((ATT_TILE, ATT_TILE), 0) > _iota2((ATT_TILE, ATT_TILE), 1)).astype(BF16)
        keeps = [_iota2((CHUNK, LANES), 1) < HEAD_DIM, _iota2((CHUNK, LANES), 1) >= HEAD_DIM]
        pair_lanes = [slice(pr * LANES, (pr + 1) * LANES) for pr in range(ATT_NP)]
        q_stacks = [_att_stack(q_ref, lanes, keeps, ATT_SCALE) for lanes in pair_lanes]
        per_pair = ATT_SUB + 1

        def tile(jt, carry, diag):
            state = list(carry)
            k0 = pl.multiple_of(jt * ATT_TILE, ATT_TILE)
            for pr, lanes in enumerate(pair_lanes):
                base = pr * per_pair
                run = state[base + ATT_SUB]
                k_tile = k_ref[pl.ds(k0, ATT_TILE), lanes].astype(BF16)
                v_tile = v_ref[pl.ds(k0, ATT_TILE), lanes].astype(BF16)
                lb, l1m, mask = _att_logits(q_stacks[pr], k_tile, diag)
                p = jnp.exp(lb + (_tri_dot(l1m, upper) + run))
                if diag:
                    p = jnp.where(mask, p, 0.0)
                pv = _dot(p.astype(BF16), v_tile)
                for a in range(ATT_SUB):
                    state[base + a] = state[base + a] + _att_unstack(pv, a, keeps)
                state[base + ATT_SUB] = run + jnp.sum(l1m, axis=1, keepdims=True)
            return tuple(state)

        init = (tuple(jnp.zeros((CHUNK, LANES), F32) for _ in range(ATT_SUB)) + (jnp.zeros((ATT_ROWS, 1), F32),)) * ATT_NP
        carry = tile(i, init, True)
        carry = lax.fori_loop(0, i, lambda it, c: tile(i - 1 - it, c, False), carry)
        for pr, lanes in enumerate(pair_lanes):
            for a in range(ATT_SUB):
                o_ref[a * CHUNK:(a + 1) * CHUNK, lanes] = carry[pr * per_pair + a]
                rt_ref[a * CHUNK:(a + 1) * CHUNK, lanes] = _att_unstack(carry[pr * per_pair + ATT_SUB], a, keeps)

    width = ATT_NP * LANES
    qblk = pl.BlockSpec((ATT_TILE, width), lambda s, p, i: (s * nq + i, p))
    return pl.pallas_call(
        body, name=name, grid=(n_seq, N_PAIRS // ATT_NP, nq),
        in_specs=[pl.BlockSpec((ATT_TILE, width), lambda s, p, i: (s * nq + i, Q_COL // ATT_NP + p)),
                  pl.BlockSpec((seq, width), lambda s, p, i: (s, K_COL // ATT_NP + p)),
                  pl.BlockSpec((seq, width), lambda s, p, i: (s, V_COL // ATT_NP + p))],
        out_specs=[qblk, qblk],
        out_shape=[_sds((t, D_MODEL)), _sds((t, D_MODEL))],
        compiler_params=_params("parallel", "parallel", "arbitrary"),
    )(proj, proj, proj)


def _att_bwd(proj, rtot, datt, n_seq, name):
    t = proj.shape[0]
    seq = t // n_seq
    nq = seq // ATT_TILE

    def body(q_ref, k_ref, v_ref, rt_ref, do_ref, dq_ref, dk_out, dv_out, dk_ref, dv_ref):
        i = pl.program_id(2)

        @pl.when(i == 0)
        def _():
            dk_ref[...] = jnp.zeros_like(dk_ref)
            dv_ref[...] = jnp.zeros_like(dv_ref)

        row, col = _iota2((ATT_TILE, ATT_TILE), 0), _iota2((ATT_TILE, ATT_TILE), 1)
        upper = (row > col).astype(BF16)
        before = (row < col).astype(BF16)
        keeps = [_iota2((CHUNK, LANES), 1) < HEAD_DIM, _iota2((CHUNK, LANES), 1) >= HEAD_DIM]
        pair_lanes = [slice(pr * LANES, (pr + 1) * LANES) for pr in range(ATT_NP)]
        q_stacks = [_att_stack(q_ref, lanes, keeps, ATT_SCALE) for lanes in pair_lanes]
        do_stacks = [_att_stack(do_ref, lanes, keeps) for lanes in pair_lanes]
        totals = [jnp.concatenate([rt_ref[a * CHUNK:(a + 1) * CHUNK, pr * LANES + r * HEAD_DIM:pr * LANES + r * HEAD_DIM + 1]
                                   for a in range(ATT_SUB) for r in range(2)], axis=0) for pr in range(ATT_NP)]
        per_pair = ATT_SUB + 2

        def tile(jt, carry, diag):
            state = list(carry)
            k0 = pl.multiple_of(jt * ATT_TILE, ATT_TILE)
            for pr, lanes in enumerate(pair_lanes):
                base = pr * per_pair
                seen, dseen = state[base + ATT_SUB], state[base + ATT_SUB + 1]
                q_stack, do_stack = q_stacks[pr], do_stacks[pr]
                k_tile = k_ref[pl.ds(k0, ATT_TILE), lanes].astype(BF16)
                v_tile = v_ref[pl.ds(k0, ATT_TILE), lanes].astype(BF16)
                lb, l1m, mask = _att_logits(q_stack, k_tile, diag)
                here = jnp.sum(l1m, axis=1, keepdims=True)
                p = jnp.exp(lb + (_tri_dot(l1m, upper) + (totals[pr] - seen - here)))
                if diag:
                    p = jnp.where(mask, p, 0.0)
                pb = p.astype(BF16)
                dz = _dot(do_stack, v_tile, _DIMS["nt"]) * p
                dl1m = dseen + _tri_dot(dz, before)
                sig = jnp.exp(lb)
                ds = dz * (1.0 - sig) - dl1m * sig
                if diag:
                    ds = jnp.where(mask, ds, 0.0)
                dsb = ds.astype(BF16)
                dq_all = _dot(dsb, k_tile)
                for a in range(ATT_SUB):
                    state[base + a] = state[base + a] + _att_unstack(dq_all, a, keeps)
                dk_ref[pl.ds(k0, ATT_TILE), lanes] += _dot(dsb, q_stack, _DIMS["tn"])
                dv_ref[pl.ds(k0, ATT_TILE), lanes] += _dot(pb, do_stack, _DIMS["tn"])
                state[base + ATT_SUB] = seen + here
                state[base + ATT_SUB + 1] = dseen + jnp.sum(dz, axis=1, keepdims=True)
            return tuple(state)

        init = (tuple(jnp.zeros((CHUNK, LANES), F32) for _ in range(ATT_SUB)) +
                (jnp.zeros((ATT_ROWS, 1), F32), jnp.zeros((ATT_ROWS, 1), F32))) * ATT_NP
        carry = lax.fori_loop(0, i, lambda jt, c: tile(jt, c, False), init)
        carry = tile(i, carry, True)
        for pr, lanes in enumerate(pair_lanes):
            for a in range(ATT_SUB):
                dq_ref[a * CHUNK:(a + 1) * CHUNK, lanes] = (carry[pr * per_pair + a] * ATT_SCALE).astype(BF16)

        @pl.when(i == nq - 1)
        def _():
            dk_out[...] = dk_ref[...].astype(BF16)
            dv_out[...] = dv_ref[...].astype(BF16)

    width = ATT_NP * LANES
    qblk = pl.BlockSpec((ATT_TILE, width), lambda s, p, i: (s * nq + i, p))
    kv_out = pl.BlockSpec((seq, width), lambda s, p, i: (s, p))
    return pl.pallas_call(
        body, name=name, grid=(n_seq, N_PAIRS // ATT_NP, nq),
        in_specs=[pl.BlockSpec((ATT_TILE, width), lambda s, p, i: (s * nq + i, Q_COL // ATT_NP + p)),
                  pl.BlockSpec((seq, width), lambda s, p, i: (s, K_COL // ATT_NP + p)),
                  pl.BlockSpec((seq, width), lambda s, p, i: (s, V_COL // ATT_NP + p)),
                  qblk, qblk],
        out_specs=[qblk, kv_out, kv_out],
        out_shape=[_sds((t, D_MODEL), BF16)] * 3,
        scratch_shapes=[pltpu.VMEM((seq, width), F32), pltpu.VMEM((seq, width), F32)],
        compiler_params=_params("parallel", "parallel", "arbitrary"),
    )(proj, proj, proj, rtot, datt)


def _pad_lanes(v):
    return jnp.pad(v.reshape(1, -1), ((0, 0), (0, LANES - v.shape[0])))


def _layer_fwd(x, p, n_seq, tag):
    h = _rms_fwd(x, p["norm_mix_g"], f"rms_mix_fwd{tag}")
    proj, = _matmul(h, p["w_main"], "nn", [F32], f"in_proj{tag}", tn=1536)
    dtraw, = _matmul(h, p["w_dt"], "nn", [F32], f"dt_proj{tag}")
    xbc = _conv_fwd(proj, p["conv_w"], p["conv_b"], n_seq, f"conv_fwd{tag}")
    y, states = _ssd_fwd(xbc, dtraw, p["dt_bias"], p["a_log"], p["d_skip_x"], n_seq, f"ssd_fwd{tag}")
    att, rtot = _att_fwd(proj, n_seq, f"att_fwd{tag}")
    ycat = _mixnorm_fwd(y, proj, att, p["ssd_norm_g"], p["att_norm_g"], f"mixnorm_fwd{tag}")
    x1, = _matmul(ycat, p["w_out"], "nn", [F32], f"out_proj{tag}", extras=[x], epilogue=lambda acc, xb: (xb + acc,))
    h2 = _rms_fwd(x1, p["norm_mlp_g"], f"rms_mlp_fwd{tag}")
    u, act = _matmul(h2, p["w_up"], "nn", [F32, BF16], f"up_proj{tag}",
                     epilogue=lambda acc: (acc, jnp.square(jnp.maximum(acc, 0.0))))
    x2, = _matmul(act, p["w_down"], "nn", [F32], f"down_proj{tag}", extras=[x1], epilogue=lambda acc, xb: (xb + acc,))
    saved = dict(x=x, h=h, proj=proj, dtraw=dtraw, xbc=xbc, y=y, states=states, att=att, rtot=rtot, ycat=ycat,
                 x1=x1, h2=h2, u=u, act=act)
    return x2, saved


def _layer_bwd(dx2, dx2b, p, s, n_seq, tag):
    g = {}
    g["w_down"], = _matmul(s["act"], dx2b, "tn", [F32], f"dw_down{tag}")
    du, = _matmul(dx2b, p["w_down"], "nt", [BF16], f"d_act{tag}", extras=[s["u"]],
                  epilogue=lambda acc, ub: (acc * (2.0 * jnp.maximum(ub, 0.0)),))
    g["w_up"], = _matmul(s["h2"], du, "tn", [F32], f"dw_up{tag}")
    dh2, = _matmul(du, p["w_up"], "nt", [F32], f"d_h2{tag}")
    dx1, dx1b, g["norm_mlp_g"] = _rms_bwd(s["x1"], p["norm_mlp_g"], dh2, dx2, f"rms_mlp_bwd{tag}")
    g["w_out"], = _matmul(s["ycat"], dx1b, "tn", [F32], f"dw_out{tag}")
    dycat, = _matmul(dx1b, p["w_out"], "nt", [F32], f"d_ycat{tag}")
    dy, dz, datt, g["ssd_norm_g"], g["att_norm_g"] = _mixnorm_bwd(
        s["y"], s["proj"], s["att"], p["ssd_norm_g"], p["att_norm_g"], dycat, f"mixnorm_bwd{tag}")
    dq, dk, dv = _att_bwd(s["proj"], s["rtot"], datt, n_seq, f"att_bwd{tag}")
    dxbc, ddtraw, pg = _ssd_bwd(s["xbc"], s["dtraw"], p["dt_bias"], p["a_log"], p["d_skip_x"], s["states"], dy,
                                n_seq, f"ssd_bwd{tag}")
    g["dt_bias"], g["a_log"], g["d_skip"] = pg[0, :N_HEADS], pg[1, :N_HEADS], pg[2, :N_HEADS]
    du_conv, wg = _conv_bwd(s["proj"], p["conv_w"], p["conv_b"], dxbc, n_seq, f"conv_bwd{tag}")
    g["conv_w"], g["conv_b"] = wg[:CONV_WIDTH], wg[CONV_WIDTH]
    dproj = jnp.concatenate([dz, du_conv, dq, dk, dv], axis=1)
    g["w_main"], = _matmul(s["h"], dproj, "tn", [F32], f"dw_in{tag}", tn=1536)
    g["w_dt"], = _matmul(s["h"], ddtraw, "tn", [F32], f"dw_dt{tag}")
    dh_dt, = _matmul(ddtraw, p["w_dt"], "nt", [F32], f"d_h_dt{tag}")
    dh, = _matmul(dproj, p["w_main"], "nt", [F32], f"d_h{tag}", extras=[dh_dt], epilogue=lambda acc, e: (acc + e,))
    dx, dxb, g["norm_mix_g"] = _rms_bwd(s["x"], p["norm_mix_g"], dh, dx1, f"rms_mix_bwd{tag}")
    return dx, dxb, g


def _split_w_in(w_full):
    c0 = D_MODEL + D_CONV
    main = jnp.concatenate([w_full[:, :c0], w_full[:, c0 + N_HEADS:]], axis=1)
    dt = jnp.pad(w_full[:, c0:c0 + N_HEADS], ((0, 0), (0, LANES - N_HEADS)))
    return main, dt


def _merge_w_in(main, dt):
    c0 = D_MODEL + D_CONV
    return jnp.concatenate([main[:, :c0], dt[:, :N_HEADS], main[:, c0:]], axis=1)


def _prep_layer(l, w_in_full, w_out, w_up, w_down, conv_w, small):
    w_main, w_dt = _split_w_in(w_in_full)
    return dict(
        w_main=w_main, w_dt=w_dt, w_out=w_out, w_up=w_up, w_down=w_down, conv_w=conv_w,
        conv_b=small["conv_b"][l].reshape(1, -1),
        dt_bias=_pad_lanes(small["dt_bias"][l]), a_log=_pad_lanes(small["a_log"][l]),
        d_skip_x=jnp.repeat(small["d_skip"][l], HEAD_DIM).reshape(1, -1),
        norm_mix_g=small["norm_mix_g"][l].reshape(1, -1), ssd_norm_g=small["ssd_norm_g"][l].reshape(1, -1),
        att_norm_g=small["att_norm_g"][l].reshape(1, -1), norm_mlp_g=small["norm_mlp_g"][l].reshape(1, -1),
    )


ANY = pl.BlockSpec(memory_space=pl.ANY)


def _place():
    x, y, c = lax.axis_index("x"), lax.axis_index("y"), lax.axis_index("c")
    return x, y, c, (x, y, 1 - c), [(1 - x, y), (x, 1 - y), (1 - x, 1 - y)]


def _all_gather_chips(shard, chip, name):
    r, ccols = shard.shape
    half = r // 2

    def body(s_ref, o_ref, send_sems, recv_sems):
        x, y, c, sibling, chips = _place()

        def slab(px, py, hc):
            return o_ref.at[2 * px + py, pl.ds(hc * half, half), :]

        def copy(k, src, dst, to):
            return pltpu.make_async_remote_copy(src_ref=src, dst_ref=dst, send_sem=send_sems.at[k],
                                                recv_sem=recv_sems.at[k], device_id=to, device_id_type=MESH)

        sends = [copy(k, s_ref.at[pl.ds(c * half, half), :], slab(x, y, c), (px, py, c))
                 for k, (px, py) in enumerate(chips)]
        for cp in sends:
            cp.start()
        passed = []
        for k, (px, py) in enumerate(chips):
            copy(k, slab(px, py, c), slab(px, py, c), (px, py, c)).wait_recv()
            cp = copy(3 + k, slab(px, py, c), slab(px, py, c), sibling)
            cp.start()
            passed.append(cp)
        for k, (px, py) in enumerate(chips):
            copy(3 + k, slab(px, py, 1 - c), slab(px, py, 1 - c), sibling).wait_recv()
        for cp in sends + passed:
            cp.wait_send()

    others = pl.pallas_call(
        body, name=name, in_specs=[ANY], out_specs=ANY,
        out_shape=_sds((N_CHIPS, r, ccols), shard.dtype),
        scratch_shapes=[pltpu.SemaphoreType.DMA((6,)), pltpu.SemaphoreType.DMA((6,))],
    )(shard)
    return lax.dynamic_update_slice(others, shard[None], (chip, 0, 0))


def _sibling_swap(g, name):
    n, r, ccols = g.shape
    half = r // 2

    def body(g_ref, o_ref, send_sem, recv_sem):
        _, _, c, sibling, _ = _place()
        cp = pltpu.make_async_remote_copy(src_ref=g_ref.at[:, pl.ds((1 - c) * half, half), :], dst_ref=o_ref,
                                          send_sem=send_sem, recv_sem=recv_sem, device_id=sibling, device_id_type=MESH)
        cp.start()
        cp.wait()

    return pl.pallas_call(
        body, name=name, in_specs=[ANY], out_specs=ANY, out_shape=_sds((n, half, ccols), g.dtype),
        scratch_shapes=[pltpu.SemaphoreType.DMA, pltpu.SemaphoreType.DMA],
    )(g)


def _chip_exchange(p, name):
    _, h, ccols = p.shape

    def body(p_ref, o_ref, send_sems, recv_sems):
        _, _, c, _, chips = _place()
        copies = [pltpu.make_async_remote_copy(src_ref=p_ref.at[2 * px + py], dst_ref=o_ref.at[k],
                                               send_sem=send_sems.at[k], recv_sem=recv_sems.at[k],
                                               device_id=(px, py, c), device_id_type=MESH)
                  for k, (px, py) in enumerate(chips)]
        for cp in copies:
            cp.start()
        for cp in copies:
            cp.wait()

    return pl.pallas_call(
        body, name=name, in_specs=[ANY], out_specs=ANY, out_shape=_sds((3, h, ccols), p.dtype),
        scratch_shapes=[pltpu.SemaphoreType.DMA((3,)), pltpu.SemaphoreType.DMA((3,))],
    )(p)


def _sibling_share(full, name):
    r, ccols = full.shape
    h = r // 2

    def body(f_ref, o_ref, send_sem, recv_sem):
        _, _, c, sibling, _ = _place()
        mine = pl.ds(c * h, h)
        cp = pltpu.make_async_remote_copy(src_ref=f_ref.at[mine, :], dst_ref=o_ref.at[mine, :], send_sem=send_sem,
                                          recv_sem=recv_sem, device_id=sibling, device_id_type=MESH)
        cp.start()
        theirs = o_ref.at[pl.ds((1 - c) * h, h), :]
        pltpu.make_async_remote_copy(src_ref=theirs, dst_ref=theirs, send_sem=send_sem, recv_sem=recv_sem,
                                     device_id=sibling, device_id_type=MESH).wait_recv()
        cp.wait_send()

    return pl.pallas_call(
        body, name=name, in_specs=[ANY], out_specs=ANY, out_shape=_sds((r, ccols), full.dtype),
        input_output_aliases={0: 0},
        scratch_shapes=[pltpu.SemaphoreType.DMA, pltpu.SemaphoreType.DMA],
    )(full)


def _add_halves(g, a, c, name):
    n, r, ccols = g.shape
    half = r // 2
    tr = _tile(half, 256)
    nb = half // tr

    def body(c_ref, g_ref, a_ref, o_ref):
        o_ref[...] = (g_ref[...] + a_ref[...]).astype(BF16)

    blk = (1, tr, ccols)
    return pl.pallas_call(
        body, name=name,
        grid_spec=pltpu.PrefetchScalarGridSpec(
            num_scalar_prefetch=1, grid=(n, nb),
            in_specs=[pl.BlockSpec(blk, lambda j, i, c_ref: (j, c_ref[0] * nb + i, 0)),
                      pl.BlockSpec(blk, lambda j, i, c_ref: (j, i, 0))],
            out_specs=pl.BlockSpec(blk, lambda j, i, c_ref: (j, i, 0))),
        out_shape=_sds((n, half, ccols), BF16),
        compiler_params=_params("parallel", "parallel"),
    )(c.reshape(1).astype(jnp.int32), g, a)


def _sum_chips(p, got, chip, c, name):
    _, h, ccols = p.shape
    tr = _tile(h, 256)

    def body(j_ref, h_ref, p_ref, a_ref, b_ref, c_ref, o_ref):
        f32 = [ref[...].astype(F32) for ref in (p_ref, a_ref, b_ref, c_ref)]
        o_ref[...] = ((f32[0] + f32[1]) + f32[2]) + f32[3]

    blk = (1, tr, ccols)

    def slot(k):
        return pl.BlockSpec(blk, lambda i, j_ref, h_ref: (k, i, 0))

    return pl.pallas_call(
        body, name=name,
        grid_spec=pltpu.PrefetchScalarGridSpec(
            num_scalar_prefetch=2, grid=(h // tr,),
            in_specs=[pl.BlockSpec(blk, lambda i, j_ref, h_ref: (j_ref[0], i, 0)), slot(0), slot(1), slot(2)],
            out_specs=pl.BlockSpec(blk, lambda i, j_ref, h_ref: (h_ref[0], i, 0))),
        out_shape=_sds((2, h, ccols)),
        compiler_params=_params("parallel"),
    )(chip.reshape(1).astype(jnp.int32), c.reshape(1).astype(jnp.int32), p, got, got, got)


def _reduce_scatter(g, chip, c, name):
    a = _sibling_swap(g, f"{name}_swap")
    p = _add_halves(g, a, c, f"{name}_add")
    got = _chip_exchange(p, f"{name}_xchg")
    halves = _sum_chips(p, got, chip, c, f"{name}_sum")
    return _sibling_share(halves.reshape(g.shape[1], g.shape[2]), f"{name}_share")


def _all_reduce_small(v, name):
    r = v.shape[0]

    def body(v_ref, o_ref, buf, send_sems, recv_sems):
        x, y, c, _, _ = _place()
        buf[0] = v_ref[...]
        copies = []
        for rel in range(1, 8):
            fx, fy, fc = (rel >> 2) & 1, (rel >> 1) & 1, rel & 1
            peer = (1 - x if fx else x, 1 - y if fy else y, 1 - c if fc else c)
            cp = pltpu.make_async_remote_copy(src_ref=v_ref, dst_ref=buf.at[rel], send_sem=send_sems.at[rel - 1],
                                              recv_sem=recv_sems.at[rel - 1], device_id=peer, device_id_type=MESH)
            cp.start()
            copies.append(cp)
        for cp in copies:
            cp.wait()
        me = 4 * x + 2 * y + c
        acc = buf[jnp.bitwise_xor(me, 0)]
        for src in range(1, 8):
            acc = acc + buf[jnp.bitwise_xor(me, src)]
        o_ref[...] = acc

    vm = pl.BlockSpec(memory_space=pltpu.VMEM)
    return pl.pallas_call(
        body, name=name, in_specs=[vm], out_specs=vm, out_shape=_sds((r, LANES)),
        scratch_shapes=[pltpu.VMEM((8, r, LANES), F32), pltpu.SemaphoreType.DMA((7,)), pltpu.SemaphoreType.DMA((7,))],
    )(v)


WEIGHTS = ["norm_mix_g", "w_in", "conv_w", "conv_b", "dt_bias", "a_log", "d_skip", "ssd_norm_g", "att_norm_g",
           "w_out", "norm_mlp_g", "w_up", "w_down", "final_norm_g"]
BIG = ["w_in", "w_out", "w_up", "w_down"]
SMALL = [n for n in WEIGHTS if n not in BIG]


def _pack(arrays):
    flat = []
    for a in arrays:
        a = a.reshape(-1)
        flat.append(jnp.pad(a, (0, (-a.shape[0]) % LANES)))
    flat = jnp.concatenate(flat)
    flat = jnp.pad(flat, (0, (-flat.shape[0]) % (8 * LANES)))
    return flat.reshape(-1, LANES)


def _unpack(packed, shapes):
    flat, out, pos = packed.reshape(-1), [], 0
    for shp in shapes:
        n = math.prod(shp)
        out.append(flat[pos:pos + n].reshape(shp))
        pos += n + (-n) % LANES
    return out


def _to_shards(name, g):
    if name in ("w_in", "w_up"):
        l, r, ccols = g.shape
        return g.reshape(l, r, N_CHIPS, ccols // N_CHIPS).transpose(2, 0, 1, 3).reshape(N_CHIPS, l * r, ccols // N_CHIPS)
    l, r, ccols = g.shape
    return g.reshape(l, N_CHIPS, r // N_CHIPS, ccols).transpose(1, 0, 2, 3).reshape(N_CHIPS, l * r // N_CHIPS, ccols)


def _from_gathered(name, g, l):
    rows = g.shape[1] // DEPTH
    part = g[:, l * rows:(l + 1) * rows, :]
    if name in ("w_in", "w_up", "conv_w"):
        return part.transpose(1, 0, 2).reshape(rows, N_CHIPS * g.shape[2])
    return part.reshape(N_CHIPS * rows, g.shape[2])


def kernel(x, norm_mix_g, w_in, conv_w, conv_b, dt_bias, a_log, d_skip, ssd_norm_g, att_norm_g, w_out, norm_mlp_g, w_up, w_down, final_norm_g, loss_target, m_norm_mix_g, m_w_in, m_conv_w, m_conv_b, m_dt_bias, m_a_log, m_d_skip, m_ssd_norm_g, m_att_norm_g, m_w_out, m_norm_mlp_g, m_w_up, m_w_down, m_final_norm_g, v_norm_mix_g, v_w_in, v_conv_w, v_conv_b, v_dt_bias, v_a_log, v_d_skip, v_ssd_norm_g, v_att_norm_g, v_w_out, v_norm_mlp_g, v_w_up, v_w_down, v_final_norm_g):
    w = dict(norm_mix_g=norm_mix_g, w_in=w_in, conv_w=conv_w, conv_b=conv_b, dt_bias=dt_bias, a_log=a_log,
             d_skip=d_skip, ssd_norm_g=ssd_norm_g, att_norm_g=att_norm_g, w_out=w_out, norm_mlp_g=norm_mlp_g,
             w_up=w_up, w_down=w_down, final_norm_g=final_norm_g)
    m = dict(norm_mix_g=m_norm_mix_g, w_in=m_w_in, conv_w=m_conv_w, conv_b=m_conv_b, dt_bias=m_dt_bias,
             a_log=m_a_log, d_skip=m_d_skip, ssd_norm_g=m_ssd_norm_g, att_norm_g=m_att_norm_g, w_out=m_w_out,
             norm_mlp_g=m_norm_mlp_g, w_up=m_w_up, w_down=m_w_down, final_norm_g=m_final_norm_g)
    v = dict(norm_mix_g=v_norm_mix_g, w_in=v_w_in, conv_w=v_conv_w, conv_b=v_conv_b, dt_bias=v_dt_bias,
             a_log=v_a_log, d_skip=v_d_skip, ssd_norm_g=v_ssd_norm_g, att_norm_g=v_att_norm_g, w_out=v_w_out,
             norm_mlp_g=v_norm_mlp_g, w_up=v_w_up, w_down=v_w_down, final_norm_g=v_final_norm_g)
    n_seq, seq, d = x.shape
    t = n_seq * seq
    chip = 2 * lax.axis_index("x") + lax.axis_index("y")
    core = lax.axis_index("c")

    gathered = {n: _all_gather_chips(w[n].astype(BF16).reshape(-1, w[n].shape[-1]), chip, f"gather_{n}") for n in BIG}
    gathered["conv_w"] = _all_gather_chips(conv_w.reshape(-1, conv_w.shape[-1]), chip, "gather_conv_w")
    layers = [_prep_layer(l, *[_from_gathered(n, gathered[n], l) for n in BIG + ["conv_w"]], w) for l in range(DEPTH)]

    xs = x.reshape(t, d)
    saved = []
    for l in range(DEPTH):
        xs, s = _layer_fwd(xs, layers[l], n_seq, f"_l{l}")
        saved.append(s)
    loss_vec, dx, dxb, g_final = _final_loss(xs, final_norm_g.reshape(1, d), loss_target.reshape(t, d), "final_loss")
    loss = lax.psum(loss_vec[0, 0], ("x", "y", "c"))

    grads = [None] * DEPTH
    for l in reversed(range(DEPTH)):
        dx, dxb, grads[l] = _layer_bwd(dx, dxb, layers[l], saved[l], n_seq, f"_l{l}")
        grads[l]["w_in"] = _merge_w_in(grads[l].pop("w_main"), grads[l].pop("w_dt"))
    grad_x = dx.reshape(n_seq, seq, d)

    full = {n: jnp.stack([grads[l][n] for l in range(DEPTH)]) for n in WEIGHTS if n != "final_norm_g"}
    full["final_norm_g"] = g_final.reshape(d)
    g_out = {}
    for n in BIG:
        red = _reduce_scatter(_to_shards(n, full[n]), chip, core, f"rs_{n}")
        g_out[n] = red.reshape(w[n].shape)
    small_sum = _all_reduce_small(_pack([full[n] for n in SMALL]), "allreduce_small")
    small_shapes = [(DEPTH, CONV_WIDTH, D_CONV) if n == "conv_w" else w[n].shape for n in SMALL]
    for n, val in zip(SMALL, _unpack(small_sum, small_shapes)):
        g_out[n] = val
    g_out["conv_w"] = lax.dynamic_slice_in_dim(g_out["conv_w"], chip * conv_w.shape[-1], conv_w.shape[-1], axis=2)

    delta, new_m, new_v = {}, {}, {}
    for n in BIG:
        two_d = (-1, w[n].shape[-1])
        dl, mn, vn = _adamw(w[n].reshape(two_d), g_out[n].reshape(two_d), m[n].reshape(two_d), v[n].reshape(two_d),
                            f"adamw_{n}")
        delta[n], new_m[n], new_v[n] = dl.reshape(w[n].shape), mn.reshape(w[n].shape), vn.reshape(w[n].shape)
    packs = [_pack([src[n] for n in SMALL]) for src in (w, g_out, m, v)]
    shapes = [w[n].shape for n in SMALL]
    for dst, packed in zip((delta, new_m, new_v), _adamw(*packs, "adamw_small")):
        for n, val in zip(SMALL, _unpack(packed, shapes)):
            dst[n] = val

    return (loss, grad_x, *[g_out[n] for n in WEIGHTS], *[delta[n] for n in WEIGHTS],
            *[new_m[n] for n in WEIGHTS], *[new_v[n] for n in WEIGHTS])
```

```python
import functools
import math

import jax
import jax.numpy as jnp
from jax import lax
from jax.experimental import pallas as pl
from jax.experimental.pallas import tpu as pltpu

F32 = jnp.float32
BF16 = jnp.bfloat16
HIGHEST = lax.Precision.HIGHEST

D_MODEL = 1024
DEPTH = 4
HEAD_DIM = 64
N_HEADS = 16
N_GROUPS = 4
N_STATE = 128
N_PAIRS = N_HEADS // 2
CONV_WIDTH = 4
CHUNK = 128
D_CONV = D_MODEL + 2 * N_GROUPS * N_STATE
D_MAIN = D_MODEL + D_CONV + 3 * D_MODEL
D_IN_PROJ = D_MAIN + N_HEADS
D_FF = 4 * D_MODEL
EPS = 1e-5
LANES = 128
VMEM_LIMIT = 48 * 1024 * 1024

ADAM_LR = 0.001
ADAM_B1 = 0.9
ADAM_B2 = 0.999
ADAM_EPS = 1e-08
ADAM_WD = 0.01
ADAM_STEP = 10

N_CHIPS = 4
MESH = pl.DeviceIdType.MESH


def _tile(n, cap):
    if n <= cap:
        return n
    t = cap
    while t >= 8:
        if n % t == 0:
            return t
        t //= 2
    raise ValueError(f"no tile for {n} under {cap}")


def _params(*sem):
    return pltpu.CompilerParams(dimension_semantics=sem, vmem_limit_bytes=VMEM_LIMIT)


_DIMS = {"nn": (((1,), (0,)), ((), ())), "nt": (((1,), (1,)), ((), ())), "tn": (((0,), (0,)), ((), ()))}


def _matmul(a, b, mode, out_dtypes, name, extras=(), epilogue=None, tm=1024, tn=1024, tk=1024):
    if mode == "nn":
        (m, k), (_, n) = a.shape, b.shape
    elif mode == "nt":
        (m, k), (n, _) = a.shape, b.shape
    else:
        (k, m), (_, n) = a.shape, b.shape
    tm, tn, tk = _tile(m, tm), _tile(n, tn), _tile(k, tk)
    nk = k // tk
    if mode == "tn":
        a_spec = pl.BlockSpec((tk, tm), lambda i, j, kk: (kk, i))
    else:
        a_spec = pl.BlockSpec((tm, tk), lambda i, j, kk: (i, kk))
    if mode == "nt":
        b_spec = pl.BlockSpec((tn, tk), lambda i, j, kk: (j, kk))
    else:
        b_spec = pl.BlockSpec((tk, tn), lambda i, j, kk: (kk, j))
    mn_spec = pl.BlockSpec((tm, tn), lambda i, j, kk: (i, j))
    n_extra, n_out = len(extras), len(out_dtypes)
    dims = _DIMS[mode]

    def body(a_ref, b_ref, *rest):
        extra_refs, out_refs, acc = rest[:n_extra], rest[n_extra:n_extra + n_out], rest[-1]
        kk = pl.program_id(2)

        @pl.when(kk == 0)
        def _():
            acc[...] = jnp.zeros_like(acc)

        acc[...] += lax.dot_general(a_ref[...].astype(BF16), b_ref[...].astype(BF16), dims,
                                    preferred_element_type=F32)

        @pl.when(kk == nk - 1)
        def _():
            res = acc[...]
            outs = epilogue(res, *[e[...] for e in extra_refs]) if epilogue is not None else (res,)
            for o_ref, val in zip(out_refs, outs):
                o_ref[...] = val.astype(o_ref.dtype)

    outs = pl.pallas_call(
        body, name=name, grid=(m // tm, n // tn, nk),
        in_specs=[a_spec, b_spec] + [mn_spec] * n_extra,
        out_specs=[mn_spec] * n_out,
        out_shape=[jax.ShapeDtypeStruct((m, n), dt) for dt in out_dtypes],
        scratch_shapes=[pltpu.VMEM((tm, tn), F32)],
        compiler_params=_params("parallel", "parallel", "arbitrary"),
    )(a, b, *extras)
    return tuple(outs)


def _rowwise(fn, rows, vecs, out_rows, out_accs, name, tm=256):
    t = rows[0].shape[0]
    tm = _tile(t, tm)
    n_rows, n_vecs, n_or, n_oa = len(rows), len(vecs), len(out_rows), len(out_accs)

    def body(*refs):
        ins = [r[...] for r in refs[:n_rows + n_vecs]]
        outs = fn(*ins)
        o_refs = refs[n_rows + n_vecs:]
        for o_ref, val in zip(o_refs[:n_or], outs[:n_or]):
            o_ref[...] = val.astype(o_ref.dtype)

        @pl.when(pl.program_id(0) == 0)
        def _():
            for o_ref in o_refs[n_or:]:
                o_ref[...] = jnp.zeros_like(o_ref)

        for o_ref, val in zip(o_refs[n_or:], outs[n_or:]):
            o_ref[...] += val

    outs = pl.pallas_call(
        body, name=name, grid=(t // tm,),
        in_specs=[pl.BlockSpec((tm, r.shape[1]), lambda i: (i, 0)) for r in rows]
        + [pl.BlockSpec(v.shape, lambda i: (0, 0)) for v in vecs],
        out_specs=[pl.BlockSpec((tm, o.shape[1]), lambda i: (i, 0)) for o in out_rows]
        + [pl.BlockSpec(o.shape, lambda i: (0, 0)) for o in out_accs],
        out_shape=list(out_rows) + list(out_accs),
        compiler_params=_params("arbitrary" if n_oa else "parallel"),
    )(*rows, *vecs)
    return tuple(outs)


def _stack_rows(parts, width):
    rows = lax.broadcasted_iota(jnp.int32, (8, width), 0)
    out = jnp.zeros((8, width), F32)
    for j, part in enumerate(parts):
        out = out + jnp.where(rows == j, part, 0.0)
    return out


def _sds(shape, dtype=F32):
    return jax.ShapeDtypeStruct(shape, dtype)


def _sigmoid(x):
    return 1.0 / (1.0 + jnp.exp(-x))


def _colsum(x):
    return jnp.sum(x, axis=0, keepdims=True)


def _rstd(x):
    return lax.rsqrt(jnp.mean(x * x, axis=-1, keepdims=True) + EPS)


def _rms_bwd_rows(xhat, r, g, dy):
    dxh = dy * g
    dx = r * (dxh - xhat * jnp.mean(dxh * xhat, axis=-1, keepdims=True))
    return dx, _colsum(dy * xhat)


def _rms_fwd(x, g, name):
    t, d = x.shape

    def fn(xb, gv):
        return (xb * _rstd(xb) * gv,)

    return _rowwise(fn, [x], [g], [_sds((t, d), BF16)], [], name)[0]


def _rms_bwd(x, g, dh, dres, name):
    t, d = x.shape

    def fn(xb, dhb, dresb, gv):
        r = _rstd(xb)
        dx, dg = _rms_bwd_rows(xb * r, r, gv, dhb)
        dx = dx + dresb
        return dx, dx, dg

    return _rowwise(fn, [x, dh, dres], [g], [_sds((t, d)), _sds((t, d), BF16)], [_sds((1, d))], name)


def _mixnorm_fwd(y, z, att, gs, ga, name):
    t, d = y.shape

    def fn(yb, zb, ab, gsv, gav):
        yg = yb * (zb * _sigmoid(zb))
        return (jnp.concatenate([yg * _rstd(yg) * gsv, ab * _rstd(ab) * gav], axis=1),)

    tm = _tile(t, 256)

    def body(y_ref, z_ref, a_ref, gs_ref, ga_ref, o_ref):
        o_ref[...] = fn(y_ref[...], z_ref[...], a_ref[...], gs_ref[...], ga_ref[...])[0].astype(BF16)

    row = pl.BlockSpec((tm, d), lambda i: (i, 0))
    vec = pl.BlockSpec((1, d), lambda i: (0, 0))
    out = pl.pallas_call(
        body, name=name, grid=(t // tm,),
        in_specs=[row, row, row, vec, vec],
        out_specs=pl.BlockSpec((tm, 2 * d), lambda i: (i, 0)),
        out_shape=_sds((t, 2 * d), BF16),
        compiler_params=_params("parallel"),
    )(y, z, att, gs, ga)
    return out


def _mixnorm_bwd(y, z, att, gs, ga, dycat, name):
    t, d = y.shape
    tm = _tile(t, 256)

    def body(y_ref, z_ref, a_ref, dyc_ref, gs_ref, ga_ref, dy_ref, dz_ref, da_ref, dgs_ref, dga_ref):
        yb, zb, ab = y_ref[...], z_ref[...], a_ref[...]
        dys, dya = dyc_ref[:, :d], dyc_ref[:, d:]
        sz = _sigmoid(zb)
        silu = zb * sz
        yg = yb * silu
        r = _rstd(yg)
        dyg, dgs = _rms_bwd_rows(yg * r, r, gs_ref[...], dys)
        dy_ref[...] = dyg * silu
        dz_ref[...] = (dyg * yb * (sz * (1.0 + zb * (1.0 - sz)))).astype(BF16)
        r2 = _rstd(ab)
        datt, dga = _rms_bwd_rows(ab * r2, r2, ga_ref[...], dya)
        da_ref[...] = datt

        @pl.when(pl.program_id(0) == 0)
        def _():
            dgs_ref[...] = jnp.zeros_like(dgs_ref)
            dga_ref[...] = jnp.zeros_like(dga_ref)

        dgs_ref[...] += dgs
        dga_ref[...] += dga

    row = pl.BlockSpec((tm, d), lambda i: (i, 0))
    vec = pl.BlockSpec((1, d), lambda i: (0, 0))
    return pl.pallas_call(
        body, name=name, grid=(t // tm,),
        in_specs=[row, row, row, pl.BlockSpec((tm, 2 * d), lambda i: (i, 0)), vec, vec],
        out_specs=[row, row, row, vec, vec],
        out_shape=[_sds((t, d)), _sds((t, d), BF16), _sds((t, d)), _sds((1, d)), _sds((1, d))],
        compiler_params=_params("arbitrary"),
    )(y, z, att, dycat, gs, ga)


def _final_loss(x, g, target, name):
    t, d = x.shape

    def fn(xb, tb, gv):
        r = _rstd(xb)
        xhat = xb * r
        err = xhat * gv - tb
        loss = 0.5 * jnp.sum(jnp.mean(err * err, axis=-1, keepdims=True), axis=0, keepdims=True)
        dx, dg = _rms_bwd_rows(xhat, r, gv, err * (1.0 / d))
        return dx, dx, jnp.broadcast_to(loss, (1, LANES)), dg

    dx, dxb, loss, dg = _rowwise(fn, [x, target], [g], [_sds((t, d)), _sds((t, d), BF16)],
                                 [_sds((1, LANES)), _sds((1, d))], name)
    return loss, dx, dxb, dg


def _adamw(w, g, m, v, name):
    c1 = 1.0 - ADAM_B1 ** ADAM_STEP
    c2 = 1.0 - ADAM_B2 ** ADAM_STEP

    def fn(wb, gb, mb, vb):
        mn = ADAM_B1 * mb + (1.0 - ADAM_B1) * gb
        vn = ADAM_B2 * vb + (1.0 - ADAM_B2) * (gb * gb)
        delta = -ADAM_LR * ((mn / c1) / (jnp.sqrt(vn / c2) + ADAM_EPS) + ADAM_WD * wb)
        return delta, mn, vn

    return _rowwise(fn, [w, g, m, v], [], [_sds(w.shape)] * 3, [], name)


CONV_CB = 512


def _shift_down(u, k):
    if k == 0:
        return u
    rows = lax.broadcasted_iota(jnp.int32, u.shape, 0)
    return jnp.where(rows >= k, pltpu.roll(u, k, 0), 0.0)


def _shift_up(u, k):
    if k == 0:
        return u
    n = u.shape[0]
    rows = lax.broadcasted_iota(jnp.int32, u.shape, 0)
    return jnp.where(rows < n - k, pltpu.roll(u, n - k, 0), 0.0)


def _conv_pre(u, w, b):
    pre = b
    for j in range(CONV_WIDTH):
        pre = pre + w[j:j + 1, :] * _shift_down(u, CONV_WIDTH - 1 - j)
    return pre


def _conv_fwd(proj, w, b, n_seq, name):
    t = proj.shape[0]
    seq = t // n_seq
    off = D_MODEL // CONV_CB

    def body(u_ref, w_ref, b_ref, o_ref):
        pre = _conv_pre(u_ref[...], w_ref[...], b_ref[...])
        o_ref[...] = pre * _sigmoid(pre)

    return pl.pallas_call(
        body, name=name, grid=(n_seq, D_CONV // CONV_CB),
        in_specs=[pl.BlockSpec((seq, CONV_CB), lambda s, c: (s, c + off)),
                  pl.BlockSpec((CONV_WIDTH, CONV_CB), lambda s, c: (0, c)),
                  pl.BlockSpec((1, CONV_CB), lambda s, c: (0, c))],
        out_specs=pl.BlockSpec((seq, CONV_CB), lambda s, c: (s, c)),
        out_shape=_sds((t, D_CONV)),
        compiler_params=_params("parallel", "parallel"),
    )(proj, w, b)


def _conv_bwd(proj, w, b, dxbc, n_seq, name):
    t = proj.shape[0]
    seq = t // n_seq
    off = D_MODEL // CONV_CB

    def body(u_ref, w_ref, b_ref, d_ref, du_ref, wg_ref):
        u, wv = u_ref[...], w_ref[...]
        pre = _conv_pre(u, wv, b_ref[...])
        s = _sigmoid(pre)
        dpre = d_ref[...] * (s * (1.0 + pre * (1.0 - s)))
        du = jnp.zeros_like(u)
        parts = []
        for j in range(CONV_WIDTH):
            k = CONV_WIDTH - 1 - j
            du = du + wv[j:j + 1, :] * _shift_up(dpre, k)
            parts.append(_colsum(dpre * _shift_down(u, k)))
        du_ref[...] = du.astype(BF16)
        parts.append(_colsum(dpre))

        @pl.when(pl.program_id(1) == 0)
        def _():
            wg_ref[...] = jnp.zeros_like(wg_ref)

        wg_ref[...] += _stack_rows(parts, u.shape[1])

    return pl.pallas_call(
        body, name=name, grid=(D_CONV // CONV_CB, n_seq),
        in_specs=[pl.BlockSpec((seq, CONV_CB), lambda c, s: (s, c + off)),
                  pl.BlockSpec((CONV_WIDTH, CONV_CB), lambda c, s: (0, c)),
                  pl.BlockSpec((1, CONV_CB), lambda c, s: (0, c)),
                  pl.BlockSpec((seq, CONV_CB), lambda c, s: (s, c))],
        out_specs=[pl.BlockSpec((seq, CONV_CB), lambda c, s: (s, c)),
                   pl.BlockSpec((8, CONV_CB), lambda c, s: (0, c))],
        out_shape=[_sds((t, D_CONV), BF16), _sds((8, D_CONV))],
        compiler_params=_params("parallel", "arbitrary"),
    )(proj, w, b, dxbc)


def _iota2(shape, axis):
    return lax.broadcasted_iota(jnp.int32, shape, axis)


def _dot(a, b, dims=_DIMS["nn"], precision=None):
    return lax.dot_general(a, b, dims, precision=precision, preferred_element_type=F32)


def _bdot(a, b, dims=_DIMS["nn"]):
    return lax.dot_general(a.astype(BF16), b.astype(BF16), dims, preferred_element_type=F32)


def _expand_mat():
    return (_iota2((LANES, D_MODEL), 0) == lax.shift_right_logical(_iota2((LANES, D_MODEL), 1), 6)).astype(F32)


def _reduce_mat():
    return (lax.shift_right_logical(_iota2((D_MODEL, LANES), 0), 6) == _iota2((D_MODEL, LANES), 1)).astype(F32)


def _ssd_decay(dtraw, bias, alog):
    row, col = _iota2((CHUNK, CHUNK), 0), _iota2((CHUNK, CHUNK), 1)
    pre = dtraw + bias
    dtb = jnp.maximum(pre, 0.0) + jnp.log(1.0 + jnp.exp(-jnp.abs(pre)))
    a_neg = -jnp.exp(alog)
    a = dtb * a_neg
    tril = (row >= col).astype(F32)
    triu = (row <= col).astype(F32)
    cs = _dot(tril, a, precision=HIGHEST)
    cs_t = _dot(a, triu, _DIMS["tn"], precision=HIGHEST)
    return pre, dtb, a_neg, cs, cs_t


def _pair_rowscale(vec, h0):
    top = _iota2((CHUNK, LANES), 0) < HEAD_DIM
    return jnp.where(top, vec[:, h0:h0 + 1], vec[:, h0 + 1:h0 + 2])


def _decay_mat(cs, cs_t, h):
    row, col = _iota2((CHUNK, CHUNK), 0), _iota2((CHUNK, CHUNK), 1)
    seg = cs[:, h:h + 1] - cs_t[h:h + 1, :]
    return jnp.exp(jnp.where(row >= col, seg, -jnp.inf))


def _ssd_fwd(xbc, dtraw, bias, alog, dskip_x, n_seq, name):
    t = xbc.shape[0]
    nc = t // n_seq // CHUNK

    def body(x_ref, b_ref, c_ref, dt_ref, bias_ref, alog_ref, dsk_ref, y_ref, st_ref, h_ref):
        @pl.when(pl.program_id(1) == 0)
        def _():
            h_ref[...] = jnp.zeros_like(h_ref)

        _, dtb, _, cs, cs_t = _ssd_decay(dt_ref[...], bias_ref[...], alog_ref[...])
        expand = _expand_mat()
        dt_x = _dot(dtb, expand, precision=HIGHEST)
        cs_x = _dot(cs, expand, precision=HIGHEST)
        tot = cs[CHUNK - 1:CHUNK, :]
        etot = jnp.exp(tot)
        x = x_ref[...]
        xdt = x * dt_x
        e_x = jnp.exp(cs_x)
        xdec = xdt * jnp.exp(cs_x[CHUNK - 1:CHUNK, :] - cs_x)
        keeps = [_iota2((CHUNK, LANES), 1) < HEAD_DIM, _iota2((CHUNK, LANES), 1) >= HEAD_DIM]
        for q in range(N_PAIRS):
            g = q // 2
            lanes = slice(q * LANES, (q + 1) * LANES)
            bg = b_ref[:, g * N_STATE:(g + 1) * N_STATE]
            cg = c_ref[:, g * N_STATE:(g + 1) * N_STATE]
            gmat = _bdot(cg, bg, _DIMS["nt"])
            xdt_q = xdt[:, lanes]
            ydiag = jnp.zeros((CHUNK, LANES), F32)
            for r in range(2):
                w = gmat * _decay_mat(cs, cs_t, 2 * q + r)
                ydiag = ydiag + _bdot(w, jnp.where(keeps[r], xdt_q, 0.0))
            prev = h_ref[q]
            yoff = _bdot(cg, prev, _DIMS["nt"]) * e_x[:, lanes]
            y_ref[:, lanes] = ydiag + yoff + x[:, lanes] * dsk_ref[:, lanes]
            st_ref[0, q] = prev
            h_ref[q] = prev * _pair_rowscale(etot, 2 * q) + _bdot(xdec[:, lanes], bg, _DIMS["tn"])

    vec = pl.BlockSpec((1, LANES), lambda s, c: (0, 0))
    return pl.pallas_call(
        body, name=name, grid=(n_seq, nc),
        in_specs=[pl.BlockSpec((CHUNK, D_MODEL), lambda s, c: (s * nc + c, 0)),
                  pl.BlockSpec((CHUNK, N_GROUPS * N_STATE), lambda s, c: (s * nc + c, 2)),
                  pl.BlockSpec((CHUNK, N_GROUPS * N_STATE), lambda s, c: (s * nc + c, 3)),
                  pl.BlockSpec((CHUNK, LANES), lambda s, c: (s * nc + c, 0)),
                  vec, vec, pl.BlockSpec((1, D_MODEL), lambda s, c: (0, 0))],
        out_specs=[pl.BlockSpec((CHUNK, D_MODEL), lambda s, c: (s * nc + c, 0)),
                   pl.BlockSpec((1, N_PAIRS, LANES, N_STATE), lambda s, c: (s * nc + c, 0, 0, 0))],
        out_shape=[_sds((t, D_MODEL)), _sds((t // CHUNK, N_PAIRS, LANES, N_STATE))],
        scratch_shapes=[pltpu.VMEM((N_PAIRS, LANES, N_STATE), F32)],
        compiler_params=_params("parallel", "arbitrary"),
    )(xbc, xbc, xbc, dtraw, bias, alog, dskip_x)


def _ssd_bwd(xbc, dtraw, bias, alog, dskip_x, states, dy, n_seq, name):
    t = xbc.shape[0]
    nc = t // n_seq // CHUNK

    def body(x_ref, b_ref, c_ref, dt_ref, bias_ref, alog_ref, dsk_ref, st_ref, dy_ref,
             dxbc_ref, ddt_ref, pg_ref, dh_ref):
        first = jnp.logical_and(pl.program_id(0) == 0, pl.program_id(1) == 0)

        @pl.when(first)
        def _():
            pg_ref[...] = jnp.zeros_like(pg_ref)

        @pl.when(pl.program_id(1) == 0)
        def _():
            dh_ref[...] = jnp.zeros_like(dh_ref)

        row, col = _iota2((CHUNK, CHUNK), 0), _iota2((CHUNK, CHUNK), 1)
        pre, dtb, a_neg, cs, cs_t = _ssd_decay(dt_ref[...], bias_ref[...], alog_ref[...])
        expand, reduce = _expand_mat(), _reduce_mat()
        dt_x = _dot(dtb, expand, precision=HIGHEST)
        cs_x = _dot(cs, expand, precision=HIGHEST)
        etot = jnp.exp(cs[CHUNK - 1:CHUNK, :])
        x, dy = x_ref[...], dy_ref[...]
        xdt = x * dt_x
        e_x = jnp.exp(cs_x)
        dec_x = jnp.exp(cs_x[CHUNK - 1:CHUNK, :] - cs_x)
        xdec = xdt * dec_x
        dye = dy * e_x
        keeps = [_iota2((CHUNK, LANES), 1) < HEAD_DIM, _iota2((CHUNK, LANES), 1) >= HEAD_DIM]
        row_lo = _iota2((CHUNK, LANES), 0) < HEAD_DIM
        dcs_col = jnp.zeros((CHUNK, LANES), F32)
        dcs_row = jnp.zeros((LANES, CHUNK), F32)
        dtot = jnp.zeros((1, LANES), F32)
        dxdt_parts, zdec_parts, yoff_parts = [], [], []
        for g in range(N_GROUPS):
            bg = b_ref[:, g * N_STATE:(g + 1) * N_STATE]
            cg = c_ref[:, g * N_STATE:(g + 1) * N_STATE]
            gmat = _bdot(cg, bg, _DIMS["nt"])
            dgmat = jnp.zeros((CHUNK, CHUNK), F32)
            dbg = jnp.zeros((CHUNK, N_STATE), F32)
            dcg = jnp.zeros((CHUNK, N_STATE), F32)
            for q in (2 * g, 2 * g + 1):
                lanes = slice(q * LANES, (q + 1) * LANES)
                xdt_q, dy_q = xdt[:, lanes], dy[:, lanes]
                dxdt_q = jnp.zeros((CHUNK, LANES), F32)
                for r in range(2):
                    h = 2 * q + r
                    keep = keeps[r]
                    lm = _decay_mat(cs, cs_t, h)
                    w = gmat * lm
                    dy_h = jnp.where(keep, dy_q, 0.0)
                    dm = _bdot(dy_h, xdt_q, _DIMS["nt"])
                    dxdt_q = dxdt_q + _bdot(w, dy_h, _DIMS["tn"])
                    dgmat = dgmat + dm * lm
                    tmat = dm * w
                    dcs_col = dcs_col + jnp.where(col == h, jnp.sum(tmat, axis=1, keepdims=True), 0.0)
                    dcs_row = dcs_row - jnp.where(row == h, jnp.sum(tmat, axis=0, keepdims=True), 0.0)
                prev = st_ref[0, q]
                dht = dh_ref[q]
                dxdtdec = _bdot(bg, dht, _DIMS["nt"])
                dxdt_q = dxdt_q + dxdtdec * dec_x[:, lanes]
                zdec_parts.append(dxdtdec * xdec[:, lanes])
                dbg = dbg + _bdot(xdec[:, lanes], dht)
                yoff_parts.append(dy_q * (_bdot(cg, prev, _DIMS["nt"]) * e_x[:, lanes]))
                dcg = dcg + _bdot(dye[:, lanes], prev)
                dprev = _bdot(dye[:, lanes], cg, _DIMS["tn"])
                hp = jnp.sum(dht * prev, axis=1, keepdims=True)
                d0 = jnp.sum(jnp.where(row_lo[:, :1], hp, 0.0), axis=0, keepdims=True)
                d1 = jnp.sum(jnp.where(row_lo[:, :1], 0.0, hp), axis=0, keepdims=True)
                lane1 = _iota2((1, LANES), 1)
                dtot = dtot + jnp.where(lane1 == 2 * q, d0, 0.0) + jnp.where(lane1 == 2 * q + 1, d1, 0.0)
                dh_ref[q] = dprev + dht * _pair_rowscale(etot, 2 * q)
                dxdt_parts.append(dxdt_q)
            dcg = dcg + _bdot(dgmat, bg)
            dbg = dbg + _bdot(dgmat, cg, _DIMS["tn"])
            dxbc_ref[:, D_MODEL + g * N_STATE:D_MODEL + (g + 1) * N_STATE] = dbg
            dxbc_ref[:, D_MODEL + (N_GROUPS + g) * N_STATE:D_MODEL + (N_GROUPS + g + 1) * N_STATE] = dcg
        dxdt = jnp.concatenate(dxdt_parts, axis=1)
        zdec = _dot(jnp.concatenate(zdec_parts, axis=1), reduce, precision=HIGHEST)
        yoff_d = _dot(jnp.concatenate(yoff_parts, axis=1), reduce, precision=HIGHEST)
        dtot = dtot * etot + _colsum(zdec)
        last = row[:, :LANES] == CHUNK - 1
        dcs_col = dcs_col + yoff_d - zdec + jnp.where(last, dtot, 0.0)
        triu = (row <= col).astype(F32)
        da = _dot(triu, dcs_col, precision=HIGHEST) + _dot(triu, dcs_row, _DIMS["nt"], precision=HIGHEST)
        ddt = _dot(dxdt * x, reduce, precision=HIGHEST) + da * a_neg
        ddtraw = ddt * _sigmoid(pre)
        ddt_ref[...] = ddtraw.astype(BF16)
        dxbc_ref[:, :D_MODEL] = dxdt * dt_x + dy * dsk_ref[...]
        dskip = _dot(jnp.broadcast_to(_colsum(dy * x), (8, D_MODEL)), reduce, precision=HIGHEST)[0:1, :]
        pg_ref[...] += _stack_rows([_colsum(ddtraw), _colsum(da * dtb) * a_neg, dskip], LANES)

    vec = pl.BlockSpec((1, LANES), lambda s, c: (0, 0))

    def blk(s, c):
        return s * nc + (nc - 1 - c)

    return pl.pallas_call(
        body, name=name, grid=(n_seq, nc),
        in_specs=[pl.BlockSpec((CHUNK, D_MODEL), lambda s, c: (blk(s, c), 0)),
                  pl.BlockSpec((CHUNK, N_GROUPS * N_STATE), lambda s, c: (blk(s, c), 2)),
                  pl.BlockSpec((CHUNK, N_GROUPS * N_STATE), lambda s, c: (blk(s, c), 3)),
                  pl.BlockSpec((CHUNK, LANES), lambda s, c: (blk(s, c), 0)),
                  vec, vec, pl.BlockSpec((1, D_MODEL), lambda s, c: (0, 0)),
                  pl.BlockSpec((1, N_PAIRS, LANES, N_STATE), lambda s, c: (blk(s, c), 0, 0, 0)),
                  pl.BlockSpec((CHUNK, D_MODEL), lambda s, c: (blk(s, c), 0))],
        out_specs=[pl.BlockSpec((CHUNK, D_CONV), lambda s, c: (blk(s, c), 0)),
                   pl.BlockSpec((CHUNK, LANES), lambda s, c: (blk(s, c), 0)),
                   pl.BlockSpec((8, LANES), lambda s, c: (0, 0))],
        out_shape=[_sds((t, D_CONV)), _sds((t, LANES), BF16), _sds((8, LANES))],
        scratch_shapes=[pltpu.VMEM((N_PAIRS, LANES, N_STATE), F32)],
        compiler_params=_params("arbitrary", "arbitrary"),
    )(xbc, xbc, xbc, dtraw, bias, alog, dskip_x, states, dy)


Q_COL = (D_MODEL + D_CONV) // LANES
K_COL = Q_COL + D_MODEL // LANES
V_COL = K_COL + D_MODEL // LANES
ATT_SCALE = HEAD_DIM ** -0.5


ATT_SUB = 2
ATT_TILE = ATT_SUB * CHUNK
ATT_NP = 2


ATT_ROWS = 2 * ATT_TILE


def _tri_dot(x, tri):
    return _dot(x.astype(BF16), tri)


def _att_stack(ref, lanes, keeps, scale=1.0):
    parts = []
    for a in range(ATT_SUB):
        blk = ref[a * CHUNK:(a + 1) * CHUNK, lanes] * scale
        parts += [jnp.where(keeps[r], blk, 0.0).astype(BF16) for r in range(2)]
    return jnp.concatenate(parts, axis=0)


def _att_unstack(x, a, keeps):
    return jnp.where(keeps[0], x[2 * a * CHUNK:(2 * a + 1) * CHUNK], x[(2 * a + 1) * CHUNK:(2 * a + 2) * CHUNK])


def _att_logits(s, diag):
    lb = jnp.minimum(s, 0.0) - jnp.log(1.0 + jnp.exp(-jnp.abs(s)))
    l1m = lb - s
    mask = None
    if diag:
        srow, scol = _iota2((ATT_ROWS, ATT_TILE), 0), _iota2((ATT_ROWS, ATT_TILE), 1)
        qpos = lax.shift_right_logical(srow, 8) * CHUNK + jnp.bitwise_and(srow, CHUNK - 1)
        mask = qpos > scol
        l1m = jnp.where(mask, l1m, 0.0)
    return lb, l1m, mask


def _att_fwd(proj, n_seq, name):
    t = proj.shape[0]
    seq = t // n_seq
    nq = seq // ATT_TILE

    def body(q_ref, k_ref, v_ref, o_ref, rt_ref):
        i = pl.program_id(2)
        upper = (_iota2((ATT_TILE, ATT_TILE), 0) > _iota2((ATT_TILE, ATT_TILE), 1)).astype(BF16)
        keeps = [_iota2((CHUNK, LANES), 1) < HEAD_DIM, _iota2((CHUNK, LANES), 1) >= HEAD_DIM]
        pair_lanes = [slice(pr * LANES, (pr + 1) * LANES) for pr in range(ATT_NP)]
        q_stacks = [_att_stack(q_ref, lanes, keeps, ATT_SCALE) for lanes in pair_lanes]
        per_pair = ATT_SUB + 2

        def logits(jt):
            k0 = pl.multiple_of(jt * ATT_TILE, ATT_TILE)
            return [_dot(q_stacks[pr], k_ref[pl.ds(k0, ATT_TILE), lanes].astype(BF16), _DIMS["nt"])
                    for pr, lanes in enumerate(pair_lanes)]

        def tile(jt, carry, diag):
            state = list(carry)
            ahead = logits(jnp.maximum(jt - 1, 0))
            k0 = pl.multiple_of(jt * ATT_TILE, ATT_TILE)
            for pr, lanes in enumerate(pair_lanes):
                base = pr * per_pair
                run = state[base + ATT_SUB]
                v_tile = v_ref[pl.ds(k0, ATT_TILE), lanes].astype(BF16)
                lb, l1m, mask = _att_logits(state[base + ATT_SUB + 1], diag)
                p = jnp.exp(lb + (_tri_dot(l1m, upper) + run))
                if diag:
                    p = jnp.where(mask, p, 0.0)
                pv = _dot(p.astype(BF16), v_tile)
                for a in range(ATT_SUB):
                    state[base + a] = state[base + a] + _att_unstack(pv, a, keeps)
                state[base + ATT_SUB] = run + jnp.sum(l1m, axis=1, keepdims=True)
                state[base + ATT_SUB + 1] = ahead[pr]
            return tuple(state)

        first = logits(i)
        init = ()
        for pr in range(ATT_NP):
            init += tuple(jnp.zeros((CHUNK, LANES), F32) for _ in range(ATT_SUB)) + \
                (jnp.zeros((ATT_ROWS, 1), F32), first[pr])
        carry = tile(i, init, True)
        carry = lax.fori_loop(0, i, lambda it, c: tile(i - 1 - it, c, False), carry)
        for pr, lanes in enumerate(pair_lanes):
            for a in range(ATT_SUB):
                o_ref[a * CHUNK:(a + 1) * CHUNK, lanes] = carry[pr * per_pair + a]
                rt_ref[a * CHUNK:(a + 1) * CHUNK, lanes] = _att_unstack(carry[pr * per_pair + ATT_SUB], a, keeps)

    width = ATT_NP * LANES
    qblk = pl.BlockSpec((ATT_TILE, width), lambda s, p, i: (s * nq + i, p))
    return pl.pallas_call(
        body, name=name, grid=(n_seq, N_PAIRS // ATT_NP, nq),
        in_specs=[pl.BlockSpec((ATT_TILE, width), lambda s, p, i: (s * nq + i, Q_COL // ATT_NP + p)),
                  pl.BlockSpec((seq, width), lambda s, p, i: (s, K_COL // ATT_NP + p)),
                  pl.BlockSpec((seq, width), lambda s, p, i: (s, V_COL // ATT_NP + p))],
        out_specs=[qblk, qblk],
        out_shape=[_sds((t, D_MODEL)), _sds((t, D_MODEL))],
        compiler_params=_params("parallel", "parallel", "arbitrary"),
    )(proj, proj, proj)


def _att_bwd(proj, rtot, datt, n_seq, name):
    t = proj.shape[0]
    seq = t // n_seq
    nq = seq // ATT_TILE

    def body(q_ref, k_ref, v_ref, rt_ref, do_ref, dq_ref, dk_out, dv_out, dk_ref, dv_ref):
        i = pl.program_id(2)

        @pl.when(i == 0)
        def _():
            dk_ref[...] = jnp.zeros_like(dk_ref)
            dv_ref[...] = jnp.zeros_like(dv_ref)

        row, col = _iota2((ATT_TILE, ATT_TILE), 0), _iota2((ATT_TILE, ATT_TILE), 1)
        upper = (row > col).astype(BF16)
        before = (row < col).astype(BF16)
        keeps = [_iota2((CHUNK, LANES), 1) < HEAD_DIM, _iota2((CHUNK, LANES), 1) >= HEAD_DIM]
        pair_lanes = [slice(pr * LANES, (pr + 1) * LANES) for pr in range(ATT_NP)]
        q_stacks = [_att_stack(q_ref, lanes, keeps, ATT_SCALE) for lanes in pair_lanes]
        do_stacks = [_att_stack(do_ref, lanes, keeps) for lanes in pair_lanes]
        totals = [jnp.concatenate([rt_ref[a * CHUNK:(a + 1) * CHUNK, pr * LANES + r * HEAD_DIM:pr * LANES + r * HEAD_DIM + 1]
                                   for a in range(ATT_SUB) for r in range(2)], axis=0) for pr in range(ATT_NP)]
        per_pair = ATT_SUB + 4

        def products(jt):
            k0 = pl.multiple_of(jt * ATT_TILE, ATT_TILE)
            return [(_dot(q_stacks[pr], k_ref[pl.ds(k0, ATT_TILE), lanes].astype(BF16), _DIMS["nt"]),
                     _dot(do_stacks[pr], v_ref[pl.ds(k0, ATT_TILE), lanes].astype(BF16), _DIMS["nt"]))
                    for pr, lanes in enumerate(pair_lanes)]

        def tile(jt, carry, diag):
            state = list(carry)
            ahead = None if diag else products(jt + 1)
            k0 = pl.multiple_of(jt * ATT_TILE, ATT_TILE)
            for pr, lanes in enumerate(pair_lanes):
                base = pr * per_pair
                seen, dseen = state[base + ATT_SUB], state[base + ATT_SUB + 1]
                q_stack, do_stack = q_stacks[pr], do_stacks[pr]
                k_tile = k_ref[pl.ds(k0, ATT_TILE), lanes].astype(BF16)
                lb, l1m, mask = _att_logits(state[base + ATT_SUB + 2], diag)
                here = jnp.sum(l1m, axis=1, keepdims=True)
                p = jnp.exp(lb + (_tri_dot(l1m, upper) + (totals[pr] - seen - here)))
                if diag:
                    p = jnp.where(mask, p, 0.0)
                pb = p.astype(BF16)
                dz = state[base + ATT_SUB + 3] * p
                dl1m = dseen + _tri_dot(dz, before)
                sig = jnp.exp(lb)
                ds = dz * (1.0 - sig) - dl1m * sig
                if diag:
                    ds = jnp.where(mask, ds, 0.0)
                dsb = ds.astype(BF16)
                dq_all = _dot(dsb, k_tile)
                for a in range(ATT_SUB):
                    state[base + a] = state[base + a] + _att_unstack(dq_all, a, keeps)
                dk_ref[pl.ds(k0, ATT_TILE), lanes] += _dot(dsb, q_stack, _DIMS["tn"])
                dv_ref[pl.ds(k0, ATT_TILE), lanes] += _dot(pb, do_stack, _DIMS["tn"])
                state[base + ATT_SUB] = seen + here
                state[base + ATT_SUB + 1] = dseen + jnp.sum(dz, axis=1, keepdims=True)
                if ahead is not None:
                    state[base + ATT_SUB + 2], state[base + ATT_SUB + 3] = ahead[pr]
            return tuple(state)

        first = products(0)
        init = ()
        for pr in range(ATT_NP):
            init += tuple(jnp.zeros((CHUNK, LANES), F32) for _ in range(ATT_SUB)) + \
                (jnp.zeros((ATT_ROWS, 1), F32), jnp.zeros((ATT_ROWS, 1), F32)) + first[pr]
        carry = lax.fori_loop(0, i, lambda jt, c: tile(jt, c, False), init)
        carry = tile(i, carry, True)
        for pr, lanes in enumerate(pair_lanes):
            for a in range(ATT_SUB):
                dq_ref[a * CHUNK:(a + 1) * CHUNK, lanes] = (carry[pr * per_pair + a] * ATT_SCALE).astype(BF16)

        @pl.when(i == nq - 1)
        def _():
            dk_out[...] = dk_ref[...].astype(BF16)
            dv_out[...] = dv_ref[...].astype(BF16)

    width = ATT_NP * LANES
    qblk = pl.BlockSpec((ATT_TILE, width), lambda s, p, i: (s * nq + i, p))
    kv_out = pl.BlockSpec((seq, width), lambda s, p, i: (s, p))
    return pl.pallas_call(
        body, name=name, grid=(n_seq, N_PAIRS // ATT_NP, nq),
        in_specs=[pl.BlockSpec((ATT_TILE, width), lambda s, p, i: (s * nq + i, Q_COL // ATT_NP + p)),
                  pl.BlockSpec((seq, width), lambda s, p, i: (s, K_COL // ATT_NP + p)),
                  pl.BlockSpec((seq, width), lambda s, p, i: (s, V_COL // ATT_NP + p)),
                  qblk, qblk],
        out_specs=[qblk, kv_out, kv_out],
        out_shape=[_sds((t, D_MODEL), BF16)] * 3,
        scratch_shapes=[pltpu.VMEM((seq, width), F32), pltpu.VMEM((seq, width), F32)],
        compiler_params=_params("parallel", "parallel", "arbitrary"),
    )(proj, proj, proj, rtot, datt)


def _pad_lanes(v):
    return jnp.pad(v.reshape(1, -1), ((0, 0), (0, LANES - v.shape[0])))


def _layer_fwd(x, p, n_seq, tag):
    h = _rms_fwd(x, p["norm_mix_g"], f"rms_mix_fwd{tag}")
    proj, = _matmul(h, p["w_main"], "nn", [F32], f"in_proj{tag}", tn=1536)
    dtraw, = _matmul(h, p["w_dt"], "nn", [F32], f"dt_proj{tag}")
    xbc = _conv_fwd(proj, p["conv_w"], p["conv_b"], n_seq, f"conv_fwd{tag}")
    y, states = _ssd_fwd(xbc, dtraw, p["dt_bias"], p["a_log"], p["d_skip_x"], n_seq, f"ssd_fwd{tag}")
    att, rtot = _att_fwd(proj, n_seq, f"att_fwd{tag}")
    ycat = _mixnorm_fwd(y, proj, att, p["ssd_norm_g"], p["att_norm_g"], f"mixnorm_fwd{tag}")
    x1, = _matmul(ycat, p["w_out"], "nn", [F32], f"out_proj{tag}", extras=[x], epilogue=lambda acc, xb: (xb + acc,))
    h2 = _rms_fwd(x1, p["norm_mlp_g"], f"rms_mlp_fwd{tag}")
    u, act = _matmul(h2, p["w_up"], "nn", [F32, BF16], f"up_proj{tag}",
                     epilogue=lambda acc: (acc, jnp.square(jnp.maximum(acc, 0.0))))
    x2, = _matmul(act, p["w_down"], "nn", [F32], f"down_proj{tag}", extras=[x1], epilogue=lambda acc, xb: (xb + acc,))
    saved = dict(x=x, h=h, proj=proj, dtraw=dtraw, xbc=xbc, y=y, states=states, att=att, rtot=rtot, ycat=ycat,
                 x1=x1, h2=h2, u=u, act=act)
    return x2, saved


def _layer_bwd(dx2, dx2b, p, s, n_seq, tag):
    g = {}
    g["w_down"], = _matmul(s["act"], dx2b, "tn", [F32], f"dw_down{tag}")
    du, = _matmul(dx2b, p["w_down"], "nt", [BF16], f"d_act{tag}", extras=[s["u"]],
                  epilogue=lambda acc, ub: (acc * (2.0 * jnp.maximum(ub, 0.0)),))
    g["w_up"], = _matmul(s["h2"], du, "tn", [F32], f"dw_up{tag}")
    dh2, = _matmul(du, p["w_up"], "nt", [F32], f"d_h2{tag}")
    dx1, dx1b, g["norm_mlp_g"] = _rms_bwd(s["x1"], p["norm_mlp_g"], dh2, dx2, f"rms_mlp_bwd{tag}")
    g["w_out"], = _matmul(s["ycat"], dx1b, "tn", [F32], f"dw_out{tag}")
    dycat, = _matmul(dx1b, p["w_out"], "nt", [F32], f"d_ycat{tag}")
    dy, dz, datt, g["ssd_norm_g"], g["att_norm_g"] = _mixnorm_bwd(
        s["y"], s["proj"], s["att"], p["ssd_norm_g"], p["att_norm_g"], dycat, f"mixnorm_bwd{tag}")
    dq, dk, dv = _att_bwd(s["proj"], s["rtot"], datt, n_seq, f"att_bwd{tag}")
    dxbc, ddtraw, pg = _ssd_bwd(s["xbc"], s["dtraw"], p["dt_bias"], p["a_log"], p["d_skip_x"], s["states"], dy,
                                n_seq, f"ssd_bwd{tag}")
    g["dt_bias"], g["a_log"], g["d_skip"] = pg[0, :N_HEADS], pg[1, :N_HEADS], pg[2, :N_HEADS]
    du_conv, wg = _conv_bwd(s["proj"], p["conv_w"], p["conv_b"], dxbc, n_seq, f"conv_bwd{tag}")
    g["conv_w"], g["conv_b"] = wg[:CONV_WIDTH], wg[CONV_WIDTH]
    dproj = jnp.concatenate([dz, du_conv, dq, dk, dv], axis=1)
    g["w_main"], = _matmul(s["h"], dproj, "tn", [F32], f"dw_in{tag}", tn=1536)
    g["w_dt"], = _matmul(s["h"], ddtraw, "tn", [F32], f"dw_dt{tag}")
    dh_dt, = _matmul(ddtraw, p["w_dt"], "nt", [F32], f"d_h_dt{tag}")
    dh, = _matmul(dproj, p["w_main"], "nt", [F32], f"d_h{tag}", extras=[dh_dt], epilogue=lambda acc, e: (acc + e,))
    dx, dxb, g["norm_mix_g"] = _rms_bwd(s["x"], p["norm_mix_g"], dh, dx1, f"rms_mix_bwd{tag}")
    return dx, dxb, g


def _split_w_in(w_full):
    c0 = D_MODEL + D_CONV
    main = jnp.concatenate([w_full[:, :c0], w_full[:, c0 + N_HEADS:]], axis=1)
    dt = jnp.pad(w_full[:, c0:c0 + N_HEADS], ((0, 0), (0, LANES - N_HEADS)))
    return main, dt


def _merge_w_in(main, dt):
    c0 = D_MODEL + D_CONV
    return jnp.concatenate([main[:, :c0], dt[:, :N_HEADS], main[:, c0:]], axis=1)


def _prep_layer(l, w_in_full, w_out, w_up, w_down, conv_w, small):
    w_main, w_dt = _split_w_in(w_in_full)
    return dict(
        w_main=w_main, w_dt=w_dt, w_out=w_out, w_up=w_up, w_down=w_down, conv_w=conv_w,
        conv_b=small["conv_b"][l].reshape(1, -1),
        dt_bias=_pad_lanes(small["dt_bias"][l]), a_log=_pad_lanes(small["a_log"][l]),
        d_skip_x=jnp.repeat(small["d_skip"][l], HEAD_DIM).reshape(1, -1),
        norm_mix_g=small["norm_mix_g"][l].reshape(1, -1), ssd_norm_g=small["ssd_norm_g"][l].reshape(1, -1),
        att_norm_g=small["att_norm_g"][l].reshape(1, -1), norm_mlp_g=small["norm_mlp_g"][l].reshape(1, -1),
    )


ANY = pl.BlockSpec(memory_space=pl.ANY)


def _place():
    x, y, c = lax.axis_index("x"), lax.axis_index("y"), lax.axis_index("c")
    return x, y, c, (x, y, 1 - c), [(1 - x, y), (x, 1 - y), (1 - x, 1 - y)]


def _all_gather_chips(shard, chip, name):
    r, ccols = shard.shape
    half = r // 2

    def body(s_ref, o_ref, send_sems, recv_sems):
        x, y, c, sibling, chips = _place()

        def slab(px, py, hc):
            return o_ref.at[2 * px + py, pl.ds(hc * half, half), :]

        def copy(k, src, dst, to):
            return pltpu.make_async_remote_copy(src_ref=src, dst_ref=dst, send_sem=send_sems.at[k],
                                                recv_sem=recv_sems.at[k], device_id=to, device_id_type=MESH)

        sends = [copy(k, s_ref.at[pl.ds(c * half, half), :], slab(x, y, c), (px, py, c))
                 for k, (px, py) in enumerate(chips)]
        for cp in sends:
            cp.start()
        passed = []
        for k, (px, py) in enumerate(chips):
            copy(k, slab(px, py, c), slab(px, py, c), (px, py, c)).wait_recv()
            cp = copy(3 + k, slab(px, py, c), slab(px, py, c), sibling)
            cp.start()
            passed.append(cp)
        for k, (px, py) in enumerate(chips):
            copy(3 + k, slab(px, py, 1 - c), slab(px, py, 1 - c), sibling).wait_recv()
        for cp in sends + passed:
            cp.wait_send()

    others = pl.pallas_call(
        body, name=name, in_specs=[ANY], out_specs=ANY,
        out_shape=_sds((N_CHIPS, r, ccols), shard.dtype),
        scratch_shapes=[pltpu.SemaphoreType.DMA((6,)), pltpu.SemaphoreType.DMA((6,))],
    )(shard)
    return lax.dynamic_update_slice(others, shard[None], (chip, 0, 0))


def _sibling_swap(g, name):
    n, r, ccols = g.shape
    half = r // 2

    def body(g_ref, o_ref, send_sem, recv_sem):
        _, _, c, sibling, _ = _place()
        cp = pltpu.make_async_remote_copy(src_ref=g_ref.at[:, pl.ds((1 - c) * half, half), :], dst_ref=o_ref,
                                          send_sem=send_sem, recv_sem=recv_sem, device_id=sibling, device_id_type=MESH)
        cp.start()
        cp.wait()

    return pl.pallas_call(
        body, name=name, in_specs=[ANY], out_specs=ANY, out_shape=_sds((n, half, ccols), g.dtype),
        scratch_shapes=[pltpu.SemaphoreType.DMA, pltpu.SemaphoreType.DMA],
    )(g)


def _chip_exchange(p, name):
    _, h, ccols = p.shape

    def body(p_ref, o_ref, send_sems, recv_sems):
        _, _, c, _, chips = _place()
        copies = [pltpu.make_async_remote_copy(src_ref=p_ref.at[2 * px + py], dst_ref=o_ref.at[k],
                                               send_sem=send_sems.at[k], recv_sem=recv_sems.at[k],
                                               device_id=(px, py, c), device_id_type=MESH)
                  for k, (px, py) in enumerate(chips)]
        for cp in copies:
            cp.start()
        for cp in copies:
            cp.wait()

    return pl.pallas_call(
        body, name=name, in_specs=[ANY], out_specs=ANY, out_shape=_sds((3, h, ccols), p.dtype),
        scratch_shapes=[pltpu.SemaphoreType.DMA((3,)), pltpu.SemaphoreType.DMA((3,))],
    )(p)


def _sibling_share(full, name):
    r, ccols = full.shape
    h = r // 2

    def body(f_ref, o_ref, send_sem, recv_sem):
        _, _, c, sibling, _ = _place()
        mine = pl.ds(c * h, h)
        cp = pltpu.make_async_remote_copy(src_ref=f_ref.at[mine, :], dst_ref=o_ref.at[mine, :], send_sem=send_sem,
                                          recv_sem=recv_sem, device_id=sibling, device_id_type=MESH)
        cp.start()
        theirs = o_ref.at[pl.ds((1 - c) * h, h), :]
        pltpu.make_async_remote_copy(src_ref=theirs, dst_ref=theirs, send_sem=send_sem, recv_sem=recv_sem,
                                     device_id=sibling, device_id_type=MESH).wait_recv()
        cp.wait_send()

    return pl.pallas_call(
        body, name=name, in_specs=[ANY], out_specs=ANY, out_shape=_sds((r, ccols), full.dtype),
        input_output_aliases={0: 0},
        scratch_shapes=[pltpu.SemaphoreType.DMA, pltpu.SemaphoreType.DMA],
    )(full)


def _add_halves(g, a, c, name):
    n, r, ccols = g.shape
    half = r // 2
    tr = _tile(half, 256)
    nb = half // tr

    def body(c_ref, g_ref, a_ref, o_ref):
        o_ref[...] = (g_ref[...] + a_ref[...]).astype(BF16)

    blk = (1, tr, ccols)
    return pl.pallas_call(
        body, name=name,
        grid_spec=pltpu.PrefetchScalarGridSpec(
            num_scalar_prefetch=1, grid=(n, nb),
            in_specs=[pl.BlockSpec(blk, lambda j, i, c_ref: (j, c_ref[0] * nb + i, 0)),
                      pl.BlockSpec(blk, lambda j, i, c_ref: (j, i, 0))],
            out_specs=pl.BlockSpec(blk, lambda j, i, c_ref: (j, i, 0))),
        out_shape=_sds((n, half, ccols), BF16),
        compiler_params=_params("parallel", "parallel"),
    )(c.reshape(1).astype(jnp.int32), g, a)


def _sum_chips(p, got, chip, c, name):
    _, h, ccols = p.shape
    tr = _tile(h, 256)

    def body(j_ref, h_ref, p_ref, a_ref, b_ref, c_ref, o_ref):
        f32 = [ref[...].astype(F32) for ref in (p_ref, a_ref, b_ref, c_ref)]
        o_ref[...] = ((f32[0] + f32[1]) + f32[2]) + f32[3]

    blk = (1, tr, ccols)

    def slot(k):
        return pl.BlockSpec(blk, lambda i, j_ref, h_ref: (k, i, 0))

    return pl.pallas_call(
        body, name=name,
        grid_spec=pltpu.PrefetchScalarGridSpec(
            num_scalar_prefetch=2, grid=(h // tr,),
            in_specs=[pl.BlockSpec(blk, lambda i, j_ref, h_ref: (j_ref[0], i, 0)), slot(0), slot(1), slot(2)],
            out_specs=pl.BlockSpec(blk, lambda i, j_ref, h_ref: (h_ref[0], i, 0))),
        out_shape=_sds((2, h, ccols)),
        compiler_params=_params("parallel"),
    )(chip.reshape(1).astype(jnp.int32), c.reshape(1).astype(jnp.int32), p, got, got, got)


def _reduce_scatter(g, chip, c, name):
    a = _sibling_swap(g, f"{name}_swap")
    p = _add_halves(g, a, c, f"{name}_add")
    got = _chip_exchange(p, f"{name}_xchg")
    halves = _sum_chips(p, got, chip, c, f"{name}_sum")
    return _sibling_share(halves.reshape(g.shape[1], g.shape[2]), f"{name}_share")


def _all_reduce_small(v, name):
    r = v.shape[0]

    def body(v_ref, o_ref, buf, send_sems, recv_sems):
        x, y, c, _, _ = _place()
        buf[0] = v_ref[...]
        copies = []
        for rel in range(1, 8):
            fx, fy, fc = (rel >> 2) & 1, (rel >> 1) & 1, rel & 1
            peer = (1 - x if fx else x, 1 - y if fy else y, 1 - c if fc else c)
            cp = pltpu.make_async_remote_copy(src_ref=v_ref, dst_ref=buf.at[rel], send_sem=send_sems.at[rel - 1],
                                              recv_sem=recv_sems.at[rel - 1], device_id=peer, device_id_type=MESH)
            cp.start()
            copies.append(cp)
        for cp in copies:
            cp.wait()
        me = 4 * x + 2 * y + c
        acc = buf[jnp.bitwise_xor(me, 0)]
        for src in range(1, 8):
            acc = acc + buf[jnp.bitwise_xor(me, src)]
        o_ref[...] = acc

    vm = pl.BlockSpec(memory_space=pltpu.VMEM)
    return pl.pallas_call(
        body, name=name, in_specs=[vm], out_specs=vm, out_shape=_sds((r, LANES)),
        scratch_shapes=[pltpu.VMEM((8, r, LANES), F32), pltpu.SemaphoreType.DMA((7,)), pltpu.SemaphoreType.DMA((7,))],
    )(v)


WEIGHTS = ["norm_mix_g", "w_in", "conv_w", "conv_b", "dt_bias", "a_log", "d_skip", "ssd_norm_g", "att_norm_g",
           "w_out", "norm_mlp_g", "w_up", "w_down", "final_norm_g"]
BIG = ["w_in", "w_out", "w_up", "w_down"]
SMALL = [n for n in WEIGHTS if n not in BIG]


def _pack(arrays):
    flat = []
    for a in arrays:
        a = a.reshape(-1)
        flat.append(jnp.pad(a, (0, (-a.shape[0]) % LANES)))
    flat = jnp.concatenate(flat)
    flat = jnp.pad(flat, (0, (-flat.shape[0]) % (8 * LANES)))
    return flat.reshape(-1, LANES)


def _unpack(packed, shapes):
    flat, out, pos = packed.reshape(-1), [], 0
    for shp in shapes:
        n = math.prod(shp)
        out.append(flat[pos:pos + n].reshape(shp))
        pos += n + (-n) % LANES
    return out


def _to_shards(name, g):
    if name in ("w_in", "w_up"):
        l, r, ccols = g.shape
        return g.reshape(l, r, N_CHIPS, ccols // N_CHIPS).transpose(2, 0, 1, 3).reshape(N_CHIPS, l * r, ccols // N_CHIPS)
    l, r, ccols = g.shape
    return g.reshape(l, N_CHIPS, r // N_CHIPS, ccols).transpose(1, 0, 2, 3).reshape(N_CHIPS, l * r // N_CHIPS, ccols)


def _from_gathered(name, g, l):
    rows = g.shape[1] // DEPTH
    part = g[:, l * rows:(l + 1) * rows, :]
    if name in ("w_in", "w_up", "conv_w"):
        return part.transpose(1, 0, 2).reshape(rows, N_CHIPS * g.shape[2])
    return part.reshape(N_CHIPS * rows, g.shape[2])


def kernel(x, norm_mix_g, w_in, conv_w, conv_b, dt_bias, a_log, d_skip, ssd_norm_g, att_norm_g, w_out, norm_mlp_g, w_up, w_down, final_norm_g, loss_target, m_norm_mix_g, m_w_in, m_conv_w, m_conv_b, m_dt_bias, m_a_log, m_d_skip, m_ssd_norm_g, m_att_norm_g, m_w_out, m_norm_mlp_g, m_w_up, m_w_down, m_final_norm_g, v_norm_mix_g, v_w_in, v_conv_w, v_conv_b, v_dt_bias, v_a_log, v_d_skip, v_ssd_norm_g, v_att_norm_g, v_w_out, v_norm_mlp_g, v_w_up, v_w_down, v_final_norm_g):
    w = dict(norm_mix_g=norm_mix_g, w_in=w_in, conv_w=conv_w, conv_b=conv_b, dt_bias=dt_bias, a_log=a_log,
             d_skip=d_skip, ssd_norm_g=ssd_norm_g, att_norm_g=att_norm_g, w_out=w_out, norm_mlp_g=norm_mlp_g,
             w_up=w_up, w_down=w_down, final_norm_g=final_norm_g)
    m = dict(norm_mix_g=m_norm_mix_g, w_in=m_w_in, conv_w=m_conv_w, conv_b=m_conv_b, dt_bias=m_dt_bias,
             a_log=m_a_log, d_skip=m_d_skip, ssd_norm_g=m_ssd_norm_g, att_norm_g=m_att_norm_g, w_out=m_w_out,
             norm_mlp_g=m_norm_mlp_g, w_up=m_w_up, w_down=m_w_down, final_norm_g=m_final_norm_g)
    v = dict(norm_mix_g=v_norm_mix_g, w_in=v_w_in, conv_w=v_conv_w, conv_b=v_conv_b, dt_bias=v_dt_bias,
             a_log=v_a_log, d_skip=v_d_skip, ssd_norm_g=v_ssd_norm_g, att_norm_g=v_att_norm_g, w_out=v_w_out,
             norm_mlp_g=v_norm_mlp_g, w_up=v_w_up, w_down=v_w_down, final_norm_g=v_final_norm_g)
    n_seq, seq, d = x.shape
    t = n_seq * seq
    chip = 2 * lax.axis_index("x") + lax.axis_index("y")
    core = lax.axis_index("c")

    gathered = {n: _all_gather_chips(w[n].astype(BF16).reshape(-1, w[n].shape[-1]), chip, f"gather_{n}") for n in BIG}
    gathered["conv_w"] = _all_gather_chips(conv_w.reshape(-1, conv_w.shape[-1]), chip, "gather_conv_w")
    layers = [_prep_layer(l, *[_from_gathered(n, gathered[n], l) for n in BIG + ["conv_w"]], w) for l in range(DEPTH)]

    xs = x.reshape(t, d)
    saved = []
    for l in range(DEPTH):
        xs, s = _layer_fwd(xs, layers[l], n_seq, f"_l{l}")
        saved.append(s)
    loss_vec, dx, dxb, g_final = _final_loss(xs, final_norm_g.reshape(1, d), loss_target.reshape(t, d), "final_loss")
    loss = lax.psum(loss_vec[0, 0], ("x", "y", "c"))

    grads = [None] * DEPTH
    for l in reversed(range(DEPTH)):
        dx, dxb, grads[l] = _layer_bwd(dx, dxb, layers[l], saved[l], n_seq, f"_l{l}")
        grads[l]["w_in"] = _merge_w_in(grads[l].pop("w_main"), grads[l].pop("w_dt"))
    grad_x = dx.reshape(n_seq, seq, d)

    full = {n: jnp.stack([grads[l][n] for l in range(DEPTH)]) for n in WEIGHTS if n != "final_norm_g"}
    full["final_norm_g"] = g_final.reshape(d)
    g_out = {}
    for n in BIG:
        red = _reduce_scatter(_to_shards(n, full[n]), chip, core, f"rs_{n}")
        g_out[n] = red.reshape(w[n].shape)
    small_sum = _all_reduce_small(_pack([full[n] for n in SMALL]), "allreduce_small")
    small_shapes = [(DEPTH, CONV_WIDTH, D_CONV) if n == "conv_w" else w[n].shape for n in SMALL]
    for n, val in zip(SMALL, _unpack(small_sum, small_shapes)):
        g_out[n] = val
    g_out["conv_w"] = lax.dynamic_slice_in_dim(g_out["conv_w"], chip * conv_w.shape[-1], conv_w.shape[-1], axis=2)

    delta, new_m, new_v = {}, {}, {}
    for n in BIG:
        two_d = (-1, w[n].shape[-1])
        dl, mn, vn = _adamw(w[n].reshape(two_d), g_out[n].reshape(two_d), m[n].reshape(two_d), v[n].reshape(two_d),
                            f"adamw_{n}")
        delta[n], new_m[n], new_v[n] = dl.reshape(w[n].shape), mn.reshape(w[n].shape), vn.reshape(w[n].shape)
    packs = [_pack([src[n] for n in SMALL]) for src in (w, g_out, m, v)]
    shapes = [w[n].shape for n in SMALL]
    for dst, packed in zip((delta, new_m, new_v), _adamw(*packs, "adamw_small")):
        for n, val in zip(SMALL, _unpack(packed, shapes)):
            dst[n] = val

    return (loss, grad_x, *[g_out[n] for n in WEIGHTS], *[delta[n] for n in WEIGHTS],
            *[new_m[n] for n in WEIGHTS], *[new_v[n] for n in WEIGHTS])
```

```python
import functools
import math

import jax
import jax.numpy as jnp
from jax import lax
from jax.experimental import pallas as pl
from jax.experimental.pallas import tpu as pltpu

F32 = jnp.float32
BF16 = jnp.bfloat16
HIGHEST = lax.Precision.HIGHEST

D_MODEL = 1024
DEPTH = 4
HEAD_DIM = 64
HEAD_SHIFT = HEAD_DIM.bit_length() - 1
N_HEADS = 16
N_GROUPS = 4
N_STATE = 128
N_PAIRS = N_HEADS // 2
CONV_WIDTH = 4
CHUNK = 128
D_CONV = D_MODEL + 2 * N_GROUPS * N_STATE
D_MAIN = D_MODEL + D_CONV + 3 * D_MODEL
D_IN_PROJ = D_MAIN + N_HEADS
D_FF = 4 * D_MODEL
EPS = 1e-5
LANES = 128
VMEM_LIMIT = 48 * 1024 * 1024

ADAM_LR = 0.001
ADAM_B1 = 0.9
ADAM_B2 = 0.999
ADAM_EPS = 1e-08
ADAM_WD = 0.01
ADAM_STEP = 10

N_CHIPS = 4
MESH = pl.DeviceIdType.MESH


def _tile(n, cap):
    if n <= cap:
        return n
    t = cap
    while t >= 8:
        if n % t == 0:
            return t
        t //= 2
    raise ValueError(f"no tile for {n} under {cap}")


def _params(*sem):
    return pltpu.CompilerParams(dimension_semantics=sem, vmem_limit_bytes=VMEM_LIMIT)


_DIMS = {"nn": (((1,), (0,)), ((), ())), "nt": (((1,), (1,)), ((), ())), "tn": (((0,), (0,)), ((), ()))}


def _matmul(a, b, mode, out_dtypes, name, extras=(), vecs=(), epilogue=None, slab=None, tm=1024, tn=1024, tk=1024):
    if mode == "nn":
        (m, k), (_, n) = a.shape, b.shape
    elif mode == "nt":
        (m, k), (n, _) = a.shape, b.shape
    else:
        (k, m), (_, n) = a.shape, b.shape
    tm, tn, tk = _tile(m, tm), _tile(n, tn), _tile(k, tk)
    nk = k // tk
    if mode == "tn":
        a_spec = pl.BlockSpec((tk, tm), lambda i, j, kk: (kk, i))
    else:
        a_spec = pl.BlockSpec((tm, tk), lambda i, j, kk: (i, kk))
    if mode == "nt":
        b_spec = pl.BlockSpec((tn, tk), lambda i, j, kk: (j, kk))
    else:
        b_spec = pl.BlockSpec((tk, tn), lambda i, j, kk: (kk, j))
    mn_spec = pl.BlockSpec((tm, tn), lambda i, j, kk: (i, j))
    vec_spec = pl.BlockSpec((1, tn), lambda i, j, kk: (0, j))
    n_extra, n_vec, n_out = len(extras), len(vecs), len(out_dtypes)
    n_in = n_extra + n_vec + (1 if slab is not None and slab[0] is not None else 0)
    dims = _DIMS[mode]

    def body(a_ref, b_ref, *rest):
        in_refs, out_refs, acc = rest[:n_extra + n_vec], rest[n_in:n_in + n_out], rest[-1]
        kk = pl.program_id(2)

        @pl.when(kk == 0)
        def _():
            acc[...] = jnp.zeros_like(acc)

        acc[...] += lax.dot_general(a_ref[...].astype(BF16), b_ref[...].astype(BF16), dims,
                                    preferred_element_type=F32)

        @pl.when(kk == nk - 1)
        def _():
            res = acc[...]
            outs = epilogue(res, *[e[...] for e in in_refs]) if epilogue is not None else (res,)
            for o_ref, val in zip(out_refs, outs):
                o_ref[...] = val.astype(o_ref.dtype).reshape(o_ref.shape)

    in_specs = [a_spec, b_spec] + [mn_spec] * n_extra + [vec_spec] * n_vec
    operands = [a, b, *extras, *vecs]
    out_specs = [mn_spec] * n_out
    out_shape = [jax.ShapeDtypeStruct((m, n), dt) for dt in out_dtypes]
    aliases = {}
    if slab is not None:
        buffer, shape, index = slab
        out_specs = [pl.BlockSpec((1, tm, tn), lambda i, j, kk: index(i, j))]
        out_shape = [jax.ShapeDtypeStruct(shape, out_dtypes[0])]
        if buffer is not None:
            in_specs.append(ANY)
            operands.append(buffer)
            aliases = {len(operands) - 1: 0}
    outs = pl.pallas_call(
        body, name=name, grid=(m // tm, n // tn, nk),
        in_specs=in_specs, out_specs=out_specs, out_shape=out_shape, input_output_aliases=aliases,
        scratch_shapes=[pltpu.VMEM((tm, tn), F32)],
        compiler_params=_params("parallel", "parallel", "arbitrary"),
    )(*operands)
    return tuple(outs)


def _rowwise(fn, rows, vecs, out_rows, out_accs, name, tm=256):
    t = rows[0].shape[0]
    tm = _tile(t, tm)
    n_rows, n_vecs, n_or, n_oa = len(rows), len(vecs), len(out_rows), len(out_accs)

    def body(*refs):
        ins = [r[...] for r in refs[:n_rows + n_vecs]]
        outs = fn(*ins)
        o_refs = refs[n_rows + n_vecs:]
        for o_ref, val in zip(o_refs[:n_or], outs[:n_or]):
            o_ref[...] = val.astype(o_ref.dtype)

        @pl.when(pl.program_id(0) == 0)
        def _():
            for o_ref in o_refs[n_or:]:
                o_ref[...] = jnp.zeros_like(o_ref)

        for o_ref, val in zip(o_refs[n_or:], outs[n_or:]):
            o_ref[...] += val

    outs = pl.pallas_call(
        body, name=name, grid=(t // tm,),
        in_specs=[pl.BlockSpec((tm, r.shape[1]), lambda i: (i, 0)) for r in rows]
        + [pl.BlockSpec(v.shape, lambda i: (0, 0)) for v in vecs],
        out_specs=[pl.BlockSpec((tm, o.shape[1]), lambda i: (i, 0)) for o in out_rows]
        + [pl.BlockSpec(o.shape, lambda i: (0, 0)) for o in out_accs],
        out_shape=list(out_rows) + list(out_accs),
        compiler_params=_params("arbitrary" if n_oa else "parallel"),
    )(*rows, *vecs)
    return tuple(outs)


def _stack_rows(parts, width):
    rows = lax.broadcasted_iota(jnp.int32, (8, width), 0)
    out = jnp.zeros((8, width), F32)
    for j, part in enumerate(parts):
        out = out + jnp.where(rows == j, part, 0.0)
    return out


def _sds(shape, dtype=F32):
    return jax.ShapeDtypeStruct(shape, dtype)


def _sigmoid(x):
    return 1.0 / (1.0 + jnp.exp(-x))


def _colsum(x):
    return jnp.sum(x, axis=0, keepdims=True)


def _rstd(x):
    return lax.rsqrt(jnp.mean(x * x, axis=-1, keepdims=True) + EPS)


def _rms_bwd_rows(xhat, r, g, dy):
    dxh = dy * g
    dx = r * (dxh - xhat * jnp.mean(dxh * xhat, axis=-1, keepdims=True))
    return dx, _colsum(dy * xhat)


def _rms_fwd(x, g, name):
    t, d = x.shape

    def fn(xb, gv):
        return (xb * _rstd(xb) * gv,)

    return _rowwise(fn, [x], [g], [_sds((t, d), BF16)], [], name)[0]


def _rms_bwd(x, g, dh, dres, name):
    t, d = x.shape

    def fn(xb, dhb, dresb, gv):
        r = _rstd(xb)
        dx, dg = _rms_bwd_rows(xb * r, r, gv, dhb)
        dx = dx + dresb
        return dx, dx, dg

    return _rowwise(fn, [x, dh, dres], [g], [_sds((t, d)), _sds((t, d), BF16)], [_sds((1, d))], name)


def _mixnorm_fwd(y, z, att, gs, ga, name):
    t, d = y.shape

    def fn(yb, zb, ab, gsv, gav):
        yg = yb * (zb * _sigmoid(zb))
        return (jnp.concatenate([yg * _rstd(yg) * gsv, ab * _rstd(ab) * gav], axis=1),)

    tm = _tile(t, 256)

    def body(y_ref, z_ref, a_ref, gs_ref, ga_ref, o_ref):
        o_ref[...] = fn(y_ref[...], z_ref[...], a_ref[...], gs_ref[...], ga_ref[...])[0].astype(BF16)

    row = pl.BlockSpec((tm, d), lambda i: (i, 0))
    vec = pl.BlockSpec((1, d), lambda i: (0, 0))
    out = pl.pallas_call(
        body, name=name, grid=(t // tm,),
        in_specs=[row, row, row, vec, vec],
        out_specs=pl.BlockSpec((tm, 2 * d), lambda i: (i, 0)),
        out_shape=_sds((t, 2 * d), BF16),
        compiler_params=_params("parallel"),
    )(y, z, att, gs, ga)
    return out


def _mixnorm_bwd(y, z, att, gs, ga, dycat, name):
    t, d = y.shape
    tm = _tile(t, 256)

    def body(y_ref, z_ref, a_ref, dyc_ref, gs_ref, ga_ref, dy_ref, dz_ref, da_ref, dgs_ref, dga_ref):
        yb, zb, ab = y_ref[...], z_ref[...], a_ref[...]
        dys, dya = dyc_ref[:, :d], dyc_ref[:, d:]
        sz = _sigmoid(zb)
        silu = zb * sz
        yg = yb * silu
        r = _rstd(yg)
        dyg, dgs = _rms_bwd_rows(yg * r, r, gs_ref[...], dys)
        dy_ref[...] = dyg * silu
        dz_ref[...] = (dyg * yb * (sz * (1.0 + zb * (1.0 - sz)))).astype(BF16)
        r2 = _rstd(ab)
        datt, dga = _rms_bwd_rows(ab * r2, r2, ga_ref[...], dya)
        da_ref[...] = datt

        @pl.when(pl.program_id(0) == 0)
        def _():
            dgs_ref[...] = jnp.zeros_like(dgs_ref)
            dga_ref[...] = jnp.zeros_like(dga_ref)

        dgs_ref[...] += dgs
        dga_ref[...] += dga

    row = pl.BlockSpec((tm, d), lambda i: (i, 0))
    vec = pl.BlockSpec((1, d), lambda i: (0, 0))
    return pl.pallas_call(
        body, name=name, grid=(t // tm,),
        in_specs=[row, row, row, pl.BlockSpec((tm, 2 * d), lambda i: (i, 0)), vec, vec],
        out_specs=[row, row, row, vec, vec],
        out_shape=[_sds((t, d)), _sds((t, d), BF16), _sds((t, d)), _sds((1, d)), _sds((1, d))],
        compiler_params=_params("arbitrary"),
    )(y, z, att, dycat, gs, ga)


def _final_loss(x, g, target, name):
    t, d = x.shape

    def fn(xb, tb, gv):
        r = _rstd(xb)
        xhat = xb * r
        err = xhat * gv - tb
        loss = 0.5 * jnp.sum(jnp.mean(err * err, axis=-1, keepdims=True), axis=0, keepdims=True)
        dx, dg = _rms_bwd_rows(xhat, r, gv, err * (1.0 / d))
        return dx, dx, jnp.broadcast_to(loss, (1, LANES)), dg

    dx, dxb, loss, dg = _rowwise(fn, [x, target], [g], [_sds((t, d)), _sds((t, d), BF16)],
                                 [_sds((1, LANES)), _sds((1, d))], name)
    return loss, dx, dxb, dg


def _adamw(w, g, m, v, name):
    c1 = 1.0 - ADAM_B1 ** ADAM_STEP
    c2 = 1.0 - ADAM_B2 ** ADAM_STEP

    def fn(wb, gb, mb, vb):
        mn = ADAM_B1 * mb + (1.0 - ADAM_B1) * gb
        vn = ADAM_B2 * vb + (1.0 - ADAM_B2) * (gb * gb)
        delta = -ADAM_LR * ((mn / c1) / (jnp.sqrt(vn / c2) + ADAM_EPS) + ADAM_WD * wb)
        return delta, mn, vn

    return _rowwise(fn, [w, g, m, v], [], [_sds(w.shape)] * 3, [], name)


CONV_CB = 512


def _shift_down(u, k):
    if k == 0:
        return u
    rows = lax.broadcasted_iota(jnp.int32, u.shape, 0)
    return jnp.where(rows >= k, pltpu.roll(u, k, 0), 0.0)


def _shift_up(u, k):
    if k == 0:
        return u
    n = u.shape[0]
    rows = lax.broadcasted_iota(jnp.int32, u.shape, 0)
    return jnp.where(rows < n - k, pltpu.roll(u, n - k, 0), 0.0)


def _conv_pre(u, w, b):
    pre = b
    for j in range(CONV_WIDTH):
        pre = pre + w[j:j + 1, :] * _shift_down(u, CONV_WIDTH - 1 - j)
    return pre


def _conv_fwd(proj, w, b, n_seq, name):
    t = proj.shape[0]
    seq = t // n_seq
    off = D_MODEL // CONV_CB

    def body(u_ref, w_ref, b_ref, o_ref):
        pre = _conv_pre(u_ref[...], w_ref[...], b_ref[...])
        o_ref[...] = pre * _sigmoid(pre)

    return pl.pallas_call(
        body, name=name, grid=(n_seq, D_CONV // CONV_CB),
        in_specs=[pl.BlockSpec((seq, CONV_CB), lambda s, c: (s, c + off)),
                  pl.BlockSpec((CONV_WIDTH, CONV_CB), lambda s, c: (0, c)),
                  pl.BlockSpec((1, CONV_CB), lambda s, c: (0, c))],
        out_specs=pl.BlockSpec((seq, CONV_CB), lambda s, c: (s, c)),
        out_shape=_sds((t, D_CONV)),
        compiler_params=_params("parallel", "parallel"),
    )(proj, w, b)


def _conv_bwd(proj, w, b, dxbc, n_seq, name):
    t = proj.shape[0]
    seq = t // n_seq
    off = D_MODEL // CONV_CB

    def body(u_ref, w_ref, b_ref, d_ref, du_ref, wg_ref):
        u, wv = u_ref[...], w_ref[...]
        pre = _conv_pre(u, wv, b_ref[...])
        s = _sigmoid(pre)
        dpre = d_ref[...] * (s * (1.0 + pre * (1.0 - s)))
        du = jnp.zeros_like(u)
        parts = []
        for j in range(CONV_WIDTH):
            k = CONV_WIDTH - 1 - j
            du = du + wv[j:j + 1, :] * _shift_up(dpre, k)
            parts.append(_colsum(dpre * _shift_down(u, k)))
        du_ref[...] = du.astype(BF16)
        parts.append(_colsum(dpre))

        @pl.when(pl.program_id(1) == 0)
        def _():
            wg_ref[...] = jnp.zeros_like(wg_ref)

        wg_ref[...] += _stack_rows(parts, u.shape[1])

    return pl.pallas_call(
        body, name=name, grid=(D_CONV // CONV_CB, n_seq),
        in_specs=[pl.BlockSpec((seq, CONV_CB), lambda c, s: (s, c + off)),
                  pl.BlockSpec((CONV_WIDTH, CONV_CB), lambda c, s: (0, c)),
                  pl.BlockSpec((1, CONV_CB), lambda c, s: (0, c)),
                  pl.BlockSpec((seq, CONV_CB), lambda c, s: (s, c))],
        out_specs=[pl.BlockSpec((seq, CONV_CB), lambda c, s: (s, c)),
                   pl.BlockSpec((8, CONV_CB), lambda c, s: (0, c))],
        out_shape=[_sds((t, D_CONV), BF16), _sds((8, D_CONV))],
        compiler_params=_params("parallel", "arbitrary"),
    )(proj, w, b, dxbc)


def _iota2(shape, axis):
    return lax.broadcasted_iota(jnp.int32, shape, axis)


def _dot(a, b, dims=_DIMS["nn"], precision=None):
    return lax.dot_general(a, b, dims, precision=precision, preferred_element_type=F32)


def _bdot(a, b, dims=_DIMS["nn"]):
    return lax.dot_general(a.astype(BF16), b.astype(BF16), dims, preferred_element_type=F32)


def _expand_mat():
    return (_iota2((LANES, D_MODEL), 0) == lax.shift_right_logical(_iota2((LANES, D_MODEL), 1), HEAD_SHIFT)).astype(F32)


def _reduce_mat():
    return (lax.shift_right_logical(_iota2((D_MODEL, LANES), 0), HEAD_SHIFT) == _iota2((D_MODEL, LANES), 1)).astype(F32)


def _ssd_decay(dtraw, bias, alog):
    row, col = _iota2((CHUNK, CHUNK), 0), _iota2((CHUNK, CHUNK), 1)
    pre = dtraw + bias
    dtb = jnp.maximum(pre, 0.0) + jnp.log(1.0 + jnp.exp(-jnp.abs(pre)))
    a_neg = -jnp.exp(alog)
    a = dtb * a_neg
    tril = (row >= col).astype(F32)
    triu = (row <= col).astype(F32)
    cs = _dot(tril, a, precision=HIGHEST)
    cs_t = _dot(a, triu, _DIMS["tn"], precision=HIGHEST)
    return pre, dtb, a_neg, cs, cs_t


def _pair_rowscale(vec, h0):
    top = _iota2((CHUNK, LANES), 0) < HEAD_DIM
    return jnp.where(top, vec[:, h0:h0 + 1], vec[:, h0 + 1:h0 + 2])


def _decay_mat(cs, cs_t, h):
    row, col = _iota2((CHUNK, CHUNK), 0), _iota2((CHUNK, CHUNK), 1)
    seg = cs[:, h:h + 1] - cs_t[h:h + 1, :]
    return jnp.exp(jnp.where(row >= col, seg, -jnp.inf))


def _ssd_fwd(xbc, dtraw, bias, alog, dskip_x, n_seq, name):
    t = xbc.shape[0]
    nc = t // n_seq // CHUNK

    def body(x_ref, b_ref, c_ref, dt_ref, bias_ref, alog_ref, dsk_ref, y_ref, st_ref, h_ref):
        @pl.when(pl.program_id(1) == 0)
        def _():
            h_ref[...] = jnp.zeros_like(h_ref)

        _, dtb, _, cs, cs_t = _ssd_decay(dt_ref[...], bias_ref[...], alog_ref[...])
        expand = _expand_mat()
        dt_x = _dot(dtb, expand, precision=HIGHEST)
        cs_x = _dot(cs, expand, precision=HIGHEST)
        tot = cs[CHUNK - 1:CHUNK, :]
        etot = jnp.exp(tot)
        x = x_ref[...]
        xdt = x * dt_x
        e_x = jnp.exp(cs_x)
        xdec = xdt * jnp.exp(cs_x[CHUNK - 1:CHUNK, :] - cs_x)
        keeps = [_iota2((CHUNK, LANES), 1) < HEAD_DIM, _iota2((CHUNK, LANES), 1) >= HEAD_DIM]
        for q in range(N_PAIRS):
            g = q // 2
            lanes = slice(q * LANES, (q + 1) * LANES)
            bg = b_ref[:, g * N_STATE:(g + 1) * N_STATE]
            cg = c_ref[:, g * N_STATE:(g + 1) * N_STATE]
            if q % 2 == 0:
                gmat = _bdot(cg, bg, _DIMS["nt"])
            xdt_q = xdt[:, lanes]
            ydiag = jnp.zeros((CHUNK, LANES), F32)
            for r in range(2):
                w = gmat * _decay_mat(cs, cs_t, 2 * q + r)
                ydiag = ydiag + _bdot(w, jnp.where(keeps[r], xdt_q, 0.0))
            prev = h_ref[q]
            yoff = _bdot(cg, prev, _DIMS["nt"]) * e_x[:, lanes]
            y_ref[:, lanes] = ydiag + yoff + x[:, lanes] * dsk_ref[:, lanes]
            st_ref[0, q] = prev
            h_ref[q] = prev * _pair_rowscale(etot, 2 * q) + _bdot(xdec[:, lanes], bg, _DIMS["tn"])

    vec = pl.BlockSpec((1, LANES), lambda s, c: (0, 0))
    return pl.pallas_call(
        body, name=name, grid=(n_seq, nc),
        in_specs=[pl.BlockSpec((CHUNK, D_MODEL), lambda s, c: (s * nc + c, 0)),
                  pl.BlockSpec((CHUNK, N_GROUPS * N_STATE), lambda s, c: (s * nc + c, 2)),
                  pl.BlockSpec((CHUNK, N_GROUPS * N_STATE), lambda s, c: (s * nc + c, 3)),
                  pl.BlockSpec((CHUNK, LANES), lambda s, c: (s * nc + c, 0)),
                  vec, vec, pl.BlockSpec((1, D_MODEL), lambda s, c: (0, 0))],
        out_specs=[pl.BlockSpec((CHUNK, D_MODEL), lambda s, c: (s * nc + c, 0)),
                   pl.BlockSpec((1, N_PAIRS, LANES, N_STATE), lambda s, c: (s * nc + c, 0, 0, 0))],
        out_shape=[_sds((t, D_MODEL)), _sds((t // CHUNK, N_PAIRS, LANES, N_STATE))],
        scratch_shapes=[pltpu.VMEM((N_PAIRS, LANES, N_STATE), F32)],
        compiler_params=_params("parallel", "arbitrary"),
    )(xbc, xbc, xbc, dtraw, bias, alog, dskip_x)


def _ssd_bwd(xbc, dtraw, bias, alog, dskip_x, states, dy, n_seq, name):
    t = xbc.shape[0]
    nc = t // n_seq // CHUNK

    def body(x_ref, b_ref, c_ref, dt_ref, bias_ref, alog_ref, dsk_ref, st_ref, dy_ref,
             dxbc_ref, ddt_ref, pg_ref, dh_ref):
        first = jnp.logical_and(pl.program_id(0) == 0, pl.program_id(1) == 0)

        @pl.when(first)
        def _():
            pg_ref[...] = jnp.zeros_like(pg_ref)

        @pl.when(pl.program_id(1) == 0)
        def _():
            dh_ref[...] = jnp.zeros_like(dh_ref)

        row, col = _iota2((CHUNK, CHUNK), 0), _iota2((CHUNK, CHUNK), 1)
        pre, dtb, a_neg, cs, cs_t = _ssd_decay(dt_ref[...], bias_ref[...], alog_ref[...])
        expand, reduce = _expand_mat(), _reduce_mat()
        dt_x = _dot(dtb, expand, precision=HIGHEST)
        cs_x = _dot(cs, expand, precision=HIGHEST)
        etot = jnp.exp(cs[CHUNK - 1:CHUNK, :])
        x, dy = x_ref[...], dy_ref[...]
        xdt = x * dt_x
        e_x = jnp.exp(cs_x)
        dec_x = jnp.exp(cs_x[CHUNK - 1:CHUNK, :] - cs_x)
        xdec = xdt * dec_x
        dye = dy * e_x
        keeps = [_iota2((CHUNK, LANES), 1) < HEAD_DIM, _iota2((CHUNK, LANES), 1) >= HEAD_DIM]
        row_lo = _iota2((CHUNK, LANES), 0) < HEAD_DIM
        dcs_col = jnp.zeros((CHUNK, LANES), F32)
        dcs_row = jnp.zeros((LANES, CHUNK), F32)
        dtot = jnp.zeros((1, LANES), F32)
        dxdt_parts, zdec_parts, yoff_parts = [], [], []
        for g in range(N_GROUPS):
            bg = b_ref[:, g * N_STATE:(g + 1) * N_STATE]
            cg = c_ref[:, g * N_STATE:(g + 1) * N_STATE]
            gmat = _bdot(cg, bg, _DIMS["nt"])
            dgmat = jnp.zeros((CHUNK, CHUNK), F32)
            dbg = jnp.zeros((CHUNK, N_STATE), F32)
            dcg = jnp.zeros((CHUNK, N_STATE), F32)
            for q in (2 * g, 2 * g + 1):
                lanes = slice(q * LANES, (q + 1) * LANES)
                xdt_q, dy_q = xdt[:, lanes], dy[:, lanes]
                dxdt_q = jnp.zeros((CHUNK, LANES), F32)
                for r in range(2):
                    h = 2 * q + r
                    keep = keeps[r]
                    lm = _decay_mat(cs, cs_t, h)
                    w = gmat * lm
                    dy_h = jnp.where(keep, dy_q, 0.0)
                    dm = _bdot(dy_h, xdt_q, _DIMS["nt"])
                    dxdt_q = dxdt_q + _bdot(w, dy_h, _DIMS["tn"])
                    dgmat = dgmat + dm * lm
                    tmat = dm * w
                    dcs_col = dcs_col + jnp.where(col == h, jnp.sum(tmat, axis=1, keepdims=True), 0.0)
                    dcs_row = dcs_row - jnp.where(row == h, jnp.sum(tmat, axis=0, keepdims=True), 0.0)
                prev = st_ref[0, q]
                dht = dh_ref[q]
                dxdtdec = _bdot(bg, dht, _DIMS["nt"])
                dxdt_q = dxdt_q + dxdtdec * dec_x[:, lanes]
                zdec_parts.append(dxdtdec * xdec[:, lanes])
                dbg = dbg + _bdot(xdec[:, lanes], dht)
                yoff_parts.append(dy_q * (_bdot(cg, prev, _DIMS["nt"]) * e_x[:, lanes]))
                dcg = dcg + _bdot(dye[:, lanes], prev)
                dprev = _bdot(dye[:, lanes], cg, _DIMS["tn"])
                hp = jnp.sum(dht * prev, axis=1, keepdims=True)
                d0 = jnp.sum(jnp.where(row_lo[:, :1], hp, 0.0), axis=0, keepdims=True)
                d1 = jnp.sum(jnp.where(row_lo[:, :1], 0.0, hp), axis=0, keepdims=True)
                lane1 = _iota2((1, LANES), 1)
                dtot = dtot + jnp.where(lane1 == 2 * q, d0, 0.0) + jnp.where(lane1 == 2 * q + 1, d1, 0.0)
                dh_ref[q] = dprev + dht * _pair_rowscale(etot, 2 * q)
                dxdt_parts.append(dxdt_q)
            dcg = dcg + _bdot(dgmat, bg)
            dbg = dbg + _bdot(dgmat, cg, _DIMS["tn"])
            dxbc_ref[:, D_MODEL + g * N_STATE:D_MODEL + (g + 1) * N_STATE] = dbg
            dxbc_ref[:, D_MODEL + (N_GROUPS + g) * N_STATE:D_MODEL + (N_GROUPS + g + 1) * N_STATE] = dcg
        dxdt = jnp.concatenate(dxdt_parts, axis=1)
        zdec = _dot(jnp.concatenate(zdec_parts, axis=1), reduce, precision=HIGHEST)
        yoff_d = _dot(jnp.concatenate(yoff_parts, axis=1), reduce, precision=HIGHEST)
        dtot = dtot * etot + _colsum(zdec)
        last = row[:, :LANES] == CHUNK - 1
        dcs_col = dcs_col + yoff_d - zdec + jnp.where(last, dtot, 0.0)
        triu = (row <= col).astype(F32)
        da = _dot(triu, dcs_col, precision=HIGHEST) + _dot(triu, dcs_row, _DIMS["nt"], precision=HIGHEST)
        ddt = _dot(dxdt * x, reduce, precision=HIGHEST) + da * a_neg
        ddtraw = ddt * _sigmoid(pre)
        ddt_ref[...] = ddtraw.astype(BF16)
        dxbc_ref[:, :D_MODEL] = dxdt * dt_x + dy * dsk_ref[...]
        dskip = _dot(jnp.broadcast_to(_colsum(dy * x), (8, D_MODEL)), reduce, precision=HIGHEST)[0:1, :]
        pg_ref[...] += _stack_rows([_colsum(ddtraw), _colsum(da * dtb) * a_neg, dskip], LANES)

    vec = pl.BlockSpec((1, LANES), lambda s, c: (0, 0))

    def blk(s, c):
        return s * nc + (nc - 1 - c)

    return pl.pallas_call(
        body, name=name, grid=(n_seq, nc),
        in_specs=[pl.BlockSpec((CHUNK, D_MODEL), lambda s, c: (blk(s, c), 0)),
                  pl.BlockSpec((CHUNK, N_GROUPS * N_STATE), lambda s, c: (blk(s, c), 2)),
                  pl.BlockSpec((CHUNK, N_GROUPS * N_STATE), lambda s, c: (blk(s, c), 3)),
                  pl.BlockSpec((CHUNK, LANES), lambda s, c: (blk(s, c), 0)),
                  vec, vec, pl.BlockSpec((1, D_MODEL), lambda s, c: (0, 0)),
                  pl.BlockSpec((1, N_PAIRS, LANES, N_STATE), lambda s, c: (blk(s, c), 0, 0, 0)),
                  pl.BlockSpec((CHUNK, D_MODEL), lambda s, c: (blk(s, c), 0))],
        out_specs=[pl.BlockSpec((CHUNK, D_CONV), lambda s, c: (blk(s, c), 0)),
                   pl.BlockSpec((CHUNK, LANES), lambda s, c: (blk(s, c), 0)),
                   pl.BlockSpec((8, LANES), lambda s, c: (0, 0))],
        out_shape=[_sds((t, D_CONV)), _sds((t, LANES), BF16), _sds((8, LANES))],
        scratch_shapes=[pltpu.VMEM((N_PAIRS, LANES, N_STATE), F32)],
        compiler_params=_params("arbitrary", "arbitrary"),
    )(xbc, xbc, xbc, dtraw, bias, alog, dskip_x, states, dy)


Q_COL = (D_MODEL + D_CONV) // LANES
K_COL = Q_COL + D_MODEL // LANES
V_COL = K_COL + D_MODEL // LANES
ATT_SCALE = HEAD_DIM ** -0.5


ATT_SUB = 2
ATT_TILE = ATT_SUB * CHUNK
ATT_NP = 2


ATT_ROWS = 2 * ATT_TILE


def _tri_dot(x, tri):
    return _dot(x.astype(BF16), tri)


def _att_stack(ref, lanes, keeps, scale=1.0):
    parts = []
    for a in range(ATT_SUB):
        blk = ref[a * CHUNK:(a + 1) * CHUNK, lanes] * scale
        parts += [jnp.where(keeps[r], blk, 0.0).astype(BF16) for r in range(2)]
    return jnp.concatenate(parts, axis=0)


def _att_unstack(x, a, keeps):
    return jnp.where(keeps[0], x[2 * a * CHUNK:(2 * a + 1) * CHUNK], x[(2 * a + 1) * CHUNK:(2 * a + 2) * CHUNK])


def _att_logits(s, diag):
    lb = jnp.minimum(s, 0.0) - jnp.log(1.0 + jnp.exp(-jnp.abs(s)))
    l1m = lb - s
    mask = None
    if diag:
        srow, scol = _iota2((ATT_ROWS, ATT_TILE), 0), _iota2((ATT_ROWS, ATT_TILE), 1)
        pair_shift = (2 * CHUNK).bit_length() - 1
        qpos = lax.shift_right_logical(srow, pair_shift) * CHUNK + jnp.bitwise_and(srow, CHUNK - 1)
        mask = qpos > scol
        l1m = jnp.where(mask, l1m, 0.0)
    return lb, l1m, mask


def _att_fwd(proj, n_seq, name):
    t = proj.shape[0]
    seq = t // n_seq
    nq = seq // ATT_TILE

    def body(q_ref, k_ref, v_ref, o_ref, rt_ref):
        i = pl.program_id(2)
        upper = (_iota2((ATT_TILE, ATT_TILE), 0) > _iota2((ATT_TILE, ATT_TILE), 1)).astype(BF16)
        keeps = [_iota2((CHUNK, LANES), 1) < HEAD_DIM, _iota2((CHUNK, LANES), 1) >= HEAD_DIM]
        pair_lanes = [slice(pr * LANES, (pr + 1) * LANES) for pr in range(ATT_NP)]
        q_stacks = [_att_stack(q_ref, lanes, keeps, ATT_SCALE) for lanes in pair_lanes]
        per_pair = ATT_SUB + 2

        def logits(jt):
            k0 = pl.multiple_of(jt * ATT_TILE, ATT_TILE)
            return [_dot(q_stacks[pr], k_ref[pl.ds(k0, ATT_TILE), lanes].astype(BF16), _DIMS["nt"])
                    for pr, lanes in enumerate(pair_lanes)]

        def tile(jt, carry, diag):
            state = list(carry)
            ahead = logits(jnp.maximum(jt - 1, 0))
            k0 = pl.multiple_of(jt * ATT_TILE, ATT_TILE)
            for pr, lanes in enumerate(pair_lanes):
                base = pr * per_pair
                run = state[base + ATT_SUB]
                v_tile = v_ref[pl.ds(k0, ATT_TILE), lanes].astype(BF16)
                lb, l1m, mask = _att_logits(state[base + ATT_SUB + 1], diag)
                p = jnp.exp(lb + (_tri_dot(l1m, upper) + run))
                if diag:
                    p = jnp.where(mask, p, 0.0)
                pv = _dot(p.astype(BF16), v_tile)
                for a in range(ATT_SUB):
                    state[base + a] = state[base + a] + _att_unstack(pv, a, keeps)
                state[base + ATT_SUB] = run + jnp.sum(l1m, axis=1, keepdims=True)
                state[base + ATT_SUB + 1] = ahead[pr]
            return tuple(state)

        first = logits(i)
        init = ()
        for pr in range(ATT_NP):
            init += tuple(jnp.zeros((CHUNK, LANES), F32) for _ in range(ATT_SUB)) + \
                (jnp.zeros((ATT_ROWS, 1), F32), first[pr])
        carry = tile(i, init, True)
        carry = lax.fori_loop(0, i, lambda it, c: tile(i - 1 - it, c, False), carry)
        for pr, lanes in enumerate(pair_lanes):
            for a in range(ATT_SUB):
                o_ref[a * CHUNK:(a + 1) * CHUNK, lanes] = carry[pr * per_pair + a]
                rt_ref[a * CHUNK:(a + 1) * CHUNK, lanes] = _att_unstack(carry[pr * per_pair + ATT_SUB], a, keeps)

    width = ATT_NP * LANES
    qblk = pl.BlockSpec((ATT_TILE, width), lambda s, p, i: (s * nq + i, p))
    return pl.pallas_call(
        body, name=name, grid=(n_seq, N_PAIRS // ATT_NP, nq),
        in_specs=[pl.BlockSpec((ATT_TILE, width), lambda s, p, i: (s * nq + i, Q_COL // ATT_NP + p)),
                  pl.BlockSpec((seq, width), lambda s, p, i: (s, K_COL // ATT_NP + p)),
                  pl.BlockSpec((seq, width), lambda s, p, i: (s, V_COL // ATT_NP + p))],
        out_specs=[qblk, qblk],
        out_shape=[_sds((t, D_MODEL)), _sds((t, D_MODEL))],
        compiler_params=_params("parallel", "parallel", "arbitrary"),
    )(proj, proj, proj)


def _att_bwd(proj, rtot, datt, n_seq, name):
    t = proj.shape[0]
    seq = t // n_seq
    nq = seq // ATT_TILE

    def body(q_ref, k_ref, v_ref, rt_ref, do_ref, dq_ref, dk_out, dv_out, dk_ref, dv_ref):
        i = pl.program_id(2)

        @pl.when(i == 0)
        def _():
            dk_ref[...] = jnp.zeros_like(dk_ref)
            dv_ref[...] = jnp.zeros_like(dv_ref)

        row, col = _iota2((ATT_TILE, ATT_TILE), 0), _iota2((ATT_TILE, ATT_TILE), 1)
        upper = (row > col).astype(BF16)
        before = (row < col).astype(BF16)
        keeps = [_iota2((CHUNK, LANES), 1) < HEAD_DIM, _iota2((CHUNK, LANES), 1) >= HEAD_DIM]
        pair_lanes = [slice(pr * LANES, (pr + 1) * LANES) for pr in range(ATT_NP)]
        q_stacks = [_att_stack(q_ref, lanes, keeps, ATT_SCALE) for lanes in pair_lanes]
        do_stacks = [_att_stack(do_ref, lanes, keeps) for lanes in pair_lanes]
        totals = [jnp.concatenate([rt_ref[a * CHUNK:(a + 1) * CHUNK, pr * LANES + r * HEAD_DIM:pr * LANES + r * HEAD_DIM + 1]
                                   for a in range(ATT_SUB) for r in range(2)], axis=0) for pr in range(ATT_NP)]
        per_pair = ATT_SUB + 4

        def products(jt):
            k0 = pl.multiple_of(jt * ATT_TILE, ATT_TILE)
            return [(_dot(q_stacks[pr], k_ref[pl.ds(k0, ATT_TILE), lanes].astype(BF16), _DIMS["nt"]),
                     _dot(do_stacks[pr], v_ref[pl.ds(k0, ATT_TILE), lanes].astype(BF16), _DIMS["nt"]))
                    for pr, lanes in enumerate(pair_lanes)]

        def tile(jt, carry, diag):
            state = list(carry)
            ahead = None if diag else products(jt + 1)
            k0 = pl.multiple_of(jt * ATT_TILE, ATT_TILE)
            for pr, lanes in enumerate(pair_lanes):
                base = pr * per_pair
                seen, dseen = state[base + ATT_SUB], state[base + ATT_SUB + 1]
                q_stack, do_stack = q_stacks[pr], do_stacks[pr]
                k_tile = k_ref[pl.ds(k0, ATT_TILE), lanes].astype(BF16)
                lb, l1m, mask = _att_logits(state[base + ATT_SUB + 2], diag)
                here = jnp.sum(l1m, axis=1, keepdims=True)
                p = jnp.exp(lb + (_tri_dot(l1m, upper) + (totals[pr] - seen - here)))
                if diag:
                    p = jnp.where(mask, p, 0.0)
                pb = p.astype(BF16)
                dz = state[base + ATT_SUB + 3] * p
                dl1m = dseen + _tri_dot(dz, before)
                sig = jnp.exp(lb)
                ds = dz * (1.0 - sig) - dl1m * sig
                if diag:
                    ds = jnp.where(mask, ds, 0.0)
                dsb = ds.astype(BF16)
                dq_all = _dot(dsb, k_tile)
                for a in range(ATT_SUB):
                    state[base + a] = state[base + a] + _att_unstack(dq_all, a, keeps)
                dk_ref[pl.ds(k0, ATT_TILE), lanes] += _dot(dsb, q_stack, _DIMS["tn"])
                dv_ref[pl.ds(k0, ATT_TILE), lanes] += _dot(pb, do_stack, _DIMS["tn"])
                state[base + ATT_SUB] = seen + here
                state[base + ATT_SUB + 1] = dseen + jnp.sum(dz, axis=1, keepdims=True)
                if ahead is not None:
                    state[base + ATT_SUB + 2], state[base + ATT_SUB + 3] = ahead[pr]
            return tuple(state)

        first = products(0)
        init = ()
        for pr in range(ATT_NP):
            init += tuple(jnp.zeros((CHUNK, LANES), F32) for _ in range(ATT_SUB)) + \
                (jnp.zeros((ATT_ROWS, 1), F32), jnp.zeros((ATT_ROWS, 1), F32)) + first[pr]
        carry = lax.fori_loop(0, i, lambda jt, c: tile(jt, c, False), init)
        carry = tile(i, carry, True)
        for pr, lanes in enumerate(pair_lanes):
            for a in range(ATT_SUB):
                dq_ref[a * CHUNK:(a + 1) * CHUNK, lanes] = (carry[pr * per_pair + a] * ATT_SCALE).astype(BF16)

        @pl.when(i == nq - 1)
        def _():
            dk_out[...] = dk_ref[...].astype(BF16)
            dv_out[...] = dv_ref[...].astype(BF16)

    width = ATT_NP * LANES
    qblk = pl.BlockSpec((ATT_TILE, width), lambda s, p, i: (s * nq + i, p))
    kv_out = pl.BlockSpec((seq, width), lambda s, p, i: (s, p))
    return pl.pallas_call(
        body, name=name, grid=(n_seq, N_PAIRS // ATT_NP, nq),
        in_specs=[pl.BlockSpec((ATT_TILE, width), lambda s, p, i: (s * nq + i, Q_COL // ATT_NP + p)),
                  pl.BlockSpec((seq, width), lambda s, p, i: (s, K_COL // ATT_NP + p)),
                  pl.BlockSpec((seq, width), lambda s, p, i: (s, V_COL // ATT_NP + p)),
                  qblk, qblk],
        out_specs=[qblk, kv_out, kv_out],
        out_shape=[_sds((t, D_MODEL), BF16)] * 3,
        scratch_shapes=[pltpu.VMEM((seq, width), F32), pltpu.VMEM((seq, width), F32)],
        compiler_params=_params("parallel", "parallel", "arbitrary"),
    )(proj, proj, proj, rtot, datt)


def _pad_lanes(v):
    return jnp.pad(v.reshape(1, -1), ((0, 0), (0, LANES - v.shape[0])))


def _add_then_norm(acc, xb, g):
    x1 = xb + acc
    return x1, x1 * _rstd(x1) * g


def _layer_fwd(x, h, p, next_g, n_seq, tag):
    proj, = _matmul(h, p["w_main"], "nn", [F32], f"in_proj{tag}", tn=1536)
    dtraw, = _matmul(h, p["w_dt"], "nn", [F32], f"dt_proj{tag}")
    xbc = _conv_fwd(proj, p["conv_w"], p["conv_b"], n_seq, f"conv_fwd{tag}")
    y, states = _ssd_fwd(xbc, dtraw, p["dt_bias"], p["a_log"], p["d_skip_x"], n_seq, f"ssd_fwd{tag}")
    att, rtot = _att_fwd(proj, n_seq, f"att_fwd{tag}")
    ycat = _mixnorm_fwd(y, proj, att, p["ssd_norm_g"], p["att_norm_g"], f"mixnorm_fwd{tag}")
    x1, h2 = _matmul(ycat, p["w_out"], "nn", [F32, BF16], f"out_proj{tag}", extras=[x], vecs=[p["norm_mlp_g"]],
                     epilogue=_add_then_norm, tn=D_MODEL)
    u, act = _matmul(h2, p["w_up"], "nn", [F32, BF16], f"up_proj{tag}",
                     epilogue=lambda acc: (acc, jnp.square(jnp.maximum(acc, 0.0))))
    if next_g is None:
        x2, = _matmul(act, p["w_down"], "nn", [F32], f"down_proj{tag}", extras=[x1],
                      epilogue=lambda acc, xb: (xb + acc,))
        h_next = None
    else:
        x2, h_next = _matmul(act, p["w_down"], "nn", [F32, BF16], f"down_proj{tag}", extras=[x1], vecs=[next_g],
                             epilogue=_add_then_norm, tn=D_MODEL)
    saved = dict(x=x, h=h, proj=proj, dtraw=dtraw, xbc=xbc, y=y, states=states, att=att, rtot=rtot, ycat=ycat,
                 x1=x1, h2=h2, u=u, act=act)
    return x2, h_next, saved


def _slab(buffers, name, layer, rows, per_chip_rows):
    shape = (N_CHIPS, DEPTH * rows, D_MODEL)
    if per_chip_rows:
        return buffers.get(name), shape, lambda i, j: (i, layer, 0)
    return buffers.get(name), shape, lambda i, j: (j, layer, 0)


def _layer_bwd(dx2, dx2b, p, s, buffers, layer, n_seq, tag):
    g = {}
    buffers["w_down"], = _matmul(s["act"], dx2b, "tn", [F32], f"dw_down{tag}",
                                 slab=_slab(buffers, "w_down", layer, D_FF // N_CHIPS, True), tm=D_FF // N_CHIPS)
    du, = _matmul(dx2b, p["w_down"], "nt", [BF16], f"d_act{tag}", extras=[s["u"]],
                  epilogue=lambda acc, ub: (acc * (2.0 * jnp.maximum(ub, 0.0)),))
    buffers["w_up"], = _matmul(s["h2"], du, "tn", [F32], f"dw_up{tag}",
                               slab=_slab(buffers, "w_up", layer, D_MODEL, False), tn=D_FF // N_CHIPS)
    dh2, = _matmul(du, p["w_up"], "nt", [F32], f"d_h2{tag}")
    dx1, dx1b, g["norm_mlp_g"] = _rms_bwd(s["x1"], p["norm_mlp_g"], dh2, dx2, f"rms_mlp_bwd{tag}")
    buffers["w_out"], = _matmul(s["ycat"], dx1b, "tn", [F32], f"dw_out{tag}",
                                slab=_slab(buffers, "w_out", layer, 2 * D_MODEL // N_CHIPS, True),
                                tm=2 * D_MODEL // N_CHIPS)
    dycat, = _matmul(dx1b, p["w_out"], "nt", [F32], f"d_ycat{tag}")
    dy, dz, datt, g["ssd_norm_g"], g["att_norm_g"] = _mixnorm_bwd(
        s["y"], s["proj"], s["att"], p["ssd_norm_g"], p["att_norm_g"], dycat, f"mixnorm_bwd{tag}")
    dq, dk, dv = _att_bwd(s["proj"], s["rtot"], datt, n_seq, f"att_bwd{tag}")
    dxbc, ddtraw, pg = _ssd_bwd(s["xbc"], s["dtraw"], p["dt_bias"], p["a_log"], p["d_skip_x"], s["states"], dy,
                                n_seq, f"ssd_bwd{tag}")
    g["dt_bias"], g["a_log"], g["d_skip"] = pg[0, :N_HEADS], pg[1, :N_HEADS], pg[2, :N_HEADS]
    du_conv, wg = _conv_bwd(s["proj"], p["conv_w"], p["conv_b"], dxbc, n_seq, f"conv_bwd{tag}")
    g["conv_w"], g["conv_b"] = wg[:CONV_WIDTH], wg[CONV_WIDTH]
    dproj = jnp.concatenate([dz, du_conv, dq, dk, dv], axis=1)
    g["w_main"], = _matmul(s["h"], dproj, "tn", [F32], f"dw_in{tag}", tn=1536)
    g["w_dt"], = _matmul(s["h"], ddtraw, "tn", [F32], f"dw_dt{tag}")
    dh_dt, = _matmul(ddtraw, p["w_dt"], "nt", [F32], f"d_h_dt{tag}")
    dh, = _matmul(dproj, p["w_main"], "nt", [F32], f"d_h{tag}", extras=[dh_dt], epilogue=lambda acc, e: (acc + e,))
    dx, dxb, g["norm_mix_g"] = _rms_bwd(s["x"], p["norm_mix_g"], dh, dx1, f"rms_mix_bwd{tag}")
    return dx, dxb, g


def _split_w_in(w_full):
    c0 = D_MODEL + D_CONV
    main = jnp.concatenate([w_full[:, :c0], w_full[:, c0 + N_HEADS:]], axis=1)
    dt = jnp.pad(w_full[:, c0:c0 + N_HEADS], ((0, 0), (0, LANES - N_HEADS)))
    return main, dt


def _merge_w_in(main, dt):
    c0 = D_MODEL + D_CONV
    return jnp.concatenate([main[:, :c0], dt[:, :N_HEADS], main[:, c0:]], axis=1)


def _prep_layer(l, w_in_full, w_out, w_up, w_down, conv_w, small):
    w_main, w_dt = _split_w_in(w_in_full)
    return dict(
        w_main=w_main, w_dt=w_dt, w_out=w_out, w_up=w_up, w_down=w_down, conv_w=conv_w,
        conv_b=small["conv_b"][l].reshape(1, -1),
        dt_bias=_pad_lanes(small["dt_bias"][l]), a_log=_pad_lanes(small["a_log"][l]),
        d_skip_x=jnp.repeat(small["d_skip"][l], HEAD_DIM).reshape(1, -1),
        norm_mix_g=small["norm_mix_g"][l].reshape(1, -1), ssd_norm_g=small["ssd_norm_g"][l].reshape(1, -1),
        att_norm_g=small["att_norm_g"][l].reshape(1, -1), norm_mlp_g=small["norm_mlp_g"][l].reshape(1, -1),
    )


ANY = pl.BlockSpec(memory_space=pl.ANY)


def _place():
    x, y, c = lax.axis_index("x"), lax.axis_index("y"), lax.axis_index("c")
    return x, y, c, (x, y, 1 - c), [(1 - x, y), (x, 1 - y), (1 - x, 1 - y)]


def _all_gather_chips(shard, chip, name):
    r, ccols = shard.shape
    half = r // 2

    def body(s_ref, o_ref, send_sems, recv_sems):
        x, y, c, sibling, chips = _place()

        def slab(px, py, hc):
            return o_ref.at[2 * px + py, pl.ds(hc * half, half), :]

        def copy(k, src, dst, to):
            return pltpu.make_async_remote_copy(src_ref=src, dst_ref=dst, send_sem=send_sems.at[k],
                                                recv_sem=recv_sems.at[k], device_id=to, device_id_type=MESH)

        sends = [copy(k, s_ref.at[pl.ds(c * half, half), :], slab(x, y, c), (px, py, c))
                 for k, (px, py) in enumerate(chips)]
        for cp in sends:
            cp.start()
        passed = []
        for k, (px, py) in enumerate(chips):
            copy(k, slab(px, py, c), slab(px, py, c), (px, py, c)).wait_recv()
            cp = copy(3 + k, slab(px, py, c), slab(px, py, c), sibling)
            cp.start()
            passed.append(cp)
        for k, (px, py) in enumerate(chips):
            copy(3 + k, slab(px, py, 1 - c), slab(px, py, 1 - c), sibling).wait_recv()
        for cp in sends + passed:
            cp.wait_send()

    others = pl.pallas_call(
        body, name=name, in_specs=[ANY], out_specs=ANY,
        out_shape=_sds((N_CHIPS, r, ccols), shard.dtype),
        scratch_shapes=[pltpu.SemaphoreType.DMA((6,)), pltpu.SemaphoreType.DMA((6,))],
    )(shard)
    return lax.dynamic_update_slice(others, shard[None], (chip, 0, 0))


def _sibling_swap(g, name):
    n, r, ccols = g.shape
    half = r // 2

    def body(g_ref, o_ref, send_sem, recv_sem):
        _, _, c, sibling, _ = _place()
        cp = pltpu.make_async_remote_copy(src_ref=g_ref.at[:, pl.ds((1 - c) * half, half), :], dst_ref=o_ref,
                                          send_sem=send_sem, recv_sem=recv_sem, device_id=sibling, device_id_type=MESH)
        cp.start()
        cp.wait()

    return pl.pallas_call(
        body, name=name, in_specs=[ANY], out_specs=ANY, out_shape=_sds((n, half, ccols), g.dtype),
        scratch_shapes=[pltpu.SemaphoreType.DMA, pltpu.SemaphoreType.DMA],
    )(g)


def _chip_exchange(p, name):
    _, h, ccols = p.shape

    def body(p_ref, o_ref, send_sems, recv_sems):
        _, _, c, _, chips = _place()
        copies = [pltpu.make_async_remote_copy(src_ref=p_ref.at[2 * px + py], dst_ref=o_ref.at[k],
                                               send_sem=send_sems.at[k], recv_sem=recv_sems.at[k],
                                               device_id=(px, py, c), device_id_type=MESH)
                  for k, (px, py) in enumerate(chips)]
        for cp in copies:
            cp.start()
        for cp in copies:
            cp.wait()

    return pl.pallas_call(
        body, name=name, in_specs=[ANY], out_specs=ANY, out_shape=_sds((3, h, ccols), p.dtype),
        scratch_shapes=[pltpu.SemaphoreType.DMA((3,)), pltpu.SemaphoreType.DMA((3,))],
    )(p)


def _sibling_share(full, name):
    r, ccols = full.shape
    h = r // 2

    def body(f_ref, o_ref, send_sem, recv_sem):
        _, _, c, sibling, _ = _place()
        mine = pl.ds(c * h, h)
        cp = pltpu.make_async_remote_copy(src_ref=f_ref.at[mine, :], dst_ref=o_ref.at[mine, :], send_sem=send_sem,
                                          recv_sem=recv_sem, device_id=sibling, device_id_type=MESH)
        cp.start()
        theirs = o_ref.at[pl.ds((1 - c) * h, h), :]
        pltpu.make_async_remote_copy(src_ref=theirs, dst_ref=theirs, send_sem=send_sem, recv_sem=recv_sem,
                                     device_id=sibling, device_id_type=MESH).wait_recv()
        cp.wait_send()

    return pl.pallas_call(
        body, name=name, in_specs=[ANY], out_specs=ANY, out_shape=_sds((r, ccols), full.dtype),
        input_output_aliases={0: 0},
        scratch_shapes=[pltpu.SemaphoreType.DMA, pltpu.SemaphoreType.DMA],
    )(full)


def _add_halves(g, a, c, name):
    n, r, ccols = g.shape
    half = r // 2
    tr = _tile(half, 256)
    nb = half // tr

    def body(c_ref, g_ref, a_ref, o_ref):
        o_ref[...] = (g_ref[...] + a_ref[...]).astype(BF16)

    blk = (1, tr, ccols)
    return pl.pallas_call(
        body, name=name,
        grid_spec=pltpu.PrefetchScalarGridSpec(
            num_scalar_prefetch=1, grid=(n, nb),
            in_specs=[pl.BlockSpec(blk, lambda j, i, c_ref: (j, c_ref[0] * nb + i, 0)),
                      pl.BlockSpec(blk, lambda j, i, c_ref: (j, i, 0))],
            out_specs=pl.BlockSpec(blk, lambda j, i, c_ref: (j, i, 0))),
        out_shape=_sds((n, half, ccols), BF16),
        compiler_params=_params("parallel", "parallel"),
    )(c.reshape(1).astype(jnp.int32), g, a)


def _sum_chips(p, got, chip, c, name):
    _, h, ccols = p.shape
    tr = _tile(h, 256)

    def body(j_ref, h_ref, p_ref, a_ref, b_ref, c_ref, o_ref):
        f32 = [ref[...].astype(F32) for ref in (p_ref, a_ref, b_ref, c_ref)]
        o_ref[...] = ((f32[0] + f32[1]) + f32[2]) + f32[3]

    blk = (1, tr, ccols)

    def slot(k):
        return pl.BlockSpec(blk, lambda i, j_ref, h_ref: (k, i, 0))

    return pl.pallas_call(
        body, name=name,
        grid_spec=pltpu.PrefetchScalarGridSpec(
            num_scalar_prefetch=2, grid=(h // tr,),
            in_specs=[pl.BlockSpec(blk, lambda i, j_ref, h_ref: (j_ref[0], i, 0)), slot(0), slot(1), slot(2)],
            out_specs=pl.BlockSpec(blk, lambda i, j_ref, h_ref: (h_ref[0], i, 0))),
        out_shape=_sds((2, h, ccols)),
        compiler_params=_params("parallel"),
    )(chip.reshape(1).astype(jnp.int32), c.reshape(1).astype(jnp.int32), p, got, got, got)


def _reduce_scatter(g, chip, c, name):
    a = _sibling_swap(g, f"{name}_swap")
    p = _add_halves(g, a, c, f"{name}_add")
    got = _chip_exchange(p, f"{name}_xchg")
    halves = _sum_chips(p, got, chip, c, f"{name}_sum")
    return _sibling_share(halves.reshape(g.shape[1], g.shape[2]), f"{name}_share")


def _all_reduce_small(v, name):
    r = v.shape[0]

    def body(v_ref, o_ref, buf, send_sems, recv_sems):
        x, y, c, _, _ = _place()
        buf[0] = v_ref[...]
        copies = []
        for rel in range(1, 8):
            fx, fy, fc = (rel >> 2) & 1, (rel >> 1) & 1, rel & 1
            peer = (1 - x if fx else x, 1 - y if fy else y, 1 - c if fc else c)
            cp = pltpu.make_async_remote_copy(src_ref=v_ref, dst_ref=buf.at[rel], send_sem=send_sems.at[rel - 1],
                                              recv_sem=recv_sems.at[rel - 1], device_id=peer, device_id_type=MESH)
            cp.start()
            copies.append(cp)
        for cp in copies:
            cp.wait()
        me = 4 * x + 2 * y + c
        acc = buf[jnp.bitwise_xor(me, 0)]
        for src in range(1, 8):
            acc = acc + buf[jnp.bitwise_xor(me, src)]
        o_ref[...] = acc

    vm = pl.BlockSpec(memory_space=pltpu.VMEM)
    return pl.pallas_call(
        body, name=name, in_specs=[vm], out_specs=vm, out_shape=_sds((r, LANES)),
        scratch_shapes=[pltpu.VMEM((8, r, LANES), F32), pltpu.SemaphoreType.DMA((7,)), pltpu.SemaphoreType.DMA((7,))],
    )(v)


WEIGHTS = ["norm_mix_g", "w_in", "conv_w", "conv_b", "dt_bias", "a_log", "d_skip", "ssd_norm_g", "att_norm_g",
           "w_out", "norm_mlp_g", "w_up", "w_down", "final_norm_g"]
BIG = ["w_in", "w_out", "w_up", "w_down"]
SMALL = [n for n in WEIGHTS if n not in BIG]


def _pack(arrays):
    flat = []
    for a in arrays:
        a = a.reshape(-1)
        flat.append(jnp.pad(a, (0, (-a.shape[0]) % LANES)))
    flat = jnp.concatenate(flat)
    flat = jnp.pad(flat, (0, (-flat.shape[0]) % (8 * LANES)))
    return flat.reshape(-1, LANES)


def _unpack(packed, shapes):
    flat, out, pos = packed.reshape(-1), [], 0
    for shp in shapes:
        n = math.prod(shp)
        out.append(flat[pos:pos + n].reshape(shp))
        pos += n + (-n) % LANES
    return out


def _to_shards(name, g):
    if name in ("w_in", "w_up"):
        l, r, ccols = g.shape
        return g.reshape(l, r, N_CHIPS, ccols // N_CHIPS).transpose(2, 0, 1, 3).reshape(N_CHIPS, l * r, ccols // N_CHIPS)
    l, r, ccols = g.shape
    return g.reshape(l, N_CHIPS, r // N_CHIPS, ccols).transpose(1, 0, 2, 3).reshape(N_CHIPS, l * r // N_CHIPS, ccols)


def _from_gathered(name, g, l):
    rows = g.shape[1] // DEPTH
    part = g[:, l * rows:(l + 1) * rows, :]
    if name in ("w_in", "w_up", "conv_w"):
        return part.transpose(1, 0, 2).reshape(rows, N_CHIPS * g.shape[2])
    return part.reshape(N_CHIPS * rows, g.shape[2])


def kernel(x, norm_mix_g, w_in, conv_w, conv_b, dt_bias, a_log, d_skip, ssd_norm_g, att_norm_g, w_out, norm_mlp_g, w_up, w_down, final_norm_g, loss_target, m_norm_mix_g, m_w_in, m_conv_w, m_conv_b, m_dt_bias, m_a_log, m_d_skip, m_ssd_norm_g, m_att_norm_g, m_w_out, m_norm_mlp_g, m_w_up, m_w_down, m_final_norm_g, v_norm_mix_g, v_w_in, v_conv_w, v_conv_b, v_dt_bias, v_a_log, v_d_skip, v_ssd_norm_g, v_att_norm_g, v_w_out, v_norm_mlp_g, v_w_up, v_w_down, v_final_norm_g):
    w = dict(norm_mix_g=norm_mix_g, w_in=w_in, conv_w=conv_w, conv_b=conv_b, dt_bias=dt_bias, a_log=a_log,
             d_skip=d_skip, ssd_norm_g=ssd_norm_g, att_norm_g=att_norm_g, w_out=w_out, norm_mlp_g=norm_mlp_g,
             w_up=w_up, w_down=w_down, final_norm_g=final_norm_g)
    m = dict(norm_mix_g=m_norm_mix_g, w_in=m_w_in, conv_w=m_conv_w, conv_b=m_conv_b, dt_bias=m_dt_bias,
             a_log=m_a_log, d_skip=m_d_skip, ssd_norm_g=m_ssd_norm_g, att_norm_g=m_att_norm_g, w_out=m_w_out,
             norm_mlp_g=m_norm_mlp_g, w_up=m_w_up, w_down=m_w_down, final_norm_g=m_final_norm_g)
    v = dict(norm_mix_g=v_norm_mix_g, w_in=v_w_in, conv_w=v_conv_w, conv_b=v_conv_b, dt_bias=v_dt_bias,
             a_log=v_a_log, d_skip=v_d_skip, ssd_norm_g=v_ssd_norm_g, att_norm_g=v_att_norm_g, w_out=v_w_out,
             norm_mlp_g=v_norm_mlp_g, w_up=v_w_up, w_down=v_w_down, final_norm_g=v_final_norm_g)
    n_seq, seq, d = x.shape
    t = n_seq * seq
    chip = 2 * lax.axis_index("x") + lax.axis_index("y")
    core = lax.axis_index("c")

    gathered = {n: _all_gather_chips(w[n].astype(BF16).reshape(-1, w[n].shape[-1]), chip, f"gather_{n}") for n in BIG}
    gathered["conv_w"] = _all_gather_chips(conv_w.reshape(-1, conv_w.shape[-1]), chip, "gather_conv_w")
    layers = [_prep_layer(l, *[_from_gathered(n, gathered[n], l) for n in BIG + ["conv_w"]], w) for l in range(DEPTH)]

    xs = x.reshape(t, d)
    hs = _rms_fwd(xs, layers[0]["norm_mix_g"], "rms_mix_fwd_l0")
    saved = []
    for l in range(DEPTH):
        next_g = layers[l + 1]["norm_mix_g"] if l + 1 < DEPTH else None
        xs, hs, s = _layer_fwd(xs, hs, layers[l], next_g, n_seq, f"_l{l}")
        saved.append(s)
    loss_vec, dx, dxb, g_final = _final_loss(xs, final_norm_g.reshape(1, d), loss_target.reshape(t, d), "final_loss")
    loss = lax.psum(loss_vec[0, 0], ("x", "y", "c"))

    grads, shard_major = [None] * DEPTH, {}
    for l in reversed(range(DEPTH)):
        dx, dxb, grads[l] = _layer_bwd(dx, dxb, layers[l], saved[l], shard_major, l, n_seq, f"_l{l}")
        grads[l]["w_in"] = _merge_w_in(grads[l].pop("w_main"), grads[l].pop("w_dt"))
    grad_x = dx.reshape(n_seq, seq, d)

    full = {n: jnp.stack([grads[l][n] for l in range(DEPTH)]) for n in SMALL + ["w_in"] if n != "final_norm_g"}
    full["final_norm_g"] = g_final.reshape(d)
    shard_major["w_in"] = _to_shards("w_in", full["w_in"])
    g_out = {}
    for n in BIG:
        red = _reduce_scatter(shard_major[n], chip, core, f"rs_{n}")
        g_out[n] = red.reshape(w[n].shape)
    small_sum = _all_reduce_small(_pack([full[n] for n in SMALL]), "allreduce_small")
    small_shapes = [(DEPTH, CONV_WIDTH, D_CONV) if n == "conv_w" else w[n].shape for n in SMALL]
    for n, val in zip(SMALL, _unpack(small_sum, small_shapes)):
        g_out[n] = val
    g_out["conv_w"] = lax.dynamic_slice_in_dim(g_out["conv_w"], chip * conv_w.shape[-1], conv_w.shape[-1], axis=2)

    delta, new_m, new_v = {}, {}, {}
    for n in BIG:
        two_d = (-1, w[n].shape[-1])
        dl, mn, vn = _adamw(w[n].reshape(two_d), g_out[n].reshape(two_d), m[n].reshape(two_d), v[n].reshape(two_d),
                            f"adamw_{n}")
        delta[n], new_m[n], new_v[n] = dl.reshape(w[n].shape), mn.reshape(w[n].shape), vn.reshape(w[n].shape)
    packs = [_pack([src[n] for n in SMALL]) for src in (w, g_out, m, v)]
    shapes = [w[n].shape for n in SMALL]
    for dst, packed in zip((delta, new_m, new_v), _adamw(*packs, "adamw_small")):
        for n, val in zip(SMALL, _unpack(packed, shapes)):
            dst[n] = val

    return (loss, grad_x, *[g_out[n] for n in WEIGHTS], *[delta[n] for n in WEIGHTS],
            *[new_m[n] for n in WEIGHTS], *[new_v[n] for n in WEIGHTS])
```

```python
import functools
import math

import jax
import jax.numpy as jnp
from jax import lax
from jax.experimental import pallas as pl
from jax.experimental.pallas import tpu as pltpu

F32 = jnp.float32
BF16 = jnp.bfloat16
HIGHEST = lax.Precision.HIGHEST

D_MODEL = 1024
DEPTH = 4
HEAD_DIM = 64
HEAD_SHIFT = HEAD_DIM.bit_length() - 1
N_HEADS = 16
N_GROUPS = 4
N_STATE = 128
N_PAIRS = N_HEADS // 2
CONV_WIDTH = 4
CHUNK = 128
D_CONV = D_MODEL + 2 * N_GROUPS * N_STATE
D_MAIN = D_MODEL + D_CONV + 3 * D_MODEL
D_IN_PROJ = D_MAIN + N_HEADS
D_FF = 4 * D_MODEL
EPS = 1e-5
LANES = 128
VMEM_LIMIT = 48 * 1024 * 1024

ADAM_LR = 0.001
ADAM_B1 = 0.9
ADAM_B2 = 0.999
ADAM_EPS = 1e-08
ADAM_WD = 0.01
ADAM_STEP = 10

N_CHIPS = 4
MESH = pl.DeviceIdType.MESH


def _tile(n, cap):
    if n <= cap:
        return n
    t = cap
    while t >= 8:
        if n % t == 0:
            return t
        t //= 2
    raise ValueError(f"no tile for {n} under {cap}")


def _params(*sem):
    return pltpu.CompilerParams(dimension_semantics=sem, vmem_limit_bytes=VMEM_LIMIT)


_DIMS = {"nn": (((1,), (0,)), ((), ())), "nt": (((1,), (1,)), ((), ())), "tn": (((0,), (0,)), ((), ()))}


def _matmul(a, b, mode, out_dtypes, name, extras=(), vecs=(), epilogue=None, slab=None, tm=1024, tn=1024, tk=1024):
    if mode == "nn":
        (m, k), (_, n) = a.shape, b.shape
    elif mode == "nt":
        (m, k), (n, _) = a.shape, b.shape
    else:
        (k, m), (_, n) = a.shape, b.shape
    tm, tn, tk = _tile(m, tm), _tile(n, tn), _tile(k, tk)
    nk = k // tk
    if mode == "tn":
        a_spec = pl.BlockSpec((tk, tm), lambda i, j, kk: (kk, i))
    else:
        a_spec = pl.BlockSpec((tm, tk), lambda i, j, kk: (i, kk))
    if mode == "nt":
        b_spec = pl.BlockSpec((tn, tk), lambda i, j, kk: (j, kk))
    else:
        b_spec = pl.BlockSpec((tk, tn), lambda i, j, kk: (kk, j))
    mn_spec = pl.BlockSpec((tm, tn), lambda i, j, kk: (i, j))
    vec_spec = pl.BlockSpec((1, tn), lambda i, j, kk: (0, j))
    n_extra, n_vec, n_out = len(extras), len(vecs), len(out_dtypes)
    n_in = n_extra + n_vec + (1 if slab is not None and slab[0] is not None else 0)
    dims = _DIMS[mode]

    def body(a_ref, b_ref, *rest):
        in_refs, out_refs, acc = rest[:n_extra + n_vec], rest[n_in:n_in + n_out], rest[-1]
        kk = pl.program_id(2)

        @pl.when(kk == 0)
        def _():
            acc[...] = jnp.zeros_like(acc)

        acc[...] += lax.dot_general(a_ref[...].astype(BF16), b_ref[...].astype(BF16), dims,
                                    preferred_element_type=F32)

        @pl.when(kk == nk - 1)
        def _():
            res = acc[...]
            outs = epilogue(res, *[e[...] for e in in_refs]) if epilogue is not None else (res,)
            for o_ref, val in zip(out_refs, outs):
                o_ref[...] = val.astype(o_ref.dtype).reshape(o_ref.shape)

    in_specs = [a_spec, b_spec] + [mn_spec] * n_extra + [vec_spec] * n_vec
    operands = [a, b, *extras, *vecs]
    out_specs = [mn_spec] * n_out
    out_shape = [jax.ShapeDtypeStruct((m, n), dt) for dt in out_dtypes]
    aliases = {}
    if slab is not None:
        buffer, shape, index = slab
        out_specs = [pl.BlockSpec((1, tm, tn), lambda i, j, kk: index(i, j))]
        out_shape = [jax.ShapeDtypeStruct(shape, out_dtypes[0])]
        if buffer is not None:
            in_specs.append(ANY)
            operands.append(buffer)
            aliases = {len(operands) - 1: 0}
    outs = pl.pallas_call(
        body, name=name, grid=(m // tm, n // tn, nk),
        in_specs=in_specs, out_specs=out_specs, out_shape=out_shape, input_output_aliases=aliases,
        scratch_shapes=[pltpu.VMEM((tm, tn), F32)],
        compiler_params=_params("parallel", "parallel", "arbitrary"),
    )(*operands)
    return tuple(outs)


def _rowwise(fn, rows, vecs, out_rows, out_accs, name, tm=256):
    t = rows[0].shape[0]
    tm = _tile(t, tm)
    n_rows, n_vecs, n_or, n_oa = len(rows), len(vecs), len(out_rows), len(out_accs)

    def body(*refs):
        ins = [r[...] for r in refs[:n_rows + n_vecs]]
        outs = fn(*ins)
        o_refs = refs[n_rows + n_vecs:]
        for o_ref, val in zip(o_refs[:n_or], outs[:n_or]):
            o_ref[...] = val.astype(o_ref.dtype)

        @pl.when(pl.program_id(0) == 0)
        def _():
            for o_ref in o_refs[n_or:]:
                o_ref[...] = jnp.zeros_like(o_ref)

        for o_ref, val in zip(o_refs[n_or:], outs[n_or:]):
            o_ref[...] += val

    outs = pl.pallas_call(
        body, name=name, grid=(t // tm,),
        in_specs=[pl.BlockSpec((tm, r.shape[1]), lambda i: (i, 0)) for r in rows]
        + [pl.BlockSpec(v.shape, lambda i: (0, 0)) for v in vecs],
        out_specs=[pl.BlockSpec((tm, o.shape[1]), lambda i: (i, 0)) for o in out_rows]
        + [pl.BlockSpec(o.shape, lambda i: (0, 0)) for o in out_accs],
        out_shape=list(out_rows) + list(out_accs),
        compiler_params=_params("arbitrary" if n_oa else "parallel"),
    )(*rows, *vecs)
    return tuple(outs)


def _stack_rows(parts, width):
    rows = lax.broadcasted_iota(jnp.int32, (8, width), 0)
    out = jnp.zeros((8, width), F32)
    for j, part in enumerate(parts):
        out = out + jnp.where(rows == j, part, 0.0)
    return out


def _sds(shape, dtype=F32):
    return jax.ShapeDtypeStruct(shape, dtype)


def _sigmoid(x):
    return 1.0 / (1.0 + jnp.exp(-x))


def _colsum(x):
    return jnp.sum(x, axis=0, keepdims=True)


def _rstd(x):
    return lax.rsqrt(jnp.mean(x * x, axis=-1, keepdims=True) + EPS)


def _rms_bwd_rows(xhat, r, g, dy):
    dxh = dy * g
    dx = r * (dxh - xhat * jnp.mean(dxh * xhat, axis=-1, keepdims=True))
    return dx, _colsum(dy * xhat)


def _rms_fwd(x, g, name):
    t, d = x.shape

    def fn(xb, gv):
        return (xb * _rstd(xb) * gv,)

    return _rowwise(fn, [x], [g], [_sds((t, d), BF16)], [], name)[0]


def _rms_bwd(x, g, dh, dres, name):
    t, d = x.shape

    def fn(xb, dhb, dresb, gv):
        r = _rstd(xb)
        dx, dg = _rms_bwd_rows(xb * r, r, gv, dhb)
        dx = dx + dresb
        return dx, dx, dg

    return _rowwise(fn, [x, dh, dres], [g], [_sds((t, d)), _sds((t, d), BF16)], [_sds((1, d))], name)


def _mixnorm_fwd(y, z, att, gs, ga, name):
    t, d = y.shape

    def fn(yb, zb, ab, gsv, gav):
        yg = yb * (zb * _sigmoid(zb))
        return (jnp.concatenate([yg * _rstd(yg) * gsv, ab * _rstd(ab) * gav], axis=1),)

    tm = _tile(t, 256)

    def body(y_ref, z_ref, a_ref, gs_ref, ga_ref, o_ref):
        o_ref[...] = fn(y_ref[...], z_ref[...], a_ref[...], gs_ref[...], ga_ref[...])[0].astype(BF16)

    row = pl.BlockSpec((tm, d), lambda i: (i, 0))
    vec = pl.BlockSpec((1, d), lambda i: (0, 0))
    out = pl.pallas_call(
        body, name=name, grid=(t // tm,),
        in_specs=[row, row, row, vec, vec],
        out_specs=pl.BlockSpec((tm, 2 * d), lambda i: (i, 0)),
        out_shape=_sds((t, 2 * d), BF16),
        compiler_params=_params("parallel"),
    )(y, z, att, gs, ga)
    return out


def _mixnorm_bwd(y, z, att, gs, ga, dycat, name):
    t, d = y.shape
    tm = _tile(t, 256)

    def body(y_ref, z_ref, a_ref, dyc_ref, gs_ref, ga_ref, dy_ref, dz_ref, da_ref, dgs_ref, dga_ref):
        yb, zb, ab = y_ref[...], z_ref[...], a_ref[...]
        dys, dya = dyc_ref[:, :d], dyc_ref[:, d:]
        sz = _sigmoid(zb)
        silu = zb * sz
        yg = yb * silu
        r = _rstd(yg)
        dyg, dgs = _rms_bwd_rows(yg * r, r, gs_ref[...], dys)
        dy_ref[...] = dyg * silu
        dz_ref[...] = (dyg * yb * (sz * (1.0 + zb * (1.0 - sz)))).astype(BF16)
        r2 = _rstd(ab)
        datt, dga = _rms_bwd_rows(ab * r2, r2, ga_ref[...], dya)
        da_ref[...] = datt

        @pl.when(pl.program_id(0) == 0)
        def _():
            dgs_ref[...] = jnp.zeros_like(dgs_ref)
            dga_ref[...] = jnp.zeros_like(dga_ref)

        dgs_ref[...] += dgs
        dga_ref[...] += dga

    row = pl.BlockSpec((tm, d), lambda i: (i, 0))
    vec = pl.BlockSpec((1, d), lambda i: (0, 0))
    return pl.pallas_call(
        body, name=name, grid=(t // tm,),
        in_specs=[row, row, row, pl.BlockSpec((tm, 2 * d), lambda i: (i, 0)), vec, vec],
        out_specs=[row, row, row, vec, vec],
        out_shape=[_sds((t, d)), _sds((t, d), BF16), _sds((t, d)), _sds((1, d)), _sds((1, d))],
        compiler_params=_params("arbitrary"),
    )(y, z, att, dycat, gs, ga)


def _final_loss(x, g, target, name):
    t, d = x.shape

    def fn(xb, tb, gv):
        r = _rstd(xb)
        xhat = xb * r
        err = xhat * gv - tb
        loss = 0.5 * jnp.sum(jnp.mean(err * err, axis=-1, keepdims=True), axis=0, keepdims=True)
        dx, dg = _rms_bwd_rows(xhat, r, gv, err * (1.0 / d))
        return dx, dx, jnp.broadcast_to(loss, (1, LANES)), dg

    dx, dxb, loss, dg = _rowwise(fn, [x, target], [g], [_sds((t, d)), _sds((t, d), BF16)],
                                 [_sds((1, LANES)), _sds((1, d))], name)
    return loss, dx, dxb, dg


def _adamw(w, g, m, v, name):
    c1 = 1.0 - ADAM_B1 ** ADAM_STEP
    c2 = 1.0 - ADAM_B2 ** ADAM_STEP

    def fn(wb, gb, mb, vb):
        mn = ADAM_B1 * mb + (1.0 - ADAM_B1) * gb
        vn = ADAM_B2 * vb + (1.0 - ADAM_B2) * (gb * gb)
        delta = -ADAM_LR * ((mn / c1) / (jnp.sqrt(vn / c2) + ADAM_EPS) + ADAM_WD * wb)
        return delta, mn, vn

    return _rowwise(fn, [w, g, m, v], [], [_sds(w.shape)] * 3, [], name)


CONV_CB = 512


def _shift_down(u, k):
    if k == 0:
        return u
    rows = lax.broadcasted_iota(jnp.int32, u.shape, 0)
    return jnp.where(rows >= k, pltpu.roll(u, k, 0), 0.0)


def _shift_up(u, k):
    if k == 0:
        return u
    n = u.shape[0]
    rows = lax.broadcasted_iota(jnp.int32, u.shape, 0)
    return jnp.where(rows < n - k, pltpu.roll(u, n - k, 0), 0.0)


def _conv_pre(u, w, b):
    pre = b
    for j in range(CONV_WIDTH):
        pre = pre + w[j:j + 1, :] * _shift_down(u, CONV_WIDTH - 1 - j)
    return pre


def _conv_fwd(proj, w, b, n_seq, name):
    t = proj.shape[0]
    seq = t // n_seq
    off = D_MODEL // CONV_CB

    def body(u_ref, w_ref, b_ref, o_ref):
        pre = _conv_pre(u_ref[...], w_ref[...], b_ref[...])
        o_ref[...] = pre * _sigmoid(pre)

    return pl.pallas_call(
        body, name=name, grid=(n_seq, D_CONV // CONV_CB),
        in_specs=[pl.BlockSpec((seq, CONV_CB), lambda s, c: (s, c + off)),
                  pl.BlockSpec((CONV_WIDTH, CONV_CB), lambda s, c: (0, c)),
                  pl.BlockSpec((1, CONV_CB), lambda s, c: (0, c))],
        out_specs=pl.BlockSpec((seq, CONV_CB), lambda s, c: (s, c)),
        out_shape=_sds((t, D_CONV)),
        compiler_params=_params("parallel", "parallel"),
    )(proj, w, b)


def _conv_bwd(proj, w, b, dxbc, n_seq, name):
    t = proj.shape[0]
    seq = t // n_seq
    off = D_MODEL // CONV_CB

    def body(u_ref, w_ref, b_ref, d_ref, du_ref, wg_ref):
        u, wv = u_ref[...], w_ref[...]
        taps = [_shift_down(u, CONV_WIDTH - 1 - j) for j in range(CONV_WIDTH)]
        pre = b_ref[...]
        for j in range(CONV_WIDTH):
            pre = pre + wv[j:j + 1, :] * taps[j]
        s = _sigmoid(pre)
        dpre = d_ref[...] * (s * (1.0 + pre * (1.0 - s)))
        du = jnp.zeros_like(u)
        parts = []
        for j in range(CONV_WIDTH):
            du = du + wv[j:j + 1, :] * _shift_up(dpre, CONV_WIDTH - 1 - j)
            parts.append(_colsum(dpre * taps[j]))
        du_ref[...] = du.astype(BF16)
        parts.append(_colsum(dpre))

        @pl.when(pl.program_id(1) == 0)
        def _():
            wg_ref[...] = jnp.zeros_like(wg_ref)

        wg_ref[...] += _stack_rows(parts, u.shape[1])

    return pl.pallas_call(
        body, name=name, grid=(D_CONV // CONV_CB, n_seq),
        in_specs=[pl.BlockSpec((seq, CONV_CB), lambda c, s: (s, c + off)),
                  pl.BlockSpec((CONV_WIDTH, CONV_CB), lambda c, s: (0, c)),
                  pl.BlockSpec((1, CONV_CB), lambda c, s: (0, c)),
                  pl.BlockSpec((seq, CONV_CB), lambda c, s: (s, c))],
        out_specs=[pl.BlockSpec((seq, CONV_CB), lambda c, s: (s, c)),
                   pl.BlockSpec((8, CONV_CB), lambda c, s: (0, c))],
        out_shape=[_sds((t, D_CONV), BF16), _sds((8, D_CONV))],
        compiler_params=_params("parallel", "arbitrary"),
    )(proj, w, b, dxbc)


def _iota2(shape, axis):
    return lax.broadcasted_iota(jnp.int32, shape, axis)


def _dot(a, b, dims=_DIMS["nn"], precision=None):
    return lax.dot_general(a, b, dims, precision=precision, preferred_element_type=F32)


def _bdot(a, b, dims=_DIMS["nn"]):
    return lax.dot_general(a.astype(BF16), b.astype(BF16), dims, preferred_element_type=F32)


def _expand_mat():
    return (_iota2((LANES, D_MODEL), 0) == lax.shift_right_logical(_iota2((LANES, D_MODEL), 1), HEAD_SHIFT)).astype(F32)


def _reduce_mat():
    return (lax.shift_right_logical(_iota2((D_MODEL, LANES), 0), HEAD_SHIFT) == _iota2((D_MODEL, LANES), 1)).astype(F32)


def _ssd_decay(dtraw, bias, alog):
    row, col = _iota2((CHUNK, CHUNK), 0), _iota2((CHUNK, CHUNK), 1)
    pre = dtraw + bias
    dtb = jnp.maximum(pre, 0.0) + jnp.log(1.0 + jnp.exp(-jnp.abs(pre)))
    a_neg = -jnp.exp(alog)
    a = dtb * a_neg
    tril = (row >= col).astype(F32)
    triu = (row <= col).astype(F32)
    cs = _dot(tril, a, precision=HIGHEST)
    cs_t = _dot(a, triu, _DIMS["tn"], precision=HIGHEST)
    return pre, dtb, a_neg, cs, cs_t


def _pair_rowscale(vec, h0):
    top = _iota2((CHUNK, LANES), 0) < HEAD_DIM
    return jnp.where(top, vec[:, h0:h0 + 1], vec[:, h0 + 1:h0 + 2])


def _decay_mat(cs, cs_t, h):
    row, col = _iota2((CHUNK, CHUNK), 0), _iota2((CHUNK, CHUNK), 1)
    seg = cs[:, h:h + 1] - cs_t[h:h + 1, :]
    return jnp.exp(jnp.where(row >= col, seg, -jnp.inf))


def _ssd_fwd(xbc, dtraw, bias, alog, dskip_x, n_seq, name):
    t = xbc.shape[0]
    nc = t // n_seq // CHUNK

    def body(x_ref, b_ref, c_ref, dt_ref, bias_ref, alog_ref, dsk_ref, y_ref, st_ref, h_ref):
        @pl.when(pl.program_id(1) == 0)
        def _():
            h_ref[...] = jnp.zeros_like(h_ref)

        _, dtb, _, cs, cs_t = _ssd_decay(dt_ref[...], bias_ref[...], alog_ref[...])
        expand = _expand_mat()
        dt_x = _dot(dtb, expand, precision=HIGHEST)
        cs_x = _dot(cs, expand, precision=HIGHEST)
        tot = cs[CHUNK - 1:CHUNK, :]
        etot = jnp.exp(tot)
        x = x_ref[...]
        xdt = x * dt_x
        e_x = jnp.exp(cs_x)
        xdec = xdt * jnp.exp(cs_x[CHUNK - 1:CHUNK, :] - cs_x)
        keeps = [_iota2((CHUNK, LANES), 1) < HEAD_DIM, _iota2((CHUNK, LANES), 1) >= HEAD_DIM]
        for g in range(N_GROUPS):
            glanes = slice(2 * g * LANES, (2 * g + 2) * LANES)
            bg = b_ref[:, g * N_STATE:(g + 1) * N_STATE]
            cg = c_ref[:, g * N_STATE:(g + 1) * N_STATE]
            gmat = _bdot(cg, bg, _DIMS["nt"])
            prev = h_ref[2 * g:2 * g + 2].reshape(2 * LANES, N_STATE)
            st_ref[0, 2 * g:2 * g + 2] = prev.reshape(2, LANES, N_STATE)
            yoff = _bdot(cg, prev, _DIMS["nt"]) * e_x[:, glanes]
            new = _bdot(xdec[:, glanes], bg, _DIMS["tn"])
            scale = jnp.concatenate([_pair_rowscale(etot, 2 * q) for q in (2 * g, 2 * g + 1)], axis=0)
            h_ref[2 * g:2 * g + 2] = (prev * scale + new).reshape(2, LANES, N_STATE)
            for q in (2 * g, 2 * g + 1):
                lanes = slice(q * LANES, (q + 1) * LANES)
                xdt_q = xdt[:, lanes]
                w_cat = jnp.concatenate([gmat * _decay_mat(cs, cs_t, 2 * q + r) for r in range(2)], axis=1)
                x_cat = jnp.concatenate([jnp.where(keeps[r], xdt_q, 0.0) for r in range(2)], axis=0)
                y_ref[:, lanes] = (_bdot(w_cat, x_cat) + yoff[:, lanes.start - glanes.start:lanes.stop - glanes.start]
                                   + x[:, lanes] * dsk_ref[:, lanes])

    vec = pl.BlockSpec((1, LANES), lambda s, c: (0, 0))
    return pl.pallas_call(
        body, name=name, grid=(n_seq, nc),
        in_specs=[pl.BlockSpec((CHUNK, D_MODEL), lambda s, c: (s * nc + c, 0)),
                  pl.BlockSpec((CHUNK, N_GROUPS * N_STATE), lambda s, c: (s * nc + c, 2)),
                  pl.BlockSpec((CHUNK, N_GROUPS * N_STATE), lambda s, c: (s * nc + c, 3)),
                  pl.BlockSpec((CHUNK, LANES), lambda s, c: (s * nc + c, 0)),
                  vec, vec, pl.BlockSpec((1, D_MODEL), lambda s, c: (0, 0))],
        out_specs=[pl.BlockSpec((CHUNK, D_MODEL), lambda s, c: (s * nc + c, 0)),
                   pl.BlockSpec((1, N_PAIRS, LANES, N_STATE), lambda s, c: (s * nc + c, 0, 0, 0))],
        out_shape=[_sds((t, D_MODEL)), _sds((t // CHUNK, N_PAIRS, LANES, N_STATE))],
        scratch_shapes=[pltpu.VMEM((N_PAIRS, LANES, N_STATE), F32)],
        compiler_params=_params("parallel", "arbitrary"),
    )(xbc, xbc, xbc, dtraw, bias, alog, dskip_x)


def _ssd_bwd(xbc, dtraw, bias, alog, dskip_x, states, dy, n_seq, name):
    t = xbc.shape[0]
    nc = t // n_seq // CHUNK

    def body(x_ref, b_ref, c_ref, dt_ref, bias_ref, alog_ref, dsk_ref, st_ref, dy_ref,
             dxbc_ref, ddt_ref, pg_ref, dh_ref):
        first = jnp.logical_and(pl.program_id(0) == 0, pl.program_id(1) == 0)

        @pl.when(first)
        def _():
            pg_ref[...] = jnp.zeros_like(pg_ref)

        @pl.when(pl.program_id(1) == 0)
        def _():
            dh_ref[...] = jnp.zeros_like(dh_ref)

        row, col = _iota2((CHUNK, CHUNK), 0), _iota2((CHUNK, CHUNK), 1)
        pre, dtb, a_neg, cs, cs_t = _ssd_decay(dt_ref[...], bias_ref[...], alog_ref[...])
        expand, reduce = _expand_mat(), _reduce_mat()
        dt_x = _dot(dtb, expand, precision=HIGHEST)
        cs_x = _dot(cs, expand, precision=HIGHEST)
        etot = jnp.exp(cs[CHUNK - 1:CHUNK, :])
        x, dy = x_ref[...], dy_ref[...]
        xdt = x * dt_x
        e_x = jnp.exp(cs_x)
        dec_x = jnp.exp(cs_x[CHUNK - 1:CHUNK, :] - cs_x)
        xdec = xdt * dec_x
        dye = dy * e_x
        keeps = [_iota2((CHUNK, LANES), 1) < HEAD_DIM, _iota2((CHUNK, LANES), 1) >= HEAD_DIM]
        dcs_col = jnp.zeros((CHUNK, LANES), F32)
        dcs_row = jnp.zeros((LANES, CHUNK), F32)
        dtot = jnp.zeros((1, LANES), F32)
        dxdt_parts, zdec_parts, yoff_parts = [], [], []
        for g in range(N_GROUPS):
            bg = b_ref[:, g * N_STATE:(g + 1) * N_STATE]
            cg = c_ref[:, g * N_STATE:(g + 1) * N_STATE]
            gmat = _bdot(cg, bg, _DIMS["nt"])
            dgmat = jnp.zeros((CHUNK, CHUNK), F32)
            glanes = slice(2 * g * LANES, (2 * g + 2) * LANES)
            prev = st_ref[0, 2 * g:2 * g + 2].reshape(2 * LANES, N_STATE)
            dht = dh_ref[2 * g:2 * g + 2].reshape(2 * LANES, N_STATE)
            dxdtdec = _bdot(bg, dht, _DIMS["nt"])
            zdec_parts.append(dxdtdec * xdec[:, glanes])
            dbg = _bdot(xdec[:, glanes], dht)
            yoff_parts.append(dy[:, glanes] * (_bdot(cg, prev, _DIMS["nt"]) * e_x[:, glanes]))
            dcg = _bdot(dye[:, glanes], prev)
            dprev = _bdot(dye[:, glanes], cg, _DIMS["tn"])
            hp = jnp.sum(dht * prev, axis=1, keepdims=True)
            rows4 = lax.shift_right_logical(_iota2((2 * LANES, 1), 0), HEAD_SHIFT)
            lane1 = _iota2((1, LANES), 1)
            for k in range(4):
                dk_tot = jnp.sum(jnp.where(rows4 == k, hp, 0.0), axis=0, keepdims=True)
                dtot = dtot + jnp.where(lane1 == 4 * g + k, dk_tot, 0.0)
            scale = jnp.concatenate([_pair_rowscale(etot, 2 * q) for q in (2 * g, 2 * g + 1)], axis=0)
            dh_ref[2 * g:2 * g + 2] = (dprev + dht * scale).reshape(2, LANES, N_STATE)
            for q in (2 * g, 2 * g + 1):
                lanes = slice(q * LANES, (q + 1) * LANES)
                local = slice(lanes.start - glanes.start, lanes.stop - glanes.start)
                xdt_q, dy_q = xdt[:, lanes], dy[:, lanes]
                lms = [_decay_mat(cs, cs_t, 2 * q + r) for r in range(2)]
                ws = [gmat * lm for lm in lms]
                dy_cat = jnp.concatenate([jnp.where(keeps[r], dy_q, 0.0) for r in range(2)], axis=0)
                dm_cat = _bdot(dy_cat, xdt_q, _DIMS["nt"])
                dxdt_q = _bdot(jnp.concatenate(ws, axis=0), dy_cat, _DIMS["tn"])
                for r in range(2):
                    h = 2 * q + r
                    dm = dm_cat[r * CHUNK:(r + 1) * CHUNK]
                    dgmat = dgmat + dm * lms[r]
                    tmat = dm * ws[r]
                    dcs_col = dcs_col + jnp.where(col == h, jnp.sum(tmat, axis=1, keepdims=True), 0.0)
                    dcs_row = dcs_row - jnp.where(row == h, jnp.sum(tmat, axis=0, keepdims=True), 0.0)
                dxdt_parts.append(dxdt_q + dxdtdec[:, local] * dec_x[:, lanes])
            dcg = dcg + _bdot(dgmat, bg)
            dbg = dbg + _bdot(dgmat, cg, _DIMS["tn"])
            dxbc_ref[:, D_MODEL + g * N_STATE:D_MODEL + (g + 1) * N_STATE] = dbg
            dxbc_ref[:, D_MODEL + (N_GROUPS + g) * N_STATE:D_MODEL + (N_GROUPS + g + 1) * N_STATE] = dcg
        dxdt = jnp.concatenate(dxdt_parts, axis=1)
        zdec = _dot(jnp.concatenate(zdec_parts, axis=1), reduce, precision=HIGHEST)
        yoff_d = _dot(jnp.concatenate(yoff_parts, axis=1), reduce, precision=HIGHEST)
        dtot = dtot * etot + _colsum(zdec)
        last = row[:, :LANES] == CHUNK - 1
        dcs_col = dcs_col + yoff_d - zdec + jnp.where(last, dtot, 0.0)
        triu = (row <= col).astype(F32)
        da = _dot(triu, dcs_col, precision=HIGHEST) + _dot(triu, dcs_row, _DIMS["nt"], precision=HIGHEST)
        ddt = _dot(dxdt * x, reduce, precision=HIGHEST) + da * a_neg
        ddtraw = ddt * _sigmoid(pre)
        ddt_ref[...] = ddtraw.astype(BF16)
        dxbc_ref[:, :D_MODEL] = dxdt * dt_x + dy * dsk_ref[...]
        dskip = _dot(jnp.broadcast_to(_colsum(dy * x), (8, D_MODEL)), reduce, precision=HIGHEST)[0:1, :]
        pg_ref[...] += _stack_rows([_colsum(ddtraw), _colsum(da * dtb) * a_neg, dskip], LANES)

    vec = pl.BlockSpec((1, LANES), lambda s, c: (0, 0))

    def blk(s, c):
        return s * nc + (nc - 1 - c)

    return pl.pallas_call(
        body, name=name, grid=(n_seq, nc),
        in_specs=[pl.BlockSpec((CHUNK, D_MODEL), lambda s, c: (blk(s, c), 0)),
                  pl.BlockSpec((CHUNK, N_GROUPS * N_STATE), lambda s, c: (blk(s, c), 2)),
                  pl.BlockSpec((CHUNK, N_GROUPS * N_STATE), lambda s, c: (blk(s, c), 3)),
                  pl.BlockSpec((CHUNK, LANES), lambda s, c: (blk(s, c), 0)),
                  vec, vec, pl.BlockSpec((1, D_MODEL), lambda s, c: (0, 0)),
                  pl.BlockSpec((1, N_PAIRS, LANES, N_STATE), lambda s, c: (blk(s, c), 0, 0, 0)),
                  pl.BlockSpec((CHUNK, D_MODEL), lambda s, c: (blk(s, c), 0))],
        out_specs=[pl.BlockSpec((CHUNK, D_CONV), lambda s, c: (blk(s, c), 0)),
                   pl.BlockSpec((CHUNK, LANES), lambda s, c: (blk(s, c), 0)),
                   pl.BlockSpec((8, LANES), lambda s, c: (0, 0))],
        out_shape=[_sds((t, D_CONV)), _sds((t, LANES), BF16), _sds((8, LANES))],
        scratch_shapes=[pltpu.VMEM((N_PAIRS, LANES, N_STATE), F32)],
        compiler_params=_params("arbitrary", "arbitrary"),
    )(xbc, xbc, xbc, dtraw, bias, alog, dskip_x, states, dy)


Q_COL = (D_MODEL + D_CONV) // LANES
K_COL = Q_COL + D_MODEL // LANES
V_COL = K_COL + D_MODEL // LANES
ATT_SCALE = HEAD_DIM ** -0.5


ATT_SUB = 2
ATT_TILE = ATT_SUB * CHUNK
ATT_NP = 2


ATT_ROWS = 2 * ATT_TILE


def _tri_dot(x, tri):
    return _dot(x.astype(BF16), tri)


def _att_stack(ref, lanes, keeps, scale=1.0):
    parts = []
    for a in range(ATT_SUB):
        blk = ref[a * CHUNK:(a + 1) * CHUNK, lanes] * scale
        parts += [jnp.where(keeps[r], blk, 0.0).astype(BF16) for r in range(2)]
    return jnp.concatenate(parts, axis=0)


def _att_unstack(x, a, keeps):
    return jnp.where(keeps[0], x[2 * a * CHUNK:(2 * a + 1) * CHUNK], x[(2 * a + 1) * CHUNK:(2 * a + 2) * CHUNK])


def _att_logits(s, diag):
    lb = jnp.minimum(s, 0.0) - jnp.log(1.0 + jnp.exp(-jnp.abs(s)))
    l1m = lb - s
    mask = None
    if diag:
        srow, scol = _iota2((ATT_ROWS, ATT_TILE), 0), _iota2((ATT_ROWS, ATT_TILE), 1)
        pair_shift = (2 * CHUNK).bit_length() - 1
        qpos = lax.shift_right_logical(srow, pair_shift) * CHUNK + jnp.bitwise_and(srow, CHUNK - 1)
        mask = qpos > scol
        l1m = jnp.where(mask, l1m, 0.0)
    return lb, l1m, mask


def _att_fwd(proj, n_seq, name):
    t = proj.shape[0]
    seq = t // n_seq
    nq = seq // ATT_TILE

    def body(q_ref, k_ref, v_ref, o_ref, rt_ref):
        i = pl.program_id(2)
        upper = (_iota2((ATT_TILE, ATT_TILE), 0) > _iota2((ATT_TILE, ATT_TILE), 1)).astype(BF16)
        keeps = [_iota2((CHUNK, LANES), 1) < HEAD_DIM, _iota2((CHUNK, LANES), 1) >= HEAD_DIM]
        pair_lanes = [slice(pr * LANES, (pr + 1) * LANES) for pr in range(ATT_NP)]
        q_stacks = [_att_stack(q_ref, lanes, keeps, ATT_SCALE) for lanes in pair_lanes]
        per_pair = ATT_SUB + 2

        def logits(jt):
            k0 = pl.multiple_of(jt * ATT_TILE, ATT_TILE)
            return [_dot(q_stacks[pr], k_ref[pl.ds(k0, ATT_TILE), lanes].astype(BF16), _DIMS["nt"])
                    for pr, lanes in enumerate(pair_lanes)]

        def tile(jt, carry, diag):
            state = list(carry)
            ahead = logits(jnp.maximum(jt - 1, 0))
            k0 = pl.multiple_of(jt * ATT_TILE, ATT_TILE)
            for pr, lanes in enumerate(pair_lanes):
                base = pr * per_pair
                run = state[base + ATT_SUB]
                v_tile = v_ref[pl.ds(k0, ATT_TILE), lanes].astype(BF16)
                lb, l1m, mask = _att_logits(state[base + ATT_SUB + 1], diag)
                p = jnp.exp(lb + (_tri_dot(l1m, upper) + run))
                if diag:
                    p = jnp.where(mask, p, 0.0)
                pv = _dot(p.astype(BF16), v_tile)
                for a in range(ATT_SUB):
                    state[base + a] = state[base + a] + _att_unstack(pv, a, keeps)
                state[base + ATT_SUB] = run + jnp.sum(l1m, axis=1, keepdims=True)
                state[base + ATT_SUB + 1] = ahead[pr]
            return tuple(state)

        first = logits(i)
        init = ()
        for pr in range(ATT_NP):
            init += tuple(jnp.zeros((CHUNK, LANES), F32) for _ in range(ATT_SUB)) + \
                (jnp.zeros((ATT_ROWS, 1), F32), first[pr])
        carry = tile(i, init, True)
        carry = lax.fori_loop(0, i, lambda it, c: tile(i - 1 - it, c, False), carry)
        for pr, lanes in enumerate(pair_lanes):
            for a in range(ATT_SUB):
                o_ref[a * CHUNK:(a + 1) * CHUNK, lanes] = carry[pr * per_pair + a]
                rt_ref[a * CHUNK:(a + 1) * CHUNK, lanes] = _att_unstack(carry[pr * per_pair + ATT_SUB], a, keeps)

    width = ATT_NP * LANES
    qblk = pl.BlockSpec((ATT_TILE, width), lambda s, p, i: (s * nq + i, p))
    return pl.pallas_call(
        body, name=name, grid=(n_seq, N_PAIRS // ATT_NP, nq),
        in_specs=[pl.BlockSpec((ATT_TILE, width), lambda s, p, i: (s * nq + i, Q_COL // ATT_NP + p)),
                  pl.BlockSpec((seq, width), lambda s, p, i: (s, K_COL // ATT_NP + p)),
                  pl.BlockSpec((seq, width), lambda s, p, i: (s, V_COL // ATT_NP + p))],
        out_specs=[qblk, qblk],
        out_shape=[_sds((t, D_MODEL)), _sds((t, D_MODEL))],
        compiler_params=_params("parallel", "parallel", "arbitrary"),
    )(proj, proj, proj)


def _att_bwd(proj, rtot, datt, n_seq, name):
    t = proj.shape[0]
    seq = t // n_seq
    nq = seq // ATT_TILE

    def body(q_ref, k_ref, v_ref, rt_ref, do_ref, dq_ref, dk_out, dv_out, dk_ref, dv_ref):
        i = pl.program_id(2)

        @pl.when(i == 0)
        def _():
            dk_ref[...] = jnp.zeros_like(dk_ref)
            dv_ref[...] = jnp.zeros_like(dv_ref)

        row, col = _iota2((ATT_TILE, ATT_TILE), 0), _iota2((ATT_TILE, ATT_TILE), 1)
        upper = (row > col).astype(BF16)
        before = (row < col).astype(BF16)
        keeps = [_iota2((CHUNK, LANES), 1) < HEAD_DIM, _iota2((CHUNK, LANES), 1) >= HEAD_DIM]
        pair_lanes = [slice(pr * LANES, (pr + 1) * LANES) for pr in range(ATT_NP)]
        q_stacks = [_att_stack(q_ref, lanes, keeps, ATT_SCALE) for lanes in pair_lanes]
        do_stacks = [_att_stack(do_ref, lanes, keeps) for lanes in pair_lanes]
        totals = [jnp.concatenate([rt_ref[a * CHUNK:(a + 1) * CHUNK, pr * LANES + r * HEAD_DIM:pr * LANES + r * HEAD_DIM + 1]
                                   for a in range(ATT_SUB) for r in range(2)], axis=0) for pr in range(ATT_NP)]
        per_pair = ATT_SUB + 4

        def products(jt):
            k0 = pl.multiple_of(jt * ATT_TILE, ATT_TILE)
            return [(_dot(q_stacks[pr], k_ref[pl.ds(k0, ATT_TILE), lanes].astype(BF16), _DIMS["nt"]),
                     _dot(do_stacks[pr], v_ref[pl.ds(k0, ATT_TILE), lanes].astype(BF16), _DIMS["nt"]))
                    for pr, lanes in enumerate(pair_lanes)]

        def tile(jt, carry, diag):
            state = list(carry)
            ahead = None if diag else products(jt + 1)
            k0 = pl.multiple_of(jt * ATT_TILE, ATT_TILE)
            for pr, lanes in enumerate(pair_lanes):
                base = pr * per_pair
                seen, dseen = state[base + ATT_SUB], state[base + ATT_SUB + 1]
                q_stack, do_stack = q_stacks[pr], do_stacks[pr]
                k_tile = k_ref[pl.ds(k0, ATT_TILE), lanes].astype(BF16)
                lb, l1m, mask = _att_logits(state[base + ATT_SUB + 2], diag)
                here = jnp.sum(l1m, axis=1, keepdims=True)
                p = jnp.exp(lb + (_tri_dot(l1m, upper) + (totals[pr] - seen - here)))
                if diag:
                    p = jnp.where(mask, p, 0.0)
                pb = p.astype(BF16)
                dz = state[base + ATT_SUB + 3] * p
                dl1m = dseen + _tri_dot(dz, before)
                sig = jnp.exp(lb)
                ds = dz * (1.0 - sig) - dl1m * sig
                if diag:
                    ds = jnp.where(mask, ds, 0.0)
                dsb = ds.astype(BF16)
                dq_all = _dot(dsb, k_tile)
                for a in range(ATT_SUB):
                    state[base + a] = state[base + a] + _att_unstack(dq_all, a, keeps)
                dk_ref[pl.ds(k0, ATT_TILE), lanes] += _dot(dsb, q_stack, _DIMS["tn"])
                dv_ref[pl.ds(k0, ATT_TILE), lanes] += _dot(pb, do_stack, _DIMS["tn"])
                state[base + ATT_SUB] = seen + here
                state[base + ATT_SUB + 1] = dseen + jnp.sum(dz, axis=1, keepdims=True)
                if ahead is not None:
                    state[base + ATT_SUB + 2], state[base + ATT_SUB + 3] = ahead[pr]
            return tuple(state)

        first = products(0)
        init = ()
        for pr in range(ATT_NP):
            init += tuple(jnp.zeros((CHUNK, LANES), F32) for _ in range(ATT_SUB)) + \
                (jnp.zeros((ATT_ROWS, 1), F32), jnp.zeros((ATT_ROWS, 1), F32)) + first[pr]
        carry = lax.fori_loop(0, i, lambda jt, c: tile(jt, c, False), init)
        carry = tile(i, carry, True)
        for pr, lanes in enumerate(pair_lanes):
            for a in range(ATT_SUB):
                dq_ref[a * CHUNK:(a + 1) * CHUNK, lanes] = (carry[pr * per_pair + a] * ATT_SCALE).astype(BF16)

        @pl.when(i == nq - 1)
        def _():
            dk_out[...] = dk_ref[...].astype(BF16)
            dv_out[...] = dv_ref[...].astype(BF16)

    width = ATT_NP * LANES
    qblk = pl.BlockSpec((ATT_TILE, width), lambda s, p, i: (s * nq + i, p))
    kv_out = pl.BlockSpec((seq, width), lambda s, p, i: (s, p))
    return pl.pallas_call(
        body, name=name, grid=(n_seq, N_PAIRS // ATT_NP, nq),
        in_specs=[pl.BlockSpec((ATT_TILE, width), lambda s, p, i: (s * nq + i, Q_COL // ATT_NP + p)),
                  pl.BlockSpec((seq, width), lambda s, p, i: (s, K_COL // ATT_NP + p)),
                  pl.BlockSpec((seq, width), lambda s, p, i: (s, V_COL // ATT_NP + p)),
                  qblk, qblk],
        out_specs=[qblk, kv_out, kv_out],
        out_shape=[_sds((t, D_MODEL), BF16)] * 3,
        scratch_shapes=[pltpu.VMEM((seq, width), F32), pltpu.VMEM((seq, width), F32)],
        compiler_params=_params("parallel", "parallel", "arbitrary"),
    )(proj, proj, proj, rtot, datt)


def _pad_lanes(v):
    return jnp.pad(v.reshape(1, -1), ((0, 0), (0, LANES - v.shape[0])))


def _add_then_norm(acc, xb, g):
    x1 = xb + acc
    return x1, x1 * _rstd(x1) * g


def _layer_fwd(x, h, p, next_g, n_seq, tag):
    proj, = _matmul(h, p["w_main"], "nn", [F32], f"in_proj{tag}", tn=1536)
    dtraw, = _matmul(h, p["w_dt"], "nn", [F32], f"dt_proj{tag}")
    xbc = _conv_fwd(proj, p["conv_w"], p["conv_b"], n_seq, f"conv_fwd{tag}")
    y, states = _ssd_fwd(xbc, dtraw, p["dt_bias"], p["a_log"], p["d_skip_x"], n_seq, f"ssd_fwd{tag}")
    att, rtot = _att_fwd(proj, n_seq, f"att_fwd{tag}")
    ycat = _mixnorm_fwd(y, proj, att, p["ssd_norm_g"], p["att_norm_g"], f"mixnorm_fwd{tag}")
    x1, h2 = _matmul(ycat, p["w_out"], "nn", [F32, BF16], f"out_proj{tag}", extras=[x], vecs=[p["norm_mlp_g"]],
                     epilogue=_add_then_norm, tn=D_MODEL)
    u, act = _matmul(h2, p["w_up"], "nn", [F32, BF16], f"up_proj{tag}",
                     epilogue=lambda acc: (acc, jnp.square(jnp.maximum(acc, 0.0))))
    if next_g is None:
        x2, = _matmul(act, p["w_down"], "nn", [F32], f"down_proj{tag}", extras=[x1],
                      epilogue=lambda acc, xb: (xb + acc,))
        h_next = None
    else:
        x2, h_next = _matmul(act, p["w_down"], "nn", [F32, BF16], f"down_proj{tag}", extras=[x1], vecs=[next_g],
                             epilogue=_add_then_norm, tn=D_MODEL)
    saved = dict(x=x, h=h, proj=proj, dtraw=dtraw, xbc=xbc, y=y, states=states, att=att, rtot=rtot, ycat=ycat,
                 x1=x1, h2=h2, u=u, act=act)
    return x2, h_next, saved


def _slab(buffers, name, layer, rows, per_chip_rows):
    shape = (N_CHIPS, DEPTH * rows, D_MODEL)
    if per_chip_rows:
        return buffers.get(name), shape, lambda i, j: (i, layer, 0)
    return buffers.get(name), shape, lambda i, j: (j, layer, 0)


def _layer_bwd(dx2, dx2b, p, s, buffers, layer, n_seq, tag):
    g = {}
    buffers["w_down"], = _matmul(s["act"], dx2b, "tn", [F32], f"dw_down{tag}",
                                 slab=_slab(buffers, "w_down", layer, D_FF // N_CHIPS, True), tm=D_FF // N_CHIPS)
    du, = _matmul(dx2b, p["w_down"], "nt", [BF16], f"d_act{tag}", extras=[s["u"]],
                  epilogue=lambda acc, ub: (acc * (2.0 * jnp.maximum(ub, 0.0)),))
    buffers["w_up"], = _matmul(s["h2"], du, "tn", [F32], f"dw_up{tag}",
                               slab=_slab(buffers, "w_up", layer, D_MODEL, False), tn=D_FF // N_CHIPS)
    dh2, = _matmul(du, p["w_up"], "nt", [F32], f"d_h2{tag}")
    dx1, dx1b, g["norm_mlp_g"] = _rms_bwd(s["x1"], p["norm_mlp_g"], dh2, dx2, f"rms_mlp_bwd{tag}")
    buffers["w_out"], = _matmul(s["ycat"], dx1b, "tn", [F32], f"dw_out{tag}",
                                slab=_slab(buffers, "w_out", layer, 2 * D_MODEL // N_CHIPS, True),
                                tm=2 * D_MODEL // N_CHIPS)
    dycat, = _matmul(dx1b, p["w_out"], "nt", [F32], f"d_ycat{tag}")
    dy, dz, datt, g["ssd_norm_g"], g["att_norm_g"] = _mixnorm_bwd(
        s["y"], s["proj"], s["att"], p["ssd_norm_g"], p["att_norm_g"], dycat, f"mixnorm_bwd{tag}")
    dq, dk, dv = _att_bwd(s["proj"], s["rtot"], datt, n_seq, f"att_bwd{tag}")
    dxbc, ddtraw, pg = _ssd_bwd(s["xbc"], s["dtraw"], p["dt_bias"], p["a_log"], p["d_skip_x"], s["states"], dy,
                                n_seq, f"ssd_bwd{tag}")
    g["dt_bias"], g["a_log"], g["d_skip"] = pg[0, :N_HEADS], pg[1, :N_HEADS], pg[2, :N_HEADS]
    du_conv, wg = _conv_bwd(s["proj"], p["conv_w"], p["conv_b"], dxbc, n_seq, f"conv_bwd{tag}")
    g["conv_w"], g["conv_b"] = wg[:CONV_WIDTH], wg[CONV_WIDTH]
    dproj = jnp.concatenate([dz, du_conv, dq, dk, dv], axis=1)
    g["w_main"], = _matmul(s["h"], dproj, "tn", [F32], f"dw_in{tag}", tn=1536)
    g["w_dt"], = _matmul(s["h"], ddtraw, "tn", [F32], f"dw_dt{tag}")
    dh_dt, = _matmul(ddtraw, p["w_dt"], "nt", [F32], f"d_h_dt{tag}")
    dh, = _matmul(dproj, p["w_main"], "nt", [F32], f"d_h{tag}", extras=[dh_dt], epilogue=lambda acc, e: (acc + e,))
    dx, dxb, g["norm_mix_g"] = _rms_bwd(s["x"], p["norm_mix_g"], dh, dx1, f"rms_mix_bwd{tag}")
    return dx, dxb, g


def _split_w_in(w_full):
    c0 = D_MODEL + D_CONV
    main = jnp.concatenate([w_full[:, :c0], w_full[:, c0 + N_HEADS:]], axis=1)
    dt = jnp.pad(w_full[:, c0:c0 + N_HEADS], ((0, 0), (0, LANES - N_HEADS)))
    return main, dt


def _merge_w_in(main, dt):
    c0 = D_MODEL + D_CONV
    return jnp.concatenate([main[:, :c0], dt[:, :N_HEADS], main[:, c0:]], axis=1)


def _prep_layer(l, w_in_full, w_out, w_up, w_down, conv_w, small):
    w_main, w_dt = _split_w_in(w_in_full)
    return dict(
        w_main=w_main, w_dt=w_dt, w_out=w_out, w_up=w_up, w_down=w_down, conv_w=conv_w,
        conv_b=small["conv_b"][l].reshape(1, -1),
        dt_bias=_pad_lanes(small["dt_bias"][l]), a_log=_pad_lanes(small["a_log"][l]),
        d_skip_x=jnp.repeat(small["d_skip"][l], HEAD_DIM).reshape(1, -1),
        norm_mix_g=small["norm_mix_g"][l].reshape(1, -1), ssd_norm_g=small["ssd_norm_g"][l].reshape(1, -1),
        att_norm_g=small["att_norm_g"][l].reshape(1, -1), norm_mlp_g=small["norm_mlp_g"][l].reshape(1, -1),
    )


ANY = pl.BlockSpec(memory_space=pl.ANY)


def _place():
    x, y, c = lax.axis_index("x"), lax.axis_index("y"), lax.axis_index("c")
    return x, y, c, (x, y, 1 - c), [(1 - x, y), (x, 1 - y), (1 - x, 1 - y)]


def _all_gather_chips(shard, chip, name):
    r, ccols = shard.shape
    half = r // 2

    def body(s_ref, o_ref, send_sems, recv_sems):
        x, y, c, sibling, chips = _place()

        def slab(px, py, hc):
            return o_ref.at[2 * px + py, pl.ds(hc * half, half), :]

        def copy(k, src, dst, to):
            return pltpu.make_async_remote_copy(src_ref=src, dst_ref=dst, send_sem=send_sems.at[k],
                                                recv_sem=recv_sems.at[k], device_id=to, device_id_type=MESH)

        sends = [copy(k, s_ref.at[pl.ds(c * half, half), :], slab(x, y, c), (px, py, c))
                 for k, (px, py) in enumerate(chips)]
        for cp in sends:
            cp.start()
        passed = []
        for k, (px, py) in enumerate(chips):
            copy(k, slab(px, py, c), slab(px, py, c), (px, py, c)).wait_recv()
            cp = copy(3 + k, slab(px, py, c), slab(px, py, c), sibling)
            cp.start()
            passed.append(cp)
        for k, (px, py) in enumerate(chips):
            copy(3 + k, slab(px, py, 1 - c), slab(px, py, 1 - c), sibling).wait_recv()
        for cp in sends + passed:
            cp.wait_send()

    others = pl.pallas_call(
        body, name=name, in_specs=[ANY], out_specs=ANY,
        out_shape=_sds((N_CHIPS, r, ccols), shard.dtype),
        scratch_shapes=[pltpu.SemaphoreType.DMA((6,)), pltpu.SemaphoreType.DMA((6,))],
    )(shard)
    return lax.dynamic_update_slice(others, shard[None], (chip, 0, 0))


def _sibling_swap(g, name):
    n, r, ccols = g.shape
    half = r // 2

    def body(g_ref, o_ref, send_sem, recv_sem):
        _, _, c, sibling, _ = _place()
        cp = pltpu.make_async_remote_copy(src_ref=g_ref.at[:, pl.ds((1 - c) * half, half), :], dst_ref=o_ref,
                                          send_sem=send_sem, recv_sem=recv_sem, device_id=sibling, device_id_type=MESH)
        cp.start()
        cp.wait()

    return pl.pallas_call(
        body, name=name, in_specs=[ANY], out_specs=ANY, out_shape=_sds((n, half, ccols), g.dtype),
        scratch_shapes=[pltpu.SemaphoreType.DMA, pltpu.SemaphoreType.DMA],
    )(g)


def _chip_exchange(p, name):
    _, h, ccols = p.shape

    def body(p_ref, o_ref, send_sems, recv_sems):
        _, _, c, _, chips = _place()
        copies = [pltpu.make_async_remote_copy(src_ref=p_ref.at[2 * px + py], dst_ref=o_ref.at[k],
                                               send_sem=send_sems.at[k], recv_sem=recv_sems.at[k],
                                               device_id=(px, py, c), device_id_type=MESH)
                  for k, (px, py) in enumerate(chips)]
        for cp in copies:
            cp.start()
        for cp in copies:
            cp.wait()

    return pl.pallas_call(
        body, name=name, in_specs=[ANY], out_specs=ANY, out_shape=_sds((3, h, ccols), p.dtype),
        scratch_shapes=[pltpu.SemaphoreType.DMA((3,)), pltpu.SemaphoreType.DMA((3,))],
    )(p)


def _sibling_share(full, name):
    r, ccols = full.shape
    h = r // 2

    def body(f_ref, o_ref, send_sem, recv_sem):
        _, _, c, sibling, _ = _place()
        mine = pl.ds(c * h, h)
        cp = pltpu.make_async_remote_copy(src_ref=f_ref.at[mine, :], dst_ref=o_ref.at[mine, :], send_sem=send_sem,
                                          recv_sem=recv_sem, device_id=sibling, device_id_type=MESH)
        cp.start()
        theirs = o_ref.at[pl.ds((1 - c) * h, h), :]
        pltpu.make_async_remote_copy(src_ref=theirs, dst_ref=theirs, send_sem=send_sem, recv_sem=recv_sem,
                                     device_id=sibling, device_id_type=MESH).wait_recv()
        cp.wait_send()

    return pl.pallas_call(
        body, name=name, in_specs=[ANY], out_specs=ANY, out_shape=_sds((r, ccols), full.dtype),
        input_output_aliases={0: 0},
        scratch_shapes=[pltpu.SemaphoreType.DMA, pltpu.SemaphoreType.DMA],
    )(full)


def _add_halves(g, a, c, name):
    n, r, ccols = g.shape
    half = r // 2
    tr = _tile(half, 256)
    nb = half // tr

    def body(c_ref, g_ref, a_ref, o_ref):
        o_ref[...] = (g_ref[...] + a_ref[...]).astype(BF16)

    blk = (1, tr, ccols)
    return pl.pallas_call(
        body, name=name,
        grid_spec=pltpu.PrefetchScalarGridSpec(
            num_scalar_prefetch=1, grid=(n, nb),
            in_specs=[pl.BlockSpec(blk, lambda j, i, c_ref: (j, c_ref[0] * nb + i, 0)),
                      pl.BlockSpec(blk, lambda j, i, c_ref: (j, i, 0))],
            out_specs=pl.BlockSpec(blk, lambda j, i, c_ref: (j, i, 0))),
        out_shape=_sds((n, half, ccols), BF16),
        compiler_params=_params("parallel", "parallel"),
    )(c.reshape(1).astype(jnp.int32), g, a)


def _sum_chips(p, got, chip, c, name):
    _, h, ccols = p.shape
    tr = _tile(h, 256)

    def body(j_ref, h_ref, p_ref, a_ref, b_ref, c_ref, o_ref):
        f32 = [ref[...].astype(F32) for ref in (p_ref, a_ref, b_ref, c_ref)]
        o_ref[...] = ((f32[0] + f32[1]) + f32[2]) + f32[3]

    blk = (1, tr, ccols)

    def slot(k):
        return pl.BlockSpec(blk, lambda i, j_ref, h_ref: (k, i, 0))

    return pl.pallas_call(
        body, name=name,
        grid_spec=pltpu.PrefetchScalarGridSpec(
            num_scalar_prefetch=2, grid=(h // tr,),
            in_specs=[pl.BlockSpec(blk, lambda i, j_ref, h_ref: (j_ref[0], i, 0)), slot(0), slot(1), slot(2)],
            out_specs=pl.BlockSpec(blk, lambda i, j_ref, h_ref: (h_ref[0], i, 0))),
        out_shape=_sds((2, h, ccols)),
        compiler_params=_params("parallel"),
    )(chip.reshape(1).astype(jnp.int32), c.reshape(1).astype(jnp.int32), p, got, got, got)


def _reduce_scatter(g, chip, c, name):
    a = _sibling_swap(g, f"{name}_swap")
    p = _add_halves(g, a, c, f"{name}_add")
    got = _chip_exchange(p, f"{name}_xchg")
    halves = _sum_chips(p, got, chip, c, f"{name}_sum")
    return _sibling_share(halves.reshape(g.shape[1], g.shape[2]), f"{name}_share")


def _all_reduce_small(v, name):
    r = v.shape[0]

    def body(v_ref, o_ref, buf, send_sems, recv_sems):
        x, y, c, _, _ = _place()
        buf[0] = v_ref[...]
        copies = []
        for rel in range(1, 8):
            fx, fy, fc = (rel >> 2) & 1, (rel >> 1) & 1, rel & 1
            peer = (1 - x if fx else x, 1 - y if fy else y, 1 - c if fc else c)
            cp = pltpu.make_async_remote_copy(src_ref=v_ref, dst_ref=buf.at[rel], send_sem=send_sems.at[rel - 1],
                                              recv_sem=recv_sems.at[rel - 1], device_id=peer, device_id_type=MESH)
            cp.start()
            copies.append(cp)
        for cp in copies:
            cp.wait()
        me = 4 * x + 2 * y + c
        acc = buf[jnp.bitwise_xor(me, 0)]
        for src in range(1, 8):
            acc = acc + buf[jnp.bitwise_xor(me, src)]
        o_ref[...] = acc

    vm = pl.BlockSpec(memory_space=pltpu.VMEM)
    return pl.pallas_call(
        body, name=name, in_specs=[vm], out_specs=vm, out_shape=_sds((r, LANES)),
        scratch_shapes=[pltpu.VMEM((8, r, LANES), F32), pltpu.SemaphoreType.DMA((7,)), pltpu.SemaphoreType.DMA((7,))],
    )(v)


WEIGHTS = ["norm_mix_g", "w_in", "conv_w", "conv_b", "dt_bias", "a_log", "d_skip", "ssd_norm_g", "att_norm_g",
           "w_out", "norm_mlp_g", "w_up", "w_down", "final_norm_g"]
BIG = ["w_in", "w_out", "w_up", "w_down"]
SMALL = [n for n in WEIGHTS if n not in BIG]


def _pack(arrays):
    flat = []
    for a in arrays:
        a = a.reshape(-1)
        flat.append(jnp.pad(a, (0, (-a.shape[0]) % LANES)))
    flat = jnp.concatenate(flat)
    flat = jnp.pad(flat, (0, (-flat.shape[0]) % (8 * LANES)))
    return flat.reshape(-1, LANES)


def _unpack(packed, shapes):
    flat, out, pos = packed.reshape(-1), [], 0
    for shp in shapes:
        n = math.prod(shp)
        out.append(flat[pos:pos + n].reshape(shp))
        pos += n + (-n) % LANES
    return out


def _to_shards(name, g):
    if name in ("w_in", "w_up"):
        l, r, ccols = g.shape
        return g.reshape(l, r, N_CHIPS, ccols // N_CHIPS).transpose(2, 0, 1, 3).reshape(N_CHIPS, l * r, ccols // N_CHIPS)
    l, r, ccols = g.shape
    return g.reshape(l, N_CHIPS, r // N_CHIPS, ccols).transpose(1, 0, 2, 3).reshape(N_CHIPS, l * r // N_CHIPS, ccols)


def _from_gathered(name, g, l):
    rows = g.shape[1] // DEPTH
    part = g[:, l * rows:(l + 1) * rows, :]
    if name in ("w_in", "w_up", "conv_w"):
        return part.transpose(1, 0, 2).reshape(rows, N_CHIPS * g.shape[2])
    return part.reshape(N_CHIPS * rows, g.shape[2])


def kernel(x, norm_mix_g, w_in, conv_w, conv_b, dt_bias, a_log, d_skip, ssd_norm_g, att_norm_g, w_out, norm_mlp_g, w_up, w_down, final_norm_g, loss_target, m_norm_mix_g, m_w_in, m_conv_w, m_conv_b, m_dt_bias, m_a_log, m_d_skip, m_ssd_norm_g, m_att_norm_g, m_w_out, m_norm_mlp_g, m_w_up, m_w_down, m_final_norm_g, v_norm_mix_g, v_w_in, v_conv_w, v_conv_b, v_dt_bias, v_a_log, v_d_skip, v_ssd_norm_g, v_att_norm_g, v_w_out, v_norm_mlp_g, v_w_up, v_w_down, v_final_norm_g):
    w = dict(norm_mix_g=norm_mix_g, w_in=w_in, conv_w=conv_w, conv_b=conv_b, dt_bias=dt_bias, a_log=a_log,
             d_skip=d_skip, ssd_norm_g=ssd_norm_g, att_norm_g=att_norm_g, w_out=w_out, norm_mlp_g=norm_mlp_g,
             w_up=w_up, w_down=w_down, final_norm_g=final_norm_g)
    m = dict(norm_mix_g=m_norm_mix_g, w_in=m_w_in, conv_w=m_conv_w, conv_b=m_conv_b, dt_bias=m_dt_bias,
             a_log=m_a_log, d_skip=m_d_skip, ssd_norm_g=m_ssd_norm_g, att_norm_g=m_att_norm_g, w_out=m_w_out,
             norm_mlp_g=m_norm_mlp_g, w_up=m_w_up, w_down=m_w_down, final_norm_g=m_final_norm_g)
    v = dict(norm_mix_g=v_norm_mix_g, w_in=v_w_in, conv_w=v_conv_w, conv_b=v_conv_b, dt_bias=v_dt_bias,
             a_log=v_a_log, d_skip=v_d_skip, ssd_norm_g=v_ssd_norm_g, att_norm_g=v_att_norm_g, w_out=v_w_out,
             norm_mlp_g=v_norm_mlp_g, w_up=v_w_up, w_down=v_w_down, final_norm_g=v_final_norm_g)
    n_seq, seq, d = x.shape
    t = n_seq * seq
    chip = 2 * lax.axis_index("x") + lax.axis_index("y")
    core = lax.axis_index("c")

    gathered = {n: _all_gather_chips(w[n].astype(BF16).reshape(-1, w[n].shape[-1]), chip, f"gather_{n}") for n in BIG}
    gathered["conv_w"] = _all_gather_chips(conv_w.reshape(-1, conv_w.shape[-1]), chip, "gather_conv_w")
    layers = [_prep_layer(l, *[_from_gathered(n, gathered[n], l) for n in BIG + ["conv_w"]], w) for l in range(DEPTH)]

    xs = x.reshape(t, d)
    hs = _rms_fwd(xs, layers[0]["norm_mix_g"], "rms_mix_fwd_l0")
    saved = []
    for l in range(DEPTH):
        next_g = layers[l + 1]["norm_mix_g"] if l + 1 < DEPTH else None
        xs, hs, s = _layer_fwd(xs, hs, layers[l], next_g, n_seq, f"_l{l}")
        saved.append(s)
    loss_vec, dx, dxb, g_final = _final_loss(xs, final_norm_g.reshape(1, d), loss_target.reshape(t, d), "final_loss")
    loss = lax.psum(loss_vec[0, 0], ("x", "y", "c"))

    grads, shard_major = [None] * DEPTH, {}
    for l in reversed(range(DEPTH)):
        dx, dxb, grads[l] = _layer_bwd(dx, dxb, layers[l], saved[l], shard_major, l, n_seq, f"_l{l}")
        grads[l]["w_in"] = _merge_w_in(grads[l].pop("w_main"), grads[l].pop("w_dt"))
    grad_x = dx.reshape(n_seq, seq, d)

    full = {n: jnp.stack([grads[l][n] for l in range(DEPTH)]) for n in SMALL + ["w_in"] if n != "final_norm_g"}
    full["final_norm_g"] = g_final.reshape(d)
    shard_major["w_in"] = _to_shards("w_in", full["w_in"])
    g_out = {}
    for n in BIG:
        red = _reduce_scatter(shard_major[n], chip, core, f"rs_{n}")
        g_out[n] = red.reshape(w[n].shape)
    small_sum = _all_reduce_small(_pack([full[n] for n in SMALL]), "allreduce_small")
    small_shapes = [(DEPTH, CONV_WIDTH, D_CONV) if n == "conv_w" else w[n].shape for n in SMALL]
    for n, val in zip(SMALL, _unpack(small_sum, small_shapes)):
        g_out[n] = val
    g_out["conv_w"] = lax.dynamic_slice_in_dim(g_out["conv_w"], chip * conv_w.shape[-1], conv_w.shape[-1], axis=2)

    delta, new_m, new_v = {}, {}, {}
    for n in BIG:
        two_d = (-1, w[n].shape[-1])
        dl, mn, vn = _adamw(w[n].reshape(two_d), g_out[n].reshape(two_d), m[n].reshape(two_d), v[n].reshape(two_d),
                            f"adamw_{n}")
        delta[n], new_m[n], new_v[n] = dl.reshape(w[n].shape), mn.reshape(w[n].shape), vn.reshape(w[n].shape)
    packs = [_pack([src[n] for n in SMALL]) for src in (w, g_out, m, v)]
    shapes = [w[n].shape for n in SMALL]
    for dst, packed in zip((delta, new_m, new_v), _adamw(*packs, "adamw_small")):
        for n, val in zip(SMALL, _unpack(packed, shapes)):
            dst[n] = val

    return (loss, grad_x, *[g_out[n] for n in WEIGHTS], *[delta[n] for n in WEIGHTS],
            *[new_m[n] for n in WEIGHTS], *[new_v[n] for n in WEIGHTS])
```

```python
import math

import jax
import jax.numpy as jnp
from jax import lax
from jax.experimental import pallas as pl
from jax.experimental.pallas import tpu as pltpu

F32 = jnp.float32
BF16 = jnp.bfloat16
HIGHEST = lax.Precision.HIGHEST

D_MODEL = 1024
DEPTH = 4
HEAD_DIM = 64
HEAD_SHIFT = HEAD_DIM.bit_length() - 1
N_HEADS = 16
N_GROUPS = 4
N_STATE = 128
N_PAIRS = N_HEADS // 2
CONV_WIDTH = 4
CHUNK = 128
D_CONV = D_MODEL + 2 * N_GROUPS * N_STATE
D_MAIN = D_MODEL + D_CONV + 3 * D_MODEL
D_IN_PROJ = D_MAIN + N_HEADS
D_FF = 4 * D_MODEL
EPS = 1e-5
LANES = 128
VMEM_LIMIT = 48 * 1024 * 1024
NORM_TM = 512

ADAM_LR = 0.001
ADAM_B1 = 0.9
ADAM_B2 = 0.999
ADAM_EPS = 1e-08
ADAM_WD = 0.01
ADAM_STEP = 10

N_CHIPS = 4
MESH = pl.DeviceIdType.MESH


def _tile(n, cap):
    if n <= cap:
        return n
    t = cap
    while t >= 8:
        if n % t == 0:
            return t
        t //= 2
    raise ValueError(f"no tile for {n} under {cap}")


def _params(*sem):
    return pltpu.CompilerParams(dimension_semantics=sem, vmem_limit_bytes=VMEM_LIMIT)


_DIMS = {"nn": (((1,), (0,)), ((), ())), "nt": (((1,), (1,)), ((), ())), "tn": (((0,), (0,)), ((), ()))}


def _matmul(a, b, mode, out_dtypes, name, extras=(), vecs=(), epilogue=None, slab=None, n_sums=0,
            tm=1024, tn=1024, tk=1024):
    if mode == "nn":
        (m, k), (_, n) = a.shape, b.shape
    elif mode == "nt":
        (m, k), (n, _) = a.shape, b.shape
    else:
        (k, m), (_, n) = a.shape, b.shape
    tm, tn, tk = _tile(m, tm), _tile(n, tn), _tile(k, tk)
    nk = k // tk
    if mode == "tn":
        a_spec = pl.BlockSpec((tk, tm), lambda i, j, kk: (kk, i))
    else:
        a_spec = pl.BlockSpec((tm, tk), lambda i, j, kk: (i, kk))
    if mode == "nt":
        b_spec = pl.BlockSpec((tn, tk), lambda i, j, kk: (j, kk))
    else:
        b_spec = pl.BlockSpec((tk, tn), lambda i, j, kk: (kk, j))
    mn_spec = pl.BlockSpec((tm, tn), lambda i, j, kk: (i, j))
    vec_spec = pl.BlockSpec((1, tn), lambda i, j, kk: (0, j))
    n_extra, n_vec, n_out = len(extras), len(vecs), len(out_dtypes)
    n_in = n_extra + n_vec + (1 if slab is not None and slab[0] is not None else 0)
    dims = _DIMS[mode]

    def body(a_ref, b_ref, *rest):
        in_refs, out_refs, acc = rest[:n_extra + n_vec], rest[n_in:n_in + n_out], rest[-1]
        kk = pl.program_id(2)
        first_row_block = pl.program_id(0) == 0

        @pl.when(kk == 0)
        def _():
            acc[...] = jnp.zeros_like(acc)

        acc[...] += lax.dot_general(a_ref[...].astype(BF16), b_ref[...].astype(BF16), dims,
                                    preferred_element_type=F32)

        @pl.when(kk == nk - 1)
        def _():
            res = acc[...]
            outs = epilogue(res, *[e[...] for e in in_refs]) if epilogue is not None else (res,)
            for o_ref, val in zip(out_refs[:n_out - n_sums], outs):
                o_ref[...] = val.astype(o_ref.dtype).reshape(o_ref.shape)
            for o_ref, val in zip(out_refs[n_out - n_sums:], outs[n_out - n_sums:]):
                @pl.when(first_row_block)
                def _(o_ref=o_ref):
                    o_ref[...] = jnp.zeros_like(o_ref)

                o_ref[...] += val

    in_specs = [a_spec, b_spec] + [mn_spec] * n_extra + [vec_spec] * n_vec
    operands = [a, b, *extras, *vecs]
    out_specs = [mn_spec] * n_out + [vec_spec] * n_sums
    out_shape = [jax.ShapeDtypeStruct((m, n), dt) for dt in out_dtypes] + [jax.ShapeDtypeStruct((1, n), F32)] * n_sums
    n_out += n_sums
    aliases = {}
    if slab is not None:
        buffer, shape, index = slab
        out_specs = [pl.BlockSpec((1, tm, tn), lambda i, j, kk: index(i, j))]
        out_shape = [jax.ShapeDtypeStruct(shape, out_dtypes[0])]
        if buffer is not None:
            in_specs.append(ANY)
            operands.append(buffer)
            aliases = {len(operands) - 1: 0}
    outs = pl.pallas_call(
        body, name=name, grid=(m // tm, n // tn, nk),
        in_specs=in_specs, out_specs=out_specs, out_shape=out_shape, input_output_aliases=aliases,
        scratch_shapes=[pltpu.VMEM((tm, tn), F32)],
        compiler_params=_params("arbitrary" if n_sums else "parallel", "parallel", "arbitrary"),
    )(*operands)
    return tuple(outs)


def _rowwise(fn, rows, vecs, out_rows, out_accs, name, tm=256):
    t = rows[0].shape[0]
    tm = _tile(t, tm)
    n_rows, n_vecs, n_or, n_oa = len(rows), len(vecs), len(out_rows), len(out_accs)

    def body(*refs):
        ins = [r[...] for r in refs[:n_rows + n_vecs]]
        outs = fn(*ins)
        o_refs = refs[n_rows + n_vecs:]
        for o_ref, val in zip(o_refs[:n_or], outs[:n_or]):
            o_ref[...] = val.astype(o_ref.dtype)

        @pl.when(pl.program_id(0) == 0)
        def _():
            for o_ref in o_refs[n_or:]:
                o_ref[...] = jnp.zeros_like(o_ref)

        for o_ref, val in zip(o_refs[n_or:], outs[n_or:]):
            o_ref[...] += val

    outs = pl.pallas_call(
        body, name=name, grid=(t // tm,),
        in_specs=[pl.BlockSpec((tm, r.shape[1]), lambda i: (i, 0)) for r in rows]
        + [pl.BlockSpec(v.shape, lambda i: (0, 0)) for v in vecs],
        out_specs=[pl.BlockSpec((tm, o.shape[1]), lambda i: (i, 0)) for o in out_rows]
        + [pl.BlockSpec(o.shape, lambda i: (0, 0)) for o in out_accs],
        out_shape=list(out_rows) + list(out_accs),
        compiler_params=_params("arbitrary" if n_oa else "parallel"),
    )(*rows, *vecs)
    return tuple(outs)


def _stack_rows(parts, width):
    rows = lax.broadcasted_iota(jnp.int32, (8, width), 0)
    out = jnp.zeros((8, width), F32)
    for j, part in enumerate(parts):
        out = out + jnp.where(rows == j, part, 0.0)
    return out


def _sds(shape, dtype=F32):
    return jax.ShapeDtypeStruct(shape, dtype)


def _sigmoid(x):
    return 1.0 / (1.0 + jnp.exp(-x))


def _colsum(x):
    return jnp.sum(x, axis=0, keepdims=True)


def _rstd(x):
    return lax.rsqrt(jnp.mean(x * x, axis=-1, keepdims=True) + EPS)


def _rms_bwd_rows(xhat, r, g, dy):
    dxh = dy * g
    dx = r * (dxh - xhat * jnp.mean(dxh * xhat, axis=-1, keepdims=True))
    return dx, _colsum(dy * xhat)


def _rms_fwd(x, g, name):
    t, d = x.shape

    def fn(xb, gv):
        return (xb * _rstd(xb) * gv,)

    return _rowwise(fn, [x], [g], [_sds((t, d), BF16)], [], name)[0]


def _rms_bwd_epilogue(dh, xb, dresb, gv):
    r = _rstd(xb)
    dx, dg = _rms_bwd_rows(xb * r, r, gv, dh)
    dx = dx + dresb
    return dx, dx, dg


def _mixnorm_fwd(y, z, att, gs, ga, name):
    t, d = y.shape

    def fn(yb, zb, ab, gsv, gav):
        yg = yb * (zb * _sigmoid(zb))
        return (jnp.concatenate([yg * _rstd(yg) * gsv, ab * _rstd(ab) * gav], axis=1),)

    tm = _tile(t, 256)

    def body(y_ref, z_ref, a_ref, gs_ref, ga_ref, o_ref):
        o_ref[...] = fn(y_ref[...], z_ref[...], a_ref[...], gs_ref[...], ga_ref[...])[0].astype(BF16)

    row = pl.BlockSpec((tm, d), lambda i: (i, 0))
    vec = pl.BlockSpec((1, d), lambda i: (0, 0))
    out = pl.pallas_call(
        body, name=name, grid=(t // tm,),
        in_specs=[row, row, row, vec, vec],
        out_specs=pl.BlockSpec((tm, 2 * d), lambda i: (i, 0)),
        out_shape=_sds((t, 2 * d), BF16),
        compiler_params=_params("parallel"),
    )(y, z, att, gs, ga)
    return out


def _mixnorm_bwd(y, z, att, gs, ga, dycat, name):
    t, d = y.shape
    tm = _tile(t, 256)

    def body(y_ref, z_ref, a_ref, dyc_ref, gs_ref, ga_ref, dy_ref, dz_ref, da_ref, dgs_ref, dga_ref):
        yb, zb, ab = y_ref[...], z_ref[...], a_ref[...]
        dys, dya = dyc_ref[:, :d], dyc_ref[:, d:]
        sz = _sigmoid(zb)
        silu = zb * sz
        yg = yb * silu
        r = _rstd(yg)
        dyg, dgs = _rms_bwd_rows(yg * r, r, gs_ref[...], dys)
        dy_ref[...] = dyg * silu
        dz_ref[...] = (dyg * yb * (sz * (1.0 + zb * (1.0 - sz)))).astype(BF16)
        r2 = _rstd(ab)
        datt, dga = _rms_bwd_rows(ab * r2, r2, ga_ref[...], dya)
        da_ref[...] = datt

        @pl.when(pl.program_id(0) == 0)
        def _():
            dgs_ref[...] = jnp.zeros_like(dgs_ref)
            dga_ref[...] = jnp.zeros_like(dga_ref)

        dgs_ref[...] += dgs
        dga_ref[...] += dga

    row = pl.BlockSpec((tm, d), lambda i: (i, 0))
    vec = pl.BlockSpec((1, d), lambda i: (0, 0))
    return pl.pallas_call(
        body, name=name, grid=(t // tm,),
        in_specs=[row, row, row, pl.BlockSpec((tm, 2 * d), lambda i: (i, 0)), vec, vec],
        out_specs=[row, row, row, vec, vec],
        out_shape=[_sds((t, d)), _sds((t, d), BF16), _sds((t, d)), _sds((1, d)), _sds((1, d))],
        compiler_params=_params("arbitrary"),
    )(y, z, att, dycat, gs, ga)


def _final_loss(x, g, target, name):
    t, d = x.shape

    def fn(xb, tb, gv):
        r = _rstd(xb)
        xhat = xb * r
        err = xhat * gv - tb
        loss = 0.5 * jnp.sum(jnp.mean(err * err, axis=-1, keepdims=True), axis=0, keepdims=True)
        dx, dg = _rms_bwd_rows(xhat, r, gv, err * (1.0 / d))
        return dx, dx, jnp.broadcast_to(loss, (1, LANES)), dg

    dx, dxb, loss, dg = _rowwise(fn, [x, target], [g], [_sds((t, d)), _sds((t, d), BF16)],
                                 [_sds((1, LANES)), _sds((1, d))], name)
    return loss, dx, dxb, dg


def _adamw(w, g, m, v, name):
    c1 = 1.0 - ADAM_B1 ** ADAM_STEP
    c2 = 1.0 - ADAM_B2 ** ADAM_STEP

    def fn(wb, gb, mb, vb):
        mn = ADAM_B1 * mb + (1.0 - ADAM_B1) * gb
        vn = ADAM_B2 * vb + (1.0 - ADAM_B2) * (gb * gb)
        delta = -ADAM_LR * ((mn / c1) / (jnp.sqrt(vn / c2) + ADAM_EPS) + ADAM_WD * wb)
        return delta, mn, vn

    return _rowwise(fn, [w, g, m, v], [], [_sds(w.shape)] * 3, [], name)


CONV_CB = 512


def _shift_down(u, k):
    if k == 0:
        return u
    rows = lax.broadcasted_iota(jnp.int32, u.shape, 0)
    return jnp.where(rows >= k, pltpu.roll(u, k, 0), 0.0)


def _shift_up(u, k):
    if k == 0:
        return u
    n = u.shape[0]
    rows = lax.broadcasted_iota(jnp.int32, u.shape, 0)
    return jnp.where(rows < n - k, pltpu.roll(u, n - k, 0), 0.0)


def _conv_pre(u, w, b):
    pre = b
    for j in range(CONV_WIDTH):
        pre = pre + w[j:j + 1, :] * _shift_down(u, CONV_WIDTH - 1 - j)
    return pre


def _conv_fwd(proj, w, b, n_seq, name):
    t = proj.shape[0]
    seq = t // n_seq
    off = D_MODEL // CONV_CB

    def body(u_ref, w_ref, b_ref, o_ref):
        pre = _conv_pre(u_ref[...], w_ref[...], b_ref[...])
        o_ref[...] = pre * _sigmoid(pre)

    return pl.pallas_call(
        body, name=name, grid=(n_seq, D_CONV // CONV_CB),
        in_specs=[pl.BlockSpec((seq, CONV_CB), lambda s, c: (s, c + off)),
                  pl.BlockSpec((CONV_WIDTH, CONV_CB), lambda s, c: (0, c)),
                  pl.BlockSpec((1, CONV_CB), lambda s, c: (0, c))],
        out_specs=pl.BlockSpec((seq, CONV_CB), lambda s, c: (s, c)),
        out_shape=_sds((t, D_CONV)),
        compiler_params=_params("parallel", "parallel"),
    )(proj, w, b)


def _conv_bwd(proj, w, b, dxbc, n_seq, name):
    t = proj.shape[0]
    seq = t // n_seq
    off = D_MODEL // CONV_CB

    def body(u_ref, w_ref, b_ref, d_ref, du_ref, wg_ref):
        u, wv = u_ref[...], w_ref[...]
        taps = [_shift_down(u, CONV_WIDTH - 1 - j) for j in range(CONV_WIDTH)]
        pre = b_ref[...]
        for j in range(CONV_WIDTH):
            pre = pre + wv[j:j + 1, :] * taps[j]
        s = _sigmoid(pre)
        dpre = d_ref[...] * (s * (1.0 + pre * (1.0 - s)))
        du = jnp.zeros_like(u)
        parts = []
        for j in range(CONV_WIDTH):
            du = du + wv[j:j + 1, :] * _shift_up(dpre, CONV_WIDTH - 1 - j)
            parts.append(_colsum(dpre * taps[j]))
        du_ref[...] = du.astype(BF16)
        parts.append(_colsum(dpre))

        @pl.when(pl.program_id(1) == 0)
        def _():
            wg_ref[...] = jnp.zeros_like(wg_ref)

        wg_ref[...] += _stack_rows(parts, u.shape[1])

    return pl.pallas_call(
        body, name=name, grid=(D_CONV // CONV_CB, n_seq),
        in_specs=[pl.BlockSpec((seq, CONV_CB), lambda c, s: (s, c + off)),
                  pl.BlockSpec((CONV_WIDTH, CONV_CB), lambda c, s: (0, c)),
                  pl.BlockSpec((1, CONV_CB), lambda c, s: (0, c)),
                  pl.BlockSpec((seq, CONV_CB), lambda c, s: (s, c))],
        out_specs=[pl.BlockSpec((seq, CONV_CB), lambda c, s: (s, c)),
                   pl.BlockSpec((8, CONV_CB), lambda c, s: (0, c))],
        out_shape=[_sds((t, D_CONV), BF16), _sds((8, D_CONV))],
        compiler_params=_params("parallel", "arbitrary"),
    )(proj, w, b, dxbc)


def _iota2(shape, axis):
    return lax.broadcasted_iota(jnp.int32, shape, axis)


def _dot(a, b, dims=_DIMS["nn"], precision=None):
    return lax.dot_general(a, b, dims, precision=precision, preferred_element_type=F32)


def _bdot(a, b, dims=_DIMS["nn"]):
    return lax.dot_general(a.astype(BF16), b.astype(BF16), dims, preferred_element_type=F32)


def _expand_mat():
    return (_iota2((LANES, D_MODEL), 0) == lax.shift_right_logical(_iota2((LANES, D_MODEL), 1), HEAD_SHIFT)).astype(F32)


def _reduce_mat():
    return (lax.shift_right_logical(_iota2((D_MODEL, LANES), 0), HEAD_SHIFT) == _iota2((D_MODEL, LANES), 1)).astype(F32)


def _ssd_decay(dtraw, bias, alog):
    row, col = _iota2((CHUNK, CHUNK), 0), _iota2((CHUNK, CHUNK), 1)
    pre = dtraw + bias
    dtb = jnp.maximum(pre, 0.0) + jnp.log(1.0 + jnp.exp(-jnp.abs(pre)))
    a_neg = -jnp.exp(alog)
    a = dtb * a_neg
    tril = (row >= col).astype(F32)
    triu = (row <= col).astype(F32)
    cs = _dot(tril, a, precision=HIGHEST)
    cs_t = _dot(a, triu, _DIMS["tn"], precision=HIGHEST)
    return pre, dtb, a_neg, cs, cs_t


def _pair_rowscale(vec, h0):
    top = _iota2((CHUNK, LANES), 0) < HEAD_DIM
    return jnp.where(top, vec[:, h0:h0 + 1], vec[:, h0 + 1:h0 + 2])


def _decay_mat(cs, cs_t, h):
    row, col = _iota2((CHUNK, CHUNK), 0), _iota2((CHUNK, CHUNK), 1)
    seg = cs[:, h:h + 1] - cs_t[h:h + 1, :]
    return jnp.exp(jnp.where(row >= col, seg, -jnp.inf))


def _ssd_fwd(xbc, dtraw, bias, alog, dskip_x, n_seq, name):
    t = xbc.shape[0]
    nc = t // n_seq // CHUNK

    def body(x_ref, b_ref, c_ref, dt_ref, bias_ref, alog_ref, dsk_ref, y_ref, st_ref, h_ref):
        @pl.when(pl.program_id(1) == 0)
        def _():
            h_ref[...] = jnp.zeros_like(h_ref)

        _, dtb, _, cs, cs_t = _ssd_decay(dt_ref[...], bias_ref[...], alog_ref[...])
        expand = _expand_mat()
        dt_x = _dot(dtb, expand, precision=HIGHEST)
        cs_x = _dot(cs, expand, precision=HIGHEST)
        tot = cs[CHUNK - 1:CHUNK, :]
        etot = jnp.exp(tot)
        x = x_ref[...]
        xdt = x * dt_x
        e_x = jnp.exp(cs_x)
        xdec = xdt * jnp.exp(cs_x[CHUNK - 1:CHUNK, :] - cs_x)
        keeps = [_iota2((CHUNK, LANES), 1) < HEAD_DIM, _iota2((CHUNK, LANES), 1) >= HEAD_DIM]
        for g in range(N_GROUPS):
            glanes = slice(2 * g * LANES, (2 * g + 2) * LANES)
            bg = b_ref[:, g * N_STATE:(g + 1) * N_STATE]
            cg = c_ref[:, g * N_STATE:(g + 1) * N_STATE]
            gmat = _bdot(cg, bg, _DIMS["nt"])
            prev = h_ref[2 * g:2 * g + 2].reshape(2 * LANES, N_STATE)
            st_ref[0, 2 * g:2 * g + 2] = prev.reshape(2, LANES, N_STATE)
            yoff = _bdot(cg, prev, _DIMS["nt"]) * e_x[:, glanes]
            new = _bdot(xdec[:, glanes], bg, _DIMS["tn"])
            scale = jnp.concatenate([_pair_rowscale(etot, 2 * q) for q in (2 * g, 2 * g + 1)], axis=0)
            h_ref[2 * g:2 * g + 2] = (prev * scale + new).reshape(2, LANES, N_STATE)
            for q in (2 * g, 2 * g + 1):
                lanes = slice(q * LANES, (q + 1) * LANES)
                xdt_q = xdt[:, lanes]
                w_cat = jnp.concatenate([gmat * _decay_mat(cs, cs_t, 2 * q + r) for r in range(2)], axis=1)
                x_cat = jnp.concatenate([jnp.where(keeps[r], xdt_q, 0.0) for r in range(2)], axis=0)
                y_ref[:, lanes] = (_bdot(w_cat, x_cat) + yoff[:, lanes.start - glanes.start:lanes.stop - glanes.start]
                                   + x[:, lanes] * dsk_ref[:, lanes])

    vec = pl.BlockSpec((1, LANES), lambda s, c: (0, 0))
    return pl.pallas_call(
        body, name=name, grid=(n_seq, nc),
        in_specs=[pl.BlockSpec((CHUNK, D_MODEL), lambda s, c: (s * nc + c, 0)),
                  pl.BlockSpec((CHUNK, N_GROUPS * N_STATE), lambda s, c: (s * nc + c, 2)),
                  pl.BlockSpec((CHUNK, N_GROUPS * N_STATE), lambda s, c: (s * nc + c, 3)),
                  pl.BlockSpec((CHUNK, LANES), lambda s, c: (s * nc + c, 0)),
                  vec, vec, pl.BlockSpec((1, D_MODEL), lambda s, c: (0, 0))],
        out_specs=[pl.BlockSpec((CHUNK, D_MODEL), lambda s, c: (s * nc + c, 0)),
                   pl.BlockSpec((1, N_PAIRS, LANES, N_STATE), lambda s, c: (s * nc + c, 0, 0, 0))],
        out_shape=[_sds((t, D_MODEL)), _sds((t // CHUNK, N_PAIRS, LANES, N_STATE))],
        scratch_shapes=[pltpu.VMEM((N_PAIRS, LANES, N_STATE), F32)],
        compiler_params=_params("parallel", "arbitrary"),
    )(xbc, xbc, xbc, dtraw, bias, alog, dskip_x)


def _ssd_bwd(xbc, dtraw, bias, alog, dskip_x, states, dy, n_seq, name):
    t = xbc.shape[0]
    nc = t // n_seq // CHUNK

    def body(x_ref, b_ref, c_ref, dt_ref, bias_ref, alog_ref, dsk_ref, st_ref, dy_ref,
             dxbc_ref, ddt_ref, pg_ref, dh_ref):
        first = jnp.logical_and(pl.program_id(0) == 0, pl.program_id(1) == 0)

        @pl.when(first)
        def _():
            pg_ref[...] = jnp.zeros_like(pg_ref)

        @pl.when(pl.program_id(1) == 0)
        def _():
            dh_ref[...] = jnp.zeros_like(dh_ref)

        row, col = _iota2((CHUNK, CHUNK), 0), _iota2((CHUNK, CHUNK), 1)
        pre, dtb, a_neg, cs, cs_t = _ssd_decay(dt_ref[...], bias_ref[...], alog_ref[...])
        expand, reduce = _expand_mat(), _reduce_mat()
        dt_x = _dot(dtb, expand, precision=HIGHEST)
        cs_x = _dot(cs, expand, precision=HIGHEST)
        etot = jnp.exp(cs[CHUNK - 1:CHUNK, :])
        x, dy = x_ref[...], dy_ref[...]
        xdt = x * dt_x
        e_x = jnp.exp(cs_x)
        dec_x = jnp.exp(cs_x[CHUNK - 1:CHUNK, :] - cs_x)
        xdec = xdt * dec_x
        dye = dy * e_x
        keeps = [_iota2((CHUNK, LANES), 1) < HEAD_DIM, _iota2((CHUNK, LANES), 1) >= HEAD_DIM]
        dcs_col = jnp.zeros((CHUNK, LANES), F32)
        dcs_row = jnp.zeros((LANES, CHUNK), F32)
        dtot = jnp.zeros((1, LANES), F32)
        dxdt_parts, zdec_parts, yoff_parts = [], [], []
        for g in range(N_GROUPS):
            bg = b_ref[:, g * N_STATE:(g + 1) * N_STATE]
            cg = c_ref[:, g * N_STATE:(g + 1) * N_STATE]
            gmat = _bdot(cg, bg, _DIMS["nt"])
            dgmat = jnp.zeros((CHUNK, CHUNK), F32)
            glanes = slice(2 * g * LANES, (2 * g + 2) * LANES)
            prev = st_ref[0, 2 * g:2 * g + 2].reshape(2 * LANES, N_STATE)
            dht = dh_ref[2 * g:2 * g + 2].reshape(2 * LANES, N_STATE)
            dxdtdec = _bdot(bg, dht, _DIMS["nt"])
            zdec_parts.append(dxdtdec * xdec[:, glanes])
            dbg = _bdot(xdec[:, glanes], dht)
            yoff_parts.append(dy[:, glanes] * (_bdot(cg, prev, _DIMS["nt"]) * e_x[:, glanes]))
            dcg = _bdot(dye[:, glanes], prev)
            dprev = _bdot(dye[:, glanes], cg, _DIMS["tn"])
            hp = jnp.sum(dht * prev, axis=1, keepdims=True)
            rows4 = lax.shift_right_logical(_iota2((2 * LANES, 1), 0), HEAD_SHIFT)
            lane1 = _iota2((1, LANES), 1)
            for k in range(4):
                dk_tot = jnp.sum(jnp.where(rows4 == k, hp, 0.0), axis=0, keepdims=True)
                dtot = dtot + jnp.where(lane1 == 4 * g + k, dk_tot, 0.0)
            scale = jnp.concatenate([_pair_rowscale(etot, 2 * q) for q in (2 * g, 2 * g + 1)], axis=0)
            dh_ref[2 * g:2 * g + 2] = (dprev + dht * scale).reshape(2, LANES, N_STATE)
            for q in (2 * g, 2 * g + 1):
                lanes = slice(q * LANES, (q + 1) * LANES)
                local = slice(lanes.start - glanes.start, lanes.stop - glanes.start)
                xdt_q, dy_q = xdt[:, lanes], dy[:, lanes]
                lms = [_decay_mat(cs, cs_t, 2 * q + r) for r in range(2)]
                ws = [gmat * lm for lm in lms]
                dy_cat = jnp.concatenate([jnp.where(keeps[r], dy_q, 0.0) for r in range(2)], axis=0)
                dm_cat = _bdot(dy_cat, xdt_q, _DIMS["nt"])
                dxdt_q = _bdot(jnp.concatenate(ws, axis=0), dy_cat, _DIMS["tn"])
                for r in range(2):
                    h = 2 * q + r
                    dm = dm_cat[r * CHUNK:(r + 1) * CHUNK]
                    dgmat = dgmat + dm * lms[r]
                    tmat = dm * ws[r]
                    dcs_col = dcs_col + jnp.where(col == h, jnp.sum(tmat, axis=1, keepdims=True), 0.0)
                    dcs_row = dcs_row - jnp.where(row == h, jnp.sum(tmat, axis=0, keepdims=True), 0.0)
                dxdt_parts.append(dxdt_q + dxdtdec[:, local] * dec_x[:, lanes])
            dcg = dcg + _bdot(dgmat, bg)
            dbg = dbg + _bdot(dgmat, cg, _DIMS["tn"])
            dxbc_ref[:, D_MODEL + g * N_STATE:D_MODEL + (g + 1) * N_STATE] = dbg
            dxbc_ref[:, D_MODEL + (N_GROUPS + g) * N_STATE:D_MODEL + (N_GROUPS + g + 1) * N_STATE] = dcg
        dxdt = jnp.concatenate(dxdt_parts, axis=1)
        zdec = _dot(jnp.concatenate(zdec_parts, axis=1), reduce, precision=HIGHEST)
        yoff_d = _dot(jnp.concatenate(yoff_parts, axis=1), reduce, precision=HIGHEST)
        dtot = dtot * etot + _colsum(zdec)
        last = row[:, :LANES] == CHUNK - 1
        dcs_col = dcs_col + yoff_d - zdec + jnp.where(last, dtot, 0.0)
        triu = (row <= col).astype(F32)
        da = _dot(triu, dcs_col, precision=HIGHEST) + _dot(triu, dcs_row, _DIMS["nt"], precision=HIGHEST)
        ddt = _dot(dxdt * x, reduce, precision=HIGHEST) + da * a_neg
        ddtraw = ddt * _sigmoid(pre)
        ddt_ref[...] = ddtraw.astype(BF16)
        dxbc_ref[:, :D_MODEL] = dxdt * dt_x + dy * dsk_ref[...]
        dskip = _dot(jnp.broadcast_to(_colsum(dy * x), (8, D_MODEL)), reduce, precision=HIGHEST)[0:1, :]
        pg_ref[...] += _stack_rows([_colsum(ddtraw), _colsum(da * dtb) * a_neg, dskip], LANES)

    vec = pl.BlockSpec((1, LANES), lambda s, c: (0, 0))

    def blk(s, c):
        return s * nc + (nc - 1 - c)

    return pl.pallas_call(
        body, name=name, grid=(n_seq, nc),
        in_specs=[pl.BlockSpec((CHUNK, D_MODEL), lambda s, c: (blk(s, c), 0)),
                  pl.BlockSpec((CHUNK, N_GROUPS * N_STATE), lambda s, c: (blk(s, c), 2)),
                  pl.BlockSpec((CHUNK, N_GROUPS * N_STATE), lambda s, c: (blk(s, c), 3)),
                  pl.BlockSpec((CHUNK, LANES), lambda s, c: (blk(s, c), 0)),
                  vec, vec, pl.BlockSpec((1, D_MODEL), lambda s, c: (0, 0)),
                  pl.BlockSpec((1, N_PAIRS, LANES, N_STATE), lambda s, c: (blk(s, c), 0, 0, 0)),
                  pl.BlockSpec((CHUNK, D_MODEL), lambda s, c: (blk(s, c), 0))],
        out_specs=[pl.BlockSpec((CHUNK, D_CONV), lambda s, c: (blk(s, c), 0)),
                   pl.BlockSpec((CHUNK, LANES), lambda s, c: (blk(s, c), 0)),
                   pl.BlockSpec((8, LANES), lambda s, c: (0, 0))],
        out_shape=[_sds((t, D_CONV)), _sds((t, LANES), BF16), _sds((8, LANES))],
        scratch_shapes=[pltpu.VMEM((N_PAIRS, LANES, N_STATE), F32)],
        compiler_params=_params("arbitrary", "arbitrary"),
    )(xbc, xbc, xbc, dtraw, bias, alog, dskip_x, states, dy)


Q_COL = (D_MODEL + D_CONV) // LANES
K_COL = Q_COL + D_MODEL // LANES
V_COL = K_COL + D_MODEL // LANES
ATT_SCALE = HEAD_DIM ** -0.5


ATT_SUB = 2
ATT_TILE = ATT_SUB * CHUNK
ATT_NP = 2


ATT_ROWS = 2 * ATT_TILE


def _tri_dot(x, tri):
    return _dot(x.astype(BF16), tri)


def _att_stack(ref, lanes, keeps, scale=1.0):
    parts = []
    for a in range(ATT_SUB):
        blk = ref[a * CHUNK:(a + 1) * CHUNK, lanes] * scale
        parts += [jnp.where(keeps[r], blk, 0.0).astype(BF16) for r in range(2)]
    return jnp.concatenate(parts, axis=0)


def _att_unstack(x, a, keeps):
    return jnp.where(keeps[0], x[2 * a * CHUNK:(2 * a + 1) * CHUNK], x[(2 * a + 1) * CHUNK:(2 * a + 2) * CHUNK])


def _att_logits(s, diag):
    lb = jnp.minimum(s, 0.0) - jnp.log(1.0 + jnp.exp(-jnp.abs(s)))
    l1m = lb - s
    mask = None
    if diag:
        srow, scol = _iota2((ATT_ROWS, ATT_TILE), 0), _iota2((ATT_ROWS, ATT_TILE), 1)
        pair_shift = (2 * CHUNK).bit_length() - 1
        qpos = lax.shift_right_logical(srow, pair_shift) * CHUNK + jnp.bitwise_and(srow, CHUNK - 1)
        mask = qpos > scol
        l1m = jnp.where(mask, l1m, 0.0)
    return lb, l1m, mask


def _att_fwd(proj, n_seq, name):
    t = proj.shape[0]
    seq = t // n_seq
    nq = seq // ATT_TILE

    def body(q_ref, k_ref, v_ref, o_ref, rt_ref):
        i = pl.program_id(2)
        upper = (_iota2((ATT_TILE, ATT_TILE), 0) > _iota2((ATT_TILE, ATT_TILE), 1)).astype(BF16)
        keeps = [_iota2((CHUNK, LANES), 1) < HEAD_DIM, _iota2((CHUNK, LANES), 1) >= HEAD_DIM]
        pair_lanes = [slice(pr * LANES, (pr + 1) * LANES) for pr in range(ATT_NP)]
        q_stacks = [_att_stack(q_ref, lanes, keeps, ATT_SCALE) for lanes in pair_lanes]
        per_pair = ATT_SUB + 2

        def logits(jt):
            k0 = pl.multiple_of(jt * ATT_TILE, ATT_TILE)
            return [_dot(q_stacks[pr], k_ref[pl.ds(k0, ATT_TILE), lanes].astype(BF16), _DIMS["nt"])
                    for pr, lanes in enumerate(pair_lanes)]

        def tile(jt, carry, diag):
            state = list(carry)
            ahead = logits(jnp.maximum(jt - 1, 0))
            k0 = pl.multiple_of(jt * ATT_TILE, ATT_TILE)
            for pr, lanes in enumerate(pair_lanes):
                base = pr * per_pair
                run = state[base + ATT_SUB]
                v_tile = v_ref[pl.ds(k0, ATT_TILE), lanes].astype(BF16)
                lb, l1m, mask = _att_logits(state[base + ATT_SUB + 1], diag)
                p = jnp.exp(lb + (_tri_dot(l1m, upper) + run))
                if diag:
                    p = jnp.where(mask, p, 0.0)
                pv = _dot(p.astype(BF16), v_tile)
                for a in range(ATT_SUB):
                    state[base + a] = state[base + a] + _att_unstack(pv, a, keeps)
                state[base + ATT_SUB] = run + jnp.sum(l1m, axis=1, keepdims=True)
                state[base + ATT_SUB + 1] = ahead[pr]
            return tuple(state)

        first = logits(i)
        init = ()
        for pr in range(ATT_NP):
            init += tuple(jnp.zeros((CHUNK, LANES), F32) for _ in range(ATT_SUB)) + \
                (jnp.zeros((ATT_ROWS, 1), F32), first[pr])
        carry = tile(i, init, True)
        carry = lax.fori_loop(0, i, lambda it, c: tile(i - 1 - it, c, False), carry)
        for pr, lanes in enumerate(pair_lanes):
            for a in range(ATT_SUB):
                o_ref[a * CHUNK:(a + 1) * CHUNK, lanes] = carry[pr * per_pair + a]
                rt_ref[a * CHUNK:(a + 1) * CHUNK, lanes] = _att_unstack(carry[pr * per_pair + ATT_SUB], a, keeps)

    width = ATT_NP * LANES
    qblk = pl.BlockSpec((ATT_TILE, width), lambda s, p, i: (s * nq + i, p))
    return pl.pallas_call(
        body, name=name, grid=(n_seq, N_PAIRS // ATT_NP, nq),
        in_specs=[pl.BlockSpec((ATT_TILE, width), lambda s, p, i: (s * nq + i, Q_COL // ATT_NP + p)),
                  pl.BlockSpec((seq, width), lambda s, p, i: (s, K_COL // ATT_NP + p)),
                  pl.BlockSpec((seq, width), lambda s, p, i: (s, V_COL // ATT_NP + p))],
        out_specs=[qblk, qblk],
        out_shape=[_sds((t, D_MODEL)), _sds((t, D_MODEL))],
        compiler_params=_params("parallel", "parallel", "arbitrary"),
    )(proj, proj, proj)


def _att_bwd(proj, rtot, datt, n_seq, name):
    t = proj.shape[0]
    seq = t // n_seq
    nq = seq // ATT_TILE

    def body(q_ref, k_ref, v_ref, rt_ref, do_ref, dq_ref, dk_out, dv_out, dk_ref, dv_ref):
        i = pl.program_id(2)

        @pl.when(i == 0)
        def _():
            dk_ref[...] = jnp.zeros_like(dk_ref)
            dv_ref[...] = jnp.zeros_like(dv_ref)

        row, col = _iota2((ATT_TILE, ATT_TILE), 0), _iota2((ATT_TILE, ATT_TILE), 1)
        upper = (row > col).astype(BF16)
        before = (row < col).astype(BF16)
        keeps = [_iota2((CHUNK, LANES), 1) < HEAD_DIM, _iota2((CHUNK, LANES), 1) >= HEAD_DIM]
        pair_lanes = [slice(pr * LANES, (pr + 1) * LANES) for pr in range(ATT_NP)]
        q_stacks = [_att_stack(q_ref, lanes, keeps, ATT_SCALE) for lanes in pair_lanes]
        do_stacks = [_att_stack(do_ref, lanes, keeps) for lanes in pair_lanes]
        totals = [jnp.concatenate([rt_ref[a * CHUNK:(a + 1) * CHUNK, pr * LANES + r * HEAD_DIM:pr * LANES + r * HEAD_DIM + 1]
                                   for a in range(ATT_SUB) for r in range(2)], axis=0) for pr in range(ATT_NP)]
        per_pair = ATT_SUB + 4

        def products(jt):
            k0 = pl.multiple_of(jt * ATT_TILE, ATT_TILE)
            return [(_dot(q_stacks[pr], k_ref[pl.ds(k0, ATT_TILE), lanes].astype(BF16), _DIMS["nt"]),
                     _dot(do_stacks[pr], v_ref[pl.ds(k0, ATT_TILE), lanes].astype(BF16), _DIMS["nt"]))
                    for pr, lanes in enumerate(pair_lanes)]

        def tile(jt, carry, diag):
            state = list(carry)
            ahead = None if diag else products(jt + 1)
            k0 = pl.multiple_of(jt * ATT_TILE, ATT_TILE)
            for pr, lanes in enumerate(pair_lanes):
                base = pr * per_pair
                seen, dseen = state[base + ATT_SUB], state[base + ATT_SUB + 1]
                q_stack, do_stack = q_stacks[pr], do_stacks[pr]
                k_tile = k_ref[pl.ds(k0, ATT_TILE), lanes].astype(BF16)
                lb, l1m, mask = _att_logits(state[base + ATT_SUB + 2], diag)
                here = jnp.sum(l1m, axis=1, keepdims=True)
                p = jnp.exp(lb + (_tri_dot(l1m, upper) + (totals[pr] - seen - here)))
                if diag:
                    p = jnp.where(mask, p, 0.0)
                pb = p.astype(BF16)
                dz = state[base + ATT_SUB + 3] * p
                dl1m = dseen + _tri_dot(dz, before)
                sig = jnp.exp(lb)
                ds = dz * (1.0 - sig) - dl1m * sig
                if diag:
                    ds = jnp.where(mask, ds, 0.0)
                dsb = ds.astype(BF16)
                dq_all = _dot(dsb, k_tile)
                for a in range(ATT_SUB):
                    state[base + a] = state[base + a] + _att_unstack(dq_all, a, keeps)
                dk_ref[pl.ds(k0, ATT_TILE), lanes] += _dot(dsb, q_stack, _DIMS["tn"])
                dv_ref[pl.ds(k0, ATT_TILE), lanes] += _dot(pb, do_stack, _DIMS["tn"])
                state[base + ATT_SUB] = seen + here
                state[base + ATT_SUB + 1] = dseen + jnp.sum(dz, axis=1, keepdims=True)
                if ahead is not None:
                    state[base + ATT_SUB + 2], state[base + ATT_SUB + 3] = ahead[pr]
            return tuple(state)

        first = products(0)
        init = ()
        for pr in range(ATT_NP):
            init += tuple(jnp.zeros((CHUNK, LANES), F32) for _ in range(ATT_SUB)) + \
                (jnp.zeros((ATT_ROWS, 1), F32), jnp.zeros((ATT_ROWS, 1), F32)) + first[pr]
        carry = lax.fori_loop(0, i, lambda jt, c: tile(jt, c, False), init)
        carry = tile(i, carry, True)
        for pr, lanes in enumerate(pair_lanes):
            for a in range(ATT_SUB):
                dq_ref[a * CHUNK:(a + 1) * CHUNK, lanes] = (carry[pr * per_pair + a] * ATT_SCALE).astype(BF16)

        @pl.when(i == nq - 1)
        def _():
            dk_out[...] = dk_ref[...].astype(BF16)
            dv_out[...] = dv_ref[...].astype(BF16)

    width = ATT_NP * LANES
    qblk = pl.BlockSpec((ATT_TILE, width), lambda s, p, i: (s * nq + i, p))
    kv_out = pl.BlockSpec((seq, width), lambda s, p, i: (s, p))
    return pl.pallas_call(
        body, name=name, grid=(n_seq, N_PAIRS // ATT_NP, nq),
        in_specs=[pl.BlockSpec((ATT_TILE, width), lambda s, p, i: (s * nq + i, Q_COL // ATT_NP + p)),
                  pl.BlockSpec((seq, width), lambda s, p, i: (s, K_COL // ATT_NP + p)),
                  pl.BlockSpec((seq, width), lambda s, p, i: (s, V_COL // ATT_NP + p)),
                  qblk, qblk],
        out_specs=[qblk, kv_out, kv_out],
        out_shape=[_sds((t, D_MODEL), BF16)] * 3,
        scratch_shapes=[pltpu.VMEM((seq, width), F32), pltpu.VMEM((seq, width), F32)],
        compiler_params=_params("parallel", "parallel", "arbitrary"),
    )(proj, proj, proj, rtot, datt)


def _pad_lanes(v):
    return jnp.pad(v.reshape(1, -1), ((0, 0), (0, LANES - v.shape[0])))


def _add_then_norm(acc, xb, g):
    x1 = xb + acc
    return x1, x1 * _rstd(x1) * g


def _layer_fwd(x, h, p, next_g, n_seq, tag):
    proj, = _matmul(h, p["w_main"], "nn", [F32], f"in_proj{tag}", tn=1536)
    dtraw, = _matmul(h, p["w_dt"], "nn", [F32], f"dt_proj{tag}")
    xbc = _conv_fwd(proj, p["conv_w"], p["conv_b"], n_seq, f"conv_fwd{tag}")
    y, states = _ssd_fwd(xbc, dtraw, p["dt_bias"], p["a_log"], p["d_skip_x"], n_seq, f"ssd_fwd{tag}")
    att, rtot = _att_fwd(proj, n_seq, f"att_fwd{tag}")
    ycat = _mixnorm_fwd(y, proj, att, p["ssd_norm_g"], p["att_norm_g"], f"mixnorm_fwd{tag}")
    x1, h2 = _matmul(ycat, p["w_out"], "nn", [F32, BF16], f"out_proj{tag}", extras=[x], vecs=[p["norm_mlp_g"]],
                     epilogue=_add_then_norm, tn=D_MODEL)
    u, act = _matmul(h2, p["w_up"], "nn", [F32, BF16], f"up_proj{tag}",
                     epilogue=lambda acc: (acc, jnp.square(jnp.maximum(acc, 0.0))))
    if next_g is None:
        x2, = _matmul(act, p["w_down"], "nn", [F32], f"down_proj{tag}", extras=[x1],
                      epilogue=lambda acc, xb: (xb + acc,))
        h_next = None
    else:
        x2, h_next = _matmul(act, p["w_down"], "nn", [F32, BF16], f"down_proj{tag}", extras=[x1], vecs=[next_g],
                             epilogue=_add_then_norm, tn=D_MODEL)
    saved = dict(x=x, h=h, proj=proj, dtraw=dtraw, xbc=xbc, y=y, states=states, att=att, rtot=rtot, ycat=ycat,
                 x1=x1, h2=h2, u=u, act=act)
    return x2, h_next, saved


def _slab(buffers, name, layer, rows, per_chip_rows):
    shape = (N_CHIPS, DEPTH * rows, D_MODEL)
    if per_chip_rows:
        return buffers.get(name), shape, lambda i, j: (i, layer, 0)
    return buffers.get(name), shape, lambda i, j: (j, layer, 0)


def _layer_bwd(dx2, dx2b, p, s, buffers, layer, n_seq, tag):
    g = {}
    buffers["w_down"], = _matmul(s["act"], dx2b, "tn", [F32], f"dw_down{tag}",
                                 slab=_slab(buffers, "w_down", layer, D_FF // N_CHIPS, True), tm=D_FF // N_CHIPS)
    du, = _matmul(dx2b, p["w_down"], "nt", [BF16], f"d_act{tag}", extras=[s["u"]],
                  epilogue=lambda acc, ub: (acc * (2.0 * jnp.maximum(ub, 0.0)),))
    buffers["w_up"], = _matmul(s["h2"], du, "tn", [F32], f"dw_up{tag}",
                               slab=_slab(buffers, "w_up", layer, D_MODEL, False), tn=D_FF // N_CHIPS)
    dx1, dx1b, g["norm_mlp_g"] = _matmul(du, p["w_up"], "nt", [F32, BF16], f"d_h2{tag}", extras=[s["x1"], dx2],
                                         vecs=[p["norm_mlp_g"]], epilogue=_rms_bwd_epilogue, n_sums=1,
                                         tm=NORM_TM, tn=D_MODEL)
    buffers["w_out"], = _matmul(s["ycat"], dx1b, "tn", [F32], f"dw_out{tag}",
                                slab=_slab(buffers, "w_out", layer, 2 * D_MODEL // N_CHIPS, True),
                                tm=2 * D_MODEL // N_CHIPS)
    dycat, = _matmul(dx1b, p["w_out"], "nt", [F32], f"d_ycat{tag}")
    dy, dz, datt, g["ssd_norm_g"], g["att_norm_g"] = _mixnorm_bwd(
        s["y"], s["proj"], s["att"], p["ssd_norm_g"], p["att_norm_g"], dycat, f"mixnorm_bwd{tag}")
    dq, dk, dv = _att_bwd(s["proj"], s["rtot"], datt, n_seq, f"att_bwd{tag}")
    dxbc, ddtraw, pg = _ssd_bwd(s["xbc"], s["dtraw"], p["dt_bias"], p["a_log"], p["d_skip_x"], s["states"], dy,
                                n_seq, f"ssd_bwd{tag}")
    g["dt_bias"], g["a_log"], g["d_skip"] = pg[0, :N_HEADS], pg[1, :N_HEADS], pg[2, :N_HEADS]
    du_conv, wg = _conv_bwd(s["proj"], p["conv_w"], p["conv_b"], dxbc, n_seq, f"conv_bwd{tag}")
    g["conv_w"], g["conv_b"] = wg[:CONV_WIDTH], wg[CONV_WIDTH]
    dproj = jnp.concatenate([dz, du_conv, dq, dk, dv], axis=1)
    g["w_main"], = _matmul(s["h"], dproj, "tn", [F32], f"dw_in{tag}", tn=1536)
    g["w_dt"], = _matmul(s["h"], ddtraw, "tn", [F32], f"dw_dt{tag}")
    dh_dt, = _matmul(ddtraw, p["w_dt"], "nt", [F32], f"d_h_dt{tag}")
    dx, dxb, g["norm_mix_g"] = _matmul(
        dproj, p["w_main"], "nt", [F32, BF16], f"d_h{tag}", extras=[dh_dt, s["x"], dx1], vecs=[p["norm_mix_g"]],
        epilogue=lambda acc, e, xb, dres, gv: _rms_bwd_epilogue(acc + e, xb, dres, gv), n_sums=1,
        tm=NORM_TM, tn=D_MODEL)
    return dx, dxb, g


def _split_w_in(w_full):
    c0 = D_MODEL + D_CONV
    main = jnp.concatenate([w_full[:, :c0], w_full[:, c0 + N_HEADS:]], axis=1)
    dt = jnp.pad(w_full[:, c0:c0 + N_HEADS], ((0, 0), (0, LANES - N_HEADS)))
    return main, dt


def _merge_w_in(main, dt):
    c0 = D_MODEL + D_CONV
    return jnp.concatenate([main[:, :c0], dt[:, :N_HEADS], main[:, c0:]], axis=1)


def _prep_layer(l, w_in_full, w_out, w_up, w_down, conv_w, small):
    w_main, w_dt = _split_w_in(w_in_full)
    return dict(
        w_main=w_main, w_dt=w_dt, w_out=w_out, w_up=w_up, w_down=w_down, conv_w=conv_w,
        conv_b=small["conv_b"][l].reshape(1, -1),
        dt_bias=_pad_lanes(small["dt_bias"][l]), a_log=_pad_lanes(small["a_log"][l]),
        d_skip_x=jnp.repeat(small["d_skip"][l], HEAD_DIM).reshape(1, -1),
        norm_mix_g=small["norm_mix_g"][l].reshape(1, -1), ssd_norm_g=small["ssd_norm_g"][l].reshape(1, -1),
        att_norm_g=small["att_norm_g"][l].reshape(1, -1), norm_mlp_g=small["norm_mlp_g"][l].reshape(1, -1),
    )


ANY = pl.BlockSpec(memory_space=pl.ANY)


def _place():
    x, y, c = lax.axis_index("x"), lax.axis_index("y"), lax.axis_index("c")
    return x, y, c, (x, y, 1 - c), [(1 - x, y), (x, 1 - y), (1 - x, 1 - y)]


def _all_gather_chips(shard, chip, name):
    r, ccols = shard.shape
    half = r // 2

    def body(s_ref, o_ref, send_sems, recv_sems):
        x, y, c, sibling, chips = _place()

        def slab(px, py, hc):
            return o_ref.at[2 * px + py, pl.ds(hc * half, half), :]

        def copy(k, src, dst, to):
            return pltpu.make_async_remote_copy(src_ref=src, dst_ref=dst, send_sem=send_sems.at[k],
                                                recv_sem=recv_sems.at[k], device_id=to, device_id_type=MESH)

        sends = [copy(k, s_ref.at[pl.ds(c * half, half), :], slab(x, y, c), (px, py, c))
                 for k, (px, py) in enumerate(chips)]
        for cp in sends:
            cp.start()
        passed = []
        for k, (px, py) in enumerate(chips):
            copy(k, slab(px, py, c), slab(px, py, c), (px, py, c)).wait_recv()
            cp = copy(3 + k, slab(px, py, c), slab(px, py, c), sibling)
            cp.start()
            passed.append(cp)
        for k, (px, py) in enumerate(chips):
            copy(3 + k, slab(px, py, 1 - c), slab(px, py, 1 - c), sibling).wait_recv()
        for cp in sends + passed:
            cp.wait_send()

    others = pl.pallas_call(
        body, name=name, in_specs=[ANY], out_specs=ANY,
        out_shape=_sds((N_CHIPS, r, ccols), shard.dtype),
        scratch_shapes=[pltpu.SemaphoreType.DMA((6,)), pltpu.SemaphoreType.DMA((6,))],
    )(shard)
    return lax.dynamic_update_slice(others, shard[None], (chip, 0, 0))


def _sibling_swap(g, name):
    n, r, ccols = g.shape
    half = r // 2

    def body(g_ref, o_ref, send_sem, recv_sem):
        _, _, c, sibling, _ = _place()
        cp = pltpu.make_async_remote_copy(src_ref=g_ref.at[:, pl.ds((1 - c) * half, half), :], dst_ref=o_ref,
                                          send_sem=send_sem, recv_sem=recv_sem, device_id=sibling, device_id_type=MESH)
        cp.start()
        cp.wait()

    return pl.pallas_call(
        body, name=name, in_specs=[ANY], out_specs=ANY, out_shape=_sds((n, half, ccols), g.dtype),
        scratch_shapes=[pltpu.SemaphoreType.DMA, pltpu.SemaphoreType.DMA],
    )(g)


def _chip_exchange(p, name):
    _, h, ccols = p.shape

    def body(p_ref, o_ref, send_sems, recv_sems):
        _, _, c, _, chips = _place()
        copies = [pltpu.make_async_remote_copy(src_ref=p_ref.at[2 * px + py], dst_ref=o_ref.at[k],
                                               send_sem=send_sems.at[k], recv_sem=recv_sems.at[k],
                                               device_id=(px, py, c), device_id_type=MESH)
                  for k, (px, py) in enumerate(chips)]
        for cp in copies:
            cp.start()
        for cp in copies:
            cp.wait()

    return pl.pallas_call(
        body, name=name, in_specs=[ANY], out_specs=ANY, out_shape=_sds((3, h, ccols), p.dtype),
        scratch_shapes=[pltpu.SemaphoreType.DMA((3,)), pltpu.SemaphoreType.DMA((3,))],
    )(p)


def _sibling_share(full, name):
    r, ccols = full.shape
    h = r // 2

    def body(f_ref, o_ref, send_sem, recv_sem):
        _, _, c, sibling, _ = _place()
        mine = pl.ds(c * h, h)
        cp = pltpu.make_async_remote_copy(src_ref=f_ref.at[mine, :], dst_ref=o_ref.at[mine, :], send_sem=send_sem,
                                          recv_sem=recv_sem, device_id=sibling, device_id_type=MESH)
        cp.start()
        theirs = o_ref.at[pl.ds((1 - c) * h, h), :]
        pltpu.make_async_remote_copy(src_ref=theirs, dst_ref=theirs, send_sem=send_sem, recv_sem=recv_sem,
                                     device_id=sibling, device_id_type=MESH).wait_recv()
        cp.wait_send()

    return pl.pallas_call(
        body, name=name, in_specs=[ANY], out_specs=ANY, out_shape=_sds((r, ccols), full.dtype),
        input_output_aliases={0: 0},
        scratch_shapes=[pltpu.SemaphoreType.DMA, pltpu.SemaphoreType.DMA],
    )(full)


def _add_halves(g, a, c, name):
    n, r, ccols = g.shape
    half = r // 2
    tr = _tile(half, 256)
    nb = half // tr

    def body(c_ref, g_ref, a_ref, o_ref):
        o_ref[...] = (g_ref[...] + a_ref[...]).astype(BF16)

    blk = (1, tr, ccols)
    return pl.pallas_call(
        body, name=name,
        grid_spec=pltpu.PrefetchScalarGridSpec(
            num_scalar_prefetch=1, grid=(n, nb),
            in_specs=[pl.BlockSpec(blk, lambda j, i, c_ref: (j, c_ref[0] * nb + i, 0)),
                      pl.BlockSpec(blk, lambda j, i, c_ref: (j, i, 0))],
            out_specs=pl.BlockSpec(blk, lambda j, i, c_ref: (j, i, 0))),
        out_shape=_sds((n, half, ccols), BF16),
        compiler_params=_params("parallel", "parallel"),
    )(c.reshape(1).astype(jnp.int32), g, a)


def _sum_chips(p, got, chip, c, name):
    _, h, ccols = p.shape
    tr = _tile(h, 256)

    def body(j_ref, h_ref, p_ref, a_ref, b_ref, c_ref, o_ref):
        f32 = [ref[...].astype(F32) for ref in (p_ref, a_ref, b_ref, c_ref)]
        o_ref[...] = ((f32[0] + f32[1]) + f32[2]) + f32[3]

    blk = (1, tr, ccols)

    def slot(k):
        return pl.BlockSpec(blk, lambda i, j_ref, h_ref: (k, i, 0))

    return pl.pallas_call(
        body, name=name,
        grid_spec=pltpu.PrefetchScalarGridSpec(
            num_scalar_prefetch=2, grid=(h // tr,),
            in_specs=[pl.BlockSpec(blk, lambda i, j_ref, h_ref: (j_ref[0], i, 0)), slot(0), slot(1), slot(2)],
            out_specs=pl.BlockSpec(blk, lambda i, j_ref, h_ref: (h_ref[0], i, 0))),
        out_shape=_sds((2, h, ccols)),
        compiler_params=_params("parallel"),
    )(chip.reshape(1).astype(jnp.int32), c.reshape(1).astype(jnp.int32), p, got, got, got)


def _reduce_scatter(g, chip, c, name):
    a = _sibling_swap(g, f"{name}_swap")
    p = _add_halves(g, a, c, f"{name}_add")
    got = _chip_exchange(p, f"{name}_xchg")
    halves = _sum_chips(p, got, chip, c, f"{name}_sum")
    return _sibling_share(halves.reshape(g.shape[1], g.shape[2]), f"{name}_share")


def _all_reduce_small(v, name):
    r = v.shape[0]

    def body(v_ref, o_ref, buf, send_sems, recv_sems):
        x, y, c, _, _ = _place()
        buf[0] = v_ref[...]
        copies = []
        for rel in range(1, 8):
            fx, fy, fc = (rel >> 2) & 1, (rel >> 1) & 1, rel & 1
            peer = (1 - x if fx else x, 1 - y if fy else y, 1 - c if fc else c)
            cp = pltpu.make_async_remote_copy(src_ref=v_ref, dst_ref=buf.at[rel], send_sem=send_sems.at[rel - 1],
                                              recv_sem=recv_sems.at[rel - 1], device_id=peer, device_id_type=MESH)
            cp.start()
            copies.append(cp)
        for cp in copies:
            cp.wait()
        me = 4 * x + 2 * y + c
        acc = buf[jnp.bitwise_xor(me, 0)]
        for src in range(1, 8):
            acc = acc + buf[jnp.bitwise_xor(me, src)]
        o_ref[...] = acc

    vm = pl.BlockSpec(memory_space=pltpu.VMEM)
    return pl.pallas_call(
        body, name=name, in_specs=[vm], out_specs=vm, out_shape=_sds((r, LANES)),
        scratch_shapes=[pltpu.VMEM((8, r, LANES), F32), pltpu.SemaphoreType.DMA((7,)), pltpu.SemaphoreType.DMA((7,))],
    )(v)


WEIGHTS = ["norm_mix_g", "w_in", "conv_w", "conv_b", "dt_bias", "a_log", "d_skip", "ssd_norm_g", "att_norm_g",
           "w_out", "norm_mlp_g", "w_up", "w_down", "final_norm_g"]
BIG = ["w_in", "w_out", "w_up", "w_down"]
SMALL = [n for n in WEIGHTS if n not in BIG]


def _pack(arrays):
    flat = []
    for a in arrays:
        a = a.reshape(-1)
        flat.append(jnp.pad(a, (0, (-a.shape[0]) % LANES)))
    flat = jnp.concatenate(flat)
    flat = jnp.pad(flat, (0, (-flat.shape[0]) % (8 * LANES)))
    return flat.reshape(-1, LANES)


def _unpack(packed, shapes):
    flat, out, pos = packed.reshape(-1), [], 0
    for shp in shapes:
        n = math.prod(shp)
        out.append(flat[pos:pos + n].reshape(shp))
        pos += n + (-n) % LANES
    return out


def _columns_to_shards(g):
    l, r, ccols = g.shape
    return g.reshape(l, r, N_CHIPS, ccols // N_CHIPS).transpose(2, 0, 1, 3).reshape(N_CHIPS, l * r, ccols // N_CHIPS)


def _from_gathered(name, g, l):
    rows = g.shape[1] // DEPTH
    part = g[:, l * rows:(l + 1) * rows, :]
    if name in ("w_in", "w_up", "conv_w"):
        return part.transpose(1, 0, 2).reshape(rows, N_CHIPS * g.shape[2])
    return part.reshape(N_CHIPS * rows, g.shape[2])


def kernel(x, norm_mix_g, w_in, conv_w, conv_b, dt_bias, a_log, d_skip, ssd_norm_g, att_norm_g, w_out, norm_mlp_g, w_up, w_down, final_norm_g, loss_target, m_norm_mix_g, m_w_in, m_conv_w, m_conv_b, m_dt_bias, m_a_log, m_d_skip, m_ssd_norm_g, m_att_norm_g, m_w_out, m_norm_mlp_g, m_w_up, m_w_down, m_final_norm_g, v_norm_mix_g, v_w_in, v_conv_w, v_conv_b, v_dt_bias, v_a_log, v_d_skip, v_ssd_norm_g, v_att_norm_g, v_w_out, v_norm_mlp_g, v_w_up, v_w_down, v_final_norm_g):
    w = dict(norm_mix_g=norm_mix_g, w_in=w_in, conv_w=conv_w, conv_b=conv_b, dt_bias=dt_bias, a_log=a_log,
             d_skip=d_skip, ssd_norm_g=ssd_norm_g, att_norm_g=att_norm_g, w_out=w_out, norm_mlp_g=norm_mlp_g,
             w_up=w_up, w_down=w_down, final_norm_g=final_norm_g)
    m = dict(norm_mix_g=m_norm_mix_g, w_in=m_w_in, conv_w=m_conv_w, conv_b=m_conv_b, dt_bias=m_dt_bias,
             a_log=m_a_log, d_skip=m_d_skip, ssd_norm_g=m_ssd_norm_g, att_norm_g=m_att_norm_g, w_out=m_w_out,
             norm_mlp_g=m_norm_mlp_g, w_up=m_w_up, w_down=m_w_down, final_norm_g=m_final_norm_g)
    v = dict(norm_mix_g=v_norm_mix_g, w_in=v_w_in, conv_w=v_conv_w, conv_b=v_conv_b, dt_bias=v_dt_bias,
             a_log=v_a_log, d_skip=v_d_skip, ssd_norm_g=v_ssd_norm_g, att_norm_g=v_att_norm_g, w_out=v_w_out,
             norm_mlp_g=v_norm_mlp_g, w_up=v_w_up, w_down=v_w_down, final_norm_g=v_final_norm_g)
    n_seq, seq, d = x.shape
    t = n_seq * seq
    chip = 2 * lax.axis_index("x") + lax.axis_index("y")
    core = lax.axis_index("c")

    gathered = {n: _all_gather_chips(w[n].astype(BF16).reshape(-1, w[n].shape[-1]), chip, f"gather_{n}") for n in BIG}
    gathered["conv_w"] = _all_gather_chips(conv_w.reshape(-1, conv_w.shape[-1]), chip, "gather_conv_w")
    layers = [_prep_layer(l, *[_from_gathered(n, gathered[n], l) for n in BIG + ["conv_w"]], w) for l in range(DEPTH)]

    xs = x.reshape(t, d)
    hs = _rms_fwd(xs, layers[0]["norm_mix_g"], "rms_mix_fwd_l0")
    saved = []
    for l in range(DEPTH):
        next_g = layers[l + 1]["norm_mix_g"] if l + 1 < DEPTH else None
        xs, hs, s = _layer_fwd(xs, hs, layers[l], next_g, n_seq, f"_l{l}")
        saved.append(s)
    loss_vec, dx, dxb, g_final = _final_loss(xs, final_norm_g.reshape(1, d), loss_target.reshape(t, d), "final_loss")
    loss = lax.psum(loss_vec[0, 0], ("x", "y", "c"))

    grads, shard_major = [None] * DEPTH, {}
    for l in reversed(range(DEPTH)):
        dx, dxb, grads[l] = _layer_bwd(dx, dxb, layers[l], saved[l], shard_major, l, n_seq, f"_l{l}")
        grads[l]["w_in"] = _merge_w_in(grads[l].pop("w_main"), grads[l].pop("w_dt"))
    grad_x = dx.reshape(n_seq, seq, d)

    full = {n: jnp.stack([grads[l][n] for l in range(DEPTH)]) for n in SMALL + ["w_in"] if n != "final_norm_g"}
    full["final_norm_g"] = g_final.reshape(d)
    shard_major["w_in"] = _columns_to_shards(full["w_in"])
    g_out = {}
    for n in BIG:
        red = _reduce_scatter(shard_major[n], chip, core, f"rs_{n}")
        g_out[n] = red.reshape(w[n].shape)
    small_sum = _all_reduce_small(_pack([full[n] for n in SMALL]), "allreduce_small")
    small_shapes = [(DEPTH, CONV_WIDTH, D_CONV) if n == "conv_w" else w[n].shape for n in SMALL]
    for n, val in zip(SMALL, _unpack(small_sum, small_shapes)):
        g_out[n] = val
    g_out["conv_w"] = lax.dynamic_slice_in_dim(g_out["conv_w"], chip * conv_w.shape[-1], conv_w.shape[-1], axis=2)

    delta, new_m, new_v = {}, {}, {}
    for n in BIG:
        two_d = (-1, w[n].shape[-1])
        dl, mn, vn = _adamw(w[n].reshape(two_d), g_out[n].reshape(two_d), m[n].reshape(two_d), v[n].reshape(two_d),
                            f"adamw_{n}")
        delta[n], new_m[n], new_v[n] = dl.reshape(w[n].shape), mn.reshape(w[n].shape), vn.reshape(w[n].shape)
    packs = [_pack([src[n] for n in SMALL]) for src in (w, g_out, m, v)]
    shapes = [w[n].shape for n in SMALL]
    for dst, packed in zip((delta, new_m, new_v), _adamw(*packs, "adamw_small")):
        for n, val in zip(SMALL, _unpack(packed, shapes)):
            dst[n] = val

    return (loss, grad_x, *[g_out[n] for n in WEIGHTS], *[delta[n] for n in WEIGHTS],
            *[new_m[n] for n in WEIGHTS], *[new_v[n] for n in WEIGHTS])
```

```python
import math

import jax
import jax.numpy as jnp
from jax import lax
from jax.experimental import pallas as pl
from jax.experimental.pallas import tpu as pltpu

F32 = jnp.float32
BF16 = jnp.bfloat16
HIGHEST = lax.Precision.HIGHEST

D_MODEL = 1024
DEPTH = 4
HEAD_DIM = 64
HEAD_SHIFT = HEAD_DIM.bit_length() - 1
N_HEADS = 16
N_GROUPS = 4
N_STATE = 128
N_PAIRS = N_HEADS // 2
CONV_WIDTH = 4
CHUNK = 128
D_CONV = D_MODEL + 2 * N_GROUPS * N_STATE
D_MAIN = D_MODEL + D_CONV + 3 * D_MODEL
D_IN_PROJ = D_MAIN + N_HEADS
D_FF = 4 * D_MODEL
EPS = 1e-5
LANES = 128
VMEM_LIMIT = 48 * 1024 * 1024

ADAM_LR = 0.001
ADAM_B1 = 0.9
ADAM_B2 = 0.999
ADAM_EPS = 1e-08
ADAM_WD = 0.01
ADAM_STEP = 10

N_CHIPS = 4
MESH = pl.DeviceIdType.MESH


def _tile(n, cap):
    if n <= cap:
        return n
    t = cap
    while t >= 8:
        if n % t == 0:
            return t
        t //= 2
    raise ValueError(f"no tile for {n} under {cap}")


def _params(*sem):
    return pltpu.CompilerParams(dimension_semantics=sem, vmem_limit_bytes=VMEM_LIMIT)


_DIMS = {"nn": (((1,), (0,)), ((), ())), "nt": (((1,), (1,)), ((), ())), "tn": (((0,), (0,)), ((), ()))}


def _matmul(a, b, mode, out_dtypes, name, extras=(), vecs=(), epilogue=None, slab=None, tm=1024, tn=1024, tk=1024):
    if mode == "nn":
        (m, k), (_, n) = a.shape, b.shape
    elif mode == "nt":
        (m, k), (n, _) = a.shape, b.shape
    else:
        (k, m), (_, n) = a.shape, b.shape
    tm, tn, tk = _tile(m, tm), _tile(n, tn), _tile(k, tk)
    nk = k // tk
    if mode == "tn":
        a_spec = pl.BlockSpec((tk, tm), lambda i, j, kk: (kk, i))
    else:
        a_spec = pl.BlockSpec((tm, tk), lambda i, j, kk: (i, kk))
    if mode == "nt":
        b_spec = pl.BlockSpec((tn, tk), lambda i, j, kk: (j, kk))
    else:
        b_spec = pl.BlockSpec((tk, tn), lambda i, j, kk: (kk, j))
    mn_spec = pl.BlockSpec((tm, tn), lambda i, j, kk: (i, j))
    vec_spec = pl.BlockSpec((1, tn), lambda i, j, kk: (0, j))
    n_extra, n_vec, n_out = len(extras), len(vecs), len(out_dtypes)
    n_in = n_extra + n_vec + (1 if slab is not None and slab[0] is not None else 0)
    dims = _DIMS[mode]

    def body(a_ref, b_ref, *rest):
        in_refs, out_refs, acc = rest[:n_extra + n_vec], rest[n_in:n_in + n_out], rest[-1]
        kk = pl.program_id(2)

        @pl.when(kk == 0)
        def _():
            acc[...] = jnp.zeros_like(acc)

        acc[...] += lax.dot_general(a_ref[...].astype(BF16), b_ref[...].astype(BF16), dims,
                                    preferred_element_type=F32)

        @pl.when(kk == nk - 1)
        def _():
            res = acc[...]
            outs = epilogue(res, *[e[...] for e in in_refs]) if epilogue is not None else (res,)
            for o_ref, val in zip(out_refs, outs):
                o_ref[...] = val.astype(o_ref.dtype).reshape(o_ref.shape)

    in_specs = [a_spec, b_spec] + [mn_spec] * n_extra + [vec_spec] * n_vec
    operands = [a, b, *extras, *vecs]
    out_specs = [mn_spec] * n_out
    out_shape = [jax.ShapeDtypeStruct((m, n), dt) for dt in out_dtypes]
    aliases = {}
    if slab is not None:
        buffer, shape, index = slab
        out_specs = [pl.BlockSpec((1, tm, tn), lambda i, j, kk: index(i, j))]
        out_shape = [jax.ShapeDtypeStruct(shape, out_dtypes[0])]
        if buffer is not None:
            in_specs.append(ANY)
            operands.append(buffer)
            aliases = {len(operands) - 1: 0}
    outs = pl.pallas_call(
        body, name=name, grid=(m // tm, n // tn, nk),
        in_specs=in_specs, out_specs=out_specs, out_shape=out_shape, input_output_aliases=aliases,
        scratch_shapes=[pltpu.VMEM((tm, tn), F32)],
        compiler_params=_params("parallel", "parallel", "arbitrary"),
    )(*operands)
    return tuple(outs)


def _rowwise(fn, rows, vecs, out_rows, out_accs, name, tm=256):
    t = rows[0].shape[0]
    tm = _tile(t, tm)
    n_rows, n_vecs, n_or, n_oa = len(rows), len(vecs), len(out_rows), len(out_accs)

    def body(*refs):
        ins = [r[...] for r in refs[:n_rows + n_vecs]]
        outs = fn(*ins)
        o_refs = refs[n_rows + n_vecs:]
        for o_ref, val in zip(o_refs[:n_or], outs[:n_or]):
            o_ref[...] = val.astype(o_ref.dtype)

        @pl.when(pl.program_id(0) == 0)
        def _():
            for o_ref in o_refs[n_or:]:
                o_ref[...] = jnp.zeros_like(o_ref)

        for o_ref, val in zip(o_refs[n_or:], outs[n_or:]):
            o_ref[...] += val

    outs = pl.pallas_call(
        body, name=name, grid=(t // tm,),
        in_specs=[pl.BlockSpec((tm, r.shape[1]), lambda i: (i, 0)) for r in rows]
        + [pl.BlockSpec(v.shape, lambda i: (0, 0)) for v in vecs],
        out_specs=[pl.BlockSpec((tm, o.shape[1]), lambda i: (i, 0)) for o in out_rows]
        + [pl.BlockSpec(o.shape, lambda i: (0, 0)) for o in out_accs],
        out_shape=list(out_rows) + list(out_accs),
        compiler_params=_params("arbitrary" if n_oa else "parallel"),
    )(*rows, *vecs)
    return tuple(outs)


def _stack_rows(parts, width):
    rows = lax.broadcasted_iota(jnp.int32, (8, width), 0)
    out = jnp.zeros((8, width), F32)
    for j, part in enumerate(parts):
        out = out + jnp.where(rows == j, part, 0.0)
    return out


def _sds(shape, dtype=F32):
    return jax.ShapeDtypeStruct(shape, dtype)


def _sigmoid(x):
    return 1.0 / (1.0 + jnp.exp(-x))


def _colsum(x):
    return jnp.sum(x, axis=0, keepdims=True)


def _rstd(x):
    return lax.rsqrt(jnp.mean(x * x, axis=-1, keepdims=True) + EPS)


def _rms_bwd_rows(xhat, r, g, dy):
    dxh = dy * g
    dx = r * (dxh - xhat * jnp.mean(dxh * xhat, axis=-1, keepdims=True))
    return dx, _colsum(dy * xhat)


def _rms_fwd(x, g, name):
    t, d = x.shape

    def fn(xb, gv):
        return (xb * _rstd(xb) * gv,)

    return _rowwise(fn, [x], [g], [_sds((t, d), BF16)], [], name)[0]


def _rms_bwd(x, g, dh, dres, name):
    t, d = x.shape

    def fn(xb, dhb, dresb, gv):
        r = _rstd(xb)
        dx, dg = _rms_bwd_rows(xb * r, r, gv, dhb)
        dx = dx + dresb
        return dx, dx, dg

    return _rowwise(fn, [x, dh, dres], [g], [_sds((t, d)), _sds((t, d), BF16)], [_sds((1, d))], name)


def _mixnorm_fwd(y, z, att, gs, ga, name):
    t, d = y.shape

    def fn(yb, zb, ab, gsv, gav):
        yg = yb * (zb * _sigmoid(zb))
        return (jnp.concatenate([yg * _rstd(yg) * gsv, ab * _rstd(ab) * gav], axis=1),)

    tm = _tile(t, 256)

    def body(y_ref, z_ref, a_ref, gs_ref, ga_ref, o_ref):
        o_ref[...] = fn(y_ref[...], z_ref[...], a_ref[...], gs_ref[...], ga_ref[...])[0].astype(BF16)

    row = pl.BlockSpec((tm, d), lambda i: (i, 0))
    vec = pl.BlockSpec((1, d), lambda i: (0, 0))
    out = pl.pallas_call(
        body, name=name, grid=(t // tm,),
        in_specs=[row, row, row, vec, vec],
        out_specs=pl.BlockSpec((tm, 2 * d), lambda i: (i, 0)),
        out_shape=_sds((t, 2 * d), BF16),
        compiler_params=_params("parallel"),
    )(y, z, att, gs, ga)
    return out


def _mixnorm_bwd(y, z, att, gs, ga, dycat, name):
    t, d = y.shape
    tm = _tile(t, 256)

    def body(y_ref, z_ref, a_ref, dyc_ref, gs_ref, ga_ref, dy_ref, dz_ref, da_ref, dgs_ref, dga_ref):
        yb, zb, ab = y_ref[...], z_ref[...], a_ref[...]
        dys, dya = dyc_ref[:, :d], dyc_ref[:, d:]
        sz = _sigmoid(zb)
        silu = zb * sz
        yg = yb * silu
        r = _rstd(yg)
        dyg, dgs = _rms_bwd_rows(yg * r, r, gs_ref[...], dys)
        dy_ref[...] = dyg * silu
        dz_ref[...] = (dyg * yb * (sz * (1.0 + zb * (1.0 - sz)))).astype(BF16)
        r2 = _rstd(ab)
        datt, dga = _rms_bwd_rows(ab * r2, r2, ga_ref[...], dya)
        da_ref[...] = datt

        @pl.when(pl.program_id(0) == 0)
        def _():
            dgs_ref[...] = jnp.zeros_like(dgs_ref)
            dga_ref[...] = jnp.zeros_like(dga_ref)

        dgs_ref[...] += dgs
        dga_ref[...] += dga

    row = pl.BlockSpec((tm, d), lambda i: (i, 0))
    vec = pl.BlockSpec((1, d), lambda i: (0, 0))
    return pl.pallas_call(
        body, name=name, grid=(t // tm,),
        in_specs=[row, row, row, pl.BlockSpec((tm, 2 * d), lambda i: (i, 0)), vec, vec],
        out_specs=[row, row, row, vec, vec],
        out_shape=[_sds((t, d)), _sds((t, d), BF16), _sds((t, d)), _sds((1, d)), _sds((1, d))],
        compiler_params=_params("arbitrary"),
    )(y, z, att, dycat, gs, ga)


def _final_loss(x, g, target, name):
    t, d = x.shape

    def fn(xb, tb, gv):
        r = _rstd(xb)
        xhat = xb * r
        err = xhat * gv - tb
        loss = 0.5 * jnp.sum(jnp.mean(err * err, axis=-1, keepdims=True), axis=0, keepdims=True)
        dx, dg = _rms_bwd_rows(xhat, r, gv, err * (1.0 / d))
        return dx, dx, jnp.broadcast_to(loss, (1, LANES)), dg

    dx, dxb, loss, dg = _rowwise(fn, [x, target], [g], [_sds((t, d)), _sds((t, d), BF16)],
                                 [_sds((1, LANES)), _sds((1, d))], name)
    return loss, dx, dxb, dg


def _adamw(w, g, m, v, name):
    c1 = 1.0 - ADAM_B1 ** ADAM_STEP
    c2 = 1.0 - ADAM_B2 ** ADAM_STEP

    def fn(wb, gb, mb, vb):
        mn = ADAM_B1 * mb + (1.0 - ADAM_B1) * gb
        vn = ADAM_B2 * vb + (1.0 - ADAM_B2) * (gb * gb)
        delta = -ADAM_LR * ((mn / c1) / (jnp.sqrt(vn / c2) + ADAM_EPS) + ADAM_WD * wb)
        return delta, mn, vn

    return _rowwise(fn, [w, g, m, v], [], [_sds(w.shape)] * 3, [], name)


CONV_CB = 512


def _shift_down(u, k):
    if k == 0:
        return u
    rows = lax.broadcasted_iota(jnp.int32, u.shape, 0)
    return jnp.where(rows >= k, pltpu.roll(u, k, 0), 0.0)


def _shift_up(u, k):
    if k == 0:
        return u
    n = u.shape[0]
    rows = lax.broadcasted_iota(jnp.int32, u.shape, 0)
    return jnp.where(rows < n - k, pltpu.roll(u, n - k, 0), 0.0)


def _conv_pre(u, w, b):
    pre = b
    for j in range(CONV_WIDTH):
        pre = pre + w[j:j + 1, :] * _shift_down(u, CONV_WIDTH - 1 - j)
    return pre


def _conv_fwd(proj, w, b, n_seq, name):
    t = proj.shape[0]
    seq = t // n_seq
    off = D_MODEL // CONV_CB

    def body(u_ref, w_ref, b_ref, o_ref):
        pre = _conv_pre(u_ref[...], w_ref[...], b_ref[...])
        o_ref[...] = pre * _sigmoid(pre)

    return pl.pallas_call(
        body, name=name, grid=(n_seq, D_CONV // CONV_CB),
        in_specs=[pl.BlockSpec((seq, CONV_CB), lambda s, c: (s, c + off)),
                  pl.BlockSpec((CONV_WIDTH, CONV_CB), lambda s, c: (0, c)),
                  pl.BlockSpec((1, CONV_CB), lambda s, c: (0, c))],
        out_specs=pl.BlockSpec((seq, CONV_CB), lambda s, c: (s, c)),
        out_shape=_sds((t, D_CONV)),
        compiler_params=_params("parallel", "parallel"),
    )(proj, w, b)


def _conv_bwd(proj, w, b, dxbc, n_seq, name):
    t = proj.shape[0]
    seq = t // n_seq
    off = D_MODEL // CONV_CB

    def body(u_ref, w_ref, b_ref, d_ref, du_ref, wg_ref):
        u, wv = u_ref[...], w_ref[...]
        taps = [_shift_down(u, CONV_WIDTH - 1 - j) for j in range(CONV_WIDTH)]
        pre = b_ref[...]
        for j in range(CONV_WIDTH):
            pre = pre + wv[j:j + 1, :] * taps[j]
        s = _sigmoid(pre)
        dpre = d_ref[...] * (s * (1.0 + pre * (1.0 - s)))
        du = jnp.zeros_like(u)
        parts = []
        for j in range(CONV_WIDTH):
            du = du + wv[j:j + 1, :] * _shift_up(dpre, CONV_WIDTH - 1 - j)
            parts.append(_colsum(dpre * taps[j]))
        du_ref[...] = du.astype(BF16)
        parts.append(_colsum(dpre))

        @pl.when(pl.program_id(1) == 0)
        def _():
            wg_ref[...] = jnp.zeros_like(wg_ref)

        wg_ref[...] += _stack_rows(parts, u.shape[1])

    return pl.pallas_call(
        body, name=name, grid=(D_CONV // CONV_CB, n_seq),
        in_specs=[pl.BlockSpec((seq, CONV_CB), lambda c, s: (s, c + off)),
                  pl.BlockSpec((CONV_WIDTH, CONV_CB), lambda c, s: (0, c)),
                  pl.BlockSpec((1, CONV_CB), lambda c, s: (0, c)),
                  pl.BlockSpec((seq, CONV_CB), lambda c, s: (s, c))],
        out_specs=[pl.BlockSpec((seq, CONV_CB), lambda c, s: (s, c)),
                   pl.BlockSpec((8, CONV_CB), lambda c, s: (0, c))],
        out_shape=[_sds((t, D_CONV), BF16), _sds((8, D_CONV))],
        compiler_params=_params("parallel", "arbitrary"),
    )(proj, w, b, dxbc)


def _iota2(shape, axis):
    return lax.broadcasted_iota(jnp.int32, shape, axis)


def _dot(a, b, dims=_DIMS["nn"], precision=None):
    return lax.dot_general(a, b, dims, precision=precision, preferred_element_type=F32)


def _bdot(a, b, dims=_DIMS["nn"]):
    return lax.dot_general(a.astype(BF16), b.astype(BF16), dims, preferred_element_type=F32)


def _expand_mat():
    return (_iota2((LANES, D_MODEL), 0) == lax.shift_right_logical(_iota2((LANES, D_MODEL), 1), HEAD_SHIFT)).astype(F32)


def _reduce_mat():
    return (lax.shift_right_logical(_iota2((D_MODEL, LANES), 0), HEAD_SHIFT) == _iota2((D_MODEL, LANES), 1)).astype(F32)


def _ssd_decay(dtraw, bias, alog):
    row, col = _iota2((CHUNK, CHUNK), 0), _iota2((CHUNK, CHUNK), 1)
    pre = dtraw + bias
    dtb = jnp.maximum(pre, 0.0) + jnp.log(1.0 + jnp.exp(-jnp.abs(pre)))
    a_neg = -jnp.exp(alog)
    a = dtb * a_neg
    tril = (row >= col).astype(F32)
    triu = (row <= col).astype(F32)
    cs = _dot(tril, a, precision=HIGHEST)
    cs_t = _dot(a, triu, _DIMS["tn"], precision=HIGHEST)
    return pre, dtb, a_neg, cs, cs_t


def _pair_rowscale(vec, h0):
    top = _iota2((CHUNK, LANES), 0) < HEAD_DIM
    return jnp.where(top, vec[:, h0:h0 + 1], vec[:, h0 + 1:h0 + 2])


def _decay_mat(cs, cs_t, h):
    row, col = _iota2((CHUNK, CHUNK), 0), _iota2((CHUNK, CHUNK), 1)
    seg = cs[:, h:h + 1] - cs_t[h:h + 1, :]
    return jnp.exp(jnp.where(row >= col, seg, -jnp.inf))


def _ssd_fwd(xbc, dtraw, bias, alog, dskip_x, n_seq, name):
    t = xbc.shape[0]
    nc = t // n_seq // CHUNK

    def body(x_ref, b_ref, c_ref, dt_ref, bias_ref, alog_ref, dsk_ref, y_ref, st_ref, h_ref):
        @pl.when(pl.program_id(1) == 0)
        def _():
            h_ref[...] = jnp.zeros_like(h_ref)

        _, dtb, _, cs, cs_t = _ssd_decay(dt_ref[...], bias_ref[...], alog_ref[...])
        expand = _expand_mat()
        dt_x = _dot(dtb, expand, precision=HIGHEST)
        cs_x = _dot(cs, expand, precision=HIGHEST)
        tot = cs[CHUNK - 1:CHUNK, :]
        etot = jnp.exp(tot)
        x = x_ref[...]
        xdt = x * dt_x
        e_x = jnp.exp(cs_x)
        xdec = xdt * jnp.exp(cs_x[CHUNK - 1:CHUNK, :] - cs_x)
        keeps = [_iota2((CHUNK, LANES), 1) < HEAD_DIM, _iota2((CHUNK, LANES), 1) >= HEAD_DIM]
        for g in range(N_GROUPS):
            glanes = slice(2 * g * LANES, (2 * g + 2) * LANES)
            bg = b_ref[:, g * N_STATE:(g + 1) * N_STATE]
            cg = c_ref[:, g * N_STATE:(g + 1) * N_STATE]
            gmat = _bdot(cg, bg, _DIMS["nt"])
            prev = h_ref[2 * g:2 * g + 2].reshape(2 * LANES, N_STATE)
            st_ref[0, 2 * g:2 * g + 2] = prev.reshape(2, LANES, N_STATE)
            yoff = _bdot(cg, prev, _DIMS["nt"]) * e_x[:, glanes]
            new = _bdot(xdec[:, glanes], bg, _DIMS["tn"])
            scale = jnp.concatenate([_pair_rowscale(etot, 2 * q) for q in (2 * g, 2 * g + 1)], axis=0)
            h_ref[2 * g:2 * g + 2] = (prev * scale + new).reshape(2, LANES, N_STATE)
            for q in (2 * g, 2 * g + 1):
                lanes = slice(q * LANES, (q + 1) * LANES)
                xdt_q = xdt[:, lanes]
                w_cat = jnp.concatenate([gmat * _decay_mat(cs, cs_t, 2 * q + r) for r in range(2)], axis=1)
                x_cat = jnp.concatenate([jnp.where(keeps[r], xdt_q, 0.0) for r in range(2)], axis=0)
                y_ref[:, lanes] = (_bdot(w_cat, x_cat) + yoff[:, lanes.start - glanes.start:lanes.stop - glanes.start]
                                   + x[:, lanes] * dsk_ref[:, lanes])

    vec = pl.BlockSpec((1, LANES), lambda s, c: (0, 0))
    return pl.pallas_call(
        body, name=name, grid=(n_seq, nc),
        in_specs=[pl.BlockSpec((CHUNK, D_MODEL), lambda s, c: (s * nc + c, 0)),
                  pl.BlockSpec((CHUNK, N_GROUPS * N_STATE), lambda s, c: (s * nc + c, 2)),
                  pl.BlockSpec((CHUNK, N_GROUPS * N_STATE), lambda s, c: (s * nc + c, 3)),
                  pl.BlockSpec((CHUNK, LANES), lambda s, c: (s * nc + c, 0)),
                  vec, vec, pl.BlockSpec((1, D_MODEL), lambda s, c: (0, 0))],
        out_specs=[pl.BlockSpec((CHUNK, D_MODEL), lambda s, c: (s * nc + c, 0)),
                   pl.BlockSpec((1, N_PAIRS, LANES, N_STATE), lambda s, c: (s * nc + c, 0, 0, 0))],
        out_shape=[_sds((t, D_MODEL)), _sds((t // CHUNK, N_PAIRS, LANES, N_STATE))],
        scratch_shapes=[pltpu.VMEM((N_PAIRS, LANES, N_STATE), F32)],
        compiler_params=_params("parallel", "arbitrary"),
    )(xbc, xbc, xbc, dtraw, bias, alog, dskip_x)


def _ssd_bwd(xbc, dtraw, bias, alog, dskip_x, states, dy, n_seq, name):
    t = xbc.shape[0]
    nc = t // n_seq // CHUNK

    def body(x_ref, b_ref, c_ref, dt_ref, bias_ref, alog_ref, dsk_ref, st_ref, dy_ref,
             dxbc_ref, ddt_ref, pg_ref, dh_ref):
        first = jnp.logical_and(pl.program_id(0) == 0, pl.program_id(1) == 0)

        @pl.when(first)
        def _():
            pg_ref[...] = jnp.zeros_like(pg_ref)

        @pl.when(pl.program_id(1) == 0)
        def _():
            dh_ref[...] = jnp.zeros_like(dh_ref)

        row, col = _iota2((CHUNK, CHUNK), 0), _iota2((CHUNK, CHUNK), 1)
        pre, dtb, a_neg, cs, cs_t = _ssd_decay(dt_ref[...], bias_ref[...], alog_ref[...])
        expand, reduce = _expand_mat(), _reduce_mat()
        dt_x = _dot(dtb, expand, precision=HIGHEST)
        cs_x = _dot(cs, expand, precision=HIGHEST)
        etot = jnp.exp(cs[CHUNK - 1:CHUNK, :])
        x, dy = x_ref[...], dy_ref[...]
        xdt = x * dt_x
        e_x = jnp.exp(cs_x)
        dec_x = jnp.exp(cs_x[CHUNK - 1:CHUNK, :] - cs_x)
        xdec = xdt * dec_x
        dye = dy * e_x
        keeps = [_iota2((CHUNK, LANES), 1) < HEAD_DIM, _iota2((CHUNK, LANES), 1) >= HEAD_DIM]
        dcs_col = jnp.zeros((CHUNK, LANES), F32)
        dcs_row = jnp.zeros((LANES, CHUNK), F32)
        dtot = jnp.zeros((1, LANES), F32)
        dxdt_parts, zdec_parts, yoff_parts = [], [], []
        for g in range(N_GROUPS):
            bg = b_ref[:, g * N_STATE:(g + 1) * N_STATE]
            cg = c_ref[:, g * N_STATE:(g + 1) * N_STATE]
            gmat = _bdot(cg, bg, _DIMS["nt"])
            dgmat = jnp.zeros((CHUNK, CHUNK), F32)
            glanes = slice(2 * g * LANES, (2 * g + 2) * LANES)
            prev = st_ref[0, 2 * g:2 * g + 2].reshape(2 * LANES, N_STATE)
            dht = dh_ref[2 * g:2 * g + 2].reshape(2 * LANES, N_STATE)
            dxdtdec = _bdot(bg, dht, _DIMS["nt"])
            zdec_parts.append(dxdtdec * xdec[:, glanes])
            dbg = _bdot(xdec[:, glanes], dht)
            yoff_parts.append(dy[:, glanes] * (_bdot(cg, prev, _DIMS["nt"]) * e_x[:, glanes]))
            dcg = _bdot(dye[:, glanes], prev)
            dprev = _bdot(dye[:, glanes], cg, _DIMS["tn"])
            hp = jnp.sum(dht * prev, axis=1, keepdims=True)
            rows4 = lax.shift_right_logical(_iota2((2 * LANES, 1), 0), HEAD_SHIFT)
            lane1 = _iota2((1, LANES), 1)
            for k in range(4):
                dk_tot = jnp.sum(jnp.where(rows4 == k, hp, 0.0), axis=0, keepdims=True)
                dtot = dtot + jnp.where(lane1 == 4 * g + k, dk_tot, 0.0)
            scale = jnp.concatenate([_pair_rowscale(etot, 2 * q) for q in (2 * g, 2 * g + 1)], axis=0)
            dh_ref[2 * g:2 * g + 2] = (dprev + dht * scale).reshape(2, LANES, N_STATE)
            for q in (2 * g, 2 * g + 1):
                lanes = slice(q * LANES, (q + 1) * LANES)
                local = slice(lanes.start - glanes.start, lanes.stop - glanes.start)
                xdt_q, dy_q = xdt[:, lanes], dy[:, lanes]
                lms = [_decay_mat(cs, cs_t, 2 * q + r) for r in range(2)]
                ws = [gmat * lm for lm in lms]
                dy_cat = jnp.concatenate([jnp.where(keeps[r], dy_q, 0.0) for r in range(2)], axis=0)
                dm_cat = _bdot(dy_cat, xdt_q, _DIMS["nt"])
                dxdt_q = _bdot(jnp.concatenate(ws, axis=0), dy_cat, _DIMS["tn"])
                for r in range(2):
                    h = 2 * q + r
                    dm = dm_cat[r * CHUNK:(r + 1) * CHUNK]
                    dgmat = dgmat + dm * lms[r]
                    tmat = dm * ws[r]
                    dcs_col = dcs_col + jnp.where(col == h, jnp.sum(tmat, axis=1, keepdims=True), 0.0)
                    dcs_row = dcs_row - jnp.where(row == h, jnp.sum(tmat, axis=0, keepdims=True), 0.0)
                dxdt_parts.append(dxdt_q + dxdtdec[:, local] * dec_x[:, lanes])
            dcg = dcg + _bdot(dgmat, bg)
            dbg = dbg + _bdot(dgmat, cg, _DIMS["tn"])
            dxbc_ref[:, D_MODEL + g * N_STATE:D_MODEL + (g + 1) * N_STATE] = dbg
            dxbc_ref[:, D_MODEL + (N_GROUPS + g) * N_STATE:D_MODEL + (N_GROUPS + g + 1) * N_STATE] = dcg
        dxdt = jnp.concatenate(dxdt_parts, axis=1)
        zdec = _dot(jnp.concatenate(zdec_parts, axis=1), reduce, precision=HIGHEST)
        yoff_d = _dot(jnp.concatenate(yoff_parts, axis=1), reduce, precision=HIGHEST)
        dtot = dtot * etot + _colsum(zdec)
        last = row[:, :LANES] == CHUNK - 1
        dcs_col = dcs_col + yoff_d - zdec + jnp.where(last, dtot, 0.0)
        triu = (row <= col).astype(F32)
        da = _dot(triu, dcs_col, precision=HIGHEST) + _dot(triu, dcs_row, _DIMS["nt"], precision=HIGHEST)
        ddt = _dot(dxdt * x, reduce, precision=HIGHEST) + da * a_neg
        ddtraw = ddt * _sigmoid(pre)
        ddt_ref[...] = ddtraw.astype(BF16)
        dxbc_ref[:, :D_MODEL] = dxdt * dt_x + dy * dsk_ref[...]
        dskip = _dot(jnp.broadcast_to(_colsum(dy * x), (8, D_MODEL)), reduce, precision=HIGHEST)[0:1, :]
        pg_ref[...] += _stack_rows([_colsum(ddtraw), _colsum(da * dtb) * a_neg, dskip], LANES)

    vec = pl.BlockSpec((1, LANES), lambda s, c: (0, 0))

    def blk(s, c):
        return s * nc + (nc - 1 - c)

    return pl.pallas_call(
        body, name=name, grid=(n_seq, nc),
        in_specs=[pl.BlockSpec((CHUNK, D_MODEL), lambda s, c: (blk(s, c), 0)),
                  pl.BlockSpec((CHUNK, N_GROUPS * N_STATE), lambda s, c: (blk(s, c), 2)),
                  pl.BlockSpec((CHUNK, N_GROUPS * N_STATE), lambda s, c: (blk(s, c), 3)),
                  pl.BlockSpec((CHUNK, LANES), lambda s, c: (blk(s, c), 0)),
                  vec, vec, pl.BlockSpec((1, D_MODEL), lambda s, c: (0, 0)),
                  pl.BlockSpec((1, N_PAIRS, LANES, N_STATE), lambda s, c: (blk(s, c), 0, 0, 0)),
                  pl.BlockSpec((CHUNK, D_MODEL), lambda s, c: (blk(s, c), 0))],
        out_specs=[pl.BlockSpec((CHUNK, D_CONV), lambda s, c: (blk(s, c), 0)),
                   pl.BlockSpec((CHUNK, LANES), lambda s, c: (blk(s, c), 0)),
                   pl.BlockSpec((8, LANES), lambda s, c: (0, 0))],
        out_shape=[_sds((t, D_CONV)), _sds((t, LANES), BF16), _sds((8, LANES))],
        scratch_shapes=[pltpu.VMEM((N_PAIRS, LANES, N_STATE), F32)],
        compiler_params=_params("arbitrary", "arbitrary"),
    )(xbc, xbc, xbc, dtraw, bias, alog, dskip_x, states, dy)


Q_COL = (D_MODEL + D_CONV) // LANES
K_COL = Q_COL + D_MODEL // LANES
V_COL = K_COL + D_MODEL // LANES
ATT_SCALE = HEAD_DIM ** -0.5


ATT_SUB = 2
ATT_TILE = ATT_SUB * CHUNK
ATT_NP = 2


ATT_ROWS = 2 * ATT_TILE


def _tri_dot(x, tri):
    return _dot(x.astype(BF16), tri)


def _att_stack(ref, lanes, keeps, scale=1.0):
    parts = []
    for a in range(ATT_SUB):
        blk = ref[a * CHUNK:(a + 1) * CHUNK, lanes] * scale
        parts += [jnp.where(keeps[r], blk, 0.0).astype(BF16) for r in range(2)]
    return jnp.concatenate(parts, axis=0)


def _att_unstack(x, a, keeps):
    return jnp.where(keeps[0], x[2 * a * CHUNK:(2 * a + 1) * CHUNK], x[(2 * a + 1) * CHUNK:(2 * a + 2) * CHUNK])


def _att_logits(s, diag):
    lb = jnp.minimum(s, 0.0) - jnp.log(1.0 + jnp.exp(-jnp.abs(s)))
    l1m = lb - s
    mask = None
    if diag:
        srow, scol = _iota2((ATT_ROWS, ATT_TILE), 0), _iota2((ATT_ROWS, ATT_TILE), 1)
        pair_shift = (2 * CHUNK).bit_length() - 1
        qpos = lax.shift_right_logical(srow, pair_shift) * CHUNK + jnp.bitwise_and(srow, CHUNK - 1)
        mask = qpos > scol
        l1m = jnp.where(mask, l1m, 0.0)
    return lb, l1m, mask


def _att_fwd(proj, n_seq, name, riders=()):
    t = proj.shape[0]
    seq = t // n_seq
    nq = seq // ATT_TILE
    n_ride = len(riders)
    n_steps = n_seq * (N_PAIRS // ATT_NP) * nq

    def body(q_ref, k_ref, v_ref, *rest):
        s_refs, (o_ref, rt_ref), g_refs = rest[:n_ride], rest[n_ride:n_ride + 2], rest[n_ride + 2:2 * n_ride + 2]
        i = pl.program_id(2)
        if n_ride:
            send_sems, recv_sems = rest[-2:]
            step = (pl.program_id(0) * (N_PAIRS // ATT_NP) + pl.program_id(1)) * nq + i
            copies = [_gather_copies(s_refs[k], g_refs[k], send_sems, recv_sems, 6 * k) for k in range(n_ride)]

            @pl.when(step == 0)
            def _():
                for sends, _, _, _ in copies:
                    for cp in sends:
                        cp.start()

            @pl.when(step == (3 * n_steps) // 4)
            def _():
                for _, landed, passed, _ in copies:
                    for got, onward in zip(landed, passed):
                        got.wait_recv()
                        onward.start()

        upper = (_iota2((ATT_TILE, ATT_TILE), 0) > _iota2((ATT_TILE, ATT_TILE), 1)).astype(BF16)
        keeps = [_iota2((CHUNK, LANES), 1) < HEAD_DIM, _iota2((CHUNK, LANES), 1) >= HEAD_DIM]
        pair_lanes = [slice(pr * LANES, (pr + 1) * LANES) for pr in range(ATT_NP)]
        q_stacks = [_att_stack(q_ref, lanes, keeps, ATT_SCALE) for lanes in pair_lanes]
        per_pair = ATT_SUB + 2

        def logits(jt):
            k0 = pl.multiple_of(jt * ATT_TILE, ATT_TILE)
            return [_dot(q_stacks[pr], k_ref[pl.ds(k0, ATT_TILE), lanes].astype(BF16), _DIMS["nt"])
                    for pr, lanes in enumerate(pair_lanes)]

        def tile(jt, carry, diag):
            state = list(carry)
            ahead = logits(jnp.maximum(jt - 1, 0))
            k0 = pl.multiple_of(jt * ATT_TILE, ATT_TILE)
            for pr, lanes in enumerate(pair_lanes):
                base = pr * per_pair
                run = state[base + ATT_SUB]
                v_tile = v_ref[pl.ds(k0, ATT_TILE), lanes].astype(BF16)
                lb, l1m, mask = _att_logits(state[base + ATT_SUB + 1], diag)
                p = jnp.exp(lb + (_tri_dot(l1m, upper) + run))
                if diag:
                    p = jnp.where(mask, p, 0.0)
                pv = _dot(p.astype(BF16), v_tile)
                for a in range(ATT_SUB):
                    state[base + a] = state[base + a] + _att_unstack(pv, a, keeps)
                state[base + ATT_SUB] = run + jnp.sum(l1m, axis=1, keepdims=True)
                state[base + ATT_SUB + 1] = ahead[pr]
            return tuple(state)

        first = logits(i)
        init = ()
        for pr in range(ATT_NP):
            init += tuple(jnp.zeros((CHUNK, LANES), F32) for _ in range(ATT_SUB)) + \
                (jnp.zeros((ATT_ROWS, 1), F32), first[pr])
        carry = tile(i, init, True)
        carry = lax.fori_loop(0, i, lambda it, c: tile(i - 1 - it, c, False), carry)
        for pr, lanes in enumerate(pair_lanes):
            for a in range(ATT_SUB):
                o_ref[a * CHUNK:(a + 1) * CHUNK, lanes] = carry[pr * per_pair + a]
                rt_ref[a * CHUNK:(a + 1) * CHUNK, lanes] = _att_unstack(carry[pr * per_pair + ATT_SUB], a, keeps)

        if n_ride:
            @pl.when(step == n_steps - 1)
            def _():
                for sends, _, passed, arrived in copies:
                    for cp in arrived:
                        cp.wait_recv()
                    for cp in sends + passed:
                        cp.wait_send()

    width = ATT_NP * LANES
    qblk = pl.BlockSpec((ATT_TILE, width), lambda s, p, i: (s * nq + i, p))
    sems = [pltpu.SemaphoreType.DMA((6 * n_ride,)), pltpu.SemaphoreType.DMA((6 * n_ride,))] if n_ride else []
    order = ("arbitrary",) * 3 if n_ride else ("parallel", "parallel", "arbitrary")
    return pl.pallas_call(
        body, name=name, grid=(n_seq, N_PAIRS // ATT_NP, nq),
        in_specs=[pl.BlockSpec((ATT_TILE, width), lambda s, p, i: (s * nq + i, Q_COL // ATT_NP + p)),
                  pl.BlockSpec((seq, width), lambda s, p, i: (s, K_COL // ATT_NP + p)),
                  pl.BlockSpec((seq, width), lambda s, p, i: (s, V_COL // ATT_NP + p))] + [ANY] * n_ride,
        out_specs=[qblk, qblk] + [ANY] * n_ride,
        out_shape=[_sds((t, D_MODEL)), _sds((t, D_MODEL))] +
        [_sds((N_CHIPS,) + r.shape, r.dtype) for r in riders],
        scratch_shapes=sems,
        compiler_params=_params(*order),
    )(proj, proj, proj, *riders)


def _att_bwd(proj, rtot, datt, n_seq, name):
    t = proj.shape[0]
    seq = t // n_seq
    nq = seq // ATT_TILE

    def body(q_ref, k_ref, v_ref, rt_ref, do_ref, dq_ref, dk_out, dv_out, dk_ref, dv_ref):
        i = pl.program_id(2)

        @pl.when(i == 0)
        def _():
            dk_ref[...] = jnp.zeros_like(dk_ref)
            dv_ref[...] = jnp.zeros_like(dv_ref)

        row, col = _iota2((ATT_TILE, ATT_TILE), 0), _iota2((ATT_TILE, ATT_TILE), 1)
        upper = (row > col).astype(BF16)
        before = (row < col).astype(BF16)
        keeps = [_iota2((CHUNK, LANES), 1) < HEAD_DIM, _iota2((CHUNK, LANES), 1) >= HEAD_DIM]
        pair_lanes = [slice(pr * LANES, (pr + 1) * LANES) for pr in range(ATT_NP)]
        q_stacks = [_att_stack(q_ref, lanes, keeps, ATT_SCALE) for lanes in pair_lanes]
        do_stacks = [_att_stack(do_ref, lanes, keeps) for lanes in pair_lanes]
        totals = [jnp.concatenate([rt_ref[a * CHUNK:(a + 1) * CHUNK, pr * LANES + r * HEAD_DIM:pr * LANES + r * HEAD_DIM + 1]
                                   for a in range(ATT_SUB) for r in range(2)], axis=0) for pr in range(ATT_NP)]
        per_pair = ATT_SUB + 4

        def products(jt):
            k0 = pl.multiple_of(jt * ATT_TILE, ATT_TILE)
            return [(_dot(q_stacks[pr], k_ref[pl.ds(k0, ATT_TILE), lanes].astype(BF16), _DIMS["nt"]),
                     _dot(do_stacks[pr], v_ref[pl.ds(k0, ATT_TILE), lanes].astype(BF16), _DIMS["nt"]))
                    for pr, lanes in enumerate(pair_lanes)]

        def tile(jt, carry, diag):
            state = list(carry)
            ahead = None if diag else products(jt + 1)
            k0 = pl.multiple_of(jt * ATT_TILE, ATT_TILE)
            for pr, lanes in enumerate(pair_lanes):
                base = pr * per_pair
                seen, dseen = state[base + ATT_SUB], state[base + ATT_SUB + 1]
                q_stack, do_stack = q_stacks[pr], do_stacks[pr]
                k_tile = k_ref[pl.ds(k0, ATT_TILE), lanes].astype(BF16)
                lb, l1m, mask = _att_logits(state[base + ATT_SUB + 2], diag)
                here = jnp.sum(l1m, axis=1, keepdims=True)
                p = jnp.exp(lb + (_tri_dot(l1m, upper) + (totals[pr] - seen - here)))
                if diag:
                    p = jnp.where(mask, p, 0.0)
                pb = p.astype(BF16)
                dz = state[base + ATT_SUB + 3] * p
                dl1m = dseen + _tri_dot(dz, before)
                sig = jnp.exp(lb)
                ds = dz * (1.0 - sig) - dl1m * sig
                if diag:
                    ds = jnp.where(mask, ds, 0.0)
                dsb = ds.astype(BF16)
                dq_all = _dot(dsb, k_tile)
                for a in range(ATT_SUB):
                    state[base + a] = state[base + a] + _att_unstack(dq_all, a, keeps)
                dk_ref[pl.ds(k0, ATT_TILE), lanes] += _dot(dsb, q_stack, _DIMS["tn"])
                dv_ref[pl.ds(k0, ATT_TILE), lanes] += _dot(pb, do_stack, _DIMS["tn"])
                state[base + ATT_SUB] = seen + here
                state[base + ATT_SUB + 1] = dseen + jnp.sum(dz, axis=1, keepdims=True)
                if ahead is not None:
                    state[base + ATT_SUB + 2], state[base + ATT_SUB + 3] = ahead[pr]
            return tuple(state)

        first = products(0)
        init = ()
        for pr in range(ATT_NP):
            init += tuple(jnp.zeros((CHUNK, LANES), F32) for _ in range(ATT_SUB)) + \
                (jnp.zeros((ATT_ROWS, 1), F32), jnp.zeros((ATT_ROWS, 1), F32)) + first[pr]
        carry = lax.fori_loop(0, i, lambda jt, c: tile(jt, c, False), init)
        carry = tile(i, carry, True)
        for pr, lanes in enumerate(pair_lanes):
            for a in range(ATT_SUB):
                dq_ref[a * CHUNK:(a + 1) * CHUNK, lanes] = (carry[pr * per_pair + a] * ATT_SCALE).astype(BF16)

        @pl.when(i == nq - 1)
        def _():
            dk_out[...] = dk_ref[...].astype(BF16)
            dv_out[...] = dv_ref[...].astype(BF16)

    width = ATT_NP * LANES
    qblk = pl.BlockSpec((ATT_TILE, width), lambda s, p, i: (s * nq + i, p))
    kv_out = pl.BlockSpec((seq, width), lambda s, p, i: (s, p))
    return pl.pallas_call(
        body, name=name, grid=(n_seq, N_PAIRS // ATT_NP, nq),
        in_specs=[pl.BlockSpec((ATT_TILE, width), lambda s, p, i: (s * nq + i, Q_COL // ATT_NP + p)),
                  pl.BlockSpec((seq, width), lambda s, p, i: (s, K_COL // ATT_NP + p)),
                  pl.BlockSpec((seq, width), lambda s, p, i: (s, V_COL // ATT_NP + p)),
                  qblk, qblk],
        out_specs=[qblk, kv_out, kv_out],
        out_shape=[_sds((t, D_MODEL), BF16)] * 3,
        scratch_shapes=[pltpu.VMEM((seq, width), F32), pltpu.VMEM((seq, width), F32)],
        compiler_params=_params("parallel", "parallel", "arbitrary"),
    )(proj, proj, proj, rtot, datt)


def _pad_lanes(v):
    return jnp.pad(v.reshape(1, -1), ((0, 0), (0, LANES - v.shape[0])))


def _add_then_norm(acc, xb, g):
    x1 = xb + acc
    return x1, x1 * _rstd(x1) * g


def _layer_fwd(x, h, p, next_g, n_seq, tag, riders=(), complete=None):
    proj, = _matmul(h, p["w_main"], "nn", [F32], f"in_proj{tag}", tn=1536)
    dtraw, = _matmul(h, p["w_dt"], "nn", [F32], f"dt_proj{tag}")
    xbc = _conv_fwd(proj, p["conv_w"], p["conv_b"], n_seq, f"conv_fwd{tag}")
    y, states = _ssd_fwd(xbc, dtraw, p["dt_bias"], p["a_log"], p["d_skip_x"], n_seq, f"ssd_fwd{tag}")
    att, rtot, *rode = _att_fwd(proj, n_seq, f"att_fwd{tag}", riders)
    if complete is not None:
        p = complete(rode)
    ycat = _mixnorm_fwd(y, proj, att, p["ssd_norm_g"], p["att_norm_g"], f"mixnorm_fwd{tag}")
    x1, h2 = _matmul(ycat, p["w_out"], "nn", [F32, BF16], f"out_proj{tag}", extras=[x], vecs=[p["norm_mlp_g"]],
                     epilogue=_add_then_norm, tn=D_MODEL)
    u, act = _matmul(h2, p["w_up"], "nn", [F32, BF16], f"up_proj{tag}",
                     epilogue=lambda acc: (acc, jnp.square(jnp.maximum(acc, 0.0))))
    if next_g is None:
        x2, = _matmul(act, p["w_down"], "nn", [F32], f"down_proj{tag}", extras=[x1],
                      epilogue=lambda acc, xb: (xb + acc,))
        h_next = None
    else:
        x2, h_next = _matmul(act, p["w_down"], "nn", [F32, BF16], f"down_proj{tag}", extras=[x1], vecs=[next_g],
                             epilogue=_add_then_norm, tn=D_MODEL)
    saved = dict(x=x, h=h, proj=proj, dtraw=dtraw, xbc=xbc, y=y, states=states, att=att, rtot=rtot, ycat=ycat,
                 x1=x1, h2=h2, u=u, act=act)
    return x2, h_next, saved


def _slab(buffers, name, layer, rows, per_chip_rows):
    shape = (N_CHIPS, DEPTH * rows, D_MODEL)
    if per_chip_rows:
        return buffers.get(name), shape, lambda i, j: (i, layer, 0)
    return buffers.get(name), shape, lambda i, j: (j, layer, 0)


def _layer_bwd(dx2, dx2b, p, s, buffers, layer, n_seq, tag):
    g = {}
    buffers["w_down"], = _matmul(s["act"], dx2b, "tn", [F32], f"dw_down{tag}",
                                 slab=_slab(buffers, "w_down", layer, D_FF // N_CHIPS, True), tm=D_FF // N_CHIPS)
    du, = _matmul(dx2b, p["w_down"], "nt", [BF16], f"d_act{tag}", extras=[s["u"]],
                  epilogue=lambda acc, ub: (acc * (2.0 * jnp.maximum(ub, 0.0)),))
    buffers["w_up"], = _matmul(s["h2"], du, "tn", [F32], f"dw_up{tag}",
                               slab=_slab(buffers, "w_up", layer, D_MODEL, False), tn=D_FF // N_CHIPS)
    dh2, = _matmul(du, p["w_up"], "nt", [F32], f"d_h2{tag}")
    dx1, dx1b, g["norm_mlp_g"] = _rms_bwd(s["x1"], p["norm_mlp_g"], dh2, dx2, f"rms_mlp_bwd{tag}")
    buffers["w_out"], = _matmul(s["ycat"], dx1b, "tn", [F32], f"dw_out{tag}",
                                slab=_slab(buffers, "w_out", layer, 2 * D_MODEL // N_CHIPS, True),
                                tm=2 * D_MODEL // N_CHIPS)
    dycat, = _matmul(dx1b, p["w_out"], "nt", [F32], f"d_ycat{tag}")
    dy, dz, datt, g["ssd_norm_g"], g["att_norm_g"] = _mixnorm_bwd(
        s["y"], s["proj"], s["att"], p["ssd_norm_g"], p["att_norm_g"], dycat, f"mixnorm_bwd{tag}")
    dq, dk, dv = _att_bwd(s["proj"], s["rtot"], datt, n_seq, f"att_bwd{tag}")
    dxbc, ddtraw, pg = _ssd_bwd(s["xbc"], s["dtraw"], p["dt_bias"], p["a_log"], p["d_skip_x"], s["states"], dy,
                                n_seq, f"ssd_bwd{tag}")
    g["dt_bias"], g["a_log"], g["d_skip"] = pg[0, :N_HEADS], pg[1, :N_HEADS], pg[2, :N_HEADS]
    du_conv, wg = _conv_bwd(s["proj"], p["conv_w"], p["conv_b"], dxbc, n_seq, f"conv_bwd{tag}")
    g["conv_w"], g["conv_b"] = wg[:CONV_WIDTH], wg[CONV_WIDTH]
    dproj = jnp.concatenate([dz, du_conv, dq, dk, dv], axis=1)
    g["w_main"], = _matmul(s["h"], dproj, "tn", [F32], f"dw_in{tag}", tn=1536)
    g["w_dt"], = _matmul(s["h"], ddtraw, "tn", [F32], f"dw_dt{tag}")
    dh_dt, = _matmul(ddtraw, p["w_dt"], "nt", [F32], f"d_h_dt{tag}")
    dh, = _matmul(dproj, p["w_main"], "nt", [F32], f"d_h{tag}", extras=[dh_dt], epilogue=lambda acc, e: (acc + e,))
    dx, dxb, g["norm_mix_g"] = _rms_bwd(s["x"], p["norm_mix_g"], dh, dx1, f"rms_mix_bwd{tag}")
    return dx, dxb, g


def _split_w_in(w_full):
    c0 = D_MODEL + D_CONV
    main = jnp.concatenate([w_full[:, :c0], w_full[:, c0 + N_HEADS:]], axis=1)
    dt = jnp.pad(w_full[:, c0:c0 + N_HEADS], ((0, 0), (0, LANES - N_HEADS)))
    return main, dt


def _merge_w_in(main, dt):
    c0 = D_MODEL + D_CONV
    return jnp.concatenate([main[:, :c0], dt[:, :N_HEADS], main[:, c0:]], axis=1)


def _prep_layer(l, w_in_full, w_out, w_up, w_down, conv_w, small):
    w_main, w_dt = _split_w_in(w_in_full)
    return dict(
        w_main=w_main, w_dt=w_dt, w_out=w_out, w_up=w_up, w_down=w_down, conv_w=conv_w,
        conv_b=small["conv_b"][l].reshape(1, -1),
        dt_bias=_pad_lanes(small["dt_bias"][l]), a_log=_pad_lanes(small["a_log"][l]),
        d_skip_x=jnp.repeat(small["d_skip"][l], HEAD_DIM).reshape(1, -1),
        norm_mix_g=small["norm_mix_g"][l].reshape(1, -1), ssd_norm_g=small["ssd_norm_g"][l].reshape(1, -1),
        att_norm_g=small["att_norm_g"][l].reshape(1, -1), norm_mlp_g=small["norm_mlp_g"][l].reshape(1, -1),
    )


ANY = pl.BlockSpec(memory_space=pl.ANY)


def _place():
    x, y, c = lax.axis_index("x"), lax.axis_index("y"), lax.axis_index("c")
    return x, y, c, (x, y, 1 - c), [(1 - x, y), (x, 1 - y), (1 - x, 1 - y)]


def _gather_copies(s_ref, o_ref, send_sems, recv_sems, base):
    x, y, c, sibling, chips = _place()
    half = s_ref.shape[0] // 2

    def slab(px, py, hc):
        return o_ref.at[2 * px + py, pl.ds(hc * half, half), :]

    def copy(k, src, dst, to):
        return pltpu.make_async_remote_copy(src_ref=src, dst_ref=dst, send_sem=send_sems.at[base + k],
                                            recv_sem=recv_sems.at[base + k], device_id=to, device_id_type=MESH)

    sends = [copy(k, s_ref.at[pl.ds(c * half, half), :], slab(x, y, c), (px, py, c)) for k, (px, py) in enumerate(chips)]
    landed = [copy(k, slab(px, py, c), slab(px, py, c), (px, py, c)) for k, (px, py) in enumerate(chips)]
    passed = [copy(3 + k, slab(px, py, c), slab(px, py, c), sibling) for k, (px, py) in enumerate(chips)]
    arrived = [copy(3 + k, slab(px, py, 1 - c), slab(px, py, 1 - c), sibling) for k, (px, py) in enumerate(chips)]
    return sends, landed, passed, arrived


def _own_slot(others, shard, chip):
    return lax.dynamic_update_slice(others, shard[None], (chip, 0, 0))


def _all_gather_chips(shard, chip, name):
    r, ccols = shard.shape

    def body(s_ref, o_ref, send_sems, recv_sems):
        sends, landed, passed, arrived = _gather_copies(s_ref, o_ref, send_sems, recv_sems, 0)
        for cp in sends:
            cp.start()
        for got, onward in zip(landed, passed):
            got.wait_recv()
            onward.start()
        for cp in arrived:
            cp.wait_recv()
        for cp in sends + passed:
            cp.wait_send()

    others = pl.pallas_call(
        body, name=name, in_specs=[ANY], out_specs=ANY,
        out_shape=_sds((N_CHIPS, r, ccols), shard.dtype),
        scratch_shapes=[pltpu.SemaphoreType.DMA((6,)), pltpu.SemaphoreType.DMA((6,))],
    )(shard)
    return _own_slot(others, shard, chip)


def _sibling_swap(g, name):
    n, r, ccols = g.shape
    half = r // 2

    def body(g_ref, o_ref, send_sem, recv_sem):
        _, _, c, sibling, _ = _place()
        cp = pltpu.make_async_remote_copy(src_ref=g_ref.at[:, pl.ds((1 - c) * half, half), :], dst_ref=o_ref,
                                          send_sem=send_sem, recv_sem=recv_sem, device_id=sibling, device_id_type=MESH)
        cp.start()
        cp.wait()

    return pl.pallas_call(
        body, name=name, in_specs=[ANY], out_specs=ANY, out_shape=_sds((n, half, ccols), g.dtype),
        scratch_shapes=[pltpu.SemaphoreType.DMA, pltpu.SemaphoreType.DMA],
    )(g)


def _chip_exchange(p, name):
    _, h, ccols = p.shape

    def body(p_ref, o_ref, send_sems, recv_sems):
        _, _, c, _, chips = _place()
        copies = [pltpu.make_async_remote_copy(src_ref=p_ref.at[2 * px + py], dst_ref=o_ref.at[k],
                                               send_sem=send_sems.at[k], recv_sem=recv_sems.at[k],
                                               device_id=(px, py, c), device_id_type=MESH)
                  for k, (px, py) in enumerate(chips)]
        for cp in copies:
            cp.start()
        for cp in copies:
            cp.wait()

    return pl.pallas_call(
        body, name=name, in_specs=[ANY], out_specs=ANY, out_shape=_sds((3, h, ccols), p.dtype),
        scratch_shapes=[pltpu.SemaphoreType.DMA((3,)), pltpu.SemaphoreType.DMA((3,))],
    )(p)


def _sibling_share(full, name):
    r, ccols = full.shape
    h = r // 2

    def body(f_ref, o_ref, send_sem, recv_sem):
        _, _, c, sibling, _ = _place()
        mine = pl.ds(c * h, h)
        cp = pltpu.make_async_remote_copy(src_ref=f_ref.at[mine, :], dst_ref=o_ref.at[mine, :], send_sem=send_sem,
                                          recv_sem=recv_sem, device_id=sibling, device_id_type=MESH)
        cp.start()
        theirs = o_ref.at[pl.ds((1 - c) * h, h), :]
        pltpu.make_async_remote_copy(src_ref=theirs, dst_ref=theirs, send_sem=send_sem, recv_sem=recv_sem,
                                     device_id=sibling, device_id_type=MESH).wait_recv()
        cp.wait_send()

    return pl.pallas_call(
        body, name=name, in_specs=[ANY], out_specs=ANY, out_shape=_sds((r, ccols), full.dtype),
        input_output_aliases={0: 0},
        scratch_shapes=[pltpu.SemaphoreType.DMA, pltpu.SemaphoreType.DMA],
    )(full)


def _add_halves(g, a, c, name):
    n, r, ccols = g.shape
    half = r // 2
    tr = _tile(half, 256)
    nb = half // tr

    def body(c_ref, g_ref, a_ref, o_ref):
        o_ref[...] = (g_ref[...] + a_ref[...]).astype(BF16)

    blk = (1, tr, ccols)
    return pl.pallas_call(
        body, name=name,
        grid_spec=pltpu.PrefetchScalarGridSpec(
            num_scalar_prefetch=1, grid=(n, nb),
            in_specs=[pl.BlockSpec(blk, lambda j, i, c_ref: (j, c_ref[0] * nb + i, 0)),
                      pl.BlockSpec(blk, lambda j, i, c_ref: (j, i, 0))],
            out_specs=pl.BlockSpec(blk, lambda j, i, c_ref: (j, i, 0))),
        out_shape=_sds((n, half, ccols), BF16),
        compiler_params=_params("parallel", "parallel"),
    )(c.reshape(1).astype(jnp.int32), g, a)


def _sum_chips(p, got, chip, c, name):
    _, h, ccols = p.shape
    tr = _tile(h, 256)

    def body(j_ref, h_ref, p_ref, a_ref, b_ref, c_ref, o_ref):
        f32 = [ref[...].astype(F32) for ref in (p_ref, a_ref, b_ref, c_ref)]
        o_ref[...] = ((f32[0] + f32[1]) + f32[2]) + f32[3]

    blk = (1, tr, ccols)

    def slot(k):
        return pl.BlockSpec(blk, lambda i, j_ref, h_ref: (k, i, 0))

    return pl.pallas_call(
        body, name=name,
        grid_spec=pltpu.PrefetchScalarGridSpec(
            num_scalar_prefetch=2, grid=(h // tr,),
            in_specs=[pl.BlockSpec(blk, lambda i, j_ref, h_ref: (j_ref[0], i, 0)), slot(0), slot(1), slot(2)],
            out_specs=pl.BlockSpec(blk, lambda i, j_ref, h_ref: (h_ref[0], i, 0))),
        out_shape=_sds((2, h, ccols)),
        compiler_params=_params("parallel"),
    )(chip.reshape(1).astype(jnp.int32), c.reshape(1).astype(jnp.int32), p, got, got, got)


def _reduce_scatter(g, chip, c, name):
    a = _sibling_swap(g, f"{name}_swap")
    p = _add_halves(g, a, c, f"{name}_add")
    got = _chip_exchange(p, f"{name}_xchg")
    halves = _sum_chips(p, got, chip, c, f"{name}_sum")
    return _sibling_share(halves.reshape(g.shape[1], g.shape[2]), f"{name}_share")


def _all_reduce_small(v, name):
    r = v.shape[0]

    def body(v_ref, o_ref, buf, send_sems, recv_sems):
        x, y, c, _, _ = _place()
        buf[0] = v_ref[...]
        copies = []
        for rel in range(1, 8):
            fx, fy, fc = (rel >> 2) & 1, (rel >> 1) & 1, rel & 1
            peer = (1 - x if fx else x, 1 - y if fy else y, 1 - c if fc else c)
            cp = pltpu.make_async_remote_copy(src_ref=v_ref, dst_ref=buf.at[rel], send_sem=send_sems.at[rel - 1],
                                              recv_sem=recv_sems.at[rel - 1], device_id=peer, device_id_type=MESH)
            cp.start()
            copies.append(cp)
        for cp in copies:
            cp.wait()
        me = 4 * x + 2 * y + c
        acc = buf[jnp.bitwise_xor(me, 0)]
        for src in range(1, 8):
            acc = acc + buf[jnp.bitwise_xor(me, src)]
        o_ref[...] = acc

    vm = pl.BlockSpec(memory_space=pltpu.VMEM)
    return pl.pallas_call(
        body, name=name, in_specs=[vm], out_specs=vm, out_shape=_sds((r, LANES)),
        scratch_shapes=[pltpu.VMEM((8, r, LANES), F32), pltpu.SemaphoreType.DMA((7,)), pltpu.SemaphoreType.DMA((7,))],
    )(v)


WEIGHTS = ["norm_mix_g", "w_in", "conv_w", "conv_b", "dt_bias", "a_log", "d_skip", "ssd_norm_g", "att_norm_g",
           "w_out", "norm_mlp_g", "w_up", "w_down", "final_norm_g"]
BIG = ["w_in", "w_out", "w_up", "w_down"]
SMALL = [n for n in WEIGHTS if n not in BIG]


def _pack(arrays):
    flat = []
    for a in arrays:
        a = a.reshape(-1)
        flat.append(jnp.pad(a, (0, (-a.shape[0]) % LANES)))
    flat = jnp.concatenate(flat)
    flat = jnp.pad(flat, (0, (-flat.shape[0]) % (8 * LANES)))
    return flat.reshape(-1, LANES)


def _unpack(packed, shapes):
    flat, out, pos = packed.reshape(-1), [], 0
    for shp in shapes:
        n = math.prod(shp)
        out.append(flat[pos:pos + n].reshape(shp))
        pos += n + (-n) % LANES
    return out


def _columns_to_shards(g):
    l, r, ccols = g.shape
    return g.reshape(l, r, N_CHIPS, ccols // N_CHIPS).transpose(2, 0, 1, 3).reshape(N_CHIPS, l * r, ccols // N_CHIPS)


def _from_gathered(name, g, l):
    rows = g.shape[1] // DEPTH
    part = g[:, l * rows:(l + 1) * rows, :]
    if name in ("w_in", "w_up", "conv_w"):
        return part.transpose(1, 0, 2).reshape(rows, N_CHIPS * g.shape[2])
    return part.reshape(N_CHIPS * rows, g.shape[2])


def kernel(x, norm_mix_g, w_in, conv_w, conv_b, dt_bias, a_log, d_skip, ssd_norm_g, att_norm_g, w_out, norm_mlp_g, w_up, w_down, final_norm_g, loss_target, m_norm_mix_g, m_w_in, m_conv_w, m_conv_b, m_dt_bias, m_a_log, m_d_skip, m_ssd_norm_g, m_att_norm_g, m_w_out, m_norm_mlp_g, m_w_up, m_w_down, m_final_norm_g, v_norm_mix_g, v_w_in, v_conv_w, v_conv_b, v_dt_bias, v_a_log, v_d_skip, v_ssd_norm_g, v_att_norm_g, v_w_out, v_norm_mlp_g, v_w_up, v_w_down, v_final_norm_g):
    w = dict(norm_mix_g=norm_mix_g, w_in=w_in, conv_w=conv_w, conv_b=conv_b, dt_bias=dt_bias, a_log=a_log,
             d_skip=d_skip, ssd_norm_g=ssd_norm_g, att_norm_g=att_norm_g, w_out=w_out, norm_mlp_g=norm_mlp_g,
             w_up=w_up, w_down=w_down, final_norm_g=final_norm_g)
    m = dict(norm_mix_g=m_norm_mix_g, w_in=m_w_in, conv_w=m_conv_w, conv_b=m_conv_b, dt_bias=m_dt_bias,
             a_log=m_a_log, d_skip=m_d_skip, ssd_norm_g=m_ssd_norm_g, att_norm_g=m_att_norm_g, w_out=m_w_out,
             norm_mlp_g=m_norm_mlp_g, w_up=m_w_up, w_down=m_w_down, final_norm_g=m_final_norm_g)
    v = dict(norm_mix_g=v_norm_mix_g, w_in=v_w_in, conv_w=v_conv_w, conv_b=v_conv_b, dt_bias=v_dt_bias,
             a_log=v_a_log, d_skip=v_d_skip, ssd_norm_g=v_ssd_norm_g, att_norm_g=v_att_norm_g, w_out=v_w_out,
             norm_mlp_g=v_norm_mlp_g, w_up=v_w_up, w_down=v_w_down, final_norm_g=v_final_norm_g)
    n_seq, seq, d = x.shape
    t = n_seq * seq
    chip = 2 * lax.axis_index("x") + lax.axis_index("y")
    core = lax.axis_index("c")

    shards = {n: w[n].astype(BF16).reshape(-1, w[n].shape[-1]) for n in BIG}
    gathered = {"w_in": _all_gather_chips(shards["w_in"], chip, "gather_w_in"),
                "conv_w": _all_gather_chips(conv_w.reshape(-1, conv_w.shape[-1]), chip, "gather_conv_w")}
    late = [n for n in BIG if n != "w_in"]
    layers = []

    def layer_weights(l):
        full = {n: _from_gathered(n, gathered[n], l) if n in gathered else None for n in BIG + ["conv_w"]}
        return _prep_layer(l, *[full[n] for n in BIG + ["conv_w"]], w)

    def complete(rode):
        for n, others in zip(late, rode):
            gathered[n] = _own_slot(others, shards[n], chip)
        layers.extend(layer_weights(l) for l in range(DEPTH))
        return layers[0]

    xs = x.reshape(t, d)
    first = layer_weights(0)
    hs = _rms_fwd(xs, first["norm_mix_g"], "rms_mix_fwd_l0")
    saved = []
    for l in range(DEPTH):
        next_g = w["norm_mix_g"][l + 1].reshape(1, -1) if l + 1 < DEPTH else None
        if l == 0:
            xs, hs, s = _layer_fwd(xs, hs, first, next_g, n_seq, "_l0", [shards[n] for n in late], complete)
        else:
            xs, hs, s = _layer_fwd(xs, hs, layers[l], next_g, n_seq, f"_l{l}")
        saved.append(s)
    loss_vec, dx, dxb, g_final = _final_loss(xs, final_norm_g.reshape(1, d), loss_target.reshape(t, d), "final_loss")
    loss = lax.psum(loss_vec[0, 0], ("x", "y", "c"))

    grads, shard_major = [None] * DEPTH, {}
    for l in reversed(range(DEPTH)):
        dx, dxb, grads[l] = _layer_bwd(dx, dxb, layers[l], saved[l], shard_major, l, n_seq, f"_l{l}")
        grads[l]["w_in"] = _merge_w_in(grads[l].pop("w_main"), grads[l].pop("w_dt"))
    grad_x = dx.reshape(n_seq, seq, d)

    full = {n: jnp.stack([grads[l][n] for l in range(DEPTH)]) for n in SMALL + ["w_in"] if n != "final_norm_g"}
    full["final_norm_g"] = g_final.reshape(d)
    shard_major["w_in"] = _columns_to_shards(full["w_in"])
    g_out = {}
    for n in BIG:
        red = _reduce_scatter(shard_major[n], chip, core, f"rs_{n}")
        g_out[n] = red.reshape(w[n].shape)
    small_sum = _all_reduce_small(_pack([full[n] for n in SMALL]), "allreduce_small")
    small_shapes = [(DEPTH, CONV_WIDTH, D_CONV) if n == "conv_w" else w[n].shape for n in SMALL]
    for n, val in zip(SMALL, _unpack(small_sum, small_shapes)):
        g_out[n] = val
    g_out["conv_w"] = lax.dynamic_slice_in_dim(g_out["conv_w"], chip * conv_w.shape[-1], conv_w.shape[-1], axis=2)

    delta, new_m, new_v = {}, {}, {}
    for n in BIG:
        two_d = (-1, w[n].shape[-1])
        dl, mn, vn = _adamw(w[n].reshape(two_d), g_out[n].reshape(two_d), m[n].reshape(two_d), v[n].reshape(two_d),
                            f"adamw_{n}")
        delta[n], new_m[n], new_v[n] = dl.reshape(w[n].shape), mn.reshape(w[n].shape), vn.reshape(w[n].shape)
    packs = [_pack([src[n] for n in SMALL]) for src in (w, g_out, m, v)]
    shapes = [w[n].shape for n in SMALL]
    for dst, packed in zip((delta, new_m, new_v), _adamw(*packs, "adamw_small")):
        for n, val in zip(SMALL, _unpack(packed, shapes)):
            dst[n] = val

    return (loss, grad_x, *[g_out[n] for n in WEIGHTS], *[delta[n] for n in WEIGHTS],
            *[new_m[n] for n in WEIGHTS], *[new_v[n] for n in WEIGHTS])
```

```python
import math

import jax
import jax.numpy as jnp
from jax import lax
from jax.experimental import pallas as pl
from jax.experimental.pallas import tpu as pltpu

F32 = jnp.float32
BF16 = jnp.bfloat16
HIGHEST = lax.Precision.HIGHEST

D_MODEL = 1024
DEPTH = 4
HEAD_DIM = 64
HEAD_SHIFT = HEAD_DIM.bit_length() - 1
N_HEADS = 16
N_GROUPS = 4
N_STATE = 128
N_PAIRS = N_HEADS // 2
CONV_WIDTH = 4
CHUNK = 128
D_CONV = D_MODEL + 2 * N_GROUPS * N_STATE
D_MAIN = D_MODEL + D_CONV + 3 * D_MODEL
D_IN_PROJ = D_MAIN + N_HEADS
D_FF = 4 * D_MODEL
EPS = 1e-5
LANES = 128
VMEM_LIMIT = 48 * 1024 * 1024

ADAM_LR = 0.001
ADAM_B1 = 0.9
ADAM_B2 = 0.999
ADAM_EPS = 1e-08
ADAM_WD = 0.01
ADAM_STEP = 10

N_CHIPS = 4
MESH = pl.DeviceIdType.MESH


def _tile(n, cap):
    if n <= cap:
        return n
    t = cap
    while t >= 8:
        if n % t == 0:
            return t
        t //= 2
    raise ValueError(f"no tile for {n} under {cap}")


def _params(*sem):
    return pltpu.CompilerParams(dimension_semantics=sem, vmem_limit_bytes=VMEM_LIMIT)


_DIMS = {"nn": (((1,), (0,)), ((), ())), "nt": (((1,), (1,)), ((), ())), "tn": (((0,), (0,)), ((), ()))}


def _matmul(a, b, mode, out_dtypes, name, extras=(), vecs=(), epilogue=None, slab=None, tm=1024, tn=1024, tk=1024):
    if mode == "nn":
        (m, k), (_, n) = a.shape, b.shape
    elif mode == "nt":
        (m, k), (n, _) = a.shape, b.shape
    else:
        (k, m), (_, n) = a.shape, b.shape
    tm, tn, tk = _tile(m, tm), _tile(n, tn), _tile(k, tk)
    nk = k // tk
    if mode == "tn":
        a_spec = pl.BlockSpec((tk, tm), lambda i, j, kk: (kk, i))
    else:
        a_spec = pl.BlockSpec((tm, tk), lambda i, j, kk: (i, kk))
    if mode == "nt":
        b_spec = pl.BlockSpec((tn, tk), lambda i, j, kk: (j, kk))
    else:
        b_spec = pl.BlockSpec((tk, tn), lambda i, j, kk: (kk, j))
    mn_spec = pl.BlockSpec((tm, tn), lambda i, j, kk: (i, j))
    vec_spec = pl.BlockSpec((1, tn), lambda i, j, kk: (0, j))
    n_extra, n_vec, n_out = len(extras), len(vecs), len(out_dtypes)
    n_in = n_extra + n_vec + (1 if slab is not None and slab[0] is not None else 0)
    dims = _DIMS[mode]

    def body(a_ref, b_ref, *rest):
        in_refs, out_refs, acc = rest[:n_extra + n_vec], rest[n_in:n_in + n_out], rest[-1]
        kk = pl.program_id(2)

        @pl.when(kk == 0)
        def _():
            acc[...] = jnp.zeros_like(acc)

        acc[...] += lax.dot_general(a_ref[...].astype(BF16), b_ref[...].astype(BF16), dims,
                                    preferred_element_type=F32)

        @pl.when(kk == nk - 1)
        def _():
            res = acc[...]
            outs = epilogue(res, *[e[...] for e in in_refs]) if epilogue is not None else (res,)
            for o_ref, val in zip(out_refs, outs):
                o_ref[...] = val.astype(o_ref.dtype).reshape(o_ref.shape)

    in_specs = [a_spec, b_spec] + [mn_spec] * n_extra + [vec_spec] * n_vec
    operands = [a, b, *extras, *vecs]
    out_specs = [mn_spec] * n_out
    out_shape = [jax.ShapeDtypeStruct((m, n), dt) for dt in out_dtypes]
    aliases = {}
    if slab is not None:
        buffer, shape, index = slab
        out_specs = [pl.BlockSpec((1, tm, tn), lambda i, j, kk: index(i, j))]
        out_shape = [jax.ShapeDtypeStruct(shape, out_dtypes[0])]
        if buffer is not None:
            in_specs.append(ANY)
            operands.append(buffer)
            aliases = {len(operands) - 1: 0}
    outs = pl.pallas_call(
        body, name=name, grid=(m // tm, n // tn, nk),
        in_specs=in_specs, out_specs=out_specs, out_shape=out_shape, input_output_aliases=aliases,
        scratch_shapes=[pltpu.VMEM((tm, tn), F32)],
        compiler_params=_params("parallel", "parallel", "arbitrary"),
    )(*operands)
    return tuple(outs)


def _rowwise(fn, rows, vecs, out_rows, out_accs, name, tm=256):
    t = rows[0].shape[0]
    tm = _tile(t, tm)
    n_rows, n_vecs, n_or, n_oa = len(rows), len(vecs), len(out_rows), len(out_accs)

    def body(*refs):
        ins = [r[...] for r in refs[:n_rows + n_vecs]]
        outs = fn(*ins)
        o_refs = refs[n_rows + n_vecs:]
        for o_ref, val in zip(o_refs[:n_or], outs[:n_or]):
            o_ref[...] = val.astype(o_ref.dtype)

        @pl.when(pl.program_id(0) == 0)
        def _():
            for o_ref in o_refs[n_or:]:
                o_ref[...] = jnp.zeros_like(o_ref)

        for o_ref, val in zip(o_refs[n_or:], outs[n_or:]):
            o_ref[...] += val

    outs = pl.pallas_call(
        body, name=name, grid=(t // tm,),
        in_specs=[pl.BlockSpec((tm, r.shape[1]), lambda i: (i, 0)) for r in rows]
        + [pl.BlockSpec(v.shape, lambda i: (0, 0)) for v in vecs],
        out_specs=[pl.BlockSpec((tm, o.shape[1]), lambda i: (i, 0)) for o in out_rows]
        + [pl.BlockSpec(o.shape, lambda i: (0, 0)) for o in out_accs],
        out_shape=list(out_rows) + list(out_accs),
        compiler_params=_params("arbitrary" if n_oa else "parallel"),
    )(*rows, *vecs)
    return tuple(outs)


def _stack_rows(parts, width):
    rows = lax.broadcasted_iota(jnp.int32, (8, width), 0)
    out = jnp.zeros((8, width), F32)
    for j, part in enumerate(parts):
        out = out + jnp.where(rows == j, part, 0.0)
    return out


def _sds(shape, dtype=F32):
    return jax.ShapeDtypeStruct(shape, dtype)


def _sigmoid(x):
    return 1.0 / (1.0 + jnp.exp(-x))


def _colsum(x):
    return jnp.sum(x, axis=0, keepdims=True)


def _rstd(x):
    return lax.rsqrt(jnp.mean(x * x, axis=-1, keepdims=True) + EPS)


def _rms_bwd_rows(xhat, r, g, dy):
    dxh = dy * g
    dx = r * (dxh - xhat * jnp.mean(dxh * xhat, axis=-1, keepdims=True))
    return dx, _colsum(dy * xhat)


def _rms_fwd(x, g, name):
    t, d = x.shape

    def fn(xb, gv):
        return (xb * _rstd(xb) * gv,)

    return _rowwise(fn, [x], [g], [_sds((t, d), BF16)], [], name)[0]


def _rms_bwd(x, g, dh, dres, name):
    t, d = x.shape

    def fn(xb, dhb, dresb, gv):
        r = _rstd(xb)
        dx, dg = _rms_bwd_rows(xb * r, r, gv, dhb)
        dx = dx + dresb
        return dx, dx, dg

    return _rowwise(fn, [x, dh, dres], [g], [_sds((t, d)), _sds((t, d), BF16)], [_sds((1, d))], name)


def _mixnorm_fwd(y, z, att, gs, ga, name):
    t, d = y.shape

    def fn(yb, zb, ab, gsv, gav):
        yg = yb * (zb * _sigmoid(zb))
        return (jnp.concatenate([yg * _rstd(yg) * gsv, ab * _rstd(ab) * gav], axis=1),)

    tm = _tile(t, 256)

    def body(y_ref, z_ref, a_ref, gs_ref, ga_ref, o_ref):
        o_ref[...] = fn(y_ref[...], z_ref[...], a_ref[...], gs_ref[...], ga_ref[...])[0].astype(BF16)

    row = pl.BlockSpec((tm, d), lambda i: (i, 0))
    vec = pl.BlockSpec((1, d), lambda i: (0, 0))
    out = pl.pallas_call(
        body, name=name, grid=(t // tm,),
        in_specs=[row, row, row, vec, vec],
        out_specs=pl.BlockSpec((tm, 2 * d), lambda i: (i, 0)),
        out_shape=_sds((t, 2 * d), BF16),
        compiler_params=_params("parallel"),
    )(y, z, att, gs, ga)
    return out


def _mixnorm_bwd(y, z, att, gs, ga, dycat, name):
    t, d = y.shape
    tm = _tile(t, 256)

    def body(y_ref, z_ref, a_ref, dyc_ref, gs_ref, ga_ref, dy_ref, dz_ref, da_ref, dgs_ref, dga_ref):
        yb, zb, ab = y_ref[...], z_ref[...], a_ref[...]
        dys, dya = dyc_ref[:, :d], dyc_ref[:, d:]
        sz = _sigmoid(zb)
        silu = zb * sz
        yg = yb * silu
        r = _rstd(yg)
        dyg, dgs = _rms_bwd_rows(yg * r, r, gs_ref[...], dys)
        dy_ref[...] = dyg * silu
        dz_ref[...] = (dyg * yb * (sz * (1.0 + zb * (1.0 - sz)))).astype(BF16)
        r2 = _rstd(ab)
        datt, dga = _rms_bwd_rows(ab * r2, r2, ga_ref[...], dya)
        da_ref[...] = datt

        @pl.when(pl.program_id(0) == 0)
        def _():
            dgs_ref[...] = jnp.zeros_like(dgs_ref)
            dga_ref[...] = jnp.zeros_like(dga_ref)

        dgs_ref[...] += dgs
        dga_ref[...] += dga

    row = pl.BlockSpec((tm, d), lambda i: (i, 0))
    vec = pl.BlockSpec((1, d), lambda i: (0, 0))
    return pl.pallas_call(
        body, name=name, grid=(t // tm,),
        in_specs=[row, row, row, pl.BlockSpec((tm, 2 * d), lambda i: (i, 0)), vec, vec],
        out_specs=[row, row, row, vec, vec],
        out_shape=[_sds((t, d)), _sds((t, d), BF16), _sds((t, d)), _sds((1, d)), _sds((1, d))],
        compiler_params=_params("arbitrary"),
    )(y, z, att, dycat, gs, ga)


def _final_loss(x, g, target, name):
    t, d = x.shape

    def fn(xb, tb, gv):
        r = _rstd(xb)
        xhat = xb * r
        err = xhat * gv - tb
        loss = 0.5 * jnp.sum(jnp.mean(err * err, axis=-1, keepdims=True), axis=0, keepdims=True)
        dx, dg = _rms_bwd_rows(xhat, r, gv, err * (1.0 / d))
        return dx, dx, jnp.broadcast_to(loss, (1, LANES)), dg

    dx, dxb, loss, dg = _rowwise(fn, [x, target], [g], [_sds((t, d)), _sds((t, d), BF16)],
                                 [_sds((1, LANES)), _sds((1, d))], name)
    return loss, dx, dxb, dg


def _adamw(w, g, m, v, name):
    c1 = 1.0 - ADAM_B1 ** ADAM_STEP
    c2 = 1.0 - ADAM_B2 ** ADAM_STEP

    def fn(wb, gb, mb, vb):
        mn = ADAM_B1 * mb + (1.0 - ADAM_B1) * gb
        vn = ADAM_B2 * vb + (1.0 - ADAM_B2) * (gb * gb)
        delta = -ADAM_LR * ((mn / c1) / (jnp.sqrt(vn / c2) + ADAM_EPS) + ADAM_WD * wb)
        return delta, mn, vn

    return _rowwise(fn, [w, g, m, v], [], [_sds(w.shape)] * 3, [], name)


CONV_CB = 512


def _shift_down(u, k):
    if k == 0:
        return u
    rows = lax.broadcasted_iota(jnp.int32, u.shape, 0)
    return jnp.where(rows >= k, pltpu.roll(u, k, 0), 0.0)


def _shift_up(u, k):
    if k == 0:
        return u
    n = u.shape[0]
    rows = lax.broadcasted_iota(jnp.int32, u.shape, 0)
    return jnp.where(rows < n - k, pltpu.roll(u, n - k, 0), 0.0)


def _conv_pre(u, w, b):
    pre = b
    for j in range(CONV_WIDTH):
        pre = pre + w[j:j + 1, :] * _shift_down(u, CONV_WIDTH - 1 - j)
    return pre


def _conv_fwd(proj, w, b, n_seq, name):
    t = proj.shape[0]
    seq = t // n_seq
    off = D_MODEL // CONV_CB

    def body(u_ref, w_ref, b_ref, o_ref):
        pre = _conv_pre(u_ref[...], w_ref[...], b_ref[...])
        o_ref[...] = pre * _sigmoid(pre)

    return pl.pallas_call(
        body, name=name, grid=(n_seq, D_CONV // CONV_CB),
        in_specs=[pl.BlockSpec((seq, CONV_CB), lambda s, c: (s, c + off)),
                  pl.BlockSpec((CONV_WIDTH, CONV_CB), lambda s, c: (0, c)),
                  pl.BlockSpec((1, CONV_CB), lambda s, c: (0, c))],
        out_specs=pl.BlockSpec((seq, CONV_CB), lambda s, c: (s, c)),
        out_shape=_sds((t, D_CONV)),
        compiler_params=_params("parallel", "parallel"),
    )(proj, w, b)


def _conv_bwd(proj, w, b, dxbc, n_seq, name):
    t = proj.shape[0]
    seq = t // n_seq
    off = D_MODEL // CONV_CB

    def body(u_ref, w_ref, b_ref, d_ref, du_ref, wg_ref):
        u, wv = u_ref[...], w_ref[...]
        taps = [_shift_down(u, CONV_WIDTH - 1 - j) for j in range(CONV_WIDTH)]
        pre = b_ref[...]
        for j in range(CONV_WIDTH):
            pre = pre + wv[j:j + 1, :] * taps[j]
        s = _sigmoid(pre)
        dpre = d_ref[...] * (s * (1.0 + pre * (1.0 - s)))
        du = jnp.zeros_like(u)
        parts = []
        for j in range(CONV_WIDTH):
            du = du + wv[j:j + 1, :] * _shift_up(dpre, CONV_WIDTH - 1 - j)
            parts.append(_colsum(dpre * taps[j]))
        du_ref[...] = du.astype(BF16)
        parts.append(_colsum(dpre))

        @pl.when(pl.program_id(1) == 0)
        def _():
            wg_ref[...] = jnp.zeros_like(wg_ref)

        wg_ref[...] += _stack_rows(parts, u.shape[1])

    return pl.pallas_call(
        body, name=name, grid=(D_CONV // CONV_CB, n_seq),
        in_specs=[pl.BlockSpec((seq, CONV_CB), lambda c, s: (s, c + off)),
                  pl.BlockSpec((CONV_WIDTH, CONV_CB), lambda c, s: (0, c)),
                  pl.BlockSpec((1, CONV_CB), lambda c, s: (0, c)),
                  pl.BlockSpec((seq, CONV_CB), lambda c, s: (s, c))],
        out_specs=[pl.BlockSpec((seq, CONV_CB), lambda c, s: (s, c)),
                   pl.BlockSpec((8, CONV_CB), lambda c, s: (0, c))],
        out_shape=[_sds((t, D_CONV), BF16), _sds((8, D_CONV))],
        compiler_params=_params("parallel", "arbitrary"),
    )(proj, w, b, dxbc)


def _iota2(shape, axis):
    return lax.broadcasted_iota(jnp.int32, shape, axis)


def _dot(a, b, dims=_DIMS["nn"], precision=None):
    return lax.dot_general(a, b, dims, precision=precision, preferred_element_type=F32)


def _bdot(a, b, dims=_DIMS["nn"]):
    return lax.dot_general(a.astype(BF16), b.astype(BF16), dims, preferred_element_type=F32)


def _expand_mat():
    return (_iota2((LANES, D_MODEL), 0) == lax.shift_right_logical(_iota2((LANES, D_MODEL), 1), HEAD_SHIFT)).astype(F32)


def _reduce_mat():
    return (lax.shift_right_logical(_iota2((D_MODEL, LANES), 0), HEAD_SHIFT) == _iota2((D_MODEL, LANES), 1)).astype(F32)


def _ssd_decay(dtraw, bias, alog):
    row, col = _iota2((CHUNK, CHUNK), 0), _iota2((CHUNK, CHUNK), 1)
    pre = dtraw + bias
    dtb = jnp.maximum(pre, 0.0) + jnp.log(1.0 + jnp.exp(-jnp.abs(pre)))
    a_neg = -jnp.exp(alog)
    a = dtb * a_neg
    tril = (row >= col).astype(F32)
    triu = (row <= col).astype(F32)
    cs = _dot(tril, a, precision=HIGHEST)
    cs_t = _dot(a, triu, _DIMS["tn"], precision=HIGHEST)
    return pre, dtb, a_neg, cs, cs_t


def _pair_rowscale(vec, h0):
    top = _iota2((CHUNK, LANES), 0) < HEAD_DIM
    return jnp.where(top, vec[:, h0:h0 + 1], vec[:, h0 + 1:h0 + 2])


def _decay_mat(cs, cs_t, h):
    row, col = _iota2((CHUNK, CHUNK), 0), _iota2((CHUNK, CHUNK), 1)
    seg = cs[:, h:h + 1] - cs_t[h:h + 1, :]
    return jnp.exp(jnp.where(row >= col, seg, -jnp.inf))


def _ssd_fwd(xbc, dtraw, bias, alog, dskip_x, n_seq, name):
    t = xbc.shape[0]
    nc = t // n_seq // CHUNK

    def body(x_ref, b_ref, c_ref, dt_ref, bias_ref, alog_ref, dsk_ref, y_ref, st_ref, h_ref):
        @pl.when(pl.program_id(1) == 0)
        def _():
            h_ref[...] = jnp.zeros_like(h_ref)

        _, dtb, _, cs, cs_t = _ssd_decay(dt_ref[...], bias_ref[...], alog_ref[...])
        expand = _expand_mat()
        dt_x = _dot(dtb, expand, precision=HIGHEST)
        cs_x = _dot(cs, expand, precision=HIGHEST)
        tot = cs[CHUNK - 1:CHUNK, :]
        etot = jnp.exp(tot)
        x = x_ref[...]
        xdt = x * dt_x
        e_x = jnp.exp(cs_x)
        xdec = xdt * jnp.exp(cs_x[CHUNK - 1:CHUNK, :] - cs_x)
        keeps = [_iota2((CHUNK, LANES), 1) < HEAD_DIM, _iota2((CHUNK, LANES), 1) >= HEAD_DIM]
        for g in range(N_GROUPS):
            glanes = slice(2 * g * LANES, (2 * g + 2) * LANES)
            bg = b_ref[:, g * N_STATE:(g + 1) * N_STATE]
            cg = c_ref[:, g * N_STATE:(g + 1) * N_STATE]
            gmat = _bdot(cg, bg, _DIMS["nt"])
            prev = h_ref[2 * g:2 * g + 2].reshape(2 * LANES, N_STATE)
            st_ref[0, 2 * g:2 * g + 2] = prev.reshape(2, LANES, N_STATE)
            yoff = _bdot(cg, prev, _DIMS["nt"]) * e_x[:, glanes]
            new = _bdot(xdec[:, glanes], bg, _DIMS["tn"])
            scale = jnp.concatenate([_pair_rowscale(etot, 2 * q) for q in (2 * g, 2 * g + 1)], axis=0)
            h_ref[2 * g:2 * g + 2] = (prev * scale + new).reshape(2, LANES, N_STATE)
            for q in (2 * g, 2 * g + 1):
                lanes = slice(q * LANES, (q + 1) * LANES)
                xdt_q = xdt[:, lanes]
                w_cat = jnp.concatenate([gmat * _decay_mat(cs, cs_t, 2 * q + r) for r in range(2)], axis=1)
                x_cat = jnp.concatenate([jnp.where(keeps[r], xdt_q, 0.0) for r in range(2)], axis=0)
                y_ref[:, lanes] = (_bdot(w_cat, x_cat) + yoff[:, lanes.start - glanes.start:lanes.stop - glanes.start]
                                   + x[:, lanes] * dsk_ref[:, lanes])

    vec = pl.BlockSpec((1, LANES), lambda s, c: (0, 0))
    return pl.pallas_call(
        body, name=name, grid=(n_seq, nc),
        in_specs=[pl.BlockSpec((CHUNK, D_MODEL), lambda s, c: (s * nc + c, 0)),
                  pl.BlockSpec((CHUNK, N_GROUPS * N_STATE), lambda s, c: (s * nc + c, 2)),
                  pl.BlockSpec((CHUNK, N_GROUPS * N_STATE), lambda s, c: (s * nc + c, 3)),
                  pl.BlockSpec((CHUNK, LANES), lambda s, c: (s * nc + c, 0)),
                  vec, vec, pl.BlockSpec((1, D_MODEL), lambda s, c: (0, 0))],
        out_specs=[pl.BlockSpec((CHUNK, D_MODEL), lambda s, c: (s * nc + c, 0)),
                   pl.BlockSpec((1, N_PAIRS, LANES, N_STATE), lambda s, c: (s * nc + c, 0, 0, 0))],
        out_shape=[_sds((t, D_MODEL)), _sds((t // CHUNK, N_PAIRS, LANES, N_STATE))],
        scratch_shapes=[pltpu.VMEM((N_PAIRS, LANES, N_STATE), F32)],
        compiler_params=_params("parallel", "arbitrary"),
    )(xbc, xbc, xbc, dtraw, bias, alog, dskip_x)


def _ssd_bwd(xbc, dtraw, bias, alog, dskip_x, states, dy, n_seq, name):
    t = xbc.shape[0]
    nc = t // n_seq // CHUNK

    def body(x_ref, b_ref, c_ref, dt_ref, bias_ref, alog_ref, dsk_ref, st_ref, dy_ref,
             dxbc_ref, ddt_ref, pg_ref, dh_ref):
        first = jnp.logical_and(pl.program_id(0) == 0, pl.program_id(1) == 0)

        @pl.when(first)
        def _():
            pg_ref[...] = jnp.zeros_like(pg_ref)

        @pl.when(pl.program_id(1) == 0)
        def _():
            dh_ref[...] = jnp.zeros_like(dh_ref)

        row, col = _iota2((CHUNK, CHUNK), 0), _iota2((CHUNK, CHUNK), 1)
        pre, dtb, a_neg, cs, cs_t = _ssd_decay(dt_ref[...], bias_ref[...], alog_ref[...])
        expand, reduce = _expand_mat(), _reduce_mat()
        dt_x = _dot(dtb, expand, precision=HIGHEST)
        cs_x = _dot(cs, expand, precision=HIGHEST)
        etot = jnp.exp(cs[CHUNK - 1:CHUNK, :])
        x, dy = x_ref[...], dy_ref[...]
        xdt = x * dt_x
        e_x = jnp.exp(cs_x)
        dec_x = jnp.exp(cs_x[CHUNK - 1:CHUNK, :] - cs_x)
        xdec = xdt * dec_x
        dye = dy * e_x
        keeps = [_iota2((CHUNK, LANES), 1) < HEAD_DIM, _iota2((CHUNK, LANES), 1) >= HEAD_DIM]
        dcs_col = jnp.zeros((CHUNK, LANES), F32)
        dcs_row = jnp.zeros((LANES, CHUNK), F32)
        dtot = jnp.zeros((1, LANES), F32)
        dxdt_parts, zdec_parts, yoff_parts = [], [], []
        for g in range(N_GROUPS):
            bg = b_ref[:, g * N_STATE:(g + 1) * N_STATE]
            cg = c_ref[:, g * N_STATE:(g + 1) * N_STATE]
            gmat = _bdot(cg, bg, _DIMS["nt"])
            dgmat = jnp.zeros((CHUNK, CHUNK), F32)
            glanes = slice(2 * g * LANES, (2 * g + 2) * LANES)
            prev = st_ref[0, 2 * g:2 * g + 2].reshape(2 * LANES, N_STATE)
            dht = dh_ref[2 * g:2 * g + 2].reshape(2 * LANES, N_STATE)
            dxdtdec = _bdot(bg, dht, _DIMS["nt"])
            zdec_parts.append(dxdtdec * xdec[:, glanes])
            dbg = _bdot(xdec[:, glanes], dht)
            yoff_parts.append(dy[:, glanes] * (_bdot(cg, prev, _DIMS["nt"]) * e_x[:, glanes]))
            dcg = _bdot(dye[:, glanes], prev)
            dprev = _bdot(dye[:, glanes], cg, _DIMS["tn"])
            hp = jnp.sum(dht * prev, axis=1, keepdims=True)
            rows4 = lax.shift_right_logical(_iota2((2 * LANES, 1), 0), HEAD_SHIFT)
            lane1 = _iota2((1, LANES), 1)
            for k in range(4):
                dk_tot = jnp.sum(jnp.where(rows4 == k, hp, 0.0), axis=0, keepdims=True)
                dtot = dtot + jnp.where(lane1 == 4 * g + k, dk_tot, 0.0)
            scale = jnp.concatenate([_pair_rowscale(etot, 2 * q) for q in (2 * g, 2 * g + 1)], axis=0)
            dh_ref[2 * g:2 * g + 2] = (dprev + dht * scale).reshape(2, LANES, N_STATE)
            for q in (2 * g, 2 * g + 1):
                lanes = slice(q * LANES, (q + 1) * LANES)
                local = slice(lanes.start - glanes.start, lanes.stop - glanes.start)
                xdt_q, dy_q = xdt[:, lanes], dy[:, lanes]
                lms = [_decay_mat(cs, cs_t, 2 * q + r) for r in range(2)]
                ws = [gmat * lm for lm in lms]
                dy_cat = jnp.concatenate([jnp.where(keeps[r], dy_q, 0.0) for r in range(2)], axis=0)
                dm_cat = _bdot(dy_cat, xdt_q, _DIMS["nt"])
                dxdt_q = _bdot(jnp.concatenate(ws, axis=0), dy_cat, _DIMS["tn"])
                for r in range(2):
                    h = 2 * q + r
                    dm = dm_cat[r * CHUNK:(r + 1) * CHUNK]
                    dgmat = dgmat + dm * lms[r]
                    tmat = dm * ws[r]
                    dcs_col = dcs_col + jnp.where(col == h, jnp.sum(tmat, axis=1, keepdims=True), 0.0)
                    dcs_row = dcs_row - jnp.where(row == h, jnp.sum(tmat, axis=0, keepdims=True), 0.0)
                dxdt_parts.append(dxdt_q + dxdtdec[:, local] * dec_x[:, lanes])
            dcg = dcg + _bdot(dgmat, bg)
            dbg = dbg + _bdot(dgmat, cg, _DIMS["tn"])
            dxbc_ref[:, D_MODEL + g * N_STATE:D_MODEL + (g + 1) * N_STATE] = dbg
            dxbc_ref[:, D_MODEL + (N_GROUPS + g) * N_STATE:D_MODEL + (N_GROUPS + g + 1) * N_STATE] = dcg
        dxdt = jnp.concatenate(dxdt_parts, axis=1)
        zdec = _dot(jnp.concatenate(zdec_parts, axis=1), reduce, precision=HIGHEST)
        yoff_d = _dot(jnp.concatenate(yoff_parts, axis=1), reduce, precision=HIGHEST)
        dtot = dtot * etot + _colsum(zdec)
        last = row[:, :LANES] == CHUNK - 1
        dcs_col = dcs_col + yoff_d - zdec + jnp.where(last, dtot, 0.0)
        triu = (row <= col).astype(F32)
        da = _dot(triu, dcs_col, precision=HIGHEST) + _dot(triu, dcs_row, _DIMS["nt"], precision=HIGHEST)
        ddt = _dot(dxdt * x, reduce, precision=HIGHEST) + da * a_neg
        ddtraw = ddt * _sigmoid(pre)
        ddt_ref[...] = ddtraw.astype(BF16)
        dxbc_ref[:, :D_MODEL] = dxdt * dt_x + dy * dsk_ref[...]
        dskip = _dot(jnp.broadcast_to(_colsum(dy * x), (8, D_MODEL)), reduce, precision=HIGHEST)[0:1, :]
        pg_ref[...] += _stack_rows([_colsum(ddtraw), _colsum(da * dtb) * a_neg, dskip], LANES)

    vec = pl.BlockSpec((1, LANES), lambda s, c: (0, 0))

    def blk(s, c):
        return s * nc + (nc - 1 - c)

    return pl.pallas_call(
        body, name=name, grid=(n_seq, nc),
        in_specs=[pl.BlockSpec((CHUNK, D_MODEL), lambda s, c: (blk(s, c), 0)),
                  pl.BlockSpec((CHUNK, N_GROUPS * N_STATE), lambda s, c: (blk(s, c), 2)),
                  pl.BlockSpec((CHUNK, N_GROUPS * N_STATE), lambda s, c: (blk(s, c), 3)),
                  pl.BlockSpec((CHUNK, LANES), lambda s, c: (blk(s, c), 0)),
                  vec, vec, pl.BlockSpec((1, D_MODEL), lambda s, c: (0, 0)),
                  pl.BlockSpec((1, N_PAIRS, LANES, N_STATE), lambda s, c: (blk(s, c), 0, 0, 0)),
                  pl.BlockSpec((CHUNK, D_MODEL), lambda s, c: (blk(s, c), 0))],
        out_specs=[pl.BlockSpec((CHUNK, D_CONV), lambda s, c: (blk(s, c), 0)),
                   pl.BlockSpec((CHUNK, LANES), lambda s, c: (blk(s, c), 0)),
                   pl.BlockSpec((8, LANES), lambda s, c: (0, 0))],
        out_shape=[_sds((t, D_CONV)), _sds((t, LANES), BF16), _sds((8, LANES))],
        scratch_shapes=[pltpu.VMEM((N_PAIRS, LANES, N_STATE), F32)],
        compiler_params=_params("arbitrary", "arbitrary"),
    )(xbc, xbc, xbc, dtraw, bias, alog, dskip_x, states, dy)


Q_COL = (D_MODEL + D_CONV) // LANES
K_COL = Q_COL + D_MODEL // LANES
V_COL = K_COL + D_MODEL // LANES
ATT_SCALE = HEAD_DIM ** -0.5


ATT_SUB = 2
ATT_TILE = ATT_SUB * CHUNK
ATT_NP = 2


ATT_ROWS = 2 * ATT_TILE


def _tri_dot(x, tri):
    return _dot(x.astype(BF16), tri)


def _att_stack(ref, lanes, keeps, scale=1.0):
    parts = []
    for a in range(ATT_SUB):
        blk = ref[a * CHUNK:(a + 1) * CHUNK, lanes] * scale
        parts += [jnp.where(keeps[r], blk, 0.0).astype(BF16) for r in range(2)]
    return jnp.concatenate(parts, axis=0)


def _att_unstack(x, a, keeps):
    return jnp.where(keeps[0], x[2 * a * CHUNK:(2 * a + 1) * CHUNK], x[(2 * a + 1) * CHUNK:(2 * a + 2) * CHUNK])


def _att_logits(s, diag):
    lb = jnp.minimum(s, 0.0) - jnp.log(1.0 + jnp.exp(-jnp.abs(s)))
    l1m = lb - s
    mask = None
    if diag:
        srow, scol = _iota2((ATT_ROWS, ATT_TILE), 0), _iota2((ATT_ROWS, ATT_TILE), 1)
        pair_shift = (2 * CHUNK).bit_length() - 1
        qpos = lax.shift_right_logical(srow, pair_shift) * CHUNK + jnp.bitwise_and(srow, CHUNK - 1)
        mask = qpos > scol
        l1m = jnp.where(mask, l1m, 0.0)
    return lb, l1m, mask


def _att_fwd(proj, n_seq, name, riders=()):
    t = proj.shape[0]
    seq = t // n_seq
    nq = seq // ATT_TILE
    n_ride = len(riders)
    n_steps = n_seq * (N_PAIRS // ATT_NP) * nq

    def body(q_ref, k_ref, v_ref, *rest):
        s_refs, (o_ref, rt_ref), g_refs = rest[:n_ride], rest[n_ride:n_ride + 2], rest[n_ride + 2:2 * n_ride + 2]
        i = pl.program_id(2)
        if n_ride:
            send_sems, recv_sems = rest[-2:]
            step = (pl.program_id(0) * (N_PAIRS // ATT_NP) + pl.program_id(1)) * nq + i
            copies = [_gather_copies(s_refs[k], g_refs[k], send_sems, recv_sems, 6 * k) for k in range(n_ride)]

            @pl.when(step == 0)
            def _():
                for sends, _, _, _ in copies:
                    for cp in sends:
                        cp.start()

            @pl.when(step == (3 * n_steps) // 4)
            def _():
                for _, landed, passed, _ in copies:
                    for got, onward in zip(landed, passed):
                        got.wait_recv()
                        onward.start()

        upper = (_iota2((ATT_TILE, ATT_TILE), 0) > _iota2((ATT_TILE, ATT_TILE), 1)).astype(BF16)
        keeps = [_iota2((CHUNK, LANES), 1) < HEAD_DIM, _iota2((CHUNK, LANES), 1) >= HEAD_DIM]
        pair_lanes = [slice(pr * LANES, (pr + 1) * LANES) for pr in range(ATT_NP)]
        q_stacks = [_att_stack(q_ref, lanes, keeps, ATT_SCALE) for lanes in pair_lanes]
        per_pair = ATT_SUB + 2

        def logits(jt):
            k0 = pl.multiple_of(jt * ATT_TILE, ATT_TILE)
            return [_dot(q_stacks[pr], k_ref[pl.ds(k0, ATT_TILE), lanes].astype(BF16), _DIMS["nt"])
                    for pr, lanes in enumerate(pair_lanes)]

        def tile(jt, carry, diag):
            state = list(carry)
            ahead = logits(jnp.maximum(jt - 1, 0))
            k0 = pl.multiple_of(jt * ATT_TILE, ATT_TILE)
            for pr, lanes in enumerate(pair_lanes):
                base = pr * per_pair
                run = state[base + ATT_SUB]
                v_tile = v_ref[pl.ds(k0, ATT_TILE), lanes].astype(BF16)
                lb, l1m, mask = _att_logits(state[base + ATT_SUB + 1], diag)
                p = jnp.exp(lb + (_tri_dot(l1m, upper) + run))
                if diag:
                    p = jnp.where(mask, p, 0.0)
                pv = _dot(p.astype(BF16), v_tile)
                for a in range(ATT_SUB):
                    state[base + a] = state[base + a] + _att_unstack(pv, a, keeps)
                state[base + ATT_SUB] = run + jnp.sum(l1m, axis=1, keepdims=True)
                state[base + ATT_SUB + 1] = ahead[pr]
            return tuple(state)

        first = logits(i)
        init = ()
        for pr in range(ATT_NP):
            init += tuple(jnp.zeros((CHUNK, LANES), F32) for _ in range(ATT_SUB)) + \
                (jnp.zeros((ATT_ROWS, 1), F32), first[pr])
        carry = tile(i, init, True)
        carry = lax.fori_loop(0, i, lambda it, c: tile(i - 1 - it, c, False), carry)
        for pr, lanes in enumerate(pair_lanes):
            for a in range(ATT_SUB):
                o_ref[a * CHUNK:(a + 1) * CHUNK, lanes] = carry[pr * per_pair + a]
                rt_ref[a * CHUNK:(a + 1) * CHUNK, lanes] = _att_unstack(carry[pr * per_pair + ATT_SUB], a, keeps)

        if n_ride:
            @pl.when(step == n_steps - 1)
            def _():
                for sends, _, passed, arrived in copies:
                    for cp in arrived:
                        cp.wait_recv()
                    for cp in sends + passed:
                        cp.wait_send()

    width = ATT_NP * LANES
    qblk = pl.BlockSpec((ATT_TILE, width), lambda s, p, i: (s * nq + i, p))
    sems = [pltpu.SemaphoreType.DMA((6 * n_ride,)), pltpu.SemaphoreType.DMA((6 * n_ride,))] if n_ride else []
    order = ("arbitrary",) * 3 if n_ride else ("parallel", "parallel", "arbitrary")
    return pl.pallas_call(
        body, name=name, grid=(n_seq, N_PAIRS // ATT_NP, nq),
        in_specs=[pl.BlockSpec((ATT_TILE, width), lambda s, p, i: (s * nq + i, Q_COL // ATT_NP + p)),
                  pl.BlockSpec((seq, width), lambda s, p, i: (s, K_COL // ATT_NP + p)),
                  pl.BlockSpec((seq, width), lambda s, p, i: (s, V_COL // ATT_NP + p))] + [ANY] * n_ride,
        out_specs=[qblk, qblk] + [ANY] * n_ride,
        out_shape=[_sds((t, D_MODEL)), _sds((t, D_MODEL))] +
        [_sds((N_CHIPS,) + r.shape, r.dtype) for r in riders],
        scratch_shapes=sems,
        compiler_params=_params(*order),
    )(proj, proj, proj, *riders)


def _att_bwd(proj, rtot, datt, n_seq, name, riders=()):
    t = proj.shape[0]
    seq = t // n_seq
    nq = seq // ATT_TILE
    n_ride = len(riders)
    n_steps = n_seq * (N_PAIRS // ATT_NP) * nq

    def body(q_ref, k_ref, v_ref, rt_ref, do_ref, *rest):
        p_refs, (dq_ref, dk_out, dv_out), got_refs = rest[:n_ride], rest[n_ride:n_ride + 3], rest[n_ride + 3:2 * n_ride + 3]
        dk_ref, dv_ref = rest[2 * n_ride + 3:2 * n_ride + 5]
        i = pl.program_id(2)
        if n_ride:
            send_sems, recv_sems = rest[-2:]
            step = (pl.program_id(0) * (N_PAIRS // ATT_NP) + pl.program_id(1)) * nq + i
            _, _, c, _, chips = _place()
            copies = [pltpu.make_async_remote_copy(src_ref=p_refs[k].at[2 * px + py], dst_ref=got_refs[k].at[j],
                                                   send_sem=send_sems.at[3 * k + j], recv_sem=recv_sems.at[3 * k + j],
                                                   device_id=(px, py, c), device_id_type=MESH)
                      for k in range(n_ride) for j, (px, py) in enumerate(chips)]

            @pl.when(step == 0)
            def _():
                for cp in copies:
                    cp.start()

        @pl.when(i == 0)
        def _():
            dk_ref[...] = jnp.zeros_like(dk_ref)
            dv_ref[...] = jnp.zeros_like(dv_ref)

        row, col = _iota2((ATT_TILE, ATT_TILE), 0), _iota2((ATT_TILE, ATT_TILE), 1)
        upper = (row > col).astype(BF16)
        before = (row < col).astype(BF16)
        keeps = [_iota2((CHUNK, LANES), 1) < HEAD_DIM, _iota2((CHUNK, LANES), 1) >= HEAD_DIM]
        pair_lanes = [slice(pr * LANES, (pr + 1) * LANES) for pr in range(ATT_NP)]
        q_stacks = [_att_stack(q_ref, lanes, keeps, ATT_SCALE) for lanes in pair_lanes]
        do_stacks = [_att_stack(do_ref, lanes, keeps) for lanes in pair_lanes]
        totals = [jnp.concatenate([rt_ref[a * CHUNK:(a + 1) * CHUNK, pr * LANES + r * HEAD_DIM:pr * LANES + r * HEAD_DIM + 1]
                                   for a in range(ATT_SUB) for r in range(2)], axis=0) for pr in range(ATT_NP)]
        per_pair = ATT_SUB + 4

        def products(jt):
            k0 = pl.multiple_of(jt * ATT_TILE, ATT_TILE)
            return [(_dot(q_stacks[pr], k_ref[pl.ds(k0, ATT_TILE), lanes].astype(BF16), _DIMS["nt"]),
                     _dot(do_stacks[pr], v_ref[pl.ds(k0, ATT_TILE), lanes].astype(BF16), _DIMS["nt"]))
                    for pr, lanes in enumerate(pair_lanes)]

        def tile(jt, carry, diag):
            state = list(carry)
            ahead = None if diag else products(jt + 1)
            k0 = pl.multiple_of(jt * ATT_TILE, ATT_TILE)
            for pr, lanes in enumerate(pair_lanes):
                base = pr * per_pair
                seen, dseen = state[base + ATT_SUB], state[base + ATT_SUB + 1]
                q_stack, do_stack = q_stacks[pr], do_stacks[pr]
                k_tile = k_ref[pl.ds(k0, ATT_TILE), lanes].astype(BF16)
                lb, l1m, mask = _att_logits(state[base + ATT_SUB + 2], diag)
                here = jnp.sum(l1m, axis=1, keepdims=True)
                p = jnp.exp(lb + (_tri_dot(l1m, upper) + (totals[pr] - seen - here)))
                if diag:
                    p = jnp.where(mask, p, 0.0)
                pb = p.astype(BF16)
                dz = state[base + ATT_SUB + 3] * p
                dl1m = dseen + _tri_dot(dz, before)
                sig = jnp.exp(lb)
                ds = dz * (1.0 - sig) - dl1m * sig
                if diag:
                    ds = jnp.where(mask, ds, 0.0)
                dsb = ds.astype(BF16)
                dq_all = _dot(dsb, k_tile)
                for a in range(ATT_SUB):
                    state[base + a] = state[base + a] + _att_unstack(dq_all, a, keeps)
                dk_ref[pl.ds(k0, ATT_TILE), lanes] += _dot(dsb, q_stack, _DIMS["tn"])
                dv_ref[pl.ds(k0, ATT_TILE), lanes] += _dot(pb, do_stack, _DIMS["tn"])
                state[base + ATT_SUB] = seen + here
                state[base + ATT_SUB + 1] = dseen + jnp.sum(dz, axis=1, keepdims=True)
                if ahead is not None:
                    state[base + ATT_SUB + 2], state[base + ATT_SUB + 3] = ahead[pr]
            return tuple(state)

        first = products(0)
        init = ()
        for pr in range(ATT_NP):
            init += tuple(jnp.zeros((CHUNK, LANES), F32) for _ in range(ATT_SUB)) + \
                (jnp.zeros((ATT_ROWS, 1), F32), jnp.zeros((ATT_ROWS, 1), F32)) + first[pr]
        carry = lax.fori_loop(0, i, lambda jt, c: tile(jt, c, False), init)
        carry = tile(i, carry, True)
        for pr, lanes in enumerate(pair_lanes):
            for a in range(ATT_SUB):
                dq_ref[a * CHUNK:(a + 1) * CHUNK, lanes] = (carry[pr * per_pair + a] * ATT_SCALE).astype(BF16)

        @pl.when(i == nq - 1)
        def _():
            dk_out[...] = dk_ref[...].astype(BF16)
            dv_out[...] = dv_ref[...].astype(BF16)

        if n_ride:
            @pl.when(step == n_steps - 1)
            def _():
                for cp in copies:
                    cp.wait()

    width = ATT_NP * LANES
    qblk = pl.BlockSpec((ATT_TILE, width), lambda s, p, i: (s * nq + i, p))
    kv_out = pl.BlockSpec((seq, width), lambda s, p, i: (s, p))
    sems = [pltpu.SemaphoreType.DMA((3 * n_ride,)), pltpu.SemaphoreType.DMA((3 * n_ride,))] if n_ride else []
    order = ("arbitrary",) * 3 if n_ride else ("parallel", "parallel", "arbitrary")
    return pl.pallas_call(
        body, name=name, grid=(n_seq, N_PAIRS // ATT_NP, nq),
        in_specs=[pl.BlockSpec((ATT_TILE, width), lambda s, p, i: (s * nq + i, Q_COL // ATT_NP + p)),
                  pl.BlockSpec((seq, width), lambda s, p, i: (s, K_COL // ATT_NP + p)),
                  pl.BlockSpec((seq, width), lambda s, p, i: (s, V_COL // ATT_NP + p)),
                  qblk, qblk] + [ANY] * n_ride,
        out_specs=[qblk, kv_out, kv_out] + [ANY] * n_ride,
        out_shape=[_sds((t, D_MODEL), BF16)] * 3 + [_sds((3,) + r.shape[1:], r.dtype) for r in riders],
        scratch_shapes=[pltpu.VMEM((seq, width), F32), pltpu.VMEM((seq, width), F32)] + sems,
        compiler_params=_params(*order),
    )(proj, proj, proj, rtot, datt, *riders)


def _pad_lanes(v):
    return jnp.pad(v.reshape(1, -1), ((0, 0), (0, LANES - v.shape[0])))


def _add_then_norm(acc, xb, g):
    x1 = xb + acc
    return x1, x1 * _rstd(x1) * g


def _layer_fwd(x, h, p, next_g, n_seq, tag, riders=(), complete=None):
    proj, = _matmul(h, p["w_main"], "nn", [F32], f"in_proj{tag}", tn=1536)
    dtraw, = _matmul(h, p["w_dt"], "nn", [F32], f"dt_proj{tag}")
    xbc = _conv_fwd(proj, p["conv_w"], p["conv_b"], n_seq, f"conv_fwd{tag}")
    y, states = _ssd_fwd(xbc, dtraw, p["dt_bias"], p["a_log"], p["d_skip_x"], n_seq, f"ssd_fwd{tag}")
    att, rtot, *rode = _att_fwd(proj, n_seq, f"att_fwd{tag}", riders)
    if complete is not None:
        p = complete(rode)
    ycat = _mixnorm_fwd(y, proj, att, p["ssd_norm_g"], p["att_norm_g"], f"mixnorm_fwd{tag}")
    x1, h2 = _matmul(ycat, p["w_out"], "nn", [F32, BF16], f"out_proj{tag}", extras=[x], vecs=[p["norm_mlp_g"]],
                     epilogue=_add_then_norm, tn=D_MODEL)
    u, act = _matmul(h2, p["w_up"], "nn", [F32, BF16], f"up_proj{tag}",
                     epilogue=lambda acc: (acc, jnp.square(jnp.maximum(acc, 0.0))))
    if next_g is None:
        x2, = _matmul(act, p["w_down"], "nn", [F32], f"down_proj{tag}", extras=[x1],
                      epilogue=lambda acc, xb: (xb + acc,))
        h_next = None
    else:
        x2, h_next = _matmul(act, p["w_down"], "nn", [F32, BF16], f"down_proj{tag}", extras=[x1], vecs=[next_g],
                             epilogue=_add_then_norm, tn=D_MODEL)
    saved = dict(x=x, h=h, proj=proj, dtraw=dtraw, xbc=xbc, y=y, states=states, att=att, rtot=rtot, ycat=ycat,
                 x1=x1, h2=h2, u=u, act=act)
    return x2, h_next, saved


def _slab(buffers, name, layer, rows, per_chip_rows):
    shape = (N_CHIPS, DEPTH * rows, D_MODEL)
    if per_chip_rows:
        return buffers.get(name), shape, lambda i, j: (i, layer, 0)
    return buffers.get(name), shape, lambda i, j: (j, layer, 0)


def _layer_bwd(dx2, dx2b, p, s, buffers, layer, n_seq, tag, ride=None):
    g = {}
    buffers["w_down"], = _matmul(s["act"], dx2b, "tn", [F32], f"dw_down{tag}",
                                 slab=_slab(buffers, "w_down", layer, D_FF // N_CHIPS, True), tm=D_FF // N_CHIPS)
    du, = _matmul(dx2b, p["w_down"], "nt", [BF16], f"d_act{tag}", extras=[s["u"]],
                  epilogue=lambda acc, ub: (acc * (2.0 * jnp.maximum(ub, 0.0)),))
    buffers["w_up"], = _matmul(s["h2"], du, "tn", [F32], f"dw_up{tag}",
                               slab=_slab(buffers, "w_up", layer, D_MODEL, False), tn=D_FF // N_CHIPS)
    dh2, = _matmul(du, p["w_up"], "nt", [F32], f"d_h2{tag}")
    dx1, dx1b, g["norm_mlp_g"] = _rms_bwd(s["x1"], p["norm_mlp_g"], dh2, dx2, f"rms_mlp_bwd{tag}")
    buffers["w_out"], = _matmul(s["ycat"], dx1b, "tn", [F32], f"dw_out{tag}",
                                slab=_slab(buffers, "w_out", layer, 2 * D_MODEL // N_CHIPS, True),
                                tm=2 * D_MODEL // N_CHIPS)
    dycat, = _matmul(dx1b, p["w_out"], "nt", [F32], f"d_ycat{tag}")
    dy, dz, datt, g["ssd_norm_g"], g["att_norm_g"] = _mixnorm_bwd(
        s["y"], s["proj"], s["att"], p["ssd_norm_g"], p["att_norm_g"], dycat, f"mixnorm_bwd{tag}")
    dq, dk, dv, *received = _att_bwd(s["proj"], s["rtot"], datt, n_seq, f"att_bwd{tag}", ride[0]() if ride else ())
    if ride:
        ride[1](received)
    dxbc, ddtraw, pg = _ssd_bwd(s["xbc"], s["dtraw"], p["dt_bias"], p["a_log"], p["d_skip_x"], s["states"], dy,
                                n_seq, f"ssd_bwd{tag}")
    g["dt_bias"], g["a_log"], g["d_skip"] = pg[0, :N_HEADS], pg[1, :N_HEADS], pg[2, :N_HEADS]
    du_conv, wg = _conv_bwd(s["proj"], p["conv_w"], p["conv_b"], dxbc, n_seq, f"conv_bwd{tag}")
    g["conv_w"], g["conv_b"] = wg[:CONV_WIDTH], wg[CONV_WIDTH]
    dproj = jnp.concatenate([dz, du_conv, dq, dk, dv], axis=1)
    g["w_main"], = _matmul(s["h"], dproj, "tn", [F32], f"dw_in{tag}", tn=1536)
    g["w_dt"], = _matmul(s["h"], ddtraw, "tn", [F32], f"dw_dt{tag}")
    dh_dt, = _matmul(ddtraw, p["w_dt"], "nt", [F32], f"d_h_dt{tag}")
    dh, = _matmul(dproj, p["w_main"], "nt", [F32], f"d_h{tag}", extras=[dh_dt], epilogue=lambda acc, e: (acc + e,))
    dx, dxb, g["norm_mix_g"] = _rms_bwd(s["x"], p["norm_mix_g"], dh, dx1, f"rms_mix_bwd{tag}")
    return dx, dxb, g


def _split_w_in(w_full):
    c0 = D_MODEL + D_CONV
    main = jnp.concatenate([w_full[:, :c0], w_full[:, c0 + N_HEADS:]], axis=1)
    dt = jnp.pad(w_full[:, c0:c0 + N_HEADS], ((0, 0), (0, LANES - N_HEADS)))
    return main, dt


def _merge_w_in(main, dt):
    c0 = D_MODEL + D_CONV
    return jnp.concatenate([main[:, :c0], dt[:, :N_HEADS], main[:, c0:]], axis=1)


def _prep_layer(l, w_in_full, w_out, w_up, w_down, conv_w, small):
    w_main, w_dt = _split_w_in(w_in_full)
    return dict(
        w_main=w_main, w_dt=w_dt, w_out=w_out, w_up=w_up, w_down=w_down, conv_w=conv_w,
        conv_b=small["conv_b"][l].reshape(1, -1),
        dt_bias=_pad_lanes(small["dt_bias"][l]), a_log=_pad_lanes(small["a_log"][l]),
        d_skip_x=jnp.repeat(small["d_skip"][l], HEAD_DIM).reshape(1, -1),
        norm_mix_g=small["norm_mix_g"][l].reshape(1, -1), ssd_norm_g=small["ssd_norm_g"][l].reshape(1, -1),
        att_norm_g=small["att_norm_g"][l].reshape(1, -1), norm_mlp_g=small["norm_mlp_g"][l].reshape(1, -1),
    )


ANY = pl.BlockSpec(memory_space=pl.ANY)


def _place():
    x, y, c = lax.axis_index("x"), lax.axis_index("y"), lax.axis_index("c")
    return x, y, c, (x, y, 1 - c), [(1 - x, y), (x, 1 - y), (1 - x, 1 - y)]


def _gather_copies(s_ref, o_ref, send_sems, recv_sems, base):
    x, y, c, sibling, chips = _place()
    half = s_ref.shape[0] // 2

    def slab(px, py, hc):
        return o_ref.at[2 * px + py, pl.ds(hc * half, half), :]

    def copy(k, src, dst, to):
        return pltpu.make_async_remote_copy(src_ref=src, dst_ref=dst, send_sem=send_sems.at[base + k],
                                            recv_sem=recv_sems.at[base + k], device_id=to, device_id_type=MESH)

    sends = [copy(k, s_ref.at[pl.ds(c * half, half), :], slab(x, y, c), (px, py, c)) for k, (px, py) in enumerate(chips)]
    landed = [copy(k, slab(px, py, c), slab(px, py, c), (px, py, c)) for k, (px, py) in enumerate(chips)]
    passed = [copy(3 + k, slab(px, py, c), slab(px, py, c), sibling) for k, (px, py) in enumerate(chips)]
    arrived = [copy(3 + k, slab(px, py, 1 - c), slab(px, py, 1 - c), sibling) for k, (px, py) in enumerate(chips)]
    return sends, landed, passed, arrived


def _own_slot(others, shard, chip):
    return lax.dynamic_update_slice(others, shard[None], (chip, 0, 0))


def _all_gather_chips(shard, chip, name):
    r, ccols = shard.shape

    def body(s_ref, o_ref, send_sems, recv_sems):
        sends, landed, passed, arrived = _gather_copies(s_ref, o_ref, send_sems, recv_sems, 0)
        for cp in sends:
            cp.start()
        for got, onward in zip(landed, passed):
            got.wait_recv()
            onward.start()
        for cp in arrived:
            cp.wait_recv()
        for cp in sends + passed:
            cp.wait_send()

    others = pl.pallas_call(
        body, name=name, in_specs=[ANY], out_specs=ANY,
        out_shape=_sds((N_CHIPS, r, ccols), shard.dtype),
        scratch_shapes=[pltpu.SemaphoreType.DMA((6,)), pltpu.SemaphoreType.DMA((6,))],
    )(shard)
    return _own_slot(others, shard, chip)


def _sibling_swap(g, name):
    n, r, ccols = g.shape
    half = r // 2

    def body(g_ref, o_ref, send_sem, recv_sem):
        _, _, c, sibling, _ = _place()
        cp = pltpu.make_async_remote_copy(src_ref=g_ref.at[:, pl.ds((1 - c) * half, half), :], dst_ref=o_ref,
                                          send_sem=send_sem, recv_sem=recv_sem, device_id=sibling, device_id_type=MESH)
        cp.start()
        cp.wait()

    return pl.pallas_call(
        body, name=name, in_specs=[ANY], out_specs=ANY, out_shape=_sds((n, half, ccols), g.dtype),
        scratch_shapes=[pltpu.SemaphoreType.DMA, pltpu.SemaphoreType.DMA],
    )(g)


def _chip_exchange(p, name):
    _, h, ccols = p.shape

    def body(p_ref, o_ref, send_sems, recv_sems):
        _, _, c, _, chips = _place()
        copies = [pltpu.make_async_remote_copy(src_ref=p_ref.at[2 * px + py], dst_ref=o_ref.at[k],
                                               send_sem=send_sems.at[k], recv_sem=recv_sems.at[k],
                                               device_id=(px, py, c), device_id_type=MESH)
                  for k, (px, py) in enumerate(chips)]
        for cp in copies:
            cp.start()
        for cp in copies:
            cp.wait()

    return pl.pallas_call(
        body, name=name, in_specs=[ANY], out_specs=ANY, out_shape=_sds((3, h, ccols), p.dtype),
        scratch_shapes=[pltpu.SemaphoreType.DMA((3,)), pltpu.SemaphoreType.DMA((3,))],
    )(p)


def _sibling_share(full, name):
    r, ccols = full.shape
    h = r // 2

    def body(f_ref, o_ref, send_sem, recv_sem):
        _, _, c, sibling, _ = _place()
        mine = pl.ds(c * h, h)
        cp = pltpu.make_async_remote_copy(src_ref=f_ref.at[mine, :], dst_ref=o_ref.at[mine, :], send_sem=send_sem,
                                          recv_sem=recv_sem, device_id=sibling, device_id_type=MESH)
        cp.start()
        theirs = o_ref.at[pl.ds((1 - c) * h, h), :]
        pltpu.make_async_remote_copy(src_ref=theirs, dst_ref=theirs, send_sem=send_sem, recv_sem=recv_sem,
                                     device_id=sibling, device_id_type=MESH).wait_recv()
        cp.wait_send()

    return pl.pallas_call(
        body, name=name, in_specs=[ANY], out_specs=ANY, out_shape=_sds((r, ccols), full.dtype),
        input_output_aliases={0: 0},
        scratch_shapes=[pltpu.SemaphoreType.DMA, pltpu.SemaphoreType.DMA],
    )(full)


def _add_halves(g, a, c, name):
    n, r, ccols = g.shape
    half = r // 2
    tr = _tile(half, 256)
    nb = half // tr

    def body(c_ref, g_ref, a_ref, o_ref):
        o_ref[...] = (g_ref[...] + a_ref[...]).astype(BF16)

    blk = (1, tr, ccols)
    return pl.pallas_call(
        body, name=name,
        grid_spec=pltpu.PrefetchScalarGridSpec(
            num_scalar_prefetch=1, grid=(n, nb),
            in_specs=[pl.BlockSpec(blk, lambda j, i, c_ref: (j, c_ref[0] * nb + i, 0)),
                      pl.BlockSpec(blk, lambda j, i, c_ref: (j, i, 0))],
            out_specs=pl.BlockSpec(blk, lambda j, i, c_ref: (j, i, 0))),
        out_shape=_sds((n, half, ccols), BF16),
        compiler_params=_params("parallel", "parallel"),
    )(c.reshape(1).astype(jnp.int32), g, a)


def _sum_chips(p, got, chip, c, name):
    _, h, ccols = p.shape
    tr = _tile(h, 256)

    def body(j_ref, h_ref, p_ref, a_ref, b_ref, c_ref, o_ref):
        f32 = [ref[...].astype(F32) for ref in (p_ref, a_ref, b_ref, c_ref)]
        o_ref[...] = ((f32[0] + f32[1]) + f32[2]) + f32[3]

    blk = (1, tr, ccols)

    def slot(k):
        return pl.BlockSpec(blk, lambda i, j_ref, h_ref: (k, i, 0))

    return pl.pallas_call(
        body, name=name,
        grid_spec=pltpu.PrefetchScalarGridSpec(
            num_scalar_prefetch=2, grid=(h // tr,),
            in_specs=[pl.BlockSpec(blk, lambda i, j_ref, h_ref: (j_ref[0], i, 0)), slot(0), slot(1), slot(2)],
            out_specs=pl.BlockSpec(blk, lambda i, j_ref, h_ref: (h_ref[0], i, 0))),
        out_shape=_sds((2, h, ccols)),
        compiler_params=_params("parallel"),
    )(chip.reshape(1).astype(jnp.int32), c.reshape(1).astype(jnp.int32), p, got, got, got)


def _reduce_scatter_begin(g, c, name):
    return _add_halves(g, _sibling_swap(g, f"{name}_swap"), c, f"{name}_add")


def _reduce_scatter_end(p, got, chip, c, name):
    halves = _sum_chips(p, got, chip, c, f"{name}_sum")
    return _sibling_share(halves.reshape(2 * p.shape[1], p.shape[2]), f"{name}_share")


def _reduce_scatter(g, chip, c, name):
    p = _reduce_scatter_begin(g, c, name)
    return _reduce_scatter_end(p, _chip_exchange(p, f"{name}_xchg"), chip, c, name)


def _all_reduce_small(v, name):
    r = v.shape[0]

    def body(v_ref, o_ref, buf, send_sems, recv_sems):
        x, y, c, _, _ = _place()
        buf[0] = v_ref[...]
        copies = []
        for rel in range(1, 8):
            fx, fy, fc = (rel >> 2) & 1, (rel >> 1) & 1, rel & 1
            peer = (1 - x if fx else x, 1 - y if fy else y, 1 - c if fc else c)
            cp = pltpu.make_async_remote_copy(src_ref=v_ref, dst_ref=buf.at[rel], send_sem=send_sems.at[rel - 1],
                                              recv_sem=recv_sems.at[rel - 1], device_id=peer, device_id_type=MESH)
            cp.start()
            copies.append(cp)
        for cp in copies:
            cp.wait()
        me = 4 * x + 2 * y + c
        acc = buf[jnp.bitwise_xor(me, 0)]
        for src in range(1, 8):
            acc = acc + buf[jnp.bitwise_xor(me, src)]
        o_ref[...] = acc

    vm = pl.BlockSpec(memory_space=pltpu.VMEM)
    return pl.pallas_call(
        body, name=name, in_specs=[vm], out_specs=vm, out_shape=_sds((r, LANES)),
        scratch_shapes=[pltpu.VMEM((8, r, LANES), F32), pltpu.SemaphoreType.DMA((7,)), pltpu.SemaphoreType.DMA((7,))],
    )(v)


WEIGHTS = ["norm_mix_g", "w_in", "conv_w", "conv_b", "dt_bias", "a_log", "d_skip", "ssd_norm_g", "att_norm_g",
           "w_out", "norm_mlp_g", "w_up", "w_down", "final_norm_g"]
BIG = ["w_in", "w_out", "w_up", "w_down"]
SMALL = [n for n in WEIGHTS if n not in BIG]


def _pack(arrays):
    flat = []
    for a in arrays:
        a = a.reshape(-1)
        flat.append(jnp.pad(a, (0, (-a.shape[0]) % LANES)))
    flat = jnp.concatenate(flat)
    flat = jnp.pad(flat, (0, (-flat.shape[0]) % (8 * LANES)))
    return flat.reshape(-1, LANES)


def _unpack(packed, shapes):
    flat, out, pos = packed.reshape(-1), [], 0
    for shp in shapes:
        n = math.prod(shp)
        out.append(flat[pos:pos + n].reshape(shp))
        pos += n + (-n) % LANES
    return out


def _columns_to_shards(g):
    l, r, ccols = g.shape
    return g.reshape(l, r, N_CHIPS, ccols // N_CHIPS).transpose(2, 0, 1, 3).reshape(N_CHIPS, l * r, ccols // N_CHIPS)


def _from_gathered(name, g, l):
    rows = g.shape[1] // DEPTH
    part = g[:, l * rows:(l + 1) * rows, :]
    if name in ("w_in", "w_up", "conv_w"):
        return part.transpose(1, 0, 2).reshape(rows, N_CHIPS * g.shape[2])
    return part.reshape(N_CHIPS * rows, g.shape[2])


def kernel(x, norm_mix_g, w_in, conv_w, conv_b, dt_bias, a_log, d_skip, ssd_norm_g, att_norm_g, w_out, norm_mlp_g, w_up, w_down, final_norm_g, loss_target, m_norm_mix_g, m_w_in, m_conv_w, m_conv_b, m_dt_bias, m_a_log, m_d_skip, m_ssd_norm_g, m_att_norm_g, m_w_out, m_norm_mlp_g, m_w_up, m_w_down, m_final_norm_g, v_norm_mix_g, v_w_in, v_conv_w, v_conv_b, v_dt_bias, v_a_log, v_d_skip, v_ssd_norm_g, v_att_norm_g, v_w_out, v_norm_mlp_g, v_w_up, v_w_down, v_final_norm_g):
    w = dict(norm_mix_g=norm_mix_g, w_in=w_in, conv_w=conv_w, conv_b=conv_b, dt_bias=dt_bias, a_log=a_log,
             d_skip=d_skip, ssd_norm_g=ssd_norm_g, att_norm_g=att_norm_g, w_out=w_out, norm_mlp_g=norm_mlp_g,
             w_up=w_up, w_down=w_down, final_norm_g=final_norm_g)
    m = dict(norm_mix_g=m_norm_mix_g, w_in=m_w_in, conv_w=m_conv_w, conv_b=m_conv_b, dt_bias=m_dt_bias,
             a_log=m_a_log, d_skip=m_d_skip, ssd_norm_g=m_ssd_norm_g, att_norm_g=m_att_norm_g, w_out=m_w_out,
             norm_mlp_g=m_norm_mlp_g, w_up=m_w_up, w_down=m_w_down, final_norm_g=m_final_norm_g)
    v = dict(norm_mix_g=v_norm_mix_g, w_in=v_w_in, conv_w=v_conv_w, conv_b=v_conv_b, dt_bias=v_dt_bias,
             a_log=v_a_log, d_skip=v_d_skip, ssd_norm_g=v_ssd_norm_g, att_norm_g=v_att_norm_g, w_out=v_w_out,
             norm_mlp_g=v_norm_mlp_g, w_up=v_w_up, w_down=v_w_down, final_norm_g=v_final_norm_g)
    n_seq, seq, d = x.shape
    t = n_seq * seq
    chip = 2 * lax.axis_index("x") + lax.axis_index("y")
    core = lax.axis_index("c")

    shards = {n: w[n].astype(BF16).reshape(-1, w[n].shape[-1]) for n in BIG}
    gathered = {"w_in": _all_gather_chips(shards["w_in"], chip, "gather_w_in"),
                "conv_w": _all_gather_chips(conv_w.reshape(-1, conv_w.shape[-1]), chip, "gather_conv_w")}
    late = [n for n in BIG if n != "w_in"]
    layers = []

    def layer_weights(l):
        full = {n: _from_gathered(n, gathered[n], l) if n in gathered else None for n in BIG + ["conv_w"]}
        return _prep_layer(l, *[full[n] for n in BIG + ["conv_w"]], w)

    def complete(rode):
        for n, others in zip(late, rode):
            gathered[n] = _own_slot(others, shards[n], chip)
        layers.extend(layer_weights(l) for l in range(DEPTH))
        return layers[0]

    xs = x.reshape(t, d)
    first = layer_weights(0)
    hs = _rms_fwd(xs, first["norm_mix_g"], "rms_mix_fwd_l0")
    saved = []
    for l in range(DEPTH):
        next_g = w["norm_mix_g"][l + 1].reshape(1, -1) if l + 1 < DEPTH else None
        if l == 0:
            xs, hs, s = _layer_fwd(xs, hs, first, next_g, n_seq, "_l0", [shards[n] for n in late], complete)
        else:
            xs, hs, s = _layer_fwd(xs, hs, layers[l], next_g, n_seq, f"_l{l}")
        saved.append(s)
    loss_vec, dx, dxb, g_final = _final_loss(xs, final_norm_g.reshape(1, d), loss_target.reshape(t, d), "final_loss")
    loss = lax.psum(loss_vec[0, 0], ("x", "y", "c"))

    grads, shard_major, partial, g_out = [None] * DEPTH, {}, {}, {}

    def begin():
        for n in late:
            partial[n] = _reduce_scatter_begin(shard_major[n], core, f"rs_{n}")
        return [partial[n] for n in late]

    def end(received):
        for n, got in zip(late, received):
            g_out[n] = _reduce_scatter_end(partial[n], got, chip, core, f"rs_{n}").reshape(w[n].shape)

    for l in reversed(range(DEPTH)):
        dx, dxb, grads[l] = _layer_bwd(dx, dxb, layers[l], saved[l], shard_major, l, n_seq, f"_l{l}",
                                       (begin, end) if l == 0 else None)
        grads[l]["w_in"] = _merge_w_in(grads[l].pop("w_main"), grads[l].pop("w_dt"))
    grad_x = dx.reshape(n_seq, seq, d)

    full = {n: jnp.stack([grads[l][n] for l in range(DEPTH)]) for n in SMALL + ["w_in"] if n != "final_norm_g"}
    full["final_norm_g"] = g_final.reshape(d)
    g_out["w_in"] = _reduce_scatter(_columns_to_shards(full["w_in"]), chip, core, "rs_w_in").reshape(w_in.shape)
    small_sum = _all_reduce_small(_pack([full[n] for n in SMALL]), "allreduce_small")
    small_shapes = [(DEPTH, CONV_WIDTH, D_CONV) if n == "conv_w" else w[n].shape for n in SMALL]
    for n, val in zip(SMALL, _unpack(small_sum, small_shapes)):
        g_out[n] = val
    g_out["conv_w"] = lax.dynamic_slice_in_dim(g_out["conv_w"], chip * conv_w.shape[-1], conv_w.shape[-1], axis=2)

    delta, new_m, new_v = {}, {}, {}
    for n in BIG:
        two_d = (-1, w[n].shape[-1])
        dl, mn, vn = _adamw(w[n].reshape(two_d), g_out[n].reshape(two_d), m[n].reshape(two_d), v[n].reshape(two_d),
                            f"adamw_{n}")
        delta[n], new_m[n], new_v[n] = dl.reshape(w[n].shape), mn.reshape(w[n].shape), vn.reshape(w[n].shape)
    packs = [_pack([src[n] for n in SMALL]) for src in (w, g_out, m, v)]
    shapes = [w[n].shape for n in SMALL]
    for dst, packed in zip((delta, new_m, new_v), _adamw(*packs, "adamw_small")):
        for n, val in zip(SMALL, _unpack(packed, shapes)):
            dst[n] = val

    return (loss, grad_x, *[g_out[n] for n in WEIGHTS], *[delta[n] for n in WEIGHTS],
            *[new_m[n] for n in WEIGHTS], *[new_v[n] for n in WEIGHTS])
```

```python
import math

import jax
import jax.numpy as jnp
from jax import lax
from jax.experimental import pallas as pl
from jax.experimental.pallas import tpu as pltpu

F32 = jnp.float32
BF16 = jnp.bfloat16
HIGHEST = lax.Precision.HIGHEST

D_MODEL = 1024
DEPTH = 4
HEAD_DIM = 64
HEAD_SHIFT = HEAD_DIM.bit_length() - 1
N_HEADS = 16
N_GROUPS = 4
N_STATE = 128
N_PAIRS = N_HEADS // 2
CONV_WIDTH = 4
CHUNK = 128
D_CONV = D_MODEL + 2 * N_GROUPS * N_STATE
D_MAIN = D_MODEL + D_CONV + 3 * D_MODEL
D_IN_PROJ = D_MAIN + N_HEADS
D_FF = 4 * D_MODEL
EPS = 1e-5
LANES = 128
VMEM_LIMIT = 48 * 1024 * 1024

ADAM_LR = 0.001
ADAM_B1 = 0.9
ADAM_B2 = 0.999
ADAM_EPS = 1e-08
ADAM_WD = 0.01
ADAM_STEP = 10

N_CHIPS = 4
MESH = pl.DeviceIdType.MESH


def _tile(n, cap):
    if n <= cap:
        return n
    t = cap
    while t >= 8:
        if n % t == 0:
            return t
        t //= 2
    raise ValueError(f"no tile for {n} under {cap}")


def _params(*sem):
    return pltpu.CompilerParams(dimension_semantics=sem, vmem_limit_bytes=VMEM_LIMIT)


_DIMS = {"nn": (((1,), (0,)), ((), ())), "nt": (((1,), (1,)), ((), ())), "tn": (((0,), (0,)), ((), ()))}


def _matmul(a, b, mode, out_dtypes, name, extras=(), vecs=(), epilogue=None, slab=None, tm=1024, tn=1024, tk=1024):
    if mode == "nn":
        (m, k), (_, n) = a.shape, b.shape
    elif mode == "nt":
        (m, k), (n, _) = a.shape, b.shape
    else:
        (k, m), (_, n) = a.shape, b.shape
    tm, tn, tk = _tile(m, tm), _tile(n, tn), _tile(k, tk)
    nk = k // tk
    if mode == "tn":
        a_spec = pl.BlockSpec((tk, tm), lambda i, j, kk: (kk, i))
    else:
        a_spec = pl.BlockSpec((tm, tk), lambda i, j, kk: (i, kk))
    if mode == "nt":
        b_spec = pl.BlockSpec((tn, tk), lambda i, j, kk: (j, kk))
    else:
        b_spec = pl.BlockSpec((tk, tn), lambda i, j, kk: (kk, j))
    mn_spec = pl.BlockSpec((tm, tn), lambda i, j, kk: (i, j))
    vec_spec = pl.BlockSpec((1, tn), lambda i, j, kk: (0, j))
    n_extra, n_vec, n_out = len(extras), len(vecs), len(out_dtypes)
    n_in = n_extra + n_vec + (1 if slab is not None and slab[0] is not None else 0)
    dims = _DIMS[mode]

    def body(a_ref, b_ref, *rest):
        in_refs, out_refs, acc = rest[:n_extra + n_vec], rest[n_in:n_in + n_out], rest[-1]
        kk = pl.program_id(2)

        @pl.when(kk == 0)
        def _():
            acc[...] = jnp.zeros_like(acc)

        acc[...] += lax.dot_general(a_ref[...].astype(BF16), b_ref[...].astype(BF16), dims,
                                    preferred_element_type=F32)

        @pl.when(kk == nk - 1)
        def _():
            res = acc[...]
            outs = epilogue(res, *[e[...] for e in in_refs]) if epilogue is not None else (res,)
            for o_ref, val in zip(out_refs, outs):
                o_ref[...] = val.astype(o_ref.dtype).reshape(o_ref.shape)

    in_specs = [a_spec, b_spec] + [mn_spec] * n_extra + [vec_spec] * n_vec
    operands = [a, b, *extras, *vecs]
    out_specs = [mn_spec] * n_out
    out_shape = [jax.ShapeDtypeStruct((m, n), dt) for dt in out_dtypes]
    aliases = {}
    if slab is not None:
        buffer, shape, index = slab
        out_specs = [pl.BlockSpec((1, tm, tn), lambda i, j, kk: index(i, j))]
        out_shape = [jax.ShapeDtypeStruct(shape, out_dtypes[0])]
        if buffer is not None:
            in_specs.append(ANY)
            operands.append(buffer)
            aliases = {len(operands) - 1: 0}
    outs = pl.pallas_call(
        body, name=name, grid=(m // tm, n // tn, nk),
        in_specs=in_specs, out_specs=out_specs, out_shape=out_shape, input_output_aliases=aliases,
        scratch_shapes=[pltpu.VMEM((tm, tn), F32)],
        compiler_params=_params("parallel", "parallel", "arbitrary"),
    )(*operands)
    return tuple(outs)


def _rowwise(fn, rows, vecs, out_rows, out_accs, name, tm=256):
    t = rows[0].shape[0]
    tm = _tile(t, tm)
    n_rows, n_vecs, n_or, n_oa = len(rows), len(vecs), len(out_rows), len(out_accs)

    def body(*refs):
        ins = [r[...] for r in refs[:n_rows + n_vecs]]
        outs = fn(*ins)
        o_refs = refs[n_rows + n_vecs:]
        for o_ref, val in zip(o_refs[:n_or], outs[:n_or]):
            o_ref[...] = val.astype(o_ref.dtype)

        @pl.when(pl.program_id(0) == 0)
        def _():
            for o_ref in o_refs[n_or:]:
                o_ref[...] = jnp.zeros_like(o_ref)

        for o_ref, val in zip(o_refs[n_or:], outs[n_or:]):
            o_ref[...] += val

    outs = pl.pallas_call(
        body, name=name, grid=(t // tm,),
        in_specs=[pl.BlockSpec((tm, r.shape[1]), lambda i: (i, 0)) for r in rows]
        + [pl.BlockSpec(v.shape, lambda i: (0, 0)) for v in vecs],
        out_specs=[pl.BlockSpec((tm, o.shape[1]), lambda i: (i, 0)) for o in out_rows]
        + [pl.BlockSpec(o.shape, lambda i: (0, 0)) for o in out_accs],
        out_shape=list(out_rows) + list(out_accs),
        compiler_params=_params("arbitrary" if n_oa else "parallel"),
    )(*rows, *vecs)
    return tuple(outs)


def _stack_rows(parts, width):
    rows = lax.broadcasted_iota(jnp.int32, (8, width), 0)
    out = jnp.zeros((8, width), F32)
    for j, part in enumerate(parts):
        out = out + jnp.where(rows == j, part, 0.0)
    return out


def _sds(shape, dtype=F32):
    return jax.ShapeDtypeStruct(shape, dtype)


def _sigmoid(x):
    return 1.0 / (1.0 + jnp.exp(-x))


def _colsum(x):
    return jnp.sum(x, axis=0, keepdims=True)


def _rstd(x):
    return lax.rsqrt(jnp.mean(x * x, axis=-1, keepdims=True) + EPS)


def _rms_bwd_rows(xhat, r, g, dy):
    dxh = dy * g
    dx = r * (dxh - xhat * jnp.mean(dxh * xhat, axis=-1, keepdims=True))
    return dx, _colsum(dy * xhat)


def _rms_fwd(x, g, name):
    t, d = x.shape

    def fn(xb, gv):
        return (xb * _rstd(xb) * gv,)

    return _rowwise(fn, [x], [g], [_sds((t, d), BF16)], [], name)[0]


def _rms_bwd(x, g, dh, dres, name):
    t, d = x.shape

    def fn(xb, dhb, dresb, gv):
        r = _rstd(xb)
        dx, dg = _rms_bwd_rows(xb * r, r, gv, dhb)
        dx = dx + dresb
        return dx, dx, dg

    return _rowwise(fn, [x, dh, dres], [g], [_sds((t, d)), _sds((t, d), BF16)], [_sds((1, d))], name)


def _mixnorm_fwd(y, z, att, gs, ga, name):
    t, d = y.shape

    def fn(yb, zb, ab, gsv, gav):
        yg = yb * (zb * _sigmoid(zb))
        return (jnp.concatenate([yg * _rstd(yg) * gsv, ab * _rstd(ab) * gav], axis=1),)

    tm = _tile(t, 256)

    def body(y_ref, z_ref, a_ref, gs_ref, ga_ref, o_ref):
        o_ref[...] = fn(y_ref[...], z_ref[...], a_ref[...], gs_ref[...], ga_ref[...])[0].astype(BF16)

    row = pl.BlockSpec((tm, d), lambda i: (i, 0))
    vec = pl.BlockSpec((1, d), lambda i: (0, 0))
    out = pl.pallas_call(
        body, name=name, grid=(t // tm,),
        in_specs=[row, row, row, vec, vec],
        out_specs=pl.BlockSpec((tm, 2 * d), lambda i: (i, 0)),
        out_shape=_sds((t, 2 * d), BF16),
        compiler_params=_params("parallel"),
    )(y, z, att, gs, ga)
    return out


def _mixnorm_bwd(y, z, att, gs, ga, dycat, name):
    t, d = y.shape
    tm = _tile(t, 256)

    def body(y_ref, z_ref, a_ref, dyc_ref, gs_ref, ga_ref, dy_ref, dz_ref, da_ref, dgs_ref, dga_ref):
        yb, zb, ab = y_ref[...], z_ref[...], a_ref[...]
        dys, dya = dyc_ref[:, :d], dyc_ref[:, d:]
        sz = _sigmoid(zb)
        silu = zb * sz
        yg = yb * silu
        r = _rstd(yg)
        dyg, dgs = _rms_bwd_rows(yg * r, r, gs_ref[...], dys)
        dy_ref[...] = dyg * silu
        dz_ref[...] = (dyg * yb * (sz * (1.0 + zb * (1.0 - sz)))).astype(BF16)
        r2 = _rstd(ab)
        datt, dga = _rms_bwd_rows(ab * r2, r2, ga_ref[...], dya)
        da_ref[...] = datt

        @pl.when(pl.program_id(0) == 0)
        def _():
            dgs_ref[...] = jnp.zeros_like(dgs_ref)
            dga_ref[...] = jnp.zeros_like(dga_ref)

        dgs_ref[...] += dgs
        dga_ref[...] += dga

    row = pl.BlockSpec((tm, d), lambda i: (i, 0))
    vec = pl.BlockSpec((1, d), lambda i: (0, 0))
    return pl.pallas_call(
        body, name=name, grid=(t // tm,),
        in_specs=[row, row, row, pl.BlockSpec((tm, 2 * d), lambda i: (i, 0)), vec, vec],
        out_specs=[row, row, row, vec, vec],
        out_shape=[_sds((t, d)), _sds((t, d), BF16), _sds((t, d)), _sds((1, d)), _sds((1, d))],
        compiler_params=_params("arbitrary"),
    )(y, z, att, dycat, gs, ga)


def _final_loss(x, g, target, name):
    t, d = x.shape

    def fn(xb, tb, gv):
        r = _rstd(xb)
        xhat = xb * r
        err = xhat * gv - tb
        loss = 0.5 * jnp.sum(jnp.mean(err * err, axis=-1, keepdims=True), axis=0, keepdims=True)
        dx, dg = _rms_bwd_rows(xhat, r, gv, err * (1.0 / d))
        return dx, dx, jnp.broadcast_to(loss, (1, LANES)), dg

    dx, dxb, loss, dg = _rowwise(fn, [x, target], [g], [_sds((t, d)), _sds((t, d), BF16)],
                                 [_sds((1, LANES)), _sds((1, d))], name)
    return loss, dx, dxb, dg


def _adamw(w, g, m, v, name):
    c1 = 1.0 - ADAM_B1 ** ADAM_STEP
    c2 = 1.0 - ADAM_B2 ** ADAM_STEP

    def fn(wb, gb, mb, vb):
        mn = ADAM_B1 * mb + (1.0 - ADAM_B1) * gb
        vn = ADAM_B2 * vb + (1.0 - ADAM_B2) * (gb * gb)
        delta = -ADAM_LR * ((mn / c1) / (jnp.sqrt(vn / c2) + ADAM_EPS) + ADAM_WD * wb)
        return delta, mn, vn

    return _rowwise(fn, [w, g, m, v], [], [_sds(w.shape)] * 3, [], name)


CONV_CB = 512


def _shift_down(u, k):
    if k == 0:
        return u
    rows = lax.broadcasted_iota(jnp.int32, u.shape, 0)
    return jnp.where(rows >= k, pltpu.roll(u, k, 0), 0.0)


def _shift_up(u, k):
    if k == 0:
        return u
    n = u.shape[0]
    rows = lax.broadcasted_iota(jnp.int32, u.shape, 0)
    return jnp.where(rows < n - k, pltpu.roll(u, n - k, 0), 0.0)


def _conv_pre(u, w, b):
    pre = b
    for j in range(CONV_WIDTH):
        pre = pre + w[j:j + 1, :] * _shift_down(u, CONV_WIDTH - 1 - j)
    return pre


def _conv_fwd(proj, w, b, n_seq, name):
    t = proj.shape[0]
    seq = t // n_seq
    off = D_MODEL // CONV_CB

    def body(u_ref, w_ref, b_ref, o_ref):
        pre = _conv_pre(u_ref[...], w_ref[...], b_ref[...])
        o_ref[...] = pre * _sigmoid(pre)

    return pl.pallas_call(
        body, name=name, grid=(n_seq, D_CONV // CONV_CB),
        in_specs=[pl.BlockSpec((seq, CONV_CB), lambda s, c: (s, c + off)),
                  pl.BlockSpec((CONV_WIDTH, CONV_CB), lambda s, c: (0, c)),
                  pl.BlockSpec((1, CONV_CB), lambda s, c: (0, c))],
        out_specs=pl.BlockSpec((seq, CONV_CB), lambda s, c: (s, c)),
        out_shape=_sds((t, D_CONV)),
        compiler_params=_params("parallel", "parallel"),
    )(proj, w, b)


def _conv_bwd(proj, w, b, dxbc, n_seq, name):
    t = proj.shape[0]
    seq = t // n_seq
    off = D_MODEL // CONV_CB

    def body(u_ref, w_ref, b_ref, d_ref, du_ref, wg_ref):
        u, wv = u_ref[...], w_ref[...]
        taps = [_shift_down(u, CONV_WIDTH - 1 - j) for j in range(CONV_WIDTH)]
        pre = b_ref[...]
        for j in range(CONV_WIDTH):
            pre = pre + wv[j:j + 1, :] * taps[j]
        s = _sigmoid(pre)
        dpre = d_ref[...] * (s * (1.0 + pre * (1.0 - s)))
        du = jnp.zeros_like(u)
        parts = []
        for j in range(CONV_WIDTH):
            du = du + wv[j:j + 1, :] * _shift_up(dpre, CONV_WIDTH - 1 - j)
            parts.append(_colsum(dpre * taps[j]))
        du_ref[...] = du.astype(BF16)
        parts.append(_colsum(dpre))

        @pl.when(pl.program_id(1) == 0)
        def _():
            wg_ref[...] = jnp.zeros_like(wg_ref)

        wg_ref[...] += _stack_rows(parts, u.shape[1])

    return pl.pallas_call(
        body, name=name, grid=(D_CONV // CONV_CB, n_seq),
        in_specs=[pl.BlockSpec((seq, CONV_CB), lambda c, s: (s, c + off)),
                  pl.BlockSpec((CONV_WIDTH, CONV_CB), lambda c, s: (0, c)),
                  pl.BlockSpec((1, CONV_CB), lambda c, s: (0, c)),
                  pl.BlockSpec((seq, CONV_CB), lambda c, s: (s, c))],
        out_specs=[pl.BlockSpec((seq, CONV_CB), lambda c, s: (s, c)),
                   pl.BlockSpec((8, CONV_CB), lambda c, s: (0, c))],
        out_shape=[_sds((t, D_CONV), BF16), _sds((8, D_CONV))],
        compiler_params=_params("parallel", "arbitrary"),
    )(proj, w, b, dxbc)


def _iota2(shape, axis):
    return lax.broadcasted_iota(jnp.int32, shape, axis)


def _dot(a, b, dims=_DIMS["nn"], precision=None):
    return lax.dot_general(a, b, dims, precision=precision, preferred_element_type=F32)


def _bdot(a, b, dims=_DIMS["nn"]):
    return lax.dot_general(a.astype(BF16), b.astype(BF16), dims, preferred_element_type=F32)


def _expand_mat():
    return (_iota2((LANES, D_MODEL), 0) == lax.shift_right_logical(_iota2((LANES, D_MODEL), 1), HEAD_SHIFT)).astype(F32)


def _reduce_mat():
    return (lax.shift_right_logical(_iota2((D_MODEL, LANES), 0), HEAD_SHIFT) == _iota2((D_MODEL, LANES), 1)).astype(F32)


def _ssd_decay(dtraw, bias, alog):
    row, col = _iota2((CHUNK, CHUNK), 0), _iota2((CHUNK, CHUNK), 1)
    pre = dtraw + bias
    dtb = jnp.maximum(pre, 0.0) + jnp.log(1.0 + jnp.exp(-jnp.abs(pre)))
    a_neg = -jnp.exp(alog)
    a = dtb * a_neg
    tril = (row >= col).astype(F32)
    triu = (row <= col).astype(F32)
    cs = _dot(tril, a, precision=HIGHEST)
    cs_t = _dot(a, triu, _DIMS["tn"], precision=HIGHEST)
    return pre, dtb, a_neg, cs, cs_t


def _pair_rowscale(vec, h0):
    top = _iota2((CHUNK, LANES), 0) < HEAD_DIM
    return jnp.where(top, vec[:, h0:h0 + 1], vec[:, h0 + 1:h0 + 2])


def _decay_mat(cs, cs_t, h):
    row, col = _iota2((CHUNK, CHUNK), 0), _iota2((CHUNK, CHUNK), 1)
    seg = cs[:, h:h + 1] - cs_t[h:h + 1, :]
    return jnp.exp(jnp.where(row >= col, seg, -jnp.inf))


def _ssd_fwd(xbc, dtraw, bias, alog, dskip_x, n_seq, name):
    t = xbc.shape[0]
    nc = t // n_seq // CHUNK

    def body(x_ref, b_ref, c_ref, dt_ref, bias_ref, alog_ref, dsk_ref, y_ref, st_ref, h_ref):
        @pl.when(pl.program_id(1) == 0)
        def _():
            h_ref[...] = jnp.zeros_like(h_ref)

        _, dtb, _, cs, cs_t = _ssd_decay(dt_ref[...], bias_ref[...], alog_ref[...])
        expand = _expand_mat()
        dt_x = _dot(dtb, expand, precision=HIGHEST)
        cs_x = _dot(cs, expand, precision=HIGHEST)
        tot = cs[CHUNK - 1:CHUNK, :]
        etot = jnp.exp(tot)
        x = x_ref[...]
        xdt = x * dt_x
        e_x = jnp.exp(cs_x)
        xdec = xdt * jnp.exp(cs_x[CHUNK - 1:CHUNK, :] - cs_x)
        keeps = [_iota2((CHUNK, LANES), 1) < HEAD_DIM, _iota2((CHUNK, LANES), 1) >= HEAD_DIM]
        for g in range(N_GROUPS):
            glanes = slice(2 * g * LANES, (2 * g + 2) * LANES)
            bg = b_ref[:, g * N_STATE:(g + 1) * N_STATE]
            cg = c_ref[:, g * N_STATE:(g + 1) * N_STATE]
            gmat = _bdot(cg, bg, _DIMS["nt"])
            prev = h_ref[2 * g:2 * g + 2].reshape(2 * LANES, N_STATE)
            st_ref[0, 2 * g:2 * g + 2] = prev.reshape(2, LANES, N_STATE)
            yoff = _bdot(cg, prev, _DIMS["nt"]) * e_x[:, glanes]
            new = _bdot(xdec[:, glanes], bg, _DIMS["tn"])
            scale = jnp.concatenate([_pair_rowscale(etot, 2 * q) for q in (2 * g, 2 * g + 1)], axis=0)
            h_ref[2 * g:2 * g + 2] = (prev * scale + new).reshape(2, LANES, N_STATE)
            for q in (2 * g, 2 * g + 1):
                lanes = slice(q * LANES, (q + 1) * LANES)
                xdt_q = xdt[:, lanes]
                w_cat = jnp.concatenate([gmat * _decay_mat(cs, cs_t, 2 * q + r) for r in range(2)], axis=1)
                x_cat = jnp.concatenate([jnp.where(keeps[r], xdt_q, 0.0) for r in range(2)], axis=0)
                y_ref[:, lanes] = (_bdot(w_cat, x_cat) + yoff[:, lanes.start - glanes.start:lanes.stop - glanes.start]
                                   + x[:, lanes] * dsk_ref[:, lanes])

    vec = pl.BlockSpec((1, LANES), lambda s, c: (0, 0))
    return pl.pallas_call(
        body, name=name, grid=(n_seq, nc),
        in_specs=[pl.BlockSpec((CHUNK, D_MODEL), lambda s, c: (s * nc + c, 0)),
                  pl.BlockSpec((CHUNK, N_GROUPS * N_STATE), lambda s, c: (s * nc + c, 2)),
                  pl.BlockSpec((CHUNK, N_GROUPS * N_STATE), lambda s, c: (s * nc + c, 3)),
                  pl.BlockSpec((CHUNK, LANES), lambda s, c: (s * nc + c, 0)),
                  vec, vec, pl.BlockSpec((1, D_MODEL), lambda s, c: (0, 0))],
        out_specs=[pl.BlockSpec((CHUNK, D_MODEL), lambda s, c: (s * nc + c, 0)),
                   pl.BlockSpec((1, N_PAIRS, LANES, N_STATE), lambda s, c: (s * nc + c, 0, 0, 0))],
        out_shape=[_sds((t, D_MODEL)), _sds((t // CHUNK, N_PAIRS, LANES, N_STATE))],
        scratch_shapes=[pltpu.VMEM((N_PAIRS, LANES, N_STATE), F32)],
        compiler_params=_params("parallel", "arbitrary"),
    )(xbc, xbc, xbc, dtraw, bias, alog, dskip_x)


def _ssd_bwd(xbc, dtraw, bias, alog, dskip_x, states, dy, n_seq, name):
    t = xbc.shape[0]
    nc = t // n_seq // CHUNK

    def body(x_ref, b_ref, c_ref, dt_ref, bias_ref, alog_ref, dsk_ref, st_ref, dy_ref,
             dxbc_ref, ddt_ref, pg_ref, dh_ref):
        first = jnp.logical_and(pl.program_id(0) == 0, pl.program_id(1) == 0)

        @pl.when(first)
        def _():
            pg_ref[...] = jnp.zeros_like(pg_ref)

        @pl.when(pl.program_id(1) == 0)
        def _():
            dh_ref[...] = jnp.zeros_like(dh_ref)

        row, col = _iota2((CHUNK, CHUNK), 0), _iota2((CHUNK, CHUNK), 1)
        pre, dtb, a_neg, cs, cs_t = _ssd_decay(dt_ref[...], bias_ref[...], alog_ref[...])
        expand, reduce = _expand_mat(), _reduce_mat()
        dt_x = _dot(dtb, expand, precision=HIGHEST)
        cs_x = _dot(cs, expand, precision=HIGHEST)
        etot = jnp.exp(cs[CHUNK - 1:CHUNK, :])
        x, dy = x_ref[...], dy_ref[...]
        xdt = x * dt_x
        e_x = jnp.exp(cs_x)
        dec_x = jnp.exp(cs_x[CHUNK - 1:CHUNK, :] - cs_x)
        xdec = xdt * dec_x
        dye = dy * e_x
        keeps = [_iota2((CHUNK, LANES), 1) < HEAD_DIM, _iota2((CHUNK, LANES), 1) >= HEAD_DIM]
        dcs_col = jnp.zeros((CHUNK, LANES), F32)
        dcs_row = jnp.zeros((LANES, CHUNK), F32)
        dtot = jnp.zeros((1, LANES), F32)
        dxdt_parts, zdec_parts, yoff_parts = [], [], []
        for g in range(N_GROUPS):
            bg = b_ref[:, g * N_STATE:(g + 1) * N_STATE]
            cg = c_ref[:, g * N_STATE:(g + 1) * N_STATE]
            gmat = _bdot(cg, bg, _DIMS["nt"])
            dgmat = jnp.zeros((CHUNK, CHUNK), F32)
            glanes = slice(2 * g * LANES, (2 * g + 2) * LANES)
            prev = st_ref[0, 2 * g:2 * g + 2].reshape(2 * LANES, N_STATE)
            dht = dh_ref[2 * g:2 * g + 2].reshape(2 * LANES, N_STATE)
            dxdtdec = _bdot(bg, dht, _DIMS["nt"])
            zdec_parts.append(dxdtdec * xdec[:, glanes])
            dbg = _bdot(xdec[:, glanes], dht)
            yoff_parts.append(dy[:, glanes] * (_bdot(cg, prev, _DIMS["nt"]) * e_x[:, glanes]))
            dcg = _bdot(dye[:, glanes], prev)
            dprev = _bdot(dye[:, glanes], cg, _DIMS["tn"])
            hp = jnp.sum(dht * prev, axis=1, keepdims=True)
            rows4 = lax.shift_right_logical(_iota2((2 * LANES, 1), 0), HEAD_SHIFT)
            lane1 = _iota2((1, LANES), 1)
            for k in range(4):
                dk_tot = jnp.sum(jnp.where(rows4 == k, hp, 0.0), axis=0, keepdims=True)
                dtot = dtot + jnp.where(lane1 == 4 * g + k, dk_tot, 0.0)
            scale = jnp.concatenate([_pair_rowscale(etot, 2 * q) for q in (2 * g, 2 * g + 1)], axis=0)
            dh_ref[2 * g:2 * g + 2] = (dprev + dht * scale).reshape(2, LANES, N_STATE)
            for q in (2 * g, 2 * g + 1):
                lanes = slice(q * LANES, (q + 1) * LANES)
                local = slice(lanes.start - glanes.start, lanes.stop - glanes.start)
                xdt_q, dy_q = xdt[:, lanes], dy[:, lanes]
                lms = [_decay_mat(cs, cs_t, 2 * q + r) for r in range(2)]
                ws = [gmat * lm for lm in lms]
                dy_cat = jnp.concatenate([jnp.where(keeps[r], dy_q, 0.0) for r in range(2)], axis=0)
                dm_cat = _bdot(dy_cat, xdt_q, _DIMS["nt"])
                dxdt_q = _bdot(jnp.concatenate(ws, axis=0), dy_cat, _DIMS["tn"])
                for r in range(2):
                    h = 2 * q + r
                    dm = dm_cat[r * CHUNK:(r + 1) * CHUNK]
                    dgmat = dgmat + dm * lms[r]
                    tmat = dm * ws[r]
                    dcs_col = dcs_col + jnp.where(col == h, jnp.sum(tmat, axis=1, keepdims=True), 0.0)
                    dcs_row = dcs_row - jnp.where(row == h, jnp.sum(tmat, axis=0, keepdims=True), 0.0)
                dxdt_parts.append(dxdt_q + dxdtdec[:, local] * dec_x[:, lanes])
            dcg = dcg + _bdot(dgmat, bg)
            dbg = dbg + _bdot(dgmat, cg, _DIMS["tn"])
            dxbc_ref[:, D_MODEL + g * N_STATE:D_MODEL + (g + 1) * N_STATE] = dbg
            dxbc_ref[:, D_MODEL + (N_GROUPS + g) * N_STATE:D_MODEL + (N_GROUPS + g + 1) * N_STATE] = dcg
        dxdt = jnp.concatenate(dxdt_parts, axis=1)
        zdec = _dot(jnp.concatenate(zdec_parts, axis=1), reduce, precision=HIGHEST)
        yoff_d = _dot(jnp.concatenate(yoff_parts, axis=1), reduce, precision=HIGHEST)
        dtot = dtot * etot + _colsum(zdec)
        last = row[:, :LANES] == CHUNK - 1
        dcs_col = dcs_col + yoff_d - zdec + jnp.where(last, dtot, 0.0)
        triu = (row <= col).astype(F32)
        da = _dot(triu, dcs_col, precision=HIGHEST) + _dot(triu, dcs_row, _DIMS["nt"], precision=HIGHEST)
        ddt = _dot(dxdt * x, reduce, precision=HIGHEST) + da * a_neg
        ddtraw = ddt * _sigmoid(pre)
        ddt_ref[...] = ddtraw.astype(BF16)
        dxbc_ref[:, :D_MODEL] = dxdt * dt_x + dy * dsk_ref[...]
        dskip = _dot(jnp.broadcast_to(_colsum(dy * x), (8, D_MODEL)), reduce, precision=HIGHEST)[0:1, :]
        pg_ref[...] += _stack_rows([_colsum(ddtraw), _colsum(da * dtb) * a_neg, dskip], LANES)

    vec = pl.BlockSpec((1, LANES), lambda s, c: (0, 0))

    def blk(s, c):
        return s * nc + (nc - 1 - c)

    return pl.pallas_call(
        body, name=name, grid=(n_seq, nc),
        in_specs=[pl.BlockSpec((CHUNK, D_MODEL), lambda s, c: (blk(s, c), 0)),
                  pl.BlockSpec((CHUNK, N_GROUPS * N_STATE), lambda s, c: (blk(s, c), 2)),
                  pl.BlockSpec((CHUNK, N_GROUPS * N_STATE), lambda s, c: (blk(s, c), 3)),
                  pl.BlockSpec((CHUNK, LANES), lambda s, c: (blk(s, c), 0)),
                  vec, vec, pl.BlockSpec((1, D_MODEL), lambda s, c: (0, 0)),
                  pl.BlockSpec((1, N_PAIRS, LANES, N_STATE), lambda s, c: (blk(s, c), 0, 0, 0)),
                  pl.BlockSpec((CHUNK, D_MODEL), lambda s, c: (blk(s, c), 0))],
        out_specs=[pl.BlockSpec((CHUNK, D_CONV), lambda s, c: (blk(s, c), 0)),
                   pl.BlockSpec((CHUNK, LANES), lambda s, c: (blk(s, c), 0)),
                   pl.BlockSpec((8, LANES), lambda s, c: (0, 0))],
        out_shape=[_sds((t, D_CONV)), _sds((t, LANES), BF16), _sds((8, LANES))],
        scratch_shapes=[pltpu.VMEM((N_PAIRS, LANES, N_STATE), F32)],
        compiler_params=_params("arbitrary", "arbitrary"),
    )(xbc, xbc, xbc, dtraw, bias, alog, dskip_x, states, dy)


Q_COL = (D_MODEL + D_CONV) // LANES
K_COL = Q_COL + D_MODEL // LANES
V_COL = K_COL + D_MODEL // LANES
ATT_SCALE = HEAD_DIM ** -0.5


ATT_SUB = 2
ATT_TILE = ATT_SUB * CHUNK
ATT_NP = 2


ATT_ROWS = 2 * ATT_TILE


def _tri_dot(x, tri):
    return _dot(x.astype(BF16), tri)


def _att_stack(ref, lanes, keeps, scale=1.0):
    parts = []
    for a in range(ATT_SUB):
        blk = ref[a * CHUNK:(a + 1) * CHUNK, lanes] * scale
        parts += [jnp.where(keeps[r], blk, 0.0).astype(BF16) for r in range(2)]
    return jnp.concatenate(parts, axis=0)


def _att_unstack(x, a, keeps):
    return jnp.where(keeps[0], x[2 * a * CHUNK:(2 * a + 1) * CHUNK], x[(2 * a + 1) * CHUNK:(2 * a + 2) * CHUNK])


def _att_logits(s, diag):
    lb = jnp.minimum(s, 0.0) - jnp.log(1.0 + jnp.exp(-jnp.abs(s)))
    l1m = lb - s
    mask = None
    if diag:
        srow, scol = _iota2((ATT_ROWS, ATT_TILE), 0), _iota2((ATT_ROWS, ATT_TILE), 1)
        pair_shift = (2 * CHUNK).bit_length() - 1
        qpos = lax.shift_right_logical(srow, pair_shift) * CHUNK + jnp.bitwise_and(srow, CHUNK - 1)
        mask = qpos > scol
        l1m = jnp.where(mask, l1m, 0.0)
    return lb, l1m, mask


def _att_fwd(proj, n_seq, name, riders=()):
    t = proj.shape[0]
    seq = t // n_seq
    nq = seq // ATT_TILE
    n_ride = len(riders)
    n_steps = n_seq * (N_PAIRS // ATT_NP) * nq

    def body(q_ref, k_ref, v_ref, *rest):
        s_refs, (o_ref, rt_ref), g_refs = rest[:n_ride], rest[n_ride:n_ride + 2], rest[n_ride + 2:2 * n_ride + 2]
        i = pl.program_id(2)
        if n_ride:
            send_sems, recv_sems = rest[-2:]
            step = (pl.program_id(0) * (N_PAIRS // ATT_NP) + pl.program_id(1)) * nq + i
            copies = [_gather_copies(s_refs[k], g_refs[k], send_sems, recv_sems, 6 * k) for k in range(n_ride)]

            @pl.when(step == 0)
            def _():
                for sends, _, _, _ in copies:
                    for cp in sends:
                        cp.start()

            @pl.when(step == (3 * n_steps) // 4)
            def _():
                for _, landed, passed, _ in copies:
                    for got, onward in zip(landed, passed):
                        got.wait_recv()
                        onward.start()

        upper = (_iota2((ATT_TILE, ATT_TILE), 0) > _iota2((ATT_TILE, ATT_TILE), 1)).astype(BF16)
        keeps = [_iota2((CHUNK, LANES), 1) < HEAD_DIM, _iota2((CHUNK, LANES), 1) >= HEAD_DIM]
        pair_lanes = [slice(pr * LANES, (pr + 1) * LANES) for pr in range(ATT_NP)]
        q_stacks = [_att_stack(q_ref, lanes, keeps, ATT_SCALE) for lanes in pair_lanes]
        per_pair = ATT_SUB + 2

        def logits(jt):
            k0 = pl.multiple_of(jt * ATT_TILE, ATT_TILE)
            return [_dot(q_stacks[pr], k_ref[pl.ds(k0, ATT_TILE), lanes].astype(BF16), _DIMS["nt"])
                    for pr, lanes in enumerate(pair_lanes)]

        def tile(jt, carry, diag):
            state = list(carry)
            ahead = logits(jnp.maximum(jt - 1, 0))
            k0 = pl.multiple_of(jt * ATT_TILE, ATT_TILE)
            for pr, lanes in enumerate(pair_lanes):
                base = pr * per_pair
                run = state[base + ATT_SUB]
                v_tile = v_ref[pl.ds(k0, ATT_TILE), lanes].astype(BF16)
                lb, l1m, mask = _att_logits(state[base + ATT_SUB + 1], diag)
                p = jnp.exp(lb + (_tri_dot(l1m, upper) + run))
                if diag:
                    p = jnp.where(mask, p, 0.0)
                pv = _dot(p.astype(BF16), v_tile)
                for a in range(ATT_SUB):
                    state[base + a] = state[base + a] + _att_unstack(pv, a, keeps)
                state[base + ATT_SUB] = run + jnp.sum(l1m, axis=1, keepdims=True)
                state[base + ATT_SUB + 1] = ahead[pr]
            return tuple(state)

        first = logits(i)
        init = ()
        for pr in range(ATT_NP):
            init += tuple(jnp.zeros((CHUNK, LANES), F32) for _ in range(ATT_SUB)) + \
                (jnp.zeros((ATT_ROWS, 1), F32), first[pr])
        carry = tile(i, init, True)
        carry = lax.fori_loop(0, i, lambda it, c: tile(i - 1 - it, c, False), carry)
        for pr, lanes in enumerate(pair_lanes):
            for a in range(ATT_SUB):
                o_ref[a * CHUNK:(a + 1) * CHUNK, lanes] = carry[pr * per_pair + a]
                rt_ref[a * CHUNK:(a + 1) * CHUNK, lanes] = _att_unstack(carry[pr * per_pair + ATT_SUB], a, keeps)

        if n_ride:
            @pl.when(step == n_steps - 1)
            def _():
                for sends, _, passed, arrived in copies:
                    for cp in arrived:
                        cp.wait_recv()
                    for cp in sends + passed:
                        cp.wait_send()

    width = ATT_NP * LANES
    qblk = pl.BlockSpec((ATT_TILE, width), lambda s, p, i: (s * nq + i, p))
    sems = [pltpu.SemaphoreType.DMA((6 * n_ride,)), pltpu.SemaphoreType.DMA((6 * n_ride,))] if n_ride else []
    order = ("arbitrary",) * 3 if n_ride else ("parallel", "parallel", "arbitrary")
    return pl.pallas_call(
        body, name=name, grid=(n_seq, N_PAIRS // ATT_NP, nq),
        in_specs=[pl.BlockSpec((ATT_TILE, width), lambda s, p, i: (s * nq + i, Q_COL // ATT_NP + p)),
                  pl.BlockSpec((seq, width), lambda s, p, i: (s, K_COL // ATT_NP + p)),
                  pl.BlockSpec((seq, width), lambda s, p, i: (s, V_COL // ATT_NP + p))] + [ANY] * n_ride,
        out_specs=[qblk, qblk] + [ANY] * n_ride,
        out_shape=[_sds((t, D_MODEL)), _sds((t, D_MODEL))] +
        [_sds((N_CHIPS,) + r.shape, r.dtype) for r in riders],
        scratch_shapes=sems,
        compiler_params=_params(*order),
    )(proj, proj, proj, *riders)


def _att_bwd(proj, rtot, datt, n_seq, name, riders=()):
    t = proj.shape[0]
    seq = t // n_seq
    nq = seq // ATT_TILE
    n_ride = len(riders)
    n_steps = n_seq * (N_PAIRS // ATT_NP) * nq

    def body(q_ref, k_ref, v_ref, rt_ref, do_ref, *rest):
        p_refs, (dq_ref, dk_out, dv_out), got_refs = rest[:n_ride], rest[n_ride:n_ride + 3], rest[n_ride + 3:2 * n_ride + 3]
        dk_ref, dv_ref = rest[2 * n_ride + 3:2 * n_ride + 5]
        i = pl.program_id(2)
        if n_ride:
            send_sems, recv_sems = rest[-2:]
            step = (pl.program_id(0) * (N_PAIRS // ATT_NP) + pl.program_id(1)) * nq + i
            _, _, c, _, chips = _place()
            copies = [pltpu.make_async_remote_copy(src_ref=p_refs[k].at[2 * px + py], dst_ref=got_refs[k].at[j],
                                                   send_sem=send_sems.at[3 * k + j], recv_sem=recv_sems.at[3 * k + j],
                                                   device_id=(px, py, c), device_id_type=MESH)
                      for k in range(n_ride) for j, (px, py) in enumerate(chips)]

            @pl.when(step == 0)
            def _():
                for cp in copies:
                    cp.start()

        @pl.when(i == 0)
        def _():
            dk_ref[...] = jnp.zeros_like(dk_ref)
            dv_ref[...] = jnp.zeros_like(dv_ref)

        row, col = _iota2((ATT_TILE, ATT_TILE), 0), _iota2((ATT_TILE, ATT_TILE), 1)
        upper = (row > col).astype(BF16)
        before = (row < col).astype(BF16)
        keeps = [_iota2((CHUNK, LANES), 1) < HEAD_DIM, _iota2((CHUNK, LANES), 1) >= HEAD_DIM]
        pair_lanes = [slice(pr * LANES, (pr + 1) * LANES) for pr in range(ATT_NP)]
        q_stacks = [_att_stack(q_ref, lanes, keeps, ATT_SCALE) for lanes in pair_lanes]
        do_stacks = [_att_stack(do_ref, lanes, keeps) for lanes in pair_lanes]
        totals = [jnp.concatenate([rt_ref[a * CHUNK:(a + 1) * CHUNK, pr * LANES + r * HEAD_DIM:pr * LANES + r * HEAD_DIM + 1]
                                   for a in range(ATT_SUB) for r in range(2)], axis=0) for pr in range(ATT_NP)]
        per_pair = ATT_SUB + 4

        def products(jt):
            k0 = pl.multiple_of(jt * ATT_TILE, ATT_TILE)
            return [(_dot(q_stacks[pr], k_ref[pl.ds(k0, ATT_TILE), lanes].astype(BF16), _DIMS["nt"]),
                     _dot(do_stacks[pr], v_ref[pl.ds(k0, ATT_TILE), lanes].astype(BF16), _DIMS["nt"]))
                    for pr, lanes in enumerate(pair_lanes)]

        def tile(jt, carry, diag):
            state = list(carry)
            ahead = None if diag else products(jt + 1)
            k0 = pl.multiple_of(jt * ATT_TILE, ATT_TILE)
            for pr, lanes in enumerate(pair_lanes):
                base = pr * per_pair
                seen, dseen = state[base + ATT_SUB], state[base + ATT_SUB + 1]
                q_stack, do_stack = q_stacks[pr], do_stacks[pr]
                k_tile = k_ref[pl.ds(k0, ATT_TILE), lanes].astype(BF16)
                lb, l1m, mask = _att_logits(state[base + ATT_SUB + 2], diag)
                here = jnp.sum(l1m, axis=1, keepdims=True)
                p = jnp.exp(lb + (_tri_dot(l1m, upper) + (totals[pr] - seen - here)))
                if diag:
                    p = jnp.where(mask, p, 0.0)
                pb = p.astype(BF16)
                dz = state[base + ATT_SUB + 3] * p
                dl1m = dseen + _tri_dot(dz, before)
                sig = jnp.exp(lb)
                ds = dz * (1.0 - sig) - dl1m * sig
                if diag:
                    ds = jnp.where(mask, ds, 0.0)
                dsb = ds.astype(BF16)
                dq_all = _dot(dsb, k_tile)
                for a in range(ATT_SUB):
                    state[base + a] = state[base + a] + _att_unstack(dq_all, a, keeps)
                dk_ref[pl.ds(k0, ATT_TILE), lanes] += _dot(dsb, q_stack, _DIMS["tn"])
                dv_ref[pl.ds(k0, ATT_TILE), lanes] += _dot(pb, do_stack, _DIMS["tn"])
                state[base + ATT_SUB] = seen + here
                state[base + ATT_SUB + 1] = dseen + jnp.sum(dz, axis=1, keepdims=True)
                if ahead is not None:
                    state[base + ATT_SUB + 2], state[base + ATT_SUB + 3] = ahead[pr]
            return tuple(state)

        first = products(0)
        init = ()
        for pr in range(ATT_NP):
            init += tuple(jnp.zeros((CHUNK, LANES), F32) for _ in range(ATT_SUB)) + \
                (jnp.zeros((ATT_ROWS, 1), F32), jnp.zeros((ATT_ROWS, 1), F32)) + first[pr]
        carry = lax.fori_loop(0, i, lambda jt, c: tile(jt, c, False), init)
        carry = tile(i, carry, True)
        for pr, lanes in enumerate(pair_lanes):
            for a in range(ATT_SUB):
                dq_ref[a * CHUNK:(a + 1) * CHUNK, lanes] = (carry[pr * per_pair + a] * ATT_SCALE).astype(BF16)

        @pl.when(i == nq - 1)
        def _():
            dk_out[...] = dk_ref[...].astype(BF16)
            dv_out[...] = dv_ref[...].astype(BF16)

        if n_ride:
            @pl.when(step == n_steps - 1)
            def _():
                for cp in copies:
                    cp.wait()

    width = ATT_NP * LANES
    qblk = pl.BlockSpec((ATT_TILE, width), lambda s, p, i: (s * nq + i, p))
    kv_out = pl.BlockSpec((seq, width), lambda s, p, i: (s, p))
    sems = [pltpu.SemaphoreType.DMA((3 * n_ride,)), pltpu.SemaphoreType.DMA((3 * n_ride,))] if n_ride else []
    order = ("arbitrary",) * 3 if n_ride else ("parallel", "parallel", "arbitrary")
    return pl.pallas_call(
        body, name=name, grid=(n_seq, N_PAIRS // ATT_NP, nq),
        in_specs=[pl.BlockSpec((ATT_TILE, width), lambda s, p, i: (s * nq + i, Q_COL // ATT_NP + p)),
                  pl.BlockSpec((seq, width), lambda s, p, i: (s, K_COL // ATT_NP + p)),
                  pl.BlockSpec((seq, width), lambda s, p, i: (s, V_COL // ATT_NP + p)),
                  qblk, qblk] + [ANY] * n_ride,
        out_specs=[qblk, kv_out, kv_out] + [ANY] * n_ride,
        out_shape=[_sds((t, D_MODEL), BF16)] * 3 + [_sds((3,) + r.shape[1:], r.dtype) for r in riders],
        scratch_shapes=[pltpu.VMEM((seq, width), F32), pltpu.VMEM((seq, width), F32)] + sems,
        compiler_params=_params(*order),
    )(proj, proj, proj, rtot, datt, *riders)


def _pad_lanes(v):
    return jnp.pad(v.reshape(1, -1), ((0, 0), (0, LANES - v.shape[0])))


def _add_then_norm(acc, xb, g):
    x1 = xb + acc
    return x1, x1 * _rstd(x1) * g


def _layer_fwd(x, h, p, next_g, n_seq, tag, riders=(), complete=None):
    proj, = _matmul(h, p["w_main"], "nn", [F32], f"in_proj{tag}", tn=1536)
    dtraw, = _matmul(h, p["w_dt"], "nn", [F32], f"dt_proj{tag}")
    xbc = _conv_fwd(proj, p["conv_w"], p["conv_b"], n_seq, f"conv_fwd{tag}")
    y, states = _ssd_fwd(xbc, dtraw, p["dt_bias"], p["a_log"], p["d_skip_x"], n_seq, f"ssd_fwd{tag}")
    att, rtot, *rode = _att_fwd(proj, n_seq, f"att_fwd{tag}", riders)
    if complete is not None:
        p = complete(rode)
    ycat = _mixnorm_fwd(y, proj, att, p["ssd_norm_g"], p["att_norm_g"], f"mixnorm_fwd{tag}")
    x1, h2 = _matmul(ycat, p["w_out"], "nn", [F32, BF16], f"out_proj{tag}", extras=[x], vecs=[p["norm_mlp_g"]],
                     epilogue=_add_then_norm, tn=D_MODEL)
    u, act = _matmul(h2, p["w_up"], "nn", [F32, BF16], f"up_proj{tag}",
                     epilogue=lambda acc: (acc, jnp.square(jnp.maximum(acc, 0.0))))
    if next_g is None:
        x2, = _matmul(act, p["w_down"], "nn", [F32], f"down_proj{tag}", extras=[x1],
                      epilogue=lambda acc, xb: (xb + acc,))
        h_next = None
    else:
        x2, h_next = _matmul(act, p["w_down"], "nn", [F32, BF16], f"down_proj{tag}", extras=[x1], vecs=[next_g],
                             epilogue=_add_then_norm, tn=D_MODEL)
    saved = dict(x=x, h=h, proj=proj, dtraw=dtraw, xbc=xbc, y=y, states=states, att=att, rtot=rtot, ycat=ycat,
                 x1=x1, h2=h2, u=u, act=act)
    return x2, h_next, saved


def _slab(buffers, name, layer, rows, per_chip_rows):
    shape = (N_CHIPS, DEPTH * rows, D_MODEL)
    if per_chip_rows:
        return buffers.get(name), shape, lambda i, j: (i, layer, 0)
    return buffers.get(name), shape, lambda i, j: (j, layer, 0)


def _layer_bwd(dx2, dx2b, p, s, buffers, layer, n_seq, tag, ride=None):
    g = {}
    buffers["w_down"], = _matmul(s["act"], dx2b, "tn", [F32], f"dw_down{tag}",
                                 slab=_slab(buffers, "w_down", layer, D_FF // N_CHIPS, True), tm=D_FF // N_CHIPS)
    du, = _matmul(dx2b, p["w_down"], "nt", [BF16], f"d_act{tag}", extras=[s["u"]],
                  epilogue=lambda acc, ub: (acc * (2.0 * jnp.maximum(ub, 0.0)),))
    buffers["w_up"], = _matmul(s["h2"], du, "tn", [F32], f"dw_up{tag}",
                               slab=_slab(buffers, "w_up", layer, D_MODEL, False), tn=D_FF // N_CHIPS)
    dh2, = _matmul(du, p["w_up"], "nt", [F32], f"d_h2{tag}")
    dx1, dx1b, g["norm_mlp_g"] = _rms_bwd(s["x1"], p["norm_mlp_g"], dh2, dx2, f"rms_mlp_bwd{tag}")
    buffers["w_out"], = _matmul(s["ycat"], dx1b, "tn", [F32], f"dw_out{tag}",
                                slab=_slab(buffers, "w_out", layer, 2 * D_MODEL // N_CHIPS, True),
                                tm=2 * D_MODEL // N_CHIPS)
    dycat, = _matmul(dx1b, p["w_out"], "nt", [F32], f"d_ycat{tag}")
    dy, dz, datt, g["ssd_norm_g"], g["att_norm_g"] = _mixnorm_bwd(
        s["y"], s["proj"], s["att"], p["ssd_norm_g"], p["att_norm_g"], dycat, f"mixnorm_bwd{tag}")
    dq, dk, dv, *received = _att_bwd(s["proj"], s["rtot"], datt, n_seq, f"att_bwd{tag}", ride[0]() if ride else ())
    if ride:
        ride[1](received)
    dxbc, ddtraw, pg = _ssd_bwd(s["xbc"], s["dtraw"], p["dt_bias"], p["a_log"], p["d_skip_x"], s["states"], dy,
                                n_seq, f"ssd_bwd{tag}")
    g["dt_bias"], g["a_log"], g["d_skip"] = pg[0, :N_HEADS], pg[1, :N_HEADS], pg[2, :N_HEADS]
    du_conv, wg = _conv_bwd(s["proj"], p["conv_w"], p["conv_b"], dxbc, n_seq, f"conv_bwd{tag}")
    g["conv_w"], g["conv_b"] = wg[:CONV_WIDTH], wg[CONV_WIDTH]
    dproj = jnp.concatenate([dz, du_conv, dq, dk, dv], axis=1)
    g["w_main"], = _matmul(s["h"], dproj, "tn", [F32], f"dw_in{tag}", tn=1536)
    g["w_dt"], = _matmul(s["h"], ddtraw, "tn", [F32], f"dw_dt{tag}")
    dh_dt, = _matmul(ddtraw, p["w_dt"], "nt", [F32], f"d_h_dt{tag}")
    dh, = _matmul(dproj, p["w_main"], "nt", [F32], f"d_h{tag}", extras=[dh_dt], epilogue=lambda acc, e: (acc + e,))
    dx, dxb, g["norm_mix_g"] = _rms_bwd(s["x"], p["norm_mix_g"], dh, dx1, f"rms_mix_bwd{tag}")
    return dx, dxb, g


def _split_w_in(w_full):
    c0 = D_MODEL + D_CONV
    main = jnp.concatenate([w_full[:, :c0], w_full[:, c0 + N_HEADS:]], axis=1)
    dt = jnp.pad(w_full[:, c0:c0 + N_HEADS], ((0, 0), (0, LANES - N_HEADS)))
    return main, dt


def _merge_w_in(main, dt):
    c0 = D_MODEL + D_CONV
    return jnp.concatenate([main[:, :c0], dt[:, :N_HEADS], main[:, c0:]], axis=1)


def _prep_layer(l, w_in_full, w_out, w_up, w_down, conv_w, small):
    w_main, w_dt = _split_w_in(w_in_full)
    return dict(
        w_main=w_main, w_dt=w_dt, w_out=w_out, w_up=w_up, w_down=w_down, conv_w=conv_w,
        conv_b=small["conv_b"][l].reshape(1, -1),
        dt_bias=_pad_lanes(small["dt_bias"][l]), a_log=_pad_lanes(small["a_log"][l]),
        d_skip_x=jnp.repeat(small["d_skip"][l], HEAD_DIM).reshape(1, -1),
        norm_mix_g=small["norm_mix_g"][l].reshape(1, -1), ssd_norm_g=small["ssd_norm_g"][l].reshape(1, -1),
        att_norm_g=small["att_norm_g"][l].reshape(1, -1), norm_mlp_g=small["norm_mlp_g"][l].reshape(1, -1),
    )


ANY = pl.BlockSpec(memory_space=pl.ANY)


def _place():
    x, y, c = lax.axis_index("x"), lax.axis_index("y"), lax.axis_index("c")
    return x, y, c, (x, y, 1 - c), [(1 - x, y), (x, 1 - y), (1 - x, 1 - y)]


def _gather_copies(s_ref, o_ref, send_sems, recv_sems, base):
    x, y, c, sibling, chips = _place()
    half = s_ref.shape[0] // 2

    def slab(px, py, hc):
        return o_ref.at[2 * px + py, pl.ds(hc * half, half), :]

    def copy(k, src, dst, to):
        return pltpu.make_async_remote_copy(src_ref=src, dst_ref=dst, send_sem=send_sems.at[base + k],
                                            recv_sem=recv_sems.at[base + k], device_id=to, device_id_type=MESH)

    sends = [copy(k, s_ref.at[pl.ds(c * half, half), :], slab(x, y, c), (px, py, c)) for k, (px, py) in enumerate(chips)]
    landed = [copy(k, slab(px, py, c), slab(px, py, c), (px, py, c)) for k, (px, py) in enumerate(chips)]
    passed = [copy(3 + k, slab(px, py, c), slab(px, py, c), sibling) for k, (px, py) in enumerate(chips)]
    arrived = [copy(3 + k, slab(px, py, 1 - c), slab(px, py, 1 - c), sibling) for k, (px, py) in enumerate(chips)]
    return sends, landed, passed, arrived


def _own_slot(others, shard, chip):
    return lax.dynamic_update_slice(others, shard[None], (chip, 0, 0))


def _all_gather_chips(shard, chip, name):
    r, ccols = shard.shape

    def body(s_ref, o_ref, send_sems, recv_sems):
        sends, landed, passed, arrived = _gather_copies(s_ref, o_ref, send_sems, recv_sems, 0)
        for cp in sends:
            cp.start()
        for got, onward in zip(landed, passed):
            got.wait_recv()
            onward.start()
        for cp in arrived:
            cp.wait_recv()
        for cp in sends + passed:
            cp.wait_send()

    others = pl.pallas_call(
        body, name=name, in_specs=[ANY], out_specs=ANY,
        out_shape=_sds((N_CHIPS, r, ccols), shard.dtype),
        scratch_shapes=[pltpu.SemaphoreType.DMA((6,)), pltpu.SemaphoreType.DMA((6,))],
    )(shard)
    return _own_slot(others, shard, chip)


def _sibling_swap(g, name):
    n, r, ccols = g.shape
    half = r // 2

    def body(g_ref, o_ref, send_sem, recv_sem):
        _, _, c, sibling, _ = _place()
        cp = pltpu.make_async_remote_copy(src_ref=g_ref.at[:, pl.ds((1 - c) * half, half), :], dst_ref=o_ref,
                                          send_sem=send_sem, recv_sem=recv_sem, device_id=sibling, device_id_type=MESH)
        cp.start()
        cp.wait()

    return pl.pallas_call(
        body, name=name, in_specs=[ANY], out_specs=ANY, out_shape=_sds((n, half, ccols), g.dtype),
        scratch_shapes=[pltpu.SemaphoreType.DMA, pltpu.SemaphoreType.DMA],
    )(g)


def _chip_exchange(p, name):
    _, h, ccols = p.shape

    def body(p_ref, o_ref, send_sems, recv_sems):
        _, _, c, _, chips = _place()
        copies = [pltpu.make_async_remote_copy(src_ref=p_ref.at[2 * px + py], dst_ref=o_ref.at[k],
                                               send_sem=send_sems.at[k], recv_sem=recv_sems.at[k],
                                               device_id=(px, py, c), device_id_type=MESH)
                  for k, (px, py) in enumerate(chips)]
        for cp in copies:
            cp.start()
        for cp in copies:
            cp.wait()

    return pl.pallas_call(
        body, name=name, in_specs=[ANY], out_specs=ANY, out_shape=_sds((3, h, ccols), p.dtype),
        scratch_shapes=[pltpu.SemaphoreType.DMA((3,)), pltpu.SemaphoreType.DMA((3,))],
    )(p)


def _sibling_share(full, name):
    r, ccols = full.shape
    h = r // 2

    def body(f_ref, o_ref, send_sem, recv_sem):
        _, _, c, sibling, _ = _place()
        mine = pl.ds(c * h, h)
        cp = pltpu.make_async_remote_copy(src_ref=f_ref.at[mine, :], dst_ref=o_ref.at[mine, :], send_sem=send_sem,
                                          recv_sem=recv_sem, device_id=sibling, device_id_type=MESH)
        cp.start()
        theirs = o_ref.at[pl.ds((1 - c) * h, h), :]
        pltpu.make_async_remote_copy(src_ref=theirs, dst_ref=theirs, send_sem=send_sem, recv_sem=recv_sem,
                                     device_id=sibling, device_id_type=MESH).wait_recv()
        cp.wait_send()

    return pl.pallas_call(
        body, name=name, in_specs=[ANY], out_specs=ANY, out_shape=_sds((r, ccols), full.dtype),
        input_output_aliases={0: 0},
        scratch_shapes=[pltpu.SemaphoreType.DMA, pltpu.SemaphoreType.DMA],
    )(full)


def _add_halves(g, a, c, name):
    n, r, ccols = g.shape
    half = r // 2
    tr = _tile(half, 256)
    nb = half // tr

    def body(c_ref, g_ref, a_ref, o_ref):
        o_ref[...] = (g_ref[...] + a_ref[...]).astype(BF16)

    blk = (1, tr, ccols)
    return pl.pallas_call(
        body, name=name,
        grid_spec=pltpu.PrefetchScalarGridSpec(
            num_scalar_prefetch=1, grid=(n, nb),
            in_specs=[pl.BlockSpec(blk, lambda j, i, c_ref: (j, c_ref[0] * nb + i, 0)),
                      pl.BlockSpec(blk, lambda j, i, c_ref: (j, i, 0))],
            out_specs=pl.BlockSpec(blk, lambda j, i, c_ref: (j, i, 0))),
        out_shape=_sds((n, half, ccols), BF16),
        compiler_params=_params("parallel", "parallel"),
    )(c.reshape(1).astype(jnp.int32), g, a)


def _sum_chips(p, got, chip, c, name):
    _, h, ccols = p.shape
    tr = _tile(h, 256)

    def body(j_ref, h_ref, p_ref, a_ref, b_ref, c_ref, o_ref):
        f32 = [ref[...].astype(F32) for ref in (p_ref, a_ref, b_ref, c_ref)]
        o_ref[...] = ((f32[0] + f32[1]) + f32[2]) + f32[3]

    blk = (1, tr, ccols)

    def slot(k):
        return pl.BlockSpec(blk, lambda i, j_ref, h_ref: (k, i, 0))

    return pl.pallas_call(
        body, name=name,
        grid_spec=pltpu.PrefetchScalarGridSpec(
            num_scalar_prefetch=2, grid=(h // tr,),
            in_specs=[pl.BlockSpec(blk, lambda i, j_ref, h_ref: (j_ref[0], i, 0)), slot(0), slot(1), slot(2)],
            out_specs=pl.BlockSpec(blk, lambda i, j_ref, h_ref: (h_ref[0], i, 0))),
        out_shape=_sds((2, h, ccols)),
        compiler_params=_params("parallel"),
    )(chip.reshape(1).astype(jnp.int32), c.reshape(1).astype(jnp.int32), p, got, got, got)


def _reduce_scatter_begin(g, c, name):
    return _add_halves(g, _sibling_swap(g, f"{name}_swap"), c, f"{name}_add")


def _reduce_scatter_end(p, got, chip, c, name):
    halves = _sum_chips(p, got, chip, c, f"{name}_sum")
    return _sibling_share(halves.reshape(2 * p.shape[1], p.shape[2]), f"{name}_share")


def _reduce_scatter(g, chip, c, name):
    p = _reduce_scatter_begin(g, c, name)
    return _reduce_scatter_end(p, _chip_exchange(p, f"{name}_xchg"), chip, c, name)


def _all_reduce_small(v, name):
    r = v.shape[0]

    def body(v_ref, o_ref, buf, send_sems, recv_sems):
        x, y, c, _, _ = _place()
        buf[0] = v_ref[...]
        copies = []
        for rel in range(1, 8):
            fx, fy, fc = (rel >> 2) & 1, (rel >> 1) & 1, rel & 1
            peer = (1 - x if fx else x, 1 - y if fy else y, 1 - c if fc else c)
            cp = pltpu.make_async_remote_copy(src_ref=v_ref, dst_ref=buf.at[rel], send_sem=send_sems.at[rel - 1],
                                              recv_sem=recv_sems.at[rel - 1], device_id=peer, device_id_type=MESH)
            cp.start()
            copies.append(cp)
        for cp in copies:
            cp.wait()
        me = 4 * x + 2 * y + c
        acc = buf[jnp.bitwise_xor(me, 0)]
        for src in range(1, 8):
            acc = acc + buf[jnp.bitwise_xor(me, src)]
        o_ref[...] = acc

    vm = pl.BlockSpec(memory_space=pltpu.VMEM)
    return pl.pallas_call(
        body, name=name, in_specs=[vm], out_specs=vm, out_shape=_sds((r, LANES)),
        scratch_shapes=[pltpu.VMEM((8, r, LANES), F32), pltpu.SemaphoreType.DMA((7,)), pltpu.SemaphoreType.DMA((7,))],
    )(v)


WEIGHTS = ["norm_mix_g", "w_in", "conv_w", "conv_b", "dt_bias", "a_log", "d_skip", "ssd_norm_g", "att_norm_g",
           "w_out", "norm_mlp_g", "w_up", "w_down", "final_norm_g"]
BIG = ["w_in", "w_out", "w_up", "w_down"]
SMALL = [n for n in WEIGHTS if n not in BIG]


def _pack(arrays):
    flat = []
    for a in arrays:
        a = a.reshape(-1)
        flat.append(jnp.pad(a, (0, (-a.shape[0]) % LANES)))
    flat = jnp.concatenate(flat)
    flat = jnp.pad(flat, (0, (-flat.shape[0]) % (8 * LANES)))
    return flat.reshape(-1, LANES)


def _unpack(packed, shapes):
    flat, out, pos = packed.reshape(-1), [], 0
    for shp in shapes:
        n = math.prod(shp)
        out.append(flat[pos:pos + n].reshape(shp))
        pos += n + (-n) % LANES
    return out


def _columns_to_shards(g):
    l, r, ccols = g.shape
    return g.reshape(l, r, N_CHIPS, ccols // N_CHIPS).transpose(2, 0, 1, 3).reshape(N_CHIPS, l * r, ccols // N_CHIPS)


def _from_gathered(name, g, l):
    rows = g.shape[1] // DEPTH
    part = g[:, l * rows:(l + 1) * rows, :]
    if name in ("w_in", "w_up", "conv_w"):
        return part.transpose(1, 0, 2).reshape(rows, N_CHIPS * g.shape[2])
    return part.reshape(N_CHIPS * rows, g.shape[2])


def kernel(x, norm_mix_g, w_in, conv_w, conv_b, dt_bias, a_log, d_skip, ssd_norm_g, att_norm_g, w_out, norm_mlp_g, w_up, w_down, final_norm_g, loss_target, m_norm_mix_g, m_w_in, m_conv_w, m_conv_b, m_dt_bias, m_a_log, m_d_skip, m_ssd_norm_g, m_att_norm_g, m_w_out, m_norm_mlp_g, m_w_up, m_w_down, m_final_norm_g, v_norm_mix_g, v_w_in, v_conv_w, v_conv_b, v_dt_bias, v_a_log, v_d_skip, v_ssd_norm_g, v_att_norm_g, v_w_out, v_norm_mlp_g, v_w_up, v_w_down, v_final_norm_g):
    w = dict(norm_mix_g=norm_mix_g, w_in=w_in, conv_w=conv_w, conv_b=conv_b, dt_bias=dt_bias, a_log=a_log,
             d_skip=d_skip, ssd_norm_g=ssd_norm_g, att_norm_g=att_norm_g, w_out=w_out, norm_mlp_g=norm_mlp_g,
             w_up=w_up, w_down=w_down, final_norm_g=final_norm_g)
    m = dict(norm_mix_g=m_norm_mix_g, w_in=m_w_in, conv_w=m_conv_w, conv_b=m_conv_b, dt_bias=m_dt_bias,
             a_log=m_a_log, d_skip=m_d_skip, ssd_norm_g=m_ssd_norm_g, att_norm_g=m_att_norm_g, w_out=m_w_out,
             norm_mlp_g=m_norm_mlp_g, w_up=m_w_up, w_down=m_w_down, final_norm_g=m_final_norm_g)
    v = dict(norm_mix_g=v_norm_mix_g, w_in=v_w_in, conv_w=v_conv_w, conv_b=v_conv_b, dt_bias=v_dt_bias,
             a_log=v_a_log, d_skip=v_d_skip, ssd_norm_g=v_ssd_norm_g, att_norm_g=v_att_norm_g, w_out=v_w_out,
             norm_mlp_g=v_norm_mlp_g, w_up=v_w_up, w_down=v_w_down, final_norm_g=v_final_norm_g)
    n_seq, seq, d = x.shape
    t = n_seq * seq
    chip = 2 * lax.axis_index("x") + lax.axis_index("y")
    core = lax.axis_index("c")

    shards = {n: w[n].astype(BF16).reshape(-1, w[n].shape[-1]) for n in BIG}
    shards["w_in_rest"] = shards["w_in"][D_MODEL:]
    gathered = {"w_in_first": _all_gather_chips(shards["w_in"][:D_MODEL], chip, "gather_w_in"),
                "conv_w": _all_gather_chips(conv_w.reshape(-1, conv_w.shape[-1]), chip, "gather_conv_w")}
    late = [n for n in BIG if n != "w_in"]
    riders = ["w_in_rest"] + late
    layers = []

    def w_in_of(l):
        part = gathered["w_in_first"] if l == 0 else gathered["w_in_rest"][:, (l - 1) * D_MODEL:l * D_MODEL, :]
        return part.transpose(1, 0, 2).reshape(D_MODEL, D_IN_PROJ)

    def layer_weights(l):
        full = {n: _from_gathered(n, gathered[n], l) if n in gathered else None for n in late + ["conv_w"]}
        return _prep_layer(l, w_in_of(l), *[full[n] for n in late + ["conv_w"]], w)

    def complete(rode):
        for n, others in zip(riders, rode):
            gathered[n] = _own_slot(others, shards[n], chip)
        layers.extend(layer_weights(l) for l in range(DEPTH))
        return layers[0]

    xs = x.reshape(t, d)
    first = layer_weights(0)
    hs = _rms_fwd(xs, first["norm_mix_g"], "rms_mix_fwd_l0")
    saved = []
    for l in range(DEPTH):
        next_g = w["norm_mix_g"][l + 1].reshape(1, -1) if l + 1 < DEPTH else None
        if l == 0:
            xs, hs, s = _layer_fwd(xs, hs, first, next_g, n_seq, "_l0", [shards[n] for n in riders], complete)
        else:
            xs, hs, s = _layer_fwd(xs, hs, layers[l], next_g, n_seq, f"_l{l}")
        saved.append(s)
    loss_vec, dx, dxb, g_final = _final_loss(xs, final_norm_g.reshape(1, d), loss_target.reshape(t, d), "final_loss")
    loss = lax.psum(loss_vec[0, 0], ("x", "y", "c"))

    grads, shard_major, partial, g_out = [None] * DEPTH, {}, {}, {}

    def begin():
        for n in late:
            partial[n] = _reduce_scatter_begin(shard_major[n], core, f"rs_{n}")
        return [partial[n] for n in late]

    def end(received):
        for n, got in zip(late, received):
            g_out[n] = _reduce_scatter_end(partial[n], got, chip, core, f"rs_{n}").reshape(w[n].shape)

    for l in reversed(range(DEPTH)):
        dx, dxb, grads[l] = _layer_bwd(dx, dxb, layers[l], saved[l], shard_major, l, n_seq, f"_l{l}",
                                       (begin, end) if l == 0 else None)
        grads[l]["w_in"] = _merge_w_in(grads[l].pop("w_main"), grads[l].pop("w_dt"))
    grad_x = dx.reshape(n_seq, seq, d)

    full = {n: jnp.stack([grads[l][n] for l in range(DEPTH)]) for n in SMALL + ["w_in"] if n != "final_norm_g"}
    full["final_norm_g"] = g_final.reshape(d)
    g_out["w_in"] = _reduce_scatter(_columns_to_shards(full["w_in"]), chip, core, "rs_w_in").reshape(w_in.shape)
    small_sum = _all_reduce_small(_pack([full[n] for n in SMALL]), "allreduce_small")
    small_shapes = [(DEPTH, CONV_WIDTH, D_CONV) if n == "conv_w" else w[n].shape for n in SMALL]
    for n, val in zip(SMALL, _unpack(small_sum, small_shapes)):
        g_out[n] = val
    g_out["conv_w"] = lax.dynamic_slice_in_dim(g_out["conv_w"], chip * conv_w.shape[-1], conv_w.shape[-1], axis=2)

    delta, new_m, new_v = {}, {}, {}
    for n in BIG:
        two_d = (-1, w[n].shape[-1])
        dl, mn, vn = _adamw(w[n].reshape(two_d), g_out[n].reshape(two_d), m[n].reshape(two_d), v[n].reshape(two_d),
                            f"adamw_{n}")
        delta[n], new_m[n], new_v[n] = dl.reshape(w[n].shape), mn.reshape(w[n].shape), vn.reshape(w[n].shape)
    packs = [_pack([src[n] for n in SMALL]) for src in (w, g_out, m, v)]
    shapes = [w[n].shape for n in SMALL]
    for dst, packed in zip((delta, new_m, new_v), _adamw(*packs, "adamw_small")):
        for n, val in zip(SMALL, _unpack(packed, shapes)):
            dst[n] = val

    return (loss, grad_x, *[g_out[n] for n in WEIGHTS], *[delta[n] for n in WEIGHTS],
            *[new_m[n] for n in WEIGHTS], *[new_v[n] for n in WEIGHTS])
```

```python
import math

import jax
import jax.numpy as jnp
from jax import lax
from jax.experimental import pallas as pl
from jax.experimental.pallas import tpu as pltpu

F32 = jnp.float32
BF16 = jnp.bfloat16
HIGHEST = lax.Precision.HIGHEST

D_MODEL = 1024
DEPTH = 4
HEAD_DIM = 64
HEAD_SHIFT = HEAD_DIM.bit_length() - 1
N_HEADS = 16
N_GROUPS = 4
N_STATE = 128
N_PAIRS = N_HEADS // 2
CONV_WIDTH = 4
CHUNK = 128
D_CONV = D_MODEL + 2 * N_GROUPS * N_STATE
D_MAIN = D_MODEL + D_CONV + 3 * D_MODEL
D_IN_PROJ = D_MAIN + N_HEADS
D_FF = 4 * D_MODEL
EPS = 1e-5
LANES = 128
VMEM_LIMIT = 48 * 1024 * 1024

ADAM_LR = 0.001
ADAM_B1 = 0.9
ADAM_B2 = 0.999
ADAM_EPS = 1e-08
ADAM_WD = 0.01
ADAM_STEP = 10

N_CHIPS = 4
MESH = pl.DeviceIdType.MESH


def _tile(n, cap):
    if n <= cap:
        return n
    t = cap
    while t >= 8:
        if n % t == 0:
            return t
        t //= 2
    raise ValueError(f"no tile for {n} under {cap}")


def _params(*sem):
    return pltpu.CompilerParams(dimension_semantics=sem, vmem_limit_bytes=VMEM_LIMIT)


_DIMS = {"nn": (((1,), (0,)), ((), ())), "nt": (((1,), (1,)), ((), ())), "tn": (((0,), (0,)), ((), ()))}


def _matmul(a, b, mode, out_dtypes, name, extras=(), vecs=(), epilogue=None, slab=None, tm=1024, tn=1024, tk=1024):
    if mode == "nn":
        (m, k), (_, n) = a.shape, b.shape
    elif mode == "nt":
        (m, k), (n, _) = a.shape, b.shape
    else:
        (k, m), (_, n) = a.shape, b.shape
    tm, tn, tk = _tile(m, tm), _tile(n, tn), _tile(k, tk)
    nk = k // tk
    if mode == "tn":
        a_spec = pl.BlockSpec((tk, tm), lambda i, j, kk: (kk, i))
    else:
        a_spec = pl.BlockSpec((tm, tk), lambda i, j, kk: (i, kk))
    if mode == "nt":
        b_spec = pl.BlockSpec((tn, tk), lambda i, j, kk: (j, kk))
    else:
        b_spec = pl.BlockSpec((tk, tn), lambda i, j, kk: (kk, j))
    mn_spec = pl.BlockSpec((tm, tn), lambda i, j, kk: (i, j))
    vec_spec = pl.BlockSpec((1, tn), lambda i, j, kk: (0, j))
    n_extra, n_vec, n_out = len(extras), len(vecs), len(out_dtypes)
    n_in = n_extra + n_vec + (1 if slab is not None and slab[0] is not None else 0)
    dims = _DIMS[mode]

    def body(a_ref, b_ref, *rest):
        in_refs, out_refs, acc = rest[:n_extra + n_vec], rest[n_in:n_in + n_out], rest[-1]
        kk = pl.program_id(2)

        @pl.when(kk == 0)
        def _():
            acc[...] = jnp.zeros_like(acc)

        acc[...] += lax.dot_general(a_ref[...].astype(BF16), b_ref[...].astype(BF16), dims,
                                    preferred_element_type=F32)

        @pl.when(kk == nk - 1)
        def _():
            res = acc[...]
            outs = epilogue(res, *[e[...] for e in in_refs]) if epilogue is not None else (res,)
            for o_ref, val in zip(out_refs, outs):
                o_ref[...] = val.astype(o_ref.dtype).reshape(o_ref.shape)

    in_specs = [a_spec, b_spec] + [mn_spec] * n_extra + [vec_spec] * n_vec
    operands = [a, b, *extras, *vecs]
    out_specs = [mn_spec] * n_out
    out_shape = [jax.ShapeDtypeStruct((m, n), dt) for dt in out_dtypes]
    aliases = {}
    if slab is not None:
        buffer, shape, index = slab
        out_specs = [pl.BlockSpec((1, tm, tn), lambda i, j, kk: index(i, j))]
        out_shape = [jax.ShapeDtypeStruct(shape, out_dtypes[0])]
        if buffer is not None:
            in_specs.append(ANY)
            operands.append(buffer)
            aliases = {len(operands) - 1: 0}
    outs = pl.pallas_call(
        body, name=name, grid=(m // tm, n // tn, nk),
        in_specs=in_specs, out_specs=out_specs, out_shape=out_shape, input_output_aliases=aliases,
        scratch_shapes=[pltpu.VMEM((tm, tn), F32)],
        compiler_params=_params("parallel", "parallel", "arbitrary"),
    )(*operands)
    return tuple(outs)


def _rowwise(fn, rows, vecs, out_rows, out_accs, name, tm=256):
    t = rows[0].shape[0]
    tm = _tile(t, tm)
    n_rows, n_vecs, n_or, n_oa = len(rows), len(vecs), len(out_rows), len(out_accs)

    def body(*refs):
        ins = [r[...] for r in refs[:n_rows + n_vecs]]
        outs = fn(*ins)
        o_refs = refs[n_rows + n_vecs:]
        for o_ref, val in zip(o_refs[:n_or], outs[:n_or]):
            o_ref[...] = val.astype(o_ref.dtype)

        @pl.when(pl.program_id(0) == 0)
        def _():
            for o_ref in o_refs[n_or:]:
                o_ref[...] = jnp.zeros_like(o_ref)

        for o_ref, val in zip(o_refs[n_or:], outs[n_or:]):
            o_ref[...] += val

    outs = pl.pallas_call(
        body, name=name, grid=(t // tm,),
        in_specs=[pl.BlockSpec((tm, r.shape[1]), lambda i: (i, 0)) for r in rows]
        + [pl.BlockSpec(v.shape, lambda i: (0, 0)) for v in vecs],
        out_specs=[pl.BlockSpec((tm, o.shape[1]), lambda i: (i, 0)) for o in out_rows]
        + [pl.BlockSpec(o.shape, lambda i: (0, 0)) for o in out_accs],
        out_shape=list(out_rows) + list(out_accs),
        compiler_params=_params("arbitrary" if n_oa else "parallel"),
    )(*rows, *vecs)
    return tuple(outs)


def _stack_rows(parts, width):
    rows = lax.broadcasted_iota(jnp.int32, (8, width), 0)
    out = jnp.zeros((8, width), F32)
    for j, part in enumerate(parts):
        out = out + jnp.where(rows == j, part, 0.0)
    return out


def _sds(shape, dtype=F32):
    return jax.ShapeDtypeStruct(shape, dtype)


def _sigmoid(x):
    return 1.0 / (1.0 + jnp.exp(-x))


def _colsum(x):
    return jnp.sum(x, axis=0, keepdims=True)


def _rstd(x):
    return lax.rsqrt(jnp.mean(x * x, axis=-1, keepdims=True) + EPS)


def _rms_bwd_rows(xhat, r, g, dy):
    dxh = dy * g
    dx = r * (dxh - xhat * jnp.mean(dxh * xhat, axis=-1, keepdims=True))
    return dx, _colsum(dy * xhat)


def _rms_fwd(x, g, name):
    t, d = x.shape

    def fn(xb, gv):
        return (xb * _rstd(xb) * gv,)

    return _rowwise(fn, [x], [g], [_sds((t, d), BF16)], [], name)[0]


def _rms_bwd(x, g, dh, dres, name):
    t, d = x.shape

    def fn(xb, dhb, dresb, gv):
        r = _rstd(xb)
        dx, dg = _rms_bwd_rows(xb * r, r, gv, dhb)
        dx = dx + dresb
        return dx, dx, dg

    return _rowwise(fn, [x, dh, dres], [g], [_sds((t, d)), _sds((t, d), BF16)], [_sds((1, d))], name)


def _mixnorm_fwd(y, z, att, gs, ga, name):
    t, d = y.shape

    def fn(yb, zb, ab, gsv, gav):
        yg = yb * (zb * _sigmoid(zb))
        return (jnp.concatenate([yg * _rstd(yg) * gsv, ab * _rstd(ab) * gav], axis=1),)

    tm = _tile(t, 256)

    def body(y_ref, z_ref, a_ref, gs_ref, ga_ref, o_ref):
        o_ref[...] = fn(y_ref[...], z_ref[...], a_ref[...], gs_ref[...], ga_ref[...])[0].astype(BF16)

    row = pl.BlockSpec((tm, d), lambda i: (i, 0))
    vec = pl.BlockSpec((1, d), lambda i: (0, 0))
    out = pl.pallas_call(
        body, name=name, grid=(t // tm,),
        in_specs=[row, row, row, vec, vec],
        out_specs=pl.BlockSpec((tm, 2 * d), lambda i: (i, 0)),
        out_shape=_sds((t, 2 * d), BF16),
        compiler_params=_params("parallel"),
    )(y, z, att, gs, ga)
    return out


def _mixnorm_bwd(y, z, att, gs, ga, dycat, name):
    t, d = y.shape
    tm = _tile(t, 256)

    def body(y_ref, z_ref, a_ref, dyc_ref, gs_ref, ga_ref, dy_ref, dz_ref, da_ref, dgs_ref, dga_ref):
        yb, zb, ab = y_ref[...], z_ref[...], a_ref[...]
        dys, dya = dyc_ref[:, :d], dyc_ref[:, d:]
        sz = _sigmoid(zb)
        silu = zb * sz
        yg = yb * silu
        r = _rstd(yg)
        dyg, dgs = _rms_bwd_rows(yg * r, r, gs_ref[...], dys)
        dy_ref[...] = dyg * silu
        dz_ref[...] = (dyg * yb * (sz * (1.0 + zb * (1.0 - sz)))).astype(BF16)
        r2 = _rstd(ab)
        datt, dga = _rms_bwd_rows(ab * r2, r2, ga_ref[...], dya)
        da_ref[...] = datt

        @pl.when(pl.program_id(0) == 0)
        def _():
            dgs_ref[...] = jnp.zeros_like(dgs_ref)
            dga_ref[...] = jnp.zeros_like(dga_ref)

        dgs_ref[...] += dgs
        dga_ref[...] += dga

    row = pl.BlockSpec((tm, d), lambda i: (i, 0))
    vec = pl.BlockSpec((1, d), lambda i: (0, 0))
    return pl.pallas_call(
        body, name=name, grid=(t // tm,),
        in_specs=[row, row, row, pl.BlockSpec((tm, 2 * d), lambda i: (i, 0)), vec, vec],
        out_specs=[row, row, row, vec, vec],
        out_shape=[_sds((t, d)), _sds((t, d), BF16), _sds((t, d)), _sds((1, d)), _sds((1, d))],
        compiler_params=_params("arbitrary"),
    )(y, z, att, dycat, gs, ga)


def _final_loss(x, g, target, name):
    t, d = x.shape

    def fn(xb, tb, gv):
        r = _rstd(xb)
        xhat = xb * r
        err = xhat * gv - tb
        loss = 0.5 * jnp.sum(jnp.mean(err * err, axis=-1, keepdims=True), axis=0, keepdims=True)
        dx, dg = _rms_bwd_rows(xhat, r, gv, err * (1.0 / d))
        return dx, dx, jnp.broadcast_to(loss, (1, LANES)), dg

    dx, dxb, loss, dg = _rowwise(fn, [x, target], [g], [_sds((t, d)), _sds((t, d), BF16)],
                                 [_sds((1, LANES)), _sds((1, d))], name)
    return loss, dx, dxb, dg


def _adamw(w, g, m, v, name):
    c1 = 1.0 - ADAM_B1 ** ADAM_STEP
    c2 = 1.0 - ADAM_B2 ** ADAM_STEP

    def fn(wb, gb, mb, vb):
        mn = ADAM_B1 * mb + (1.0 - ADAM_B1) * gb
        vn = ADAM_B2 * vb + (1.0 - ADAM_B2) * (gb * gb)
        delta = -ADAM_LR * ((mn / c1) / (jnp.sqrt(vn / c2) + ADAM_EPS) + ADAM_WD * wb)
        return delta, mn, vn

    return _rowwise(fn, [w, g, m, v], [], [_sds(w.shape)] * 3, [], name)


CONV_CB = 512


def _shift_down(u, k):
    if k == 0:
        return u
    rows = lax.broadcasted_iota(jnp.int32, u.shape, 0)
    return jnp.where(rows >= k, pltpu.roll(u, k, 0), 0.0)


def _shift_up(u, k):
    if k == 0:
        return u
    n = u.shape[0]
    rows = lax.broadcasted_iota(jnp.int32, u.shape, 0)
    return jnp.where(rows < n - k, pltpu.roll(u, n - k, 0), 0.0)


def _conv_pre(u, w, b):
    pre = b
    for j in range(CONV_WIDTH):
        pre = pre + w[j:j + 1, :] * _shift_down(u, CONV_WIDTH - 1 - j)
    return pre


def _conv_fwd(proj, w, b, n_seq, name):
    t = proj.shape[0]
    seq = t // n_seq
    off = D_MODEL // CONV_CB

    def body(u_ref, w_ref, b_ref, o_ref):
        pre = _conv_pre(u_ref[...], w_ref[...], b_ref[...])
        o_ref[...] = pre * _sigmoid(pre)

    return pl.pallas_call(
        body, name=name, grid=(n_seq, D_CONV // CONV_CB),
        in_specs=[pl.BlockSpec((seq, CONV_CB), lambda s, c: (s, c + off)),
                  pl.BlockSpec((CONV_WIDTH, CONV_CB), lambda s, c: (0, c)),
                  pl.BlockSpec((1, CONV_CB), lambda s, c: (0, c))],
        out_specs=pl.BlockSpec((seq, CONV_CB), lambda s, c: (s, c)),
        out_shape=_sds((t, D_CONV)),
        compiler_params=_params("parallel", "parallel"),
    )(proj, w, b)


def _conv_bwd(proj, w, b, dxbc, n_seq, name):
    t = proj.shape[0]
    seq = t // n_seq
    off = D_MODEL // CONV_CB

    def body(u_ref, w_ref, b_ref, d_ref, du_ref, wg_ref):
        u, wv = u_ref[...], w_ref[...]
        taps = [_shift_down(u, CONV_WIDTH - 1 - j) for j in range(CONV_WIDTH)]
        pre = b_ref[...]
        for j in range(CONV_WIDTH):
            pre = pre + wv[j:j + 1, :] * taps[j]
        s = _sigmoid(pre)
        dpre = d_ref[...] * (s * (1.0 + pre * (1.0 - s)))
        du = jnp.zeros_like(u)
        parts = []
        for j in range(CONV_WIDTH):
            du = du + wv[j:j + 1, :] * _shift_up(dpre, CONV_WIDTH - 1 - j)
            parts.append(_colsum(dpre * taps[j]))
        du_ref[...] = du.astype(BF16)
        parts.append(_colsum(dpre))

        @pl.when(pl.program_id(1) == 0)
        def _():
            wg_ref[...] = jnp.zeros_like(wg_ref)

        wg_ref[...] += _stack_rows(parts, u.shape[1])

    return pl.pallas_call(
        body, name=name, grid=(D_CONV // CONV_CB, n_seq),
        in_specs=[pl.BlockSpec((seq, CONV_CB), lambda c, s: (s, c + off)),
                  pl.BlockSpec((CONV_WIDTH, CONV_CB), lambda c, s: (0, c)),
                  pl.BlockSpec((1, CONV_CB), lambda c, s: (0, c)),
                  pl.BlockSpec((seq, CONV_CB), lambda c, s: (s, c))],
        out_specs=[pl.BlockSpec((seq, CONV_CB), lambda c, s: (s, c)),
                   pl.BlockSpec((8, CONV_CB), lambda c, s: (0, c))],
        out_shape=[_sds((t, D_CONV), BF16), _sds((8, D_CONV))],
        compiler_params=_params("parallel", "arbitrary"),
    )(proj, w, b, dxbc)


def _iota2(shape, axis):
    return lax.broadcasted_iota(jnp.int32, shape, axis)


def _dot(a, b, dims=_DIMS["nn"], precision=None):
    return lax.dot_general(a, b, dims, precision=precision, preferred_element_type=F32)


def _bdot(a, b, dims=_DIMS["nn"]):
    return lax.dot_general(a.astype(BF16), b.astype(BF16), dims, preferred_element_type=F32)


def _expand_mat():
    return (_iota2((LANES, D_MODEL), 0) == lax.shift_right_logical(_iota2((LANES, D_MODEL), 1), HEAD_SHIFT)).astype(F32)


def _reduce_mat():
    return (lax.shift_right_logical(_iota2((D_MODEL, LANES), 0), HEAD_SHIFT) == _iota2((D_MODEL, LANES), 1)).astype(F32)


def _ssd_decay(dtraw, bias, alog):
    row, col = _iota2((CHUNK, CHUNK), 0), _iota2((CHUNK, CHUNK), 1)
    pre = dtraw + bias
    dtb = jnp.maximum(pre, 0.0) + jnp.log(1.0 + jnp.exp(-jnp.abs(pre)))
    a_neg = -jnp.exp(alog)
    a = dtb * a_neg
    tril = (row >= col).astype(F32)
    triu = (row <= col).astype(F32)
    cs = _dot(tril, a, precision=HIGHEST)
    cs_t = _dot(a, triu, _DIMS["tn"], precision=HIGHEST)
    return pre, dtb, a_neg, cs, cs_t


def _pair_rowscale(vec, h0):
    top = _iota2((CHUNK, LANES), 0) < HEAD_DIM
    return jnp.where(top, vec[:, h0:h0 + 1], vec[:, h0 + 1:h0 + 2])


def _decay_mat(cs, cs_t, h):
    row, col = _iota2((CHUNK, CHUNK), 0), _iota2((CHUNK, CHUNK), 1)
    seg = cs[:, h:h + 1] - cs_t[h:h + 1, :]
    return jnp.exp(jnp.where(row >= col, seg, -jnp.inf))


def _ssd_fwd(xbc, dtraw, bias, alog, dskip_x, n_seq, name):
    t = xbc.shape[0]
    nc = t // n_seq // CHUNK

    def body(x_ref, b_ref, c_ref, dt_ref, bias_ref, alog_ref, dsk_ref, y_ref, st_ref, h_ref):
        @pl.when(pl.program_id(1) == 0)
        def _():
            h_ref[...] = jnp.zeros_like(h_ref)

        _, dtb, _, cs, cs_t = _ssd_decay(dt_ref[...], bias_ref[...], alog_ref[...])
        expand = _expand_mat()
        dt_x = _dot(dtb, expand, precision=HIGHEST)
        cs_x = _dot(cs, expand, precision=HIGHEST)
        tot = cs[CHUNK - 1:CHUNK, :]
        etot = jnp.exp(tot)
        x = x_ref[...]
        xdt = x * dt_x
        e_x = jnp.exp(cs_x)
        xdec = xdt * jnp.exp(cs_x[CHUNK - 1:CHUNK, :] - cs_x)
        keeps = [_iota2((CHUNK, LANES), 1) < HEAD_DIM, _iota2((CHUNK, LANES), 1) >= HEAD_DIM]
        for g in range(N_GROUPS):
            glanes = slice(2 * g * LANES, (2 * g + 2) * LANES)
            bg = b_ref[:, g * N_STATE:(g + 1) * N_STATE]
            cg = c_ref[:, g * N_STATE:(g + 1) * N_STATE]
            gmat = _bdot(cg, bg, _DIMS["nt"])
            prev = h_ref[2 * g:2 * g + 2].reshape(2 * LANES, N_STATE)
            st_ref[0, 2 * g:2 * g + 2] = prev.reshape(2, LANES, N_STATE)
            yoff = _bdot(cg, prev, _DIMS["nt"]) * e_x[:, glanes]
            new = _bdot(xdec[:, glanes], bg, _DIMS["tn"])
            scale = jnp.concatenate([_pair_rowscale(etot, 2 * q) for q in (2 * g, 2 * g + 1)], axis=0)
            h_ref[2 * g:2 * g + 2] = (prev * scale + new).reshape(2, LANES, N_STATE)
            for q in (2 * g, 2 * g + 1):
                lanes = slice(q * LANES, (q + 1) * LANES)
                xdt_q = xdt[:, lanes]
                w_cat = jnp.concatenate([gmat * _decay_mat(cs, cs_t, 2 * q + r) for r in range(2)], axis=1)
                x_cat = jnp.concatenate([jnp.where(keeps[r], xdt_q, 0.0) for r in range(2)], axis=0)
                y_ref[:, lanes] = (_bdot(w_cat, x_cat) + yoff[:, lanes.start - glanes.start:lanes.stop - glanes.start]
                                   + x[:, lanes] * dsk_ref[:, lanes])

    vec = pl.BlockSpec((1, LANES), lambda s, c: (0, 0))
    return pl.pallas_call(
        body, name=name, grid=(n_seq, nc),
        in_specs=[pl.BlockSpec((CHUNK, D_MODEL), lambda s, c: (s * nc + c, 0)),
                  pl.BlockSpec((CHUNK, N_GROUPS * N_STATE), lambda s, c: (s * nc + c, 2)),
                  pl.BlockSpec((CHUNK, N_GROUPS * N_STATE), lambda s, c: (s * nc + c, 3)),
                  pl.BlockSpec((CHUNK, LANES), lambda s, c: (s * nc + c, 0)),
                  vec, vec, pl.BlockSpec((1, D_MODEL), lambda s, c: (0, 0))],
        out_specs=[pl.BlockSpec((CHUNK, D_MODEL), lambda s, c: (s * nc + c, 0)),
                   pl.BlockSpec((1, N_PAIRS, LANES, N_STATE), lambda s, c: (s * nc + c, 0, 0, 0))],
        out_shape=[_sds((t, D_MODEL)), _sds((t // CHUNK, N_PAIRS, LANES, N_STATE))],
        scratch_shapes=[pltpu.VMEM((N_PAIRS, LANES, N_STATE), F32)],
        compiler_params=_params("parallel", "arbitrary"),
    )(xbc, xbc, xbc, dtraw, bias, alog, dskip_x)


def _ssd_bwd(xbc, dtraw, bias, alog, dskip_x, states, dy, n_seq, name):
    t = xbc.shape[0]
    nc = t // n_seq // CHUNK

    def body(x_ref, b_ref, c_ref, dt_ref, bias_ref, alog_ref, dsk_ref, st_ref, dy_ref,
             dxbc_ref, ddt_ref, pg_ref, dh_ref):
        first = jnp.logical_and(pl.program_id(0) == 0, pl.program_id(1) == 0)

        @pl.when(first)
        def _():
            pg_ref[...] = jnp.zeros_like(pg_ref)

        @pl.when(pl.program_id(1) == 0)
        def _():
            dh_ref[...] = jnp.zeros_like(dh_ref)

        row, col = _iota2((CHUNK, CHUNK), 0), _iota2((CHUNK, CHUNK), 1)
        pre, dtb, a_neg, cs, cs_t = _ssd_decay(dt_ref[...], bias_ref[...], alog_ref[...])
        expand, reduce = _expand_mat(), _reduce_mat()
        dt_x = _dot(dtb, expand, precision=HIGHEST)
        cs_x = _dot(cs, expand, precision=HIGHEST)
        etot = jnp.exp(cs[CHUNK - 1:CHUNK, :])
        x, dy = x_ref[...], dy_ref[...]
        xdt = x * dt_x
        e_x = jnp.exp(cs_x)
        dec_x = jnp.exp(cs_x[CHUNK - 1:CHUNK, :] - cs_x)
        xdec = xdt * dec_x
        dye = dy * e_x
        keeps = [_iota2((CHUNK, LANES), 1) < HEAD_DIM, _iota2((CHUNK, LANES), 1) >= HEAD_DIM]
        dcs_col = jnp.zeros((CHUNK, LANES), F32)
        dcs_row = jnp.zeros((LANES, CHUNK), F32)
        dtot = jnp.zeros((1, LANES), F32)
        dxdt_parts, zdec_parts, yoff_parts = [], [], []
        for g in range(N_GROUPS):
            bg = b_ref[:, g * N_STATE:(g + 1) * N_STATE]
            cg = c_ref[:, g * N_STATE:(g + 1) * N_STATE]
            gmat = _bdot(cg, bg, _DIMS["nt"])
            dgmat = jnp.zeros((CHUNK, CHUNK), F32)
            glanes = slice(2 * g * LANES, (2 * g + 2) * LANES)
            prev = st_ref[0, 2 * g:2 * g + 2].reshape(2 * LANES, N_STATE)
            dht = dh_ref[2 * g:2 * g + 2].reshape(2 * LANES, N_STATE)
            dxdtdec = _bdot(bg, dht, _DIMS["nt"])
            zdec_parts.append(dxdtdec * xdec[:, glanes])
            dbg = _bdot(xdec[:, glanes], dht)
            yoff_parts.append(dy[:, glanes] * (_bdot(cg, prev, _DIMS["nt"]) * e_x[:, glanes]))
            dcg = _bdot(dye[:, glanes], prev)
            dprev = _bdot(dye[:, glanes], cg, _DIMS["tn"])
            hp = jnp.sum(dht * prev, axis=1, keepdims=True)
            rows4 = lax.shift_right_logical(_iota2((2 * LANES, 1), 0), HEAD_SHIFT)
            lane1 = _iota2((1, LANES), 1)
            for k in range(4):
                dk_tot = jnp.sum(jnp.where(rows4 == k, hp, 0.0), axis=0, keepdims=True)
                dtot = dtot + jnp.where(lane1 == 4 * g + k, dk_tot, 0.0)
            scale = jnp.concatenate([_pair_rowscale(etot, 2 * q) for q in (2 * g, 2 * g + 1)], axis=0)
            dh_ref[2 * g:2 * g + 2] = (dprev + dht * scale).reshape(2, LANES, N_STATE)
            for q in (2 * g, 2 * g + 1):
                lanes = slice(q * LANES, (q + 1) * LANES)
                local = slice(lanes.start - glanes.start, lanes.stop - glanes.start)
                xdt_q, dy_q = xdt[:, lanes], dy[:, lanes]
                lms = [_decay_mat(cs, cs_t, 2 * q + r) for r in range(2)]
                ws = [gmat * lm for lm in lms]
                dy_cat = jnp.concatenate([jnp.where(keeps[r], dy_q, 0.0) for r in range(2)], axis=0)
                dm_cat = _bdot(dy_cat, xdt_q, _DIMS["nt"])
                dxdt_q = _bdot(jnp.concatenate(ws, axis=0), dy_cat, _DIMS["tn"])
                for r in range(2):
                    h = 2 * q + r
                    dm = dm_cat[r * CHUNK:(r + 1) * CHUNK]
                    dgmat = dgmat + dm * lms[r]
                    tmat = dm * ws[r]
                    dcs_col = dcs_col + jnp.where(col == h, jnp.sum(tmat, axis=1, keepdims=True), 0.0)
                    dcs_row = dcs_row - jnp.where(row == h, jnp.sum(tmat, axis=0, keepdims=True), 0.0)
                dxdt_parts.append(dxdt_q + dxdtdec[:, local] * dec_x[:, lanes])
            dcg = dcg + _bdot(dgmat, bg)
            dbg = dbg + _bdot(dgmat, cg, _DIMS["tn"])
            dxbc_ref[:, D_MODEL + g * N_STATE:D_MODEL + (g + 1) * N_STATE] = dbg
            dxbc_ref[:, D_MODEL + (N_GROUPS + g) * N_STATE:D_MODEL + (N_GROUPS + g + 1) * N_STATE] = dcg
        dxdt = jnp.concatenate(dxdt_parts, axis=1)
        zdec = _dot(jnp.concatenate(zdec_parts, axis=1), reduce, precision=HIGHEST)
        yoff_d = _dot(jnp.concatenate(yoff_parts, axis=1), reduce, precision=HIGHEST)
        dtot = dtot * etot + _colsum(zdec)
        last = row[:, :LANES] == CHUNK - 1
        dcs_col = dcs_col + yoff_d - zdec + jnp.where(last, dtot, 0.0)
        triu = (row <= col).astype(F32)
        da = _dot(triu, dcs_col, precision=HIGHEST) + _dot(triu, dcs_row, _DIMS["nt"], precision=HIGHEST)
        ddt = _dot(dxdt * x, reduce, precision=HIGHEST) + da * a_neg
        ddtraw = ddt * _sigmoid(pre)
        ddt_ref[...] = ddtraw.astype(BF16)
        dxbc_ref[:, :D_MODEL] = dxdt * dt_x + dy * dsk_ref[...]
        dskip = _dot(jnp.broadcast_to(_colsum(dy * x), (8, D_MODEL)), reduce, precision=HIGHEST)[0:1, :]
        pg_ref[...] += _stack_rows([_colsum(ddtraw), _colsum(da * dtb) * a_neg, dskip], LANES)

    vec = pl.BlockSpec((1, LANES), lambda s, c: (0, 0))

    def blk(s, c):
        return s * nc + (nc - 1 - c)

    return pl.pallas_call(
        body, name=name, grid=(n_seq, nc),
        in_specs=[pl.BlockSpec((CHUNK, D_MODEL), lambda s, c: (blk(s, c), 0)),
                  pl.BlockSpec((CHUNK, N_GROUPS * N_STATE), lambda s, c: (blk(s, c), 2)),
                  pl.BlockSpec((CHUNK, N_GROUPS * N_STATE), lambda s, c: (blk(s, c), 3)),
                  pl.BlockSpec((CHUNK, LANES), lambda s, c: (blk(s, c), 0)),
                  vec, vec, pl.BlockSpec((1, D_MODEL), lambda s, c: (0, 0)),
                  pl.BlockSpec((1, N_PAIRS, LANES, N_STATE), lambda s, c: (blk(s, c), 0, 0, 0)),
                  pl.BlockSpec((CHUNK, D_MODEL), lambda s, c: (blk(s, c), 0))],
        out_specs=[pl.BlockSpec((CHUNK, D_CONV), lambda s, c: (blk(s, c), 0)),
                   pl.BlockSpec((CHUNK, LANES), lambda s, c: (blk(s, c), 0)),
                   pl.BlockSpec((8, LANES), lambda s, c: (0, 0))],
        out_shape=[_sds((t, D_CONV)), _sds((t, LANES), BF16), _sds((8, LANES))],
        scratch_shapes=[pltpu.VMEM((N_PAIRS, LANES, N_STATE), F32)],
        compiler_params=_params("arbitrary", "arbitrary"),
    )(xbc, xbc, xbc, dtraw, bias, alog, dskip_x, states, dy)


Q_COL = (D_MODEL + D_CONV) // LANES
K_COL = Q_COL + D_MODEL // LANES
V_COL = K_COL + D_MODEL // LANES
ATT_SCALE = HEAD_DIM ** -0.5


ATT_SUB = 2
ATT_TILE = ATT_SUB * CHUNK
ATT_NP = 2


ATT_ROWS = 2 * ATT_TILE


def _tri_dot(x, tri):
    return _dot(x.astype(BF16), tri)


def _att_stack(ref, lanes, keeps, scale=1.0):
    parts = []
    for a in range(ATT_SUB):
        blk = ref[a * CHUNK:(a + 1) * CHUNK, lanes] * scale
        parts += [jnp.where(keeps[r], blk, 0.0).astype(BF16) for r in range(2)]
    return jnp.concatenate(parts, axis=0)


def _att_unstack(x, a, keeps):
    return jnp.where(keeps[0], x[2 * a * CHUNK:(2 * a + 1) * CHUNK], x[(2 * a + 1) * CHUNK:(2 * a + 2) * CHUNK])


def _att_logits(s, diag):
    lb = jnp.minimum(s, 0.0) - jnp.log(1.0 + jnp.exp(-jnp.abs(s)))
    l1m = lb - s
    mask = None
    if diag:
        srow, scol = _iota2((ATT_ROWS, ATT_TILE), 0), _iota2((ATT_ROWS, ATT_TILE), 1)
        pair_shift = (2 * CHUNK).bit_length() - 1
        qpos = lax.shift_right_logical(srow, pair_shift) * CHUNK + jnp.bitwise_and(srow, CHUNK - 1)
        mask = qpos > scol
        l1m = jnp.where(mask, l1m, 0.0)
    return lb, l1m, mask


def _att_fwd(proj, n_seq, name, riders=()):
    t = proj.shape[0]
    seq = t // n_seq
    nq = seq // ATT_TILE
    n_ride = len(riders)
    n_steps = n_seq * (N_PAIRS // ATT_NP) * nq

    def body(q_ref, k_ref, v_ref, *rest):
        s_refs, (o_ref, rt_ref), g_refs = rest[:n_ride], rest[n_ride:n_ride + 2], rest[n_ride + 2:2 * n_ride + 2]
        i = pl.program_id(2)
        if n_ride:
            send_sems, recv_sems = rest[-2:]
            step = (pl.program_id(0) * (N_PAIRS // ATT_NP) + pl.program_id(1)) * nq + i
            copies = [_gather_copies(s_refs[k], g_refs[k], send_sems, recv_sems, 6 * k) for k in range(n_ride)]

            @pl.when(step == 0)
            def _():
                for sends, _, _, _ in copies:
                    for cp in sends:
                        cp.start()

            @pl.when(step == (3 * n_steps) // 4)
            def _():
                for _, landed, passed, _ in copies:
                    for got, onward in zip(landed, passed):
                        got.wait_recv()
                        onward.start()

        upper = (_iota2((ATT_TILE, ATT_TILE), 0) > _iota2((ATT_TILE, ATT_TILE), 1)).astype(BF16)
        keeps = [_iota2((CHUNK, LANES), 1) < HEAD_DIM, _iota2((CHUNK, LANES), 1) >= HEAD_DIM]
        pair_lanes = [slice(pr * LANES, (pr + 1) * LANES) for pr in range(ATT_NP)]
        q_stacks = [_att_stack(q_ref, lanes, keeps, ATT_SCALE) for lanes in pair_lanes]
        per_pair = ATT_SUB + 2

        def logits(jt):
            k0 = pl.multiple_of(jt * ATT_TILE, ATT_TILE)
            return [_dot(q_stacks[pr], k_ref[pl.ds(k0, ATT_TILE), lanes].astype(BF16), _DIMS["nt"])
                    for pr, lanes in enumerate(pair_lanes)]

        def tile(jt, carry, diag):
            state = list(carry)
            ahead = logits(jnp.maximum(jt - 1, 0))
            k0 = pl.multiple_of(jt * ATT_TILE, ATT_TILE)
            for pr, lanes in enumerate(pair_lanes):
                base = pr * per_pair
                run = state[base + ATT_SUB]
                v_tile = v_ref[pl.ds(k0, ATT_TILE), lanes].astype(BF16)
                lb, l1m, mask = _att_logits(state[base + ATT_SUB + 1], diag)
                p = jnp.exp(lb + (_tri_dot(l1m, upper) + run))
                if diag:
                    p = jnp.where(mask, p, 0.0)
                pv = _dot(p.astype(BF16), v_tile)
                for a in range(ATT_SUB):
                    state[base + a] = state[base + a] + _att_unstack(pv, a, keeps)
                state[base + ATT_SUB] = run + jnp.sum(l1m, axis=1, keepdims=True)
                state[base + ATT_SUB + 1] = ahead[pr]
            return tuple(state)

        first = logits(i)
        init = ()
        for pr in range(ATT_NP):
            init += tuple(jnp.zeros((CHUNK, LANES), F32) for _ in range(ATT_SUB)) + \
                (jnp.zeros((ATT_ROWS, 1), F32), first[pr])
        carry = tile(i, init, True)
        carry = lax.fori_loop(0, i, lambda it, c: tile(i - 1 - it, c, False), carry)
        for pr, lanes in enumerate(pair_lanes):
            for a in range(ATT_SUB):
                o_ref[a * CHUNK:(a + 1) * CHUNK, lanes] = carry[pr * per_pair + a]
                rt_ref[a * CHUNK:(a + 1) * CHUNK, lanes] = _att_unstack(carry[pr * per_pair + ATT_SUB], a, keeps)

        if n_ride:
            @pl.when(step == n_steps - 1)
            def _():
                for sends, _, passed, arrived in copies:
                    for cp in arrived:
                        cp.wait_recv()
                    for cp in sends + passed:
                        cp.wait_send()

    width = ATT_NP * LANES
    qblk = pl.BlockSpec((ATT_TILE, width), lambda s, p, i: (s * nq + i, p))
    sems = [pltpu.SemaphoreType.DMA((6 * n_ride,)), pltpu.SemaphoreType.DMA((6 * n_ride,))] if n_ride else []
    order = ("arbitrary",) * 3 if n_ride else ("parallel", "parallel", "arbitrary")
    return pl.pallas_call(
        body, name=name, grid=(n_seq, N_PAIRS // ATT_NP, nq),
        in_specs=[pl.BlockSpec((ATT_TILE, width), lambda s, p, i: (s * nq + i, Q_COL // ATT_NP + p)),
                  pl.BlockSpec((seq, width), lambda s, p, i: (s, K_COL // ATT_NP + p)),
                  pl.BlockSpec((seq, width), lambda s, p, i: (s, V_COL // ATT_NP + p))] + [ANY] * n_ride,
        out_specs=[qblk, qblk] + [ANY] * n_ride,
        out_shape=[_sds((t, D_MODEL)), _sds((t, D_MODEL))] +
        [_sds((N_CHIPS,) + r.shape, r.dtype) for r in riders],
        scratch_shapes=sems,
        compiler_params=_params(*order),
    )(proj, proj, proj, *riders)


def _att_bwd(proj, rtot, datt, n_seq, name, riders=()):
    t = proj.shape[0]
    seq = t // n_seq
    nq = seq // ATT_TILE
    n_ride = len(riders)
    n_steps = n_seq * (N_PAIRS // ATT_NP) * nq

    def body(q_ref, k_ref, v_ref, rt_ref, do_ref, *rest):
        p_refs, (dq_ref, dk_out, dv_out), got_refs = rest[:n_ride], rest[n_ride:n_ride + 3], rest[n_ride + 3:2 * n_ride + 3]
        dk_ref, dv_ref = rest[2 * n_ride + 3:2 * n_ride + 5]
        i = pl.program_id(2)
        if n_ride:
            send_sems, recv_sems = rest[-2:]
            step = (pl.program_id(0) * (N_PAIRS // ATT_NP) + pl.program_id(1)) * nq + i
            _, _, c, _, chips = _place()
            copies = [pltpu.make_async_remote_copy(src_ref=p_refs[k].at[2 * px + py], dst_ref=got_refs[k].at[j],
                                                   send_sem=send_sems.at[3 * k + j], recv_sem=recv_sems.at[3 * k + j],
                                                   device_id=(px, py, c), device_id_type=MESH)
                      for k in range(n_ride) for j, (px, py) in enumerate(chips)]

            @pl.when(step == 0)
            def _():
                for cp in copies:
                    cp.start()

        @pl.when(i == 0)
        def _():
            dk_ref[...] = jnp.zeros_like(dk_ref)
            dv_ref[...] = jnp.zeros_like(dv_ref)

        row, col = _iota2((ATT_TILE, ATT_TILE), 0), _iota2((ATT_TILE, ATT_TILE), 1)
        upper = (row > col).astype(BF16)
        before = (row < col).astype(BF16)
        keeps = [_iota2((CHUNK, LANES), 1) < HEAD_DIM, _iota2((CHUNK, LANES), 1) >= HEAD_DIM]
        pair_lanes = [slice(pr * LANES, (pr + 1) * LANES) for pr in range(ATT_NP)]
        q_stacks = [_att_stack(q_ref, lanes, keeps, ATT_SCALE) for lanes in pair_lanes]
        do_stacks = [_att_stack(do_ref, lanes, keeps) for lanes in pair_lanes]
        totals = [jnp.concatenate([rt_ref[a * CHUNK:(a + 1) * CHUNK, pr * LANES + r * HEAD_DIM:pr * LANES + r * HEAD_DIM + 1]
                                   for a in range(ATT_SUB) for r in range(2)], axis=0) for pr in range(ATT_NP)]
        per_pair = ATT_SUB + 4

        def products(jt):
            k0 = pl.multiple_of(jt * ATT_TILE, ATT_TILE)
            return [(_dot(q_stacks[pr], k_ref[pl.ds(k0, ATT_TILE), lanes].astype(BF16), _DIMS["nt"]),
                     _dot(do_stacks[pr], v_ref[pl.ds(k0, ATT_TILE), lanes].astype(BF16), _DIMS["nt"]))
                    for pr, lanes in enumerate(pair_lanes)]

        def tile(jt, carry, diag):
            state = list(carry)
            ahead = None if diag else products(jt + 1)
            k0 = pl.multiple_of(jt * ATT_TILE, ATT_TILE)
            for pr, lanes in enumerate(pair_lanes):
                base = pr * per_pair
                seen, dseen = state[base + ATT_SUB], state[base + ATT_SUB + 1]
                q_stack, do_stack = q_stacks[pr], do_stacks[pr]
                k_tile = k_ref[pl.ds(k0, ATT_TILE), lanes].astype(BF16)
                lb, l1m, mask = _att_logits(state[base + ATT_SUB + 2], diag)
                here = jnp.sum(l1m, axis=1, keepdims=True)
                p = jnp.exp(lb + (_tri_dot(l1m, upper) + (totals[pr] - seen - here)))
                if diag:
                    p = jnp.where(mask, p, 0.0)
                pb = p.astype(BF16)
                dz = state[base + ATT_SUB + 3] * p
                dl1m = dseen + _tri_dot(dz, before)
                sig = jnp.exp(lb)
                ds = dz * (1.0 - sig) - dl1m * sig
                if diag:
                    ds = jnp.where(mask, ds, 0.0)
                dsb = ds.astype(BF16)
                dq_all = _dot(dsb, k_tile)
                for a in range(ATT_SUB):
                    state[base + a] = state[base + a] + _att_unstack(dq_all, a, keeps)
                dk_ref[pl.ds(k0, ATT_TILE), lanes] += _dot(dsb, q_stack, _DIMS["tn"])
                dv_ref[pl.ds(k0, ATT_TILE), lanes] += _dot(pb, do_stack, _DIMS["tn"])
                state[base + ATT_SUB] = seen + here
                state[base + ATT_SUB + 1] = dseen + jnp.sum(dz, axis=1, keepdims=True)
                if ahead is not None:
                    state[base + ATT_SUB + 2], state[base + ATT_SUB + 3] = ahead[pr]
            return tuple(state)

        first = products(0)
        init = ()
        for pr in range(ATT_NP):
            init += tuple(jnp.zeros((CHUNK, LANES), F32) for _ in range(ATT_SUB)) + \
                (jnp.zeros((ATT_ROWS, 1), F32), jnp.zeros((ATT_ROWS, 1), F32)) + first[pr]
        carry = lax.fori_loop(0, i, lambda jt, c: tile(jt, c, False), init)
        carry = tile(i, carry, True)
        for pr, lanes in enumerate(pair_lanes):
            for a in range(ATT_SUB):
                dq_ref[a * CHUNK:(a + 1) * CHUNK, lanes] = (carry[pr * per_pair + a] * ATT_SCALE).astype(BF16)

        @pl.when(i == nq - 1)
        def _():
            dk_out[...] = dk_ref[...].astype(BF16)
            dv_out[...] = dv_ref[...].astype(BF16)

        if n_ride:
            @pl.when(step == n_steps - 1)
            def _():
                for cp in copies:
                    cp.wait()

    width = ATT_NP * LANES
    qblk = pl.BlockSpec((ATT_TILE, width), lambda s, p, i: (s * nq + i, p))
    kv_out = pl.BlockSpec((seq, width), lambda s, p, i: (s, p))
    sems = [pltpu.SemaphoreType.DMA((3 * n_ride,)), pltpu.SemaphoreType.DMA((3 * n_ride,))] if n_ride else []
    order = ("arbitrary",) * 3 if n_ride else ("parallel", "parallel", "arbitrary")
    return pl.pallas_call(
        body, name=name, grid=(n_seq, N_PAIRS // ATT_NP, nq),
        in_specs=[pl.BlockSpec((ATT_TILE, width), lambda s, p, i: (s * nq + i, Q_COL // ATT_NP + p)),
                  pl.BlockSpec((seq, width), lambda s, p, i: (s, K_COL // ATT_NP + p)),
                  pl.BlockSpec((seq, width), lambda s, p, i: (s, V_COL // ATT_NP + p)),
                  qblk, qblk] + [ANY] * n_ride,
        out_specs=[qblk, kv_out, kv_out] + [ANY] * n_ride,
        out_shape=[_sds((t, D_MODEL), BF16)] * 3 + [_sds((3,) + r.shape[1:], r.dtype) for r in riders],
        scratch_shapes=[pltpu.VMEM((seq, width), F32), pltpu.VMEM((seq, width), F32)] + sems,
        compiler_params=_params(*order),
    )(proj, proj, proj, rtot, datt, *riders)


def _pad_lanes(v):
    return jnp.pad(v.reshape(1, -1), ((0, 0), (0, LANES - v.shape[0])))


def _add_then_norm(acc, xb, g):
    x1 = xb + acc
    return x1, x1 * _rstd(x1) * g


def _layer_fwd(x, h, p, next_g, n_seq, tag, riders=(), complete=None):
    proj, = _matmul(h, p["w_main"], "nn", [F32], f"in_proj{tag}", tn=1536)
    dtraw, = _matmul(h, p["w_dt"], "nn", [F32], f"dt_proj{tag}")
    xbc = _conv_fwd(proj, p["conv_w"], p["conv_b"], n_seq, f"conv_fwd{tag}")
    y, states = _ssd_fwd(xbc, dtraw, p["dt_bias"], p["a_log"], p["d_skip_x"], n_seq, f"ssd_fwd{tag}")
    att, rtot, *rode = _att_fwd(proj, n_seq, f"att_fwd{tag}", riders)
    if complete is not None:
        p = complete(rode)
    ycat = _mixnorm_fwd(y, proj, att, p["ssd_norm_g"], p["att_norm_g"], f"mixnorm_fwd{tag}")
    x1, h2 = _matmul(ycat, p["w_out"], "nn", [F32, BF16], f"out_proj{tag}", extras=[x], vecs=[p["norm_mlp_g"]],
                     epilogue=_add_then_norm, tn=D_MODEL)
    u, act = _matmul(h2, p["w_up"], "nn", [F32, BF16], f"up_proj{tag}",
                     epilogue=lambda acc: (acc, jnp.square(jnp.maximum(acc, 0.0))))
    if next_g is None:
        x2, = _matmul(act, p["w_down"], "nn", [F32], f"down_proj{tag}", extras=[x1],
                      epilogue=lambda acc, xb: (xb + acc,))
        h_next = None
    else:
        x2, h_next = _matmul(act, p["w_down"], "nn", [F32, BF16], f"down_proj{tag}", extras=[x1], vecs=[next_g],
                             epilogue=_add_then_norm, tn=D_MODEL)
    saved = dict(x=x, h=h, proj=proj, dtraw=dtraw, xbc=xbc, y=y, states=states, att=att, rtot=rtot, ycat=ycat,
                 x1=x1, h2=h2, u=u, act=act)
    return x2, h_next, saved


def _slab(buffers, name, layer, rows, per_chip_rows):
    shape = (N_CHIPS, DEPTH * rows, D_MODEL)
    if per_chip_rows:
        return buffers.get(name), shape, lambda i, j: (i, layer, 0)
    return buffers.get(name), shape, lambda i, j: (j, layer, 0)


def _layer_bwd(dx2, dx2b, p, s, buffers, layer, n_seq, tag, ride=None):
    g = {}
    buffers["w_down"], = _matmul(s["act"], dx2b, "tn", [F32], f"dw_down{tag}",
                                 slab=_slab(buffers, "w_down", layer, D_FF // N_CHIPS, True), tm=D_FF // N_CHIPS)
    du, = _matmul(dx2b, p["w_down"], "nt", [BF16], f"d_act{tag}", extras=[s["u"]],
                  epilogue=lambda acc, ub: (acc * (2.0 * jnp.maximum(ub, 0.0)),))
    buffers["w_up"], = _matmul(s["h2"], du, "tn", [F32], f"dw_up{tag}",
                               slab=_slab(buffers, "w_up", layer, D_MODEL, False), tn=D_FF // N_CHIPS)
    dh2, = _matmul(du, p["w_up"], "nt", [F32], f"d_h2{tag}")
    dx1, dx1b, g["norm_mlp_g"] = _rms_bwd(s["x1"], p["norm_mlp_g"], dh2, dx2, f"rms_mlp_bwd{tag}")
    buffers["w_out"], = _matmul(s["ycat"], dx1b, "tn", [F32], f"dw_out{tag}",
                                slab=_slab(buffers, "w_out", layer, 2 * D_MODEL // N_CHIPS, True),
                                tm=2 * D_MODEL // N_CHIPS)
    dycat, = _matmul(dx1b, p["w_out"], "nt", [F32], f"d_ycat{tag}")
    dy, dz, datt, g["ssd_norm_g"], g["att_norm_g"] = _mixnorm_bwd(
        s["y"], s["proj"], s["att"], p["ssd_norm_g"], p["att_norm_g"], dycat, f"mixnorm_bwd{tag}")
    dq, dk, dv, *received = _att_bwd(s["proj"], s["rtot"], datt, n_seq, f"att_bwd{tag}", ride[0]() if ride else ())
    if ride:
        ride[1](received)
    dxbc, ddtraw, pg = _ssd_bwd(s["xbc"], s["dtraw"], p["dt_bias"], p["a_log"], p["d_skip_x"], s["states"], dy,
                                n_seq, f"ssd_bwd{tag}")
    g["dt_bias"], g["a_log"], g["d_skip"] = pg[0, :N_HEADS], pg[1, :N_HEADS], pg[2, :N_HEADS]
    du_conv, wg = _conv_bwd(s["proj"], p["conv_w"], p["conv_b"], dxbc, n_seq, f"conv_bwd{tag}")
    g["conv_w"], g["conv_b"] = wg[:CONV_WIDTH], wg[CONV_WIDTH]
    dproj = jnp.concatenate([dz, du_conv, dq, dk, dv], axis=1)
    g["w_main"], = _matmul(s["h"], dproj, "tn", [F32], f"dw_in{tag}", tn=1536)
    g["w_dt"], = _matmul(s["h"], ddtraw, "tn", [F32], f"dw_dt{tag}")
    dh_dt, = _matmul(ddtraw, p["w_dt"], "nt", [F32], f"d_h_dt{tag}")
    dh, = _matmul(dproj, p["w_main"], "nt", [F32], f"d_h{tag}", extras=[dh_dt], epilogue=lambda acc, e: (acc + e,))
    dx, dxb, g["norm_mix_g"] = _rms_bwd(s["x"], p["norm_mix_g"], dh, dx1, f"rms_mix_bwd{tag}")
    return dx, dxb, g


def _split_w_in(w_full):
    c0 = D_MODEL + D_CONV
    main = jnp.concatenate([w_full[:, :c0], w_full[:, c0 + N_HEADS:]], axis=1)
    dt = jnp.pad(w_full[:, c0:c0 + N_HEADS], ((0, 0), (0, LANES - N_HEADS)))
    return main, dt


def _merge_w_in(main, dt):
    c0 = D_MODEL + D_CONV
    return jnp.concatenate([main[:, :c0], dt[:, :N_HEADS], main[:, c0:]], axis=1)


def _prep_layer(l, w_in_full, w_out, w_up, w_down, conv_w, small):
    w_main, w_dt = _split_w_in(w_in_full)
    return dict(
        w_main=w_main, w_dt=w_dt, w_out=w_out, w_up=w_up, w_down=w_down, conv_w=conv_w,
        conv_b=small["conv_b"][l].reshape(1, -1),
        dt_bias=_pad_lanes(small["dt_bias"][l]), a_log=_pad_lanes(small["a_log"][l]),
        d_skip_x=jnp.repeat(small["d_skip"][l], HEAD_DIM).reshape(1, -1),
        norm_mix_g=small["norm_mix_g"][l].reshape(1, -1), ssd_norm_g=small["ssd_norm_g"][l].reshape(1, -1),
        att_norm_g=small["att_norm_g"][l].reshape(1, -1), norm_mlp_g=small["norm_mlp_g"][l].reshape(1, -1),
    )


ANY = pl.BlockSpec(memory_space=pl.ANY)


def _place():
    x, y, c = lax.axis_index("x"), lax.axis_index("y"), lax.axis_index("c")
    return x, y, c, (x, y, 1 - c), [(1 - x, y), (x, 1 - y), (1 - x, 1 - y)]


def _gather_copies(s_ref, o_ref, send_sems, recv_sems, base):
    x, y, c, sibling, chips = _place()
    half = s_ref.shape[0] // 2

    def slab(px, py, hc):
        return o_ref.at[2 * px + py, pl.ds(hc * half, half), :]

    def copy(k, src, dst, to):
        return pltpu.make_async_remote_copy(src_ref=src, dst_ref=dst, send_sem=send_sems.at[base + k],
                                            recv_sem=recv_sems.at[base + k], device_id=to, device_id_type=MESH)

    sends = [copy(k, s_ref.at[pl.ds(c * half, half), :], slab(x, y, c), (px, py, c)) for k, (px, py) in enumerate(chips)]
    landed = [copy(k, slab(px, py, c), slab(px, py, c), (px, py, c)) for k, (px, py) in enumerate(chips)]
    passed = [copy(3 + k, slab(px, py, c), slab(px, py, c), sibling) for k, (px, py) in enumerate(chips)]
    arrived = [copy(3 + k, slab(px, py, 1 - c), slab(px, py, 1 - c), sibling) for k, (px, py) in enumerate(chips)]
    return sends, landed, passed, arrived


def _own_slot(others, shard, chip):
    return lax.dynamic_update_slice(others, shard[None], (chip, 0, 0))


def _all_gather_chips(shard, chip, name):
    r, ccols = shard.shape

    def body(s_ref, o_ref, send_sems, recv_sems):
        sends, landed, passed, arrived = _gather_copies(s_ref, o_ref, send_sems, recv_sems, 0)
        for cp in sends:
            cp.start()
        for got, onward in zip(landed, passed):
            got.wait_recv()
            onward.start()
        for cp in arrived:
            cp.wait_recv()
        for cp in sends + passed:
            cp.wait_send()

    others = pl.pallas_call(
        body, name=name, in_specs=[ANY], out_specs=ANY,
        out_shape=_sds((N_CHIPS, r, ccols), shard.dtype),
        scratch_shapes=[pltpu.SemaphoreType.DMA((6,)), pltpu.SemaphoreType.DMA((6,))],
    )(shard)
    return _own_slot(others, shard, chip)


def _sibling_swap(g, name):
    n, r, ccols = g.shape
    half = r // 2

    def body(g_ref, o_ref, send_sem, recv_sem):
        _, _, c, sibling, _ = _place()
        cp = pltpu.make_async_remote_copy(src_ref=g_ref.at[:, pl.ds((1 - c) * half, half), :], dst_ref=o_ref,
                                          send_sem=send_sem, recv_sem=recv_sem, device_id=sibling, device_id_type=MESH)
        cp.start()
        cp.wait()

    return pl.pallas_call(
        body, name=name, in_specs=[ANY], out_specs=ANY, out_shape=_sds((n, half, ccols), g.dtype),
        scratch_shapes=[pltpu.SemaphoreType.DMA, pltpu.SemaphoreType.DMA],
    )(g)


def _chip_exchange(p, name):
    _, h, ccols = p.shape

    def body(p_ref, o_ref, send_sems, recv_sems):
        _, _, c, _, chips = _place()
        copies = [pltpu.make_async_remote_copy(src_ref=p_ref.at[2 * px + py], dst_ref=o_ref.at[k],
                                               send_sem=send_sems.at[k], recv_sem=recv_sems.at[k],
                                               device_id=(px, py, c), device_id_type=MESH)
                  for k, (px, py) in enumerate(chips)]
        for cp in copies:
            cp.start()
        for cp in copies:
            cp.wait()

    return pl.pallas_call(
        body, name=name, in_specs=[ANY], out_specs=ANY, out_shape=_sds((3, h, ccols), p.dtype),
        scratch_shapes=[pltpu.SemaphoreType.DMA((3,)), pltpu.SemaphoreType.DMA((3,))],
    )(p)


def _sibling_share(full, name):
    r, ccols = full.shape
    h = r // 2

    def body(f_ref, o_ref, send_sem, recv_sem):
        _, _, c, sibling, _ = _place()
        mine = pl.ds(c * h, h)
        cp = pltpu.make_async_remote_copy(src_ref=f_ref.at[mine, :], dst_ref=o_ref.at[mine, :], send_sem=send_sem,
                                          recv_sem=recv_sem, device_id=sibling, device_id_type=MESH)
        cp.start()
        theirs = o_ref.at[pl.ds((1 - c) * h, h), :]
        pltpu.make_async_remote_copy(src_ref=theirs, dst_ref=theirs, send_sem=send_sem, recv_sem=recv_sem,
                                     device_id=sibling, device_id_type=MESH).wait_recv()
        cp.wait_send()

    return pl.pallas_call(
        body, name=name, in_specs=[ANY], out_specs=ANY, out_shape=_sds((r, ccols), full.dtype),
        input_output_aliases={0: 0},
        scratch_shapes=[pltpu.SemaphoreType.DMA, pltpu.SemaphoreType.DMA],
    )(full)


def _add_halves(g, a, c, name):
    n, r, ccols = g.shape
    half = r // 2
    tr = _tile(half, 256)
    nb = half // tr

    def body(c_ref, g_ref, a_ref, o_ref):
        o_ref[...] = (g_ref[...] + a_ref[...]).astype(BF16)

    blk = (1, tr, ccols)
    return pl.pallas_call(
        body, name=name,
        grid_spec=pltpu.PrefetchScalarGridSpec(
            num_scalar_prefetch=1, grid=(n, nb),
            in_specs=[pl.BlockSpec(blk, lambda j, i, c_ref: (j, c_ref[0] * nb + i, 0)),
                      pl.BlockSpec(blk, lambda j, i, c_ref: (j, i, 0))],
            out_specs=pl.BlockSpec(blk, lambda j, i, c_ref: (j, i, 0))),
        out_shape=_sds((n, half, ccols), BF16),
        compiler_params=_params("parallel", "parallel"),
    )(c.reshape(1).astype(jnp.int32), g, a)


def _sum_chips(p, got, chip, c, name):
    _, h, ccols = p.shape
    tr = _tile(h, 256)

    def body(j_ref, h_ref, p_ref, a_ref, b_ref, c_ref, o_ref):
        f32 = [ref[...].astype(F32) for ref in (p_ref, a_ref, b_ref, c_ref)]
        o_ref[...] = ((f32[0] + f32[1]) + f32[2]) + f32[3]

    blk = (1, tr, ccols)

    def slot(k):
        return pl.BlockSpec(blk, lambda i, j_ref, h_ref: (k, i, 0))

    return pl.pallas_call(
        body, name=name,
        grid_spec=pltpu.PrefetchScalarGridSpec(
            num_scalar_prefetch=2, grid=(h // tr,),
            in_specs=[pl.BlockSpec(blk, lambda i, j_ref, h_ref: (j_ref[0], i, 0)), slot(0), slot(1), slot(2)],
            out_specs=pl.BlockSpec(blk, lambda i, j_ref, h_ref: (h_ref[0], i, 0))),
        out_shape=_sds((2, h, ccols)),
        compiler_params=_params("parallel"),
    )(chip.reshape(1).astype(jnp.int32), c.reshape(1).astype(jnp.int32), p, got, got, got)


def _reduce_scatter_begin(g, c, name):
    return _add_halves(g, _sibling_swap(g, f"{name}_swap"), c, f"{name}_add")


def _reduce_scatter_end(p, got, chip, c, name):
    halves = _sum_chips(p, got, chip, c, f"{name}_sum")
    return _sibling_share(halves.reshape(2 * p.shape[1], p.shape[2]), f"{name}_share")


def _reduce_scatter(g, chip, c, name):
    p = _reduce_scatter_begin(g, c, name)
    return _reduce_scatter_end(p, _chip_exchange(p, f"{name}_xchg"), chip, c, name)


def _all_reduce_small(v, name):
    r = v.shape[0]

    def body(v_ref, o_ref, buf, send_sems, recv_sems):
        x, y, c, _, _ = _place()
        buf[0] = v_ref[...]
        copies = []
        for rel in range(1, 8):
            fx, fy, fc = (rel >> 2) & 1, (rel >> 1) & 1, rel & 1
            peer = (1 - x if fx else x, 1 - y if fy else y, 1 - c if fc else c)
            cp = pltpu.make_async_remote_copy(src_ref=v_ref, dst_ref=buf.at[rel], send_sem=send_sems.at[rel - 1],
                                              recv_sem=recv_sems.at[rel - 1], device_id=peer, device_id_type=MESH)
            cp.start()
            copies.append(cp)
        for cp in copies:
            cp.wait()
        me = 4 * x + 2 * y + c
        acc = buf[jnp.bitwise_xor(me, 0)]
        for src in range(1, 8):
            acc = acc + buf[jnp.bitwise_xor(me, src)]
        o_ref[...] = acc

    vm = pl.BlockSpec(memory_space=pltpu.VMEM)
    return pl.pallas_call(
        body, name=name, in_specs=[vm], out_specs=vm, out_shape=_sds((r, LANES)),
        scratch_shapes=[pltpu.VMEM((8, r, LANES), F32), pltpu.SemaphoreType.DMA((7,)), pltpu.SemaphoreType.DMA((7,))],
    )(v)


WEIGHTS = ["norm_mix_g", "w_in", "conv_w", "conv_b", "dt_bias", "a_log", "d_skip", "ssd_norm_g", "att_norm_g",
           "w_out", "norm_mlp_g", "w_up", "w_down", "final_norm_g"]
BIG = ["w_in", "w_out", "w_up", "w_down"]
SMALL = [n for n in WEIGHTS if n not in BIG]


def _pack(arrays):
    flat = []
    for a in arrays:
        a = a.reshape(-1)
        flat.append(jnp.pad(a, (0, (-a.shape[0]) % LANES)))
    flat = jnp.concatenate(flat)
    flat = jnp.pad(flat, (0, (-flat.shape[0]) % (8 * LANES)))
    return flat.reshape(-1, LANES)


def _unpack(packed, shapes):
    flat, out, pos = packed.reshape(-1), [], 0
    for shp in shapes:
        n = math.prod(shp)
        out.append(flat[pos:pos + n].reshape(shp))
        pos += n + (-n) % LANES
    return out


def _columns_to_shards(g):
    l, r, ccols = g.shape
    return g.reshape(l, r, N_CHIPS, ccols // N_CHIPS).transpose(2, 0, 1, 3).reshape(N_CHIPS, l * r, ccols // N_CHIPS)


def _from_gathered(name, g, l):
    rows = g.shape[1] // DEPTH
    part = g[:, l * rows:(l + 1) * rows, :]
    if name in ("w_in", "w_up", "conv_w"):
        return part.transpose(1, 0, 2).reshape(rows, N_CHIPS * g.shape[2])
    return part.reshape(N_CHIPS * rows, g.shape[2])


def kernel(x, norm_mix_g, w_in, conv_w, conv_b, dt_bias, a_log, d_skip, ssd_norm_g, att_norm_g, w_out, norm_mlp_g, w_up, w_down, final_norm_g, loss_target, m_norm_mix_g, m_w_in, m_conv_w, m_conv_b, m_dt_bias, m_a_log, m_d_skip, m_ssd_norm_g, m_att_norm_g, m_w_out, m_norm_mlp_g, m_w_up, m_w_down, m_final_norm_g, v_norm_mix_g, v_w_in, v_conv_w, v_conv_b, v_dt_bias, v_a_log, v_d_skip, v_ssd_norm_g, v_att_norm_g, v_w_out, v_norm_mlp_g, v_w_up, v_w_down, v_final_norm_g):
    w = dict(norm_mix_g=norm_mix_g, w_in=w_in, conv_w=conv_w, conv_b=conv_b, dt_bias=dt_bias, a_log=a_log,
             d_skip=d_skip, ssd_norm_g=ssd_norm_g, att_norm_g=att_norm_g, w_out=w_out, norm_mlp_g=norm_mlp_g,
             w_up=w_up, w_down=w_down, final_norm_g=final_norm_g)
    m = dict(norm_mix_g=m_norm_mix_g, w_in=m_w_in, conv_w=m_conv_w, conv_b=m_conv_b, dt_bias=m_dt_bias,
             a_log=m_a_log, d_skip=m_d_skip, ssd_norm_g=m_ssd_norm_g, att_norm_g=m_att_norm_g, w_out=m_w_out,
             norm_mlp_g=m_norm_mlp_g, w_up=m_w_up, w_down=m_w_down, final_norm_g=m_final_norm_g)
    v = dict(norm_mix_g=v_norm_mix_g, w_in=v_w_in, conv_w=v_conv_w, conv_b=v_conv_b, dt_bias=v_dt_bias,
             a_log=v_a_log, d_skip=v_d_skip, ssd_norm_g=v_ssd_norm_g, att_norm_g=v_att_norm_g, w_out=v_w_out,
             norm_mlp_g=v_norm_mlp_g, w_up=v_w_up, w_down=v_w_down, final_norm_g=v_final_norm_g)
    n_seq, seq, d = x.shape
    t = n_seq * seq
    chip = 2 * lax.axis_index("x") + lax.axis_index("y")
    core = lax.axis_index("c")

    shards = {n: w[n].astype(BF16).reshape(-1, w[n].shape[-1]) for n in BIG}
    shards["w_in_l1"] = shards["w_in"][D_MODEL:2 * D_MODEL]
    shards["w_in_rest"] = shards["w_in"][2 * D_MODEL:]
    gathered = {"w_in_first": _all_gather_chips(shards["w_in"][:D_MODEL], chip, "gather_w_in"),
                "conv_w": _all_gather_chips(conv_w.reshape(-1, conv_w.shape[-1]), chip, "gather_conv_w")}
    late = [n for n in BIG if n != "w_in"]
    riders = [["w_in_l1"] + late, ["w_in_rest"]]
    layers = {}

    def w_in_of(l):
        if l < 2:
            part = gathered["w_in_first" if l == 0 else "w_in_l1"]
        else:
            part = gathered["w_in_rest"][:, (l - 2) * D_MODEL:(l - 1) * D_MODEL, :]
        return part.transpose(1, 0, 2).reshape(D_MODEL, D_IN_PROJ)

    def layer_weights(l):
        full = {n: _from_gathered(n, gathered[n], l) if n in gathered else None for n in late + ["conv_w"]}
        return _prep_layer(l, w_in_of(l), *[full[n] for n in late + ["conv_w"]], w)

    def complete(at, rode):
        for n, others in zip(riders[at], rode):
            gathered[n] = _own_slot(others, shards[n], chip)
        for l in ((0, 1) if at == 0 else range(2, DEPTH)):
            layers[l] = layer_weights(l)
        return layers[at]

    xs = x.reshape(t, d)
    first = layer_weights(0)
    hs = _rms_fwd(xs, first["norm_mix_g"], "rms_mix_fwd_l0")
    saved = []
    for l in range(DEPTH):
        next_g = w["norm_mix_g"][l + 1].reshape(1, -1) if l + 1 < DEPTH else None
        if l < 2:
            xs, hs, s = _layer_fwd(xs, hs, first if l == 0 else layers[l], next_g, n_seq, f"_l{l}",
                                   [shards[n] for n in riders[l]], lambda rode, at=l: complete(at, rode))
        else:
            xs, hs, s = _layer_fwd(xs, hs, layers[l], next_g, n_seq, f"_l{l}")
        saved.append(s)
    loss_vec, dx, dxb, g_final = _final_loss(xs, final_norm_g.reshape(1, d), loss_target.reshape(t, d), "final_loss")
    loss = lax.psum(loss_vec[0, 0], ("x", "y", "c"))

    grads, shard_major, partial, g_out = [None] * DEPTH, {}, {}, {}

    def begin():
        for n in late:
            partial[n] = _reduce_scatter_begin(shard_major[n], core, f"rs_{n}")
        return [partial[n] for n in late]

    def end(received):
        for n, got in zip(late, received):
            g_out[n] = _reduce_scatter_end(partial[n], got, chip, core, f"rs_{n}").reshape(w[n].shape)

    for l in reversed(range(DEPTH)):
        dx, dxb, grads[l] = _layer_bwd(dx, dxb, layers[l], saved[l], shard_major, l, n_seq, f"_l{l}",
                                       (begin, end) if l == 0 else None)
        grads[l]["w_in"] = _merge_w_in(grads[l].pop("w_main"), grads[l].pop("w_dt"))
    grad_x = dx.reshape(n_seq, seq, d)

    full = {n: jnp.stack([grads[l][n] for l in range(DEPTH)]) for n in SMALL + ["w_in"] if n != "final_norm_g"}
    full["final_norm_g"] = g_final.reshape(d)
    g_out["w_in"] = _reduce_scatter(_columns_to_shards(full["w_in"]), chip, core, "rs_w_in").reshape(w_in.shape)
    small_sum = _all_reduce_small(_pack([full[n] for n in SMALL]), "allreduce_small")
    small_shapes = [(DEPTH, CONV_WIDTH, D_CONV) if n == "conv_w" else w[n].shape for n in SMALL]
    for n, val in zip(SMALL, _unpack(small_sum, small_shapes)):
        g_out[n] = val
    g_out["conv_w"] = lax.dynamic_slice_in_dim(g_out["conv_w"], chip * conv_w.shape[-1], conv_w.shape[-1], axis=2)

    delta, new_m, new_v = {}, {}, {}
    for n in BIG:
        two_d = (-1, w[n].shape[-1])
        dl, mn, vn = _adamw(w[n].reshape(two_d), g_out[n].reshape(two_d), m[n].reshape(two_d), v[n].reshape(two_d),
                            f"adamw_{n}")
        delta[n], new_m[n], new_v[n] = dl.reshape(w[n].shape), mn.reshape(w[n].shape), vn.reshape(w[n].shape)
    packs = [_pack([src[n] for n in SMALL]) for src in (w, g_out, m, v)]
    shapes = [w[n].shape for n in SMALL]
    for dst, packed in zip((delta, new_m, new_v), _adamw(*packs, "adamw_small")):
        for n, val in zip(SMALL, _unpack(packed, shapes)):
            dst[n] = val

    return (loss, grad_x, *[g_out[n] for n in WEIGHTS], *[delta[n] for n in WEIGHTS],
            *[new_m[n] for n in WEIGHTS], *[new_v[n] for n in WEIGHTS])
```
